```python
import math
import jax, jax.numpy as jnp
from jax import lax
import numpy as np

D_MODEL = 1024
BATCH = 16
SEQ = 256
DEPTH = 4
DEC_BATCH = 4
DEC_SEQ = 1024
PAST_LEN = 512

GRID_W = 64
D_MIX = D_MODEL
N_MOD = 6
EPS = 1e-6
ROPE_BASE = 10000.0
Q_BLOCK = 128
MLA_HEADS = 4
MLA_Q_RANK = 192
MLA_KV_RANK = 128
MLA_NOPE = 64
MLA_ROPE = 32
MLA_V = 64
DF_HEADS = 4
DF_DIM = 32
ML_HEADS = 4
ML_DK = 64
ML_DV = 128
ML_CHUNK = 64
D_FF = 2816
CONV_W = 3

IN_SIZES = (MLA_Q_RANK, MLA_KV_RANK, MLA_ROPE,
            DF_HEADS * 2 * DF_DIM, DF_HEADS * 2 * DF_DIM, DF_HEADS * 2 * DF_DIM,
            ML_HEADS * ML_DK, ML_HEADS * ML_DK, ML_HEADS * ML_DV, ML_HEADS * ML_DV, 4 * ML_HEADS)
D_IN = sum(IN_SIZES)

kernel_name = 'hybrid_mla_diff_mlstm_diffusion_step'


def _rms(x, g):
    x32 = x.astype(jnp.float32)
    y = x32 * lax.rsqrt(jnp.mean(x32 * x32, axis=-1, keepdims=True) + EPS)
    return (y * g.astype(jnp.float32)).astype(x.dtype)


def _split_cols(a, sizes):
    out, off = [], 0
    for s in sizes:
        out.append(a[..., off:off + s])
        off += s
    return out


def _rope2d_tables(n_tok, dim):
    t = jnp.arange(n_tok)
    row = (t // GRID_W).astype(jnp.float32)
    col = (t % GRID_W).astype(jnp.float32)
    nf = dim // 4
    inv = ROPE_BASE ** (-jnp.arange(nf, dtype=jnp.float32) / nf)
    ar = row[:, None] * inv[None, :]
    ac = col[:, None] * inv[None, :]
    ang = jnp.concatenate([ar, ar, ac, ac], axis=-1)
    return jnp.cos(ang), jnp.sin(ang)


def _rotate_half(y):
    y1, y2 = jnp.split(y, 2, axis=-1)
    return jnp.concatenate([-y2, y1], axis=-1)


def _rope_apply(x, cos, sin):
    shape = (1, x.shape[1]) + (1,) * (x.ndim - 3) + (x.shape[-1],)
    cos = cos.reshape(shape).astype(x.dtype)
    sin = sin.reshape(shape).astype(x.dtype)
    xr, xc = jnp.split(x, 2, axis=-1)
    return x * cos + jnp.concatenate([_rotate_half(xr), _rotate_half(xc)], axis=-1) * sin


def _over_query_blocks(fn, *qs):
    b, s = qs[0].shape[:2]
    nb = s // Q_BLOCK
    blocked = tuple(jnp.moveaxis(q.reshape((b, nb, Q_BLOCK) + q.shape[2:]), 1, 0) for q in qs)
    out = lax.map(lambda a: fn(*a), blocked)
    out = jnp.moveaxis(out, 0, 1)
    return out.reshape((b, s) + out.shape[3:])


def _mla_attend(q_nope, q_rope, k_nope, k_rope, v):
    scale = (MLA_NOPE + MLA_ROPE) ** -0.5

    def block(qn, qr):
        s = jnp.einsum('bqhd,bkhd->bhqk', qn, k_nope) + jnp.einsum('bqhr,bkr->bhqk', qr, k_rope)
        p = jax.nn.softmax(s.astype(jnp.float32) * scale, axis=-1).astype(v.dtype)
        return jnp.einsum('bhqk,bkhe->bqhe', p, v)
    return _over_query_blocks(block, q_nope, q_rope)


def _diff_attend(q, k, v, lam):
    scale = DF_DIM ** -0.5

    def block(qb):
        s = jnp.einsum('bqhcd,bkhcd->bchqk', qb, k).astype(jnp.float32) * scale
        p = jax.nn.softmax(s, axis=-1)
        a = (p[:, 0] - lam * p[:, 1]).astype(v.dtype)
        return jnp.einsum('bhqk,bkhe->bqhe', a, v)
    return _over_query_blocks(block, q)


def _mlstm_scan(q, k, v, i_pre, f_pre, c0, n0, m0):
    b_, s_ = q.shape[:2]
    nc = s_ // ML_CHUNK

    def chunks(a):
        return jnp.moveaxis(a.reshape((b_, nc, ML_CHUNK) + a.shape[2:]), 1, 0)

    logf = jax.nn.log_sigmoid(f_pre.astype(jnp.float32))
    ig = i_pre.astype(jnp.float32)
    mask = jnp.tril(jnp.ones((ML_CHUNK, ML_CHUNK), dtype=bool))

    def step(carry, xc):
        cm, nm, mm = carry
        qc, kc, vc, ic, lfc = xc
        bcum = jnp.cumsum(lfc, axis=1)
        dmat = bcum[:, :, None, :] - bcum[:, None, :, :] + ic[:, None, :, :]
        dmat = jnp.where(mask[None, :, :, None], dmat, -jnp.inf)
        m_inter = bcum + mm[:, None, :]
        m_hat = jnp.maximum(jnp.max(dmat, axis=2), m_inter)
        w = jnp.exp(dmat - m_hat[:, :, None, :])
        wqk = w * jnp.einsum('bthd,bshd->btsh', qc, kc).astype(jnp.float32)
        decay = jnp.exp(m_inter - m_hat)
        num = jnp.einsum('btsh,bshe->bthe', wqk, vc) + decay[..., None] * jnp.einsum('bthd,bhde->bthe', qc, cm)
        den = jnp.sum(wqk, axis=2) + decay * jnp.einsum('bthd,bhd->bth', qc, nm)
        h = num / jnp.maximum(jnp.abs(den), jnp.exp(-m_hat))[..., None]
        b_last = bcum[:, -1]
        g = b_last[:, None, :] - bcum + ic
        m_new = jnp.maximum(b_last + mm, jnp.max(g, axis=1))
        ws = jnp.exp(g - m_new[:, None, :])
        carry_decay = jnp.exp(b_last + mm - m_new)
        c_new = carry_decay[..., None, None] * cm + jnp.einsum('bsh,bshd,bshe->bhde', ws, kc, vc)
        n_new = carry_decay[..., None] * nm + jnp.einsum('bsh,bshd->bhd', ws, kc)
        return (c_new, n_new, m_new), h

    init = (c0.astype(jnp.float32), n0.astype(jnp.float32), m0.astype(jnp.float32))
    (cf, nf, mf), hs = lax.scan(step, init, (chunks(q), chunks(k), chunks(v), chunks(ig), chunks(logf)))
    hs = jnp.moveaxis(hs, 0, 1).reshape(b_, s_, ML_HEADS, ML_DV)
    return hs, cf, nf, mf


def _dwconv(u, w, b):
    pad = CONV_W // 2
    n = u.shape[-2]
    up = jnp.pad(u, [(0, 0)] * (u.ndim - 2) + [(pad, pad), (0, 0)])
    out = b
    for j in range(CONV_W):
        out = out + up[..., j:j + n, :] * w[j]
    return out


def _layer(x, mod, lp, lam_init, rope, rows, ctx):
    bsz, s_len, _ = x.shape
    latent = ctx is not None
    sh1, sc1, gt1, sh2, sc2, gt2 = jnp.split(mod, N_MOD, axis=-1)
    h = _rms(x, lp['g_norm1']) * (1 + sc1) + sh1
    cq, ckv, kr, dq, dk, dv, mq, mk, mv, mo, mg = _split_cols(h @ lp['w_in'], IN_SIZES)

    q = (_rms(cq, lp['g_cq']) @ lp['w_uq']).reshape(bsz, s_len, MLA_HEADS, MLA_NOPE + MLA_ROPE)
    q_nope, q_rope = q[..., :MLA_NOPE], q[..., MLA_NOPE:]
    c_kv = _rms(ckv, lp['g_ckv'])
    kv = (c_kv @ lp['w_ukv']).reshape(bsz, s_len, MLA_HEADS, MLA_NOPE + MLA_V)
    if latent:
        ctx_ckv, ctx_kr, ctx_dk, ctx_dv, ctx_c, ctx_n, ctx_m = ctx
        t_len = ctx_ckv.shape[1]
        (cos_a, sin_a), (cos_d, sin_d) = rope
        q_rope = _rope_apply(q_rope, cos_a, sin_a)
        kv_ctx = (ctx_ckv @ lp['w_ukv']).reshape(bsz, t_len, MLA_HEADS, MLA_NOPE + MLA_V)
        kv_all = jnp.concatenate([kv_ctx, kv], axis=1)
        kr_all = jnp.concatenate([ctx_kr, _rope_apply(kr, cos_a, sin_a)], axis=1)
    else:
        kv_all, kr_all = kv, kr
    o_mla = _mla_attend(q_nope, q_rope, kv_all[..., :MLA_NOPE], kr_all, kv_all[..., MLA_NOPE:])
    o_mla = o_mla.reshape(bsz, s_len, MLA_HEADS * MLA_V)

    dq = dq.reshape(bsz, s_len, DF_HEADS, 2, DF_DIM)
    dk = dk.reshape(bsz, s_len, DF_HEADS, 2, DF_DIM)
    dv = dv.reshape(bsz, s_len, DF_HEADS, 2 * DF_DIM)
    if latent:
        dq_att = _rope_apply(dq, cos_d, sin_d)
        dk_all = jnp.concatenate([ctx_dk.reshape(bsz, t_len, DF_HEADS, 2, DF_DIM), _rope_apply(dk, cos_d, sin_d)], axis=1)
        dv_all = jnp.concatenate([ctx_dv, dv], axis=1)
    else:
        dq_att, dk_all, dv_all = dq, dk, dv
    f32 = jnp.float32
    lam = (jnp.exp(jnp.sum(lp['lam_q1'].astype(f32) * lp['lam_k1'].astype(f32)))
           - jnp.exp(jnp.sum(lp['lam_q2'].astype(f32) * lp['lam_k2'].astype(f32))) + lam_init)
    o_diff = _diff_attend(dq_att, dk_all, dv_all, lam)
    o_diff = (_rms(o_diff, lp['g_subln']) * (1 - lam_init)).reshape(bsz, s_len, DF_HEADS * 2 * DF_DIM)

    mq = mq.reshape(bsz, s_len, ML_HEADS, ML_DK)
    mk = mk.reshape(bsz, s_len, ML_HEADS, ML_DK) * (ML_DK ** -0.5)
    mv = mv.reshape(bsz, s_len, ML_HEADS, ML_DV)
    i_f, f_f, i_b, f_b = jnp.split((mg + lp['b_gate']).astype(f32), 4, axis=-1)
    if latent:
        c0, n0, m0 = ctx_c, ctx_n, ctx_m
    else:
        c0 = jnp.zeros((bsz, 2, ML_HEADS, ML_DK, ML_DV), f32)
        n0 = jnp.zeros((bsz, 2, ML_HEADS, ML_DK), f32)
        m0 = jnp.zeros((bsz, 2, ML_HEADS), f32)
    h_f, cf, nf, mf = _mlstm_scan(mq, mk, mv, i_f, f_f, c0[:, 0], n0[:, 0], m0[:, 0])
    flip = lambda a: jnp.flip(a, axis=1)
    h_b, cb, nb, mb = _mlstm_scan(flip(mq), flip(mk), flip(mv), flip(i_b), flip(f_b), c0[:, 1], n0[:, 1], m0[:, 1])
    h_m = _rms((h_f + flip(h_b)).astype(x.dtype), lp['g_mnorm'].reshape(ML_HEADS, ML_DV))
    o_ml = jax.nn.sigmoid(mo) * h_m.reshape(bsz, s_len, ML_HEADS * ML_DV)

    mix = jnp.concatenate([o_mla, o_diff, o_ml], axis=-1) @ lp['w_out']
    x = x + gt1 * mix

    h2 = _rms(x, lp['g_norm2']) * (1 + sc2) + sh2
    u = h2 @ lp['w_up']
    if latent:
        u = _dwconv(u.reshape(bsz, rows, GRID_W, 2 * D_FF), lp['conv_w'], lp['conv_b']).reshape(bsz, s_len, 2 * D_FF)
    else:
        u = _dwconv(u, lp['conv_w'], lp['conv_b'])
    val, gate = jnp.split(u, 2, axis=-1)
    x = x + gt2 * ((jax.nn.silu(gate) * val) @ lp['w_down'])

    if latent:
        return x, None
    state = (c_kv, kr, dk.reshape(bsz, s_len, DF_HEADS, 2 * DF_DIM), dv,
             jnp.stack([cf, cb], axis=1).astype(x.dtype),
             jnp.stack([nf, nb], axis=1).astype(x.dtype),
             jnp.stack([mf, mb], axis=1).astype(x.dtype))
    return x, state


def setup_inputs(seed: int = 0) -> dict:
    key = jax.random.key(seed)
    ks = iter(jax.random.split(key, 40))
    nrm = lambda shape, s: jax.random.normal(next(ks), shape, jnp.float32) * s
    gain = lambda shape: 1.0 + nrm(shape, 0.01)
    fgate = lambda: jnp.linspace(3.0, 6.0, ML_HEADS, dtype=jnp.float32)[None, :] + nrm((DEPTH, ML_HEADS), 0.1)
    d = {}
    d['x_prompt'] = nrm((BATCH, SEQ, D_MODEL), 1.0)
    d['x_sample'] = nrm((DEC_BATCH, DEC_SEQ, D_MODEL), 1.0)
    d['cache_mla_ckv'] = nrm((DEC_BATCH, DEPTH, PAST_LEN, MLA_KV_RANK), 1.0)
    d['cache_mla_krope'] = nrm((DEC_BATCH, DEPTH, PAST_LEN, MLA_ROPE), 1.0)
    d['cache_diff_k'] = nrm((DEC_BATCH, DEPTH, PAST_LEN, DF_HEADS, 2 * DF_DIM), 1.0)
    d['cache_diff_v'] = nrm((DEC_BATCH, DEPTH, PAST_LEN, DF_HEADS, 2 * DF_DIM), 1.0)
    d['state_mlstm_C'] = nrm((DEC_BATCH, DEPTH, 2, ML_HEADS, ML_DK, ML_DV), 0.5)
    d['state_mlstm_n'] = nrm((DEC_BATCH, DEPTH, 2, ML_HEADS, ML_DK), 0.5)
    d['state_mlstm_m'] = nrm((DEC_BATCH, DEPTH, 2, ML_HEADS), 1.0)
    d['c'] = nrm((DEC_BATCH, D_MODEL), 1.0)
    d['c_ctx'] = nrm((D_MODEL,), 1.0)
    d['w_ada'] = nrm((DEPTH, D_MODEL, N_MOD * D_MODEL), 0.5 * D_MODEL ** -0.5)
    d['b_ada'] = nrm((DEPTH, N_MOD * D_MODEL), 0.01)
    d['g_norm1'] = gain((DEPTH, D_MODEL))
    d['w_in'] = nrm((DEPTH, D_MODEL, D_IN), D_MODEL ** -0.5)
    d['g_cq'] = gain((DEPTH, MLA_Q_RANK))
    d['w_uq'] = nrm((DEPTH, MLA_Q_RANK, MLA_HEADS * (MLA_NOPE + MLA_ROPE)), MLA_Q_RANK ** -0.5)
    d['g_ckv'] = gain((DEPTH, MLA_KV_RANK))
    d['w_ukv'] = nrm((DEPTH, MLA_KV_RANK, MLA_HEADS * (MLA_NOPE + MLA_V)), MLA_KV_RANK ** -0.5)
    d['lam_q1'] = nrm((DEPTH, DF_DIM), 0.1)
    d['lam_k1'] = nrm((DEPTH, DF_DIM), 0.1)
    d['lam_q2'] = nrm((DEPTH, DF_DIM), 0.1)
    d['lam_k2'] = nrm((DEPTH, DF_DIM), 0.1)
    d['g_subln'] = gain((DEPTH, 2 * DF_DIM))
    i_fw = nrm((DEPTH, ML_HEADS), 0.1)
    f_fw = fgate()
    i_bw = nrm((DEPTH, ML_HEADS), 0.1)
    f_bw = fgate()
    d['b_gate'] = jnp.concatenate([i_fw, f_fw, i_bw, f_bw], axis=-1)
    d['g_mnorm'] = gain((DEPTH, ML_HEADS * ML_DV))
    d['w_out'] = nrm((DEPTH, D_MIX, D_MODEL), D_MIX ** -0.5)
    d['g_norm2'] = gain((DEPTH, D_MODEL))
    d['w_up'] = nrm((DEPTH, D_MODEL, 2 * D_FF), D_MODEL ** -0.5)
    d['conv_w'] = nrm((DEPTH, CONV_W, 2 * D_FF), CONV_W ** -0.5)
    d['conv_b'] = nrm((DEPTH, 2 * D_FF), 0.01)
    d['w_down'] = nrm((DEPTH, D_FF, D_MODEL), D_FF ** -0.5)
    d['g_final'] = gain((D_MODEL,))
    return d


def reference(x_prompt, x_sample, cache_mla_ckv, cache_mla_krope, cache_diff_k, cache_diff_v,
              state_mlstm_C, state_mlstm_n, state_mlstm_m, c, c_ctx,
              w_ada, b_ada, g_norm1, w_in, g_cq, w_uq, g_ckv, w_ukv,
              lam_q1, lam_k1, lam_q2, lam_k2, g_subln, b_gate, g_mnorm, w_out,
              g_norm2, w_up, conv_w, conv_b, w_down, g_final):
    params = dict(w_ada=w_ada, b_ada=b_ada, g_norm1=g_norm1, w_in=w_in, g_cq=g_cq, w_uq=w_uq,
                  g_ckv=g_ckv, w_ukv=w_ukv, lam_q1=lam_q1, lam_k1=lam_k1, lam_q2=lam_q2,
                  lam_k2=lam_k2, g_subln=g_subln, b_gate=b_gate, g_mnorm=g_mnorm, w_out=w_out,
                  g_norm2=g_norm2, w_up=w_up, conv_w=conv_w, conv_b=conv_b, w_down=w_down)
    n_lat = x_sample.shape[1]
    rows = n_lat // GRID_W
    rope = (_rope2d_tables(n_lat, MLA_ROPE), _rope2d_tables(n_lat, DF_DIM))
    xp, xs = x_prompt, x_sample
    collected = [[] for _ in range(7)]
    for l in range(DEPTH):
        lp = {name: arr[l] for name, arr in params.items()}
        lam_init = 0.8 - 0.6 * math.exp(-0.3 * l)
        mod_ctx = (jax.nn.silu(c_ctx) @ lp['w_ada'] + lp['b_ada'])[None, None, :]
        mod_lat = (jax.nn.silu(c) @ lp['w_ada'] + lp['b_ada'])[:, None, :]
        xp, st = _layer(xp, mod_ctx, lp, lam_init, None, None, None)
        ctx_l = (cache_mla_ckv[:, l], cache_mla_krope[:, l], cache_diff_k[:, l], cache_diff_v[:, l],
                 state_mlstm_C[:, l], state_mlstm_n[:, l], state_mlstm_m[:, l])
        xs, _ = _layer(xs, mod_lat, lp, lam_init, rope, rows, ctx_l)
        for lst, a in zip(collected, st):
            lst.append(a)
    y_prompt = _rms(xp, g_final)
    y_sample = _rms(xs, g_final)
    new_mla_ckv = jnp.stack(collected[0], axis=1)
    new_mla_krope = jnp.stack(collected[1], axis=1)
    new_diff_k = jnp.stack(collected[2], axis=1)
    new_diff_v = jnp.stack(collected[3], axis=1)
    new_mlstm_C = jnp.stack(collected[4], axis=1)
    new_mlstm_n = jnp.stack(collected[5], axis=1)
    new_mlstm_m = jnp.stack(collected[6], axis=1)
    return (y_prompt, y_sample, new_mla_ckv, new_mla_krope, new_diff_k, new_diff_v, new_mlstm_C, new_mlstm_n, new_mlstm_m)
```

```python
import functools
import math

import jax
import jax.numpy as jnp
from jax import lax
from jax.experimental import pallas as pl
from jax.experimental.pallas import tpu as pltpu

F32 = jnp.float32
BF16 = jnp.bfloat16

D_MODEL = 1024
DEPTH = 4
GRID_W = 64
N_MOD = 6
EPS = 1e-6
ROPE_BASE = 10000.0
MLA_HEADS = 4
MLA_Q_RANK = 192
MLA_KV_RANK = 128
MLA_NOPE = 64
MLA_ROPE = 32
MLA_V = 64
DF_HEADS = 4
DF_DIM = 32
ML_HEADS = 4
ML_DK = 64
ML_DV = 128
D_FF = 2816
CONV_W = 3

LANES = 128
VMEM_LIMIT = 48 * 1024 * 1024

C_CQ, C_CKV, C_AUX, C_DQ, C_DK, C_DV, C_MQ, C_MK, C_MV, C_MO = (
    0, 256, 384, 512, 768, 1024, 1280, 1536, 1792, 2304)
NP_IN = 2816
AUX_GATE = 32
LANE_F = (AUX_GATE + 4, AUX_GATE + 12)

TM_PRE = 256
TQ = 256
ML_CHUNK = 128
TM_FFN = 512
FC = 256
N_FC = D_FF // FC

NT = (((1,), (1,)), ((), ()))
TN = (((0,), (0,)), ((), ()))


def _cparams(sem):
    return pltpu.CompilerParams(dimension_semantics=sem, vmem_limit_bytes=VMEM_LIMIT)


def _dot(a, b):
    return jnp.dot(a, b, preferred_element_type=F32)


def _dot_nt(a, b):
    return lax.dot_general(a, b, NT, preferred_element_type=F32)


def _rms_rows(x, g, n):
    ms = jnp.sum(x * x, axis=-1, keepdims=True) * (1.0 / n)
    return x * lax.rsqrt(ms + EPS) * g


def _rope(x, cos, sin_up, sin_dn):
    w = x.shape[-1]
    return x * cos + pltpu.roll(x, w - 8, 1) * sin_up + pltpu.roll(x, 8, 1) * sin_dn


def _ada_kernel(c_ref, w_ref, b_ref, o_ref):
    c = c_ref[...]
    s = (c * jax.nn.sigmoid(c)).astype(BF16)
    o_ref[...] = _dot(s, w_ref[...].astype(BF16)) + b_ref[...]


def _ada_call(cond, w_ada, b_ada):
    nt = 1024
    return pl.pallas_call(
        _ada_kernel,
        out_shape=jax.ShapeDtypeStruct((DEPTH, 8, N_MOD * D_MODEL), F32),
        grid=(DEPTH, N_MOD * D_MODEL // nt),
        in_specs=[pl.BlockSpec((8, D_MODEL), lambda l, j: (0, 0)),
                  pl.BlockSpec((None, D_MODEL, nt), lambda l, j: (l, 0, j)),
                  pl.BlockSpec((None, 1, nt), lambda l, j: (l, 0, j))],
        out_specs=pl.BlockSpec((None, 8, nt), lambda l, j: (l, 0, j)),
        compiler_params=_cparams(("arbitrary", "arbitrary")),
        name="ada_mod",
    )(cond, w_ada, b_ada)


def _ctxkv_kernel(ckv_ref, kr_ref, wkk_ref, wv_ref, k_ref, v_ref):
    ckv = ckv_ref[...].astype(BF16)
    kin = jnp.concatenate([ckv, kr_ref[...].astype(BF16)], axis=1)
    k_ref[...] = _dot(kin, wkk_ref[...]).astype(BF16)
    v_ref[...] = _dot(ckv, wv_ref[...]).astype(BF16)


def _ctxkv_call(cache_ckv, cache_kr_pad, wkk, wv):
    b, _, t, _ = cache_ckv.shape
    return pl.pallas_call(
        _ctxkv_kernel,
        out_shape=(jax.ShapeDtypeStruct((DEPTH, b, t, 512), BF16),
                   jax.ShapeDtypeStruct((DEPTH, b, t, 256), BF16)),
        grid=(DEPTH, b),
        in_specs=[pl.BlockSpec((None, None, t, MLA_KV_RANK), lambda l, i: (i, l, 0, 0)),
                  pl.BlockSpec((None, None, t, LANES), lambda l, i: (i, l, 0, 0)),
                  pl.BlockSpec((None, 256, 512), lambda l, i: (l, 0, 0)),
                  pl.BlockSpec((None, MLA_KV_RANK, 256), lambda l, i: (l, 0, 0))],
        out_specs=(pl.BlockSpec((None, None, t, 512), lambda l, i: (l, i, 0, 0)),
                   pl.BlockSpec((None, None, t, 256), lambda l, i: (l, i, 0, 0))),
        compiler_params=_cparams(("arbitrary", "arbitrary")),
        name="ctx_kv",
    )(cache_ckv, cache_kr_pad, wkk, wv)


def _pre_kernel(latent, *refs):
    (x_ref, sh_ref, sc_ref, g1_ref, win_ref, gcq_ref, wuq_ref, gckv_ref, wkk_ref, wv_ref,
     bg_ref) = refs[:11]
    refs = refs[11:]
    if latent:
        (cq_t, sqa_t, sqb_t, cd_t, sda_t, sdb_t, ck_t, ska_t, skb_t) = refs[:9]
        refs = refs[9:]
    (q_ref, k_ref, v_ref, dq_ref, dk_ref, dv_ref, mq_ref, mk_ref, mv_ref, mo_ref,
     aux_ref) = refs[:11]
    refs = refs[11:]
    if not latent:
        ckv_out, kr_out, dk_out, dv_out = refs

    x = x_ref[...]
    h = _rms_rows(x, g1_ref[...], D_MODEL) * (1.0 + sc_ref[...]) + sh_ref[...]
    proj = _dot(h.astype(BF16), win_ref[...])

    cq = _rms_rows(proj[:, C_CQ:C_CQ + 256], gcq_ref[...], MLA_Q_RANK)
    q = _dot(cq.astype(BF16), wuq_ref[...])
    if latent:
        q = _rope(q, cq_t[...], sqa_t[...], sqb_t[...])
    q_ref[...] = (q * ((MLA_NOPE + MLA_ROPE) ** -0.5)).astype(BF16)

    c_kv = _rms_rows(proj[:, C_CKV:C_CKV + MLA_KV_RANK], gckv_ref[...], MLA_KV_RANK)
    aux = proj[:, C_AUX:C_AUX + LANES] + bg_ref[...]
    if not latent:
        ckv_out[...] = c_kv
        kr_out[...] = aux[:, :MLA_ROPE]
    else:
        aux = _rope(aux, ck_t[...], ska_t[...], skb_t[...])
    aux_ref[...] = aux
    ckv_b = c_kv.astype(BF16)
    kin = jnp.concatenate([ckv_b, aux.astype(BF16)], axis=1)
    k_ref[...] = _dot(kin, wkk_ref[...]).astype(BF16)
    v_ref[...] = _dot(ckv_b, wv_ref[...]).astype(BF16)

    dq = proj[:, C_DQ:C_DQ + 256]
    dk = proj[:, C_DK:C_DK + 256]
    dv = proj[:, C_DV:C_DV + 256]
    if not latent:
        dk_out[...] = dk
        dv_out[...] = dv
    else:
        dq = _rope(dq, cd_t[...], sda_t[...], sdb_t[...])
        dk = _rope(dk, cd_t[...], sda_t[...], sdb_t[...])
    dq_ref[...] = (dq * (DF_DIM ** -0.5)).astype(BF16)
    dk_ref[...] = dk.astype(BF16)
    dv_ref[...] = dv.astype(BF16)

    mq_ref[...] = proj[:, C_MQ:C_MQ + 256].astype(BF16)
    mk_ref[...] = (proj[:, C_MK:C_MK + 256] * (ML_DK ** -0.5)).astype(BF16)
    mv_ref[...] = proj[:, C_MV:C_MV + 512].astype(BF16)
    mo_ref[...] = proj[:, C_MO:C_MO + 512]


def _pre_call(latent, l, x, mods, wts, tables):
    b, s, _ = x.shape
    ns = s // TM_PRE
    grid = (ns, b)

    def tok(width):
        return pl.BlockSpec((None, TM_PRE, width), lambda j, i: (i, j, 0))

    def mod(chunk):
        if latent:
            return pl.BlockSpec((None, 1, D_MODEL), lambda j, i: (l * 48 + (1 + i) * 6 + chunk, 0, 0))
        return pl.BlockSpec((None, 1, D_MODEL), lambda j, i: (l * 48 + chunk, 0, 0))

    def lw(*shape):
        nd = len(shape)
        return pl.BlockSpec((None,) + shape, lambda j, i: (l,) + (0,) * nd)

    in_specs = [tok(D_MODEL), mod(0), mod(1), lw(1, D_MODEL), lw(D_MODEL, NP_IN), lw(1, 256),
                lw(256, 512), lw(1, MLA_KV_RANK), lw(256, 512), lw(MLA_KV_RANK, 256), lw(1, LANES)]
    args = [x, mods, mods, wts["g_norm1"], wts["w_in"], wts["g_cq"], wts["w_uq"], wts["g_ckv"],
            wts["w_kk"], wts["w_v"], wts["b_gate"]]
    if latent:
        for t in tables:
            in_specs.append(pl.BlockSpec((TM_PRE, t.shape[1]), lambda j, i: (j, 0)))
            args.append(t)

    widths = [(512, BF16), (512, BF16), (256, BF16), (256, BF16), (256, BF16), (256, BF16),
              (256, BF16), (256, BF16), (512, BF16), (512, F32), (LANES, F32)]
    if not latent:
        widths += [(MLA_KV_RANK, F32), (MLA_ROPE, F32), (256, F32), (256, F32)]
    out_shape = tuple(jax.ShapeDtypeStruct((b, s, w), dt) for w, dt in widths)
    out_specs = tuple(tok(w) for w, _ in widths)
    return pl.pallas_call(
        functools.partial(_pre_kernel, latent),
        out_shape=out_shape, grid=grid, in_specs=in_specs, out_specs=out_specs,
        compiler_params=_cparams(("arbitrary", "arbitrary")),
        name="pre_lat" if latent else "pre_ctx",
    )(*args)


def _softmax_parts(s_list):
    m = functools.reduce(jnp.maximum, [jnp.max(s, axis=1, keepdims=True) for s in s_list])
    p_list = [jnp.exp(s - m) for s in s_list]
    l = functools.reduce(jnp.add, [jnp.sum(p, axis=1, keepdims=True) for p in p_list])
    return p_list, l


def _mla_kernel(latent, *refs):
    if latent:
        q_ref, k_ref, v_ref, kc_ref, vc_ref, o_ref = refs
        segs = [(kc_ref, vc_ref), (k_ref, v_ref)]
    else:
        q_ref, k_ref, v_ref, o_ref = refs
        segs = [(k_ref, v_ref)]
    lane = lax.broadcasted_iota(jnp.int32, (TQ, LANES), 1)
    outs = []
    for h in range(MLA_HEADS):
        hs = slice(h * LANES, (h + 1) * LANES)
        ps = slice((h // 2) * LANES, (h // 2 + 1) * LANES)
        qh = q_ref[:, hs]
        p_list, l = _softmax_parts([_dot_nt(qh, kr[:, hs]) for kr, _ in segs])
        pv = functools.reduce(jnp.add, [_dot(p.astype(BF16), vr[:, ps])
                                        for p, (_, vr) in zip(p_list, segs)])
        outs.append(pv / l)
    o_ref[:, 0:LANES] = jnp.where(lane < MLA_V, outs[0], outs[1]).astype(BF16)
    o_ref[:, LANES:2 * LANES] = jnp.where(lane < MLA_V, outs[2], outs[3]).astype(BF16)


def _mla_call(latent, l, q, k, v, kctx=None, vctx=None):
    b, s, _ = q.shape
    grid = (b, s // TQ)
    in_specs = [pl.BlockSpec((None, TQ, 512), lambda i, j: (i, j, 0)),
                pl.BlockSpec((None, s, 512), lambda i, j: (i, 0, 0)),
                pl.BlockSpec((None, s, 256), lambda i, j: (i, 0, 0))]
    args = [q, k, v]
    if latent:
        t = kctx.shape[2]
        in_specs += [pl.BlockSpec((None, None, t, 512), lambda i, j: (l, i, 0, 0)),
                     pl.BlockSpec((None, None, t, 256), lambda i, j: (l, i, 0, 0))]
        args += [kctx, vctx]
    return pl.pallas_call(
        functools.partial(_mla_kernel, latent),
        out_shape=jax.ShapeDtypeStruct((b, s, 256), BF16),
        grid=grid, in_specs=in_specs,
        out_specs=pl.BlockSpec((None, TQ, 256), lambda i, j: (i, j, 0)),
        compiler_params=_cparams(("arbitrary", "arbitrary")),
        name="mla_lat" if latent else "mla_ctx",
    )(*args)


def _diff_kernel(latent, lam_init, *refs):
    if latent:
        (q_ref, k_ref, v_ref, kc_ref, vc_ref, lq1, lk1, lq2, lk2, g_ref, o_ref) = refs
        segs = [(kc_ref, vc_ref), (k_ref, v_ref)]
    else:
        (q_ref, k_ref, v_ref, lq1, lk1, lq2, lk2, g_ref, o_ref) = refs
        segs = [(k_ref, v_ref)]
    lam = (jnp.exp(jnp.sum(lq1[...] * lk1[...], axis=1, keepdims=True))
           - jnp.exp(jnp.sum(lq2[...] * lk2[...], axis=1, keepdims=True)) + lam_init)
    lane = lax.broadcasted_iota(jnp.int32, (TQ, LANES), 1)
    grp = lane >> 5
    qf = q_ref[...].astype(F32)
    outs = []
    for h in range(DF_HEADS):
        ps = slice((h // 2) * LANES, (h // 2 + 1) * LANES)
        hh = h % 2
        qblk = qf[:, ps]
        parts = []
        for c in range(2):
            qm = jnp.where(grp == 2 * hh + c, qblk, 0.0).astype(BF16)
            parts.append(_softmax_parts([_dot_nt(qm, kr[:, ps].astype(BF16)) for kr, _ in segs]))
        (p0, l0), (p1, l1) = parts
        inv0 = 1.0 / l0
        inv1 = lam / l1
        pv = functools.reduce(jnp.add, [
            _dot((a0 * inv0 - a1 * inv1).astype(BF16), vr[:, ps].astype(BF16))
            for a0, a1, (_, vr) in zip(p0, p1, segs)])
        valid = (lane >> 6) == hh
        ms = jnp.sum(jnp.where(valid, pv * pv, 0.0), axis=1, keepdims=True) * (1.0 / (2 * DF_DIM))
        outs.append(pv * lax.rsqrt(ms + EPS) * g_ref[:, ps] * (1.0 - lam_init))
    o_ref[:, 0:LANES] = jnp.where(lane < 2 * DF_DIM, outs[0], outs[1]).astype(BF16)
    o_ref[:, LANES:2 * LANES] = jnp.where(lane < 2 * DF_DIM, outs[2], outs[3]).astype(BF16)


def _diff_call(latent, l, lam_init, dq, dk, dv, wts, cdk=None, cdv=None):
    b, s, _ = dq.shape
    grid = (b, s // TQ)
    in_specs = [pl.BlockSpec((None, TQ, 256), lambda i, j: (i, j, 0)),
                pl.BlockSpec((None, s, 256), lambda i, j: (i, 0, 0)),
                pl.BlockSpec((None, s, 256), lambda i, j: (i, 0, 0))]
    args = [dq, dk, dv]
    if latent:
        t = cdk.shape[2]
        in_specs += [pl.BlockSpec((None, None, t, 256), lambda i, j: (i, l, 0, 0)),
                     pl.BlockSpec((None, None, t, 256), lambda i, j: (i, l, 0, 0))]
        args += [cdk, cdv]
    for name in ("lam_q1", "lam_k1", "lam_q2", "lam_k2"):
        in_specs.append(pl.BlockSpec((None, 1, DF_DIM), lambda i, j: (l, 0, 0)))
        args.append(wts[name])
    in_specs.append(pl.BlockSpec((None, 1, 256), lambda i, j: (l, 0, 0)))
    args.append(wts["g_subln"])
    return pl.pallas_call(
        functools.partial(_diff_kernel, latent, lam_init),
        out_shape=jax.ShapeDtypeStruct((b, s, 256), BF16),
        grid=grid, in_specs=in_specs,
        out_specs=pl.BlockSpec((None, TQ, 256), lambda i, j: (i, j, 0)),
        compiler_params=_cparams(("arbitrary", "arbitrary")),
        name="diff_lat" if latent else "diff_ctx",
    )(*args)


def _log_sigmoid(x):
    return jnp.minimum(x, 0.0) - jnp.log1p(jnp.exp(-jnp.abs(x)))


def _mlstm_chunk(d, c, mq_ref, mk_ref, mv_ref, aux_ref, s_ref, m_ref, h_ref):
    L = ML_CHUNK
    rows = pl.ds(pl.multiple_of(c * L, L), L)
    row_i = lax.broadcasted_iota(jnp.int32, (L, L), 0)
    col_i = lax.broadcasted_iota(jnp.int32, (L, L), 1)
    mask = (col_i <= row_i) if d == 0 else (col_i >= row_i)
    lane = lax.broadcasted_iota(jnp.int32, (L, LANES), 1)
    eye = (lax.broadcasted_iota(jnp.int32, (LANES, LANES), 0)
           == lax.broadcasted_iota(jnp.int32, (LANES, LANES), 1)).astype(F32)

    g = aux_ref[rows, :]
    lf = _log_sigmoid(g)
    bc = jnp.dot(mask.astype(F32), lf, precision=lax.Precision.HIGHEST,
                 preferred_element_type=F32)
    ia = pltpu.roll(g, 4, 1)
    rvec = ia - bc
    total = jnp.sum(lf, axis=0, keepdims=True)
    mm = m_ref[d]
    gvec = total + rvec
    m_new = jnp.maximum(total + mm, jnp.max(gvec, axis=0, keepdims=True))
    ws = jnp.exp(gvec - m_new)
    cdec = jnp.exp(total + mm - m_new)
    m_inter = bc + mm
    r_t = lax.dot_general(eye, rvec, NT, precision=lax.Precision.HIGHEST,
                          preferred_element_type=F32)
    m_ref[d] = m_new

    for h in range(ML_HEADS):
        r = LANE_F[d] + h
        pair, hh = h // 2, h % 2
        ps = slice(pair * LANES, (pair + 1) * LANES)
        qh = jnp.where((lane >> 6) == hh, mq_ref[rows, ps].astype(F32), 0.0).astype(BF16)
        kp = mk_ref[rows, ps]
        vh = mv_ref[rows, h * ML_DV:(h + 1) * ML_DV]
        dm = jnp.where(mask, bc[:, r:r + 1] + r_t[r:r + 1, :], -jnp.inf)
        mi = m_inter[:, r:r + 1]
        mh = jnp.maximum(jnp.max(dm, axis=1, keepdims=True), mi)
        wqk = jnp.exp(dm - mh) * _dot_nt(qh, kp)
        dec = jnp.exp(mi - mh)
        qc = _dot(qh, s_ref[d, pair].astype(BF16))
        num = _dot(wqk.astype(BF16), vh) + dec * qc[:, :ML_DV]
        den = jnp.sum(wqk, axis=1, keepdims=True) + dec * qc[:, ML_DV:ML_DV + 1]
        h_ref[d, rows, h * ML_DV:(h + 1) * ML_DV] = num / jnp.maximum(jnp.abs(den), jnp.exp(-mh))
        wsc = ws[:, r:r + 1]
        vaug = jnp.concatenate([vh.astype(F32) * wsc, jnp.where(lane == 0, wsc, 0.0)], axis=1)
        upd = lax.dot_general(kp, vaug.astype(BF16), TN, preferred_element_type=F32)
        hr = slice(hh * ML_DK, (hh + 1) * ML_DK)
        s_ref[d, pair, hr, :] = cdec[:, r:r + 1] * s_ref[d, pair, hr, :] + upd[hr, :]


def _mlstm_kernel(latent, seq, *refs):
    if latent:
        (mq_ref, mk_ref, mv_ref, mo_ref, aux_ref, g_ref, s0_ref, m0_ref,
         o_ref, s_ref, m_ref, h_ref) = refs
        s_ref[...] = s0_ref[...]
        m_ref[0] = m0_ref[...]
        m_ref[1] = m0_ref[...]
    else:
        (mq_ref, mk_ref, mv_ref, mo_ref, aux_ref, g_ref,
         o_ref, s_ref, mf_ref, m_ref, h_ref) = refs
        s_ref[...] = jnp.zeros(s_ref.shape, F32)
        m_ref[...] = jnp.zeros(m_ref.shape, F32)
    nc = seq // ML_CHUNK

    def body(j, carry):
        _mlstm_chunk(0, j, mq_ref, mk_ref, mv_ref, aux_ref, s_ref, m_ref, h_ref)
        _mlstm_chunk(1, nc - 1 - j, mq_ref, mk_ref, mv_ref, aux_ref, s_ref, m_ref, h_ref)
        return carry

    lax.fori_loop(0, nc, body, 0)

    for h in range(ML_HEADS):
        hs = slice(h * ML_DV, (h + 1) * ML_DV)
        hsum = h_ref[0, :, hs] + h_ref[1, :, hs]
        y = _rms_rows(hsum, g_ref[:, hs], ML_DV)
        o_ref[:, hs] = (jax.nn.sigmoid(mo_ref[:, hs]) * y).astype(BF16)
    if not latent:
        lane = lax.broadcasted_iota(jnp.int32, (1, LANES), 1)
        mf_ref[...] = jnp.where(lane < LANE_F[1], m_ref[0], m_ref[1])


def _mlstm_call(latent, l, mq, mk, mv, mo, aux, wts, s0=None, m0=None):
    b, s, _ = mq.shape

    def tok(w):
        return pl.BlockSpec((None, s, w), lambda i: (i, 0, 0))

    in_specs = [tok(256), tok(256), tok(512), tok(512), tok(LANES),
                pl.BlockSpec((None, 1, 512), lambda i: (l, 0, 0))]
    args = [mq, mk, mv, mo, aux, wts["g_mnorm"]]
    s_spec_shape = (2, 2, LANES, 256)
    scratch = [pltpu.VMEM((2, 1, LANES), F32), pltpu.VMEM((2, s, ML_HEADS * ML_DV), F32)]
    if latent:
        in_specs += [pl.BlockSpec((None, None) + s_spec_shape, lambda i: (i, l, 0, 0, 0, 0)),
                     pl.BlockSpec((None, None, 1, LANES), lambda i: (i, l, 0, 0))]
        args += [s0, m0]
        out_shape = jax.ShapeDtypeStruct((b, s, 512), BF16)
        out_specs = tok(512)
        scratch = [pltpu.VMEM(s_spec_shape, F32)] + scratch
    else:
        out_shape = (jax.ShapeDtypeStruct((b, s, 512), BF16),
                     jax.ShapeDtypeStruct((b,) + s_spec_shape, F32),
                     jax.ShapeDtypeStruct((b, 1, LANES), F32))
        out_specs = (tok(512),
                     pl.BlockSpec((None,) + s_spec_shape, lambda i: (i, 0, 0, 0, 0)),
                     pl.BlockSpec((None, 1, LANES), lambda i: (i, 0, 0)))
    return pl.pallas_call(
        functools.partial(_mlstm_kernel, latent, s),
        out_shape=out_shape, grid=(b,), in_specs=in_specs, out_specs=out_specs,
        scratch_shapes=scratch,
        compiler_params=_cparams(("arbitrary",)),
        name="mlstm_lat" if latent else "mlstm_ctx",
    )(*args)


def _post_kernel(seg, final, *refs):
    (x_ref, oa_ref, ob_ref, oc_ref, wo_a, wo_b, wo_c, gt1_ref, sh2_ref, sc2_ref, gt2_ref, g2_ref,
     wuv_ref, wug_ref, cwv_ref, cwg_ref, cbv_ref, cbg_ref, wd_ref, gf_ref,
     o_ref, x1_ref, h2_ref, acc_ref) = refs
    k = pl.program_id(1)

    @pl.when(k == 0)
    def _():
        mix = (_dot(oa_ref[...], wo_a[...]) + _dot(ob_ref[...], wo_b[...])
               + _dot(oc_ref[...], wo_c[...]))
        x1 = x_ref[...] + gt1_ref[...] * mix
        x1_ref[...] = x1
        h2 = _rms_rows(x1, g2_ref[...], D_MODEL) * (1.0 + sc2_ref[...]) + sh2_ref[...]
        h2_ref[...] = h2.astype(BF16)
        acc_ref[...] = jnp.zeros(acc_ref.shape, F32)

    row = lax.broadcasted_iota(jnp.int32, (TM_FFN, FC), 0) & (seg - 1)
    first = row == 0
    last = row == seg - 1

    def conv(u, cw_ref, cb_ref):
        prev = jnp.where(first, 0.0, pltpu.roll(u, 1, 0))
        nxt = jnp.where(last, 0.0, pltpu.roll(u, TM_FFN - 1, 0))
        return cb_ref[...] + prev * cw_ref[0:1, :] + u * cw_ref[1:2, :] + nxt * cw_ref[2:3, :]

    h2 = h2_ref[...]
    val = conv(_dot(h2, wuv_ref[...]), cwv_ref, cbv_ref)
    gate = conv(_dot(h2, wug_ref[...]), cwg_ref, cbg_ref)
    act = (gate * jax.nn.sigmoid(gate) * val).astype(BF16)
    acc_ref[...] += _dot(act, wd_ref[...])

    @pl.when(k == N_FC - 1)
    def _():
        x2 = x1_ref[...] + gt2_ref[...] * acc_ref[...]
        if final:
            x2 = _rms_rows(x2, gf_ref[...], D_MODEL)
        o_ref[...] = x2


def _post_call(latent, l, final, x, oa, ob, oc, mods, wts):
    b, s, _ = x.shape
    n = b * s
    x2d = x.reshape(n, D_MODEL)
    nt = n // TM_FFN
    tiles_per_batch = s // TM_FFN if latent else 1
    seg = GRID_W if latent else s

    def tok(w):
        return pl.BlockSpec((TM_FFN, w), lambda i, k: (i, 0))

    def mod(chunk):
        if latent:
            return pl.BlockSpec((None, 1, D_MODEL),
                                lambda i, k: (l * 48 + (1 + i // tiles_per_batch) * 6 + chunk, 0, 0))
        return pl.BlockSpec((None, 1, D_MODEL), lambda i, k: (l * 48 + chunk, 0, 0))

    in_specs = [
        tok(D_MODEL), tok(256), tok(256), tok(512),
        pl.BlockSpec((None, 256, D_MODEL), lambda i, k: (l, 0, 0)),
        pl.BlockSpec((None, 256, D_MODEL), lambda i, k: (l, 1, 0)),
        pl.BlockSpec((None, 512, D_MODEL), lambda i, k: (l, 1, 0)),
        mod(2), mod(3), mod(4), mod(5),
        pl.BlockSpec((None, 1, D_MODEL), lambda i, k: (l, 0, 0)),
        pl.BlockSpec((None, D_MODEL, FC), lambda i, k: (l, 0, k)),
        pl.BlockSpec((None, D_MODEL, FC), lambda i, k: (l, 0, N_FC + k)),
        pl.BlockSpec((None, CONV_W, FC), lambda i, k: (l, 0, k)),
        pl.BlockSpec((None, CONV_W, FC), lambda i, k: (l, 0, N_FC + k)),
        pl.BlockSpec((None, 1, FC), lambda i, k: (l, 0, k)),
        pl.BlockSpec((None, 1, FC), lambda i, k: (l, 0, N_FC + k)),
        pl.BlockSpec((None, FC, D_MODEL), lambda i, k: (l, k, 0)),
        pl.BlockSpec((1, D_MODEL), lambda i, k: (0, 0)),
    ]
    args = [x2d, oa.reshape(n, 256), ob.reshape(n, 256), oc.reshape(n, 512),
            wts["w_out"], wts["w_out"], wts["w_out"], mods, mods, mods, mods, wts["g_norm2"],
            wts["w_up"], wts["w_up"], wts["conv_w"], wts["conv_w"], wts["conv_b"], wts["conv_b"],
            wts["w_down"], wts["g_final"]]
    out = pl.pallas_call(
        functools.partial(_post_kernel, seg, final),
        out_shape=jax.ShapeDtypeStruct((n, D_MODEL), F32),
        grid=(nt, N_FC), in_specs=in_specs,
        out_specs=pl.BlockSpec((TM_FFN, D_MODEL), lambda i, k: (i, 0)),
        scratch_shapes=[pltpu.VMEM((TM_FFN, D_MODEL), F32), pltpu.VMEM((TM_FFN, D_MODEL), BF16),
                        pltpu.VMEM((TM_FFN, D_MODEL), F32)],
        compiler_params=_cparams(("arbitrary", "arbitrary")),
        name="post_lat" if latent else "post_ctx",
    )(*args)
    return out.reshape(b, s, D_MODEL)


def _pack_weights(w_in, g_cq, w_uq, g_ckv, w_ukv, b_gate, g_subln, g_mnorm, g_norm1, g_norm2,
                  w_out, w_up, conv_w, conv_b, w_down, g_final, lam_q1, lam_k1, lam_q2, lam_k2):
    def cols(a, lo, n, pad=0):
        blk = a[..., lo:lo + n]
        if pad:
            blk = jnp.pad(blk, [(0, 0)] * (a.ndim - 1) + [(0, pad)])
        return blk

    o = 0
    parts = {}
    for name, n in (("cq", 192), ("ckv", 128), ("kr", 32), ("dq", 256), ("dk", 256), ("dv", 256),
                    ("mq", 256), ("mk", 256), ("mv", 512), ("mo", 512), ("mg", 16)):
        parts[name] = (o, n)
        o += n
    w_in_p = jnp.concatenate([
        cols(w_in, *parts["cq"], pad=64), cols(w_in, *parts["ckv"]),
        cols(w_in, *parts["kr"]), cols(w_in, *parts["mg"], pad=80),
        cols(w_in, *parts["dq"]), cols(w_in, *parts["dk"]), cols(w_in, *parts["dv"]),
        cols(w_in, *parts["mq"]), cols(w_in, *parts["mk"]), cols(w_in, *parts["mv"]),
        cols(w_in, *parts["mo"])], axis=-1).astype(BF16)

    hd = MLA_NOPE + MLA_ROPE
    w_uq_p = jnp.pad(w_uq.reshape(DEPTH, MLA_Q_RANK, MLA_HEADS, hd),
                     [(0, 0), (0, 256 - MLA_Q_RANK), (0, 0), (0, LANES - hd)])
    w_uq_p = w_uq_p.reshape(DEPTH, 256, MLA_HEADS * LANES).astype(BF16)

    w_ukv4 = w_ukv.reshape(DEPTH, MLA_KV_RANK, MLA_HEADS, MLA_NOPE + MLA_V)
    w_k = jnp.pad(w_ukv4[..., :MLA_NOPE], [(0, 0), (0, 0), (0, 0), (0, LANES - MLA_NOPE)])
    w_k = w_k.reshape(DEPTH, MLA_KV_RANK, MLA_HEADS * LANES)
    j = jnp.arange(LANES)[:, None]
    cix = jnp.arange(MLA_HEADS * LANES)[None, :]
    place = ((j < MLA_ROPE) & ((cix % LANES) == MLA_NOPE + j)).astype(F32)
    w_kk = jnp.concatenate([w_k, jnp.broadcast_to(place, (DEPTH, LANES, MLA_HEADS * LANES))],
                           axis=1).astype(BF16)
    w_v = w_ukv4[..., MLA_NOPE:].reshape(DEPTH, MLA_KV_RANK, MLA_HEADS * MLA_V).astype(BF16)

    return dict(
        w_in=w_in_p, w_uq=w_uq_p, w_kk=w_kk, w_v=w_v,
        g_norm1=g_norm1[:, None, :], g_norm2=g_norm2[:, None, :],
        g_cq=jnp.pad(g_cq, [(0, 0), (0, 256 - MLA_Q_RANK)])[:, None, :],
        g_ckv=g_ckv[:, None, :],
        b_gate=jnp.pad(b_gate, [(0, 0), (AUX_GATE, LANES - AUX_GATE - 16)])[:, None, :],
        g_subln=jnp.tile(g_subln, (1, DF_HEADS))[:, None, :],
        g_mnorm=g_mnorm[:, None, :],
        w_out=w_out.astype(BF16), w_up=w_up.astype(BF16), w_down=w_down.astype(BF16),
        conv_w=conv_w, conv_b=conv_b[:, None, :], g_final=g_final[None, :],
        lam_q1=lam_q1[:, None, :], lam_k1=lam_k1[:, None, :],
        lam_q2=lam_q2[:, None, :], lam_k2=lam_k2[:, None, :],
    )


def _rope_tables(n_tok):
    t = jnp.arange(n_tok)
    row = (t // GRID_W).astype(F32)
    col = (t % GRID_W).astype(F32)
    nf = MLA_ROPE // 4
    inv = ROPE_BASE ** (-jnp.arange(nf, dtype=F32) / nf)
    ar = row[:, None] * inv[None, :]
    ac = col[:, None] * inv[None, :]
    ang = jnp.concatenate([ar, ar, ac, ac], axis=-1)
    cos, sin = jnp.cos(ang), jnp.sin(ang)
    quarter = (jnp.arange(MLA_ROPE) // nf) % 2
    sin_up = jnp.where(quarter == 0, -sin, 0.0)
    sin_dn = jnp.where(quarter == 1, sin, 0.0)
    ones = jnp.ones((n_tok, 1), F32)
    zeros = jnp.zeros((n_tok, 1), F32)

    def head_q(t32, fill):
        blk = jnp.concatenate([jnp.tile(fill, (1, MLA_NOPE)), t32, jnp.tile(fill, (1, 32))], axis=1)
        return jnp.tile(blk, (1, MLA_HEADS))

    def aux_k(t32, fill):
        return jnp.concatenate([t32, jnp.tile(fill, (1, LANES - MLA_ROPE))], axis=1)

    tq = (head_q(cos, ones), head_q(sin_up, zeros), head_q(sin_dn, zeros))
    td = tuple(jnp.tile(a, (1, 256 // DF_DIM)) for a in (cos, sin_up, sin_dn))
    tk = (aux_k(cos, ones), aux_k(sin_up, zeros), aux_k(sin_dn, zeros))
    return tq + td + tk


def kernel(x_prompt, x_sample, cache_mla_ckv, cache_mla_krope, cache_diff_k, cache_diff_v,
           state_mlstm_C, state_mlstm_n, state_mlstm_m, c, c_ctx, w_ada, b_ada, g_norm1, w_in,
           g_cq, w_uq, g_ckv, w_ukv, lam_q1, lam_k1, lam_q2, lam_k2, g_subln, b_gate, g_mnorm,
           w_out, g_norm2, w_up, conv_w, conv_b, w_down, g_final):
    bp, sp, _ = x_prompt.shape
    bl, sl, _ = x_sample.shape
    t_len = cache_mla_ckv.shape[2]

    wts = _pack_weights(w_in, g_cq, w_uq, g_ckv, w_ukv, b_gate, g_subln, g_mnorm, g_norm1, g_norm2,
                        w_out, w_up, conv_w, conv_b, w_down, g_final, lam_q1, lam_k1, lam_q2, lam_k2)
    tables = _rope_tables(sl)

    cond = jnp.concatenate([c_ctx[None, :], c, jnp.zeros((8 - 1 - bl, D_MODEL), F32)], axis=0)
    mods = _ada_call(cond, w_ada, b_ada[:, None, :])
    mods = mods.reshape(DEPTH * 8 * N_MOD, 1, D_MODEL)

    kr_pad = jnp.pad(cache_mla_krope, [(0, 0), (0, 0), (0, 0), (0, LANES - MLA_ROPE)])
    kctx, vctx = _ctxkv_call(cache_mla_ckv, kr_pad, wts["w_kk"], wts["w_v"])
    cdk = cache_diff_k.reshape(bl, DEPTH, t_len, 256)
    cdv = cache_diff_v.reshape(bl, DEPTH, t_len, 256)
    s0 = jnp.concatenate([state_mlstm_C, state_mlstm_n[..., None],
                          jnp.zeros(state_mlstm_n.shape + (LANES - 1,), F32)], axis=-1)
    s0 = s0.reshape(bl, DEPTH, 2, 2, LANES, 256)
    m0 = jnp.zeros((bl, DEPTH, 1, LANES), F32)
    m0 = m0.at[..., 0, LANE_F[0]:LANE_F[0] + ML_HEADS].set(state_mlstm_m[:, :, 0])
    m0 = m0.at[..., 0, LANE_F[1]:LANE_F[1] + ML_HEADS].set(state_mlstm_m[:, :, 1])

    xp, xs = x_prompt, x_sample
    col = [[] for _ in range(7)]
    for l in range(DEPTH):
        lam_init = 0.8 - 0.6 * math.exp(-0.3 * l)
        final = l == DEPTH - 1
        (q, k, v, dq, dk, dv, mq, mk, mv, mo, aux, ckv_s, kr_s, dk_s, dv_s) = _pre_call(
            False, l, xp, mods, wts, None)
        o_a = _mla_call(False, l, q, k, v)
        o_b = _diff_call(False, l, lam_init, dq, dk, dv, wts)
        o_c, s_fin, m_fin = _mlstm_call(False, l, mq, mk, mv, mo, aux, wts)
        xp = _post_call(False, l, final, xp, o_a, o_b, o_c, mods, wts)
        col[0].append(ckv_s)
        col[1].append(kr_s)
        col[2].append(dk_s.reshape(bp, sp, DF_HEADS, 2 * DF_DIM))
        col[3].append(dv_s.reshape(bp, sp, DF_HEADS, 2 * DF_DIM))
        s_fin = s_fin.reshape(bp, 2, ML_HEADS, ML_DK, 256)
        col[4].append(s_fin[..., :ML_DV])
        col[5].append(s_fin[..., ML_DV])
        col[6].append(jnp.stack([m_fin[:, 0, LANE_F[0]:LANE_F[0] + ML_HEADS],
                                 m_fin[:, 0, LANE_F[1]:LANE_F[1] + ML_HEADS]], axis=1))
        (q, k, v, dq, dk, dv, mq, mk, mv, mo, aux) = _pre_call(True, l, xs, mods, wts, tables)
        o_a = _mla_call(True, l, q, k, v, kctx, vctx)
        o_b = _diff_call(True, l, lam_init, dq, dk, dv, wts, cdk, cdv)
        o_c = _mlstm_call(True, l, mq, mk, mv, mo, aux, wts, s0, m0)
        xs = _post_call(True, l, final, xs, o_a, o_b, o_c, mods, wts)

    return (xp, xs) + tuple(jnp.stack(a, axis=1) for a in col)
```

```python
import functools
import math

import jax
import jax.numpy as jnp
from jax import lax
from jax.experimental import pallas as pl
from jax.experimental.pallas import tpu as pltpu

F32 = jnp.float32
BF16 = jnp.bfloat16

D_MODEL = 1024
DEPTH = 4
GRID_W = 64
N_MOD = 6
EPS = 1e-6
ROPE_BASE = 10000.0
MLA_HEADS = 4
MLA_Q_RANK = 192
MLA_KV_RANK = 128
MLA_NOPE = 64
MLA_ROPE = 32
MLA_V = 64
DF_HEADS = 4
DF_DIM = 32
ML_HEADS = 4
ML_DK = 64
ML_DV = 128
D_FF = 2816
CONV_W = 3

LANES = 128
VMEM_LIMIT = 48 * 1024 * 1024
VMEM_LIMIT_FFN = 56 * 1024 * 1024

C_CQ, C_CKV, C_AUX, C_DQ, C_DK, C_DV, C_MQ, C_MK, C_MV, C_MO = (
    0, 256, 384, 512, 768, 1024, 1280, 1536, 1792, 2304)
NP_IN = 2816
AUX_GATE = 32

TM_PRE = 256
TQ = 256
ML_CHUNK = 128
TM_FFN = 512
FC = 256
N_FC = D_FF // FC

NT = (((1,), (1,)), ((), ()))
TN = (((0,), (0,)), ((), ()))


def _cparams(sem):
    return pltpu.CompilerParams(dimension_semantics=sem, vmem_limit_bytes=VMEM_LIMIT)


def _dot(a, b):
    return jnp.dot(a, b, preferred_element_type=F32)


def _dot_nt(a, b):
    return lax.dot_general(a, b, NT, preferred_element_type=F32)


def _rms_rows(x, g, n):
    ms = jnp.sum(x * x, axis=-1, keepdims=True) * (1.0 / n)
    return x * lax.rsqrt(ms + EPS) * g


def _rope(x, cos, sin_up, sin_dn):
    w = x.shape[-1]
    return x * cos + pltpu.roll(x, w - 8, 1) * sin_up + pltpu.roll(x, 8, 1) * sin_dn


def _ada_kernel(c_ref, w_ref, b_ref, o_ref):
    c = c_ref[...]
    s = (c * jax.nn.sigmoid(c)).astype(BF16)
    o_ref[...] = _dot(s, w_ref[...].astype(BF16)) + b_ref[...]


def _ada_call(cond, w_ada, b_ada):
    nt = 1024
    return pl.pallas_call(
        _ada_kernel,
        out_shape=jax.ShapeDtypeStruct((DEPTH, 8, N_MOD * D_MODEL), F32),
        grid=(DEPTH, N_MOD * D_MODEL // nt),
        in_specs=[pl.BlockSpec((8, D_MODEL), lambda l, j: (0, 0)),
                  pl.BlockSpec((None, D_MODEL, nt), lambda l, j: (l, 0, j)),
                  pl.BlockSpec((None, 1, nt), lambda l, j: (l, 0, j))],
        out_specs=pl.BlockSpec((None, 8, nt), lambda l, j: (l, 0, j)),
        compiler_params=_cparams(("arbitrary", "arbitrary")),
        name="ada_mod",
    )(cond, w_ada, b_ada)


def _ctxkv_kernel(ckv_ref, kr_ref, wkk_ref, wv_ref, k_ref, v_ref):
    ckv = ckv_ref[...].astype(BF16)
    kin = jnp.concatenate([ckv, kr_ref[...].astype(BF16)], axis=1)
    k_ref[...] = _dot(kin, wkk_ref[...]).astype(BF16)
    v_ref[...] = _dot(ckv, wv_ref[...]).astype(BF16)


def _ctxkv_call(cache_ckv, cache_kr_pad, wkk, wv):
    b, _, t, _ = cache_ckv.shape
    return pl.pallas_call(
        _ctxkv_kernel,
        out_shape=(jax.ShapeDtypeStruct((DEPTH, b, t, 512), BF16),
                   jax.ShapeDtypeStruct((DEPTH, b, t, 256), BF16)),
        grid=(DEPTH, b),
        in_specs=[pl.BlockSpec((None, None, t, MLA_KV_RANK), lambda l, i: (i, l, 0, 0)),
                  pl.BlockSpec((None, None, t, LANES), lambda l, i: (i, l, 0, 0)),
                  pl.BlockSpec((None, 256, 512), lambda l, i: (l, 0, 0)),
                  pl.BlockSpec((None, MLA_KV_RANK, 256), lambda l, i: (l, 0, 0))],
        out_specs=(pl.BlockSpec((None, None, t, 512), lambda l, i: (l, i, 0, 0)),
                   pl.BlockSpec((None, None, t, 256), lambda l, i: (l, i, 0, 0))),
        compiler_params=_cparams(("arbitrary", "arbitrary")),
        name="ctx_kv",
    )(cache_ckv, cache_kr_pad, wkk, wv)


def _pre_kernel(latent, *refs):
    (x_ref, sh_ref, sc_ref, g1_ref, win_ref, gcq_ref, wuq_ref, gckv_ref, wkk_ref, wv_ref,
     bg_ref) = refs[:11]
    refs = refs[11:]
    if latent:
        (cq_t, sqa_t, sqb_t, cd_t, sda_t, sdb_t, ck_t, ska_t, skb_t) = refs[:9]
        refs = refs[9:]
    (q_ref, k_ref, v_ref, dq_ref, dk_ref, dv_ref, mq_ref, mkt_ref, mv_ref, mo_ref,
     gt_ref) = refs[:11]
    refs = refs[11:]
    if not latent:
        ckv_out, kr_out, dk_out, dv_out = refs

    x = x_ref[...]
    h = _rms_rows(x, g1_ref[...], D_MODEL) * (1.0 + sc_ref[...]) + sh_ref[...]
    proj = _dot(h.astype(BF16), win_ref[...])

    cq = _rms_rows(proj[:, C_CQ:C_CQ + 256], gcq_ref[...], MLA_Q_RANK)
    q = _dot(cq.astype(BF16), wuq_ref[...])
    if latent:
        q = _rope(q, cq_t[...], sqa_t[...], sqb_t[...])
    q_ref[...] = (q * ((MLA_NOPE + MLA_ROPE) ** -0.5)).astype(BF16)

    c_kv = _rms_rows(proj[:, C_CKV:C_CKV + MLA_KV_RANK], gckv_ref[...], MLA_KV_RANK)
    aux = proj[:, C_AUX:C_AUX + LANES] + bg_ref[...]
    if not latent:
        ckv_out[...] = c_kv
        kr_out[...] = aux[:, :MLA_ROPE]
    else:
        aux = _rope(aux, ck_t[...], ska_t[...], skb_t[...])
    gt_ref[...] = aux.T[AUX_GATE:AUX_GATE + 32, :]
    ckv_b = c_kv.astype(BF16)
    kin = jnp.concatenate([ckv_b, aux.astype(BF16)], axis=1)
    k_ref[...] = _dot(kin, wkk_ref[...]).astype(BF16)
    v_ref[...] = _dot(ckv_b, wv_ref[...]).astype(BF16)

    dq = proj[:, C_DQ:C_DQ + 256]
    dk = proj[:, C_DK:C_DK + 256]
    dv = proj[:, C_DV:C_DV + 256]
    if not latent:
        dk_out[...] = dk
        dv_out[...] = dv
    else:
        dq = _rope(dq, cd_t[...], sda_t[...], sdb_t[...])
        dk = _rope(dk, cd_t[...], sda_t[...], sdb_t[...])
    dq_ref[...] = (dq * (DF_DIM ** -0.5)).astype(BF16)
    dk_ref[...] = dk.astype(BF16)
    dv_ref[...] = dv.astype(BF16)

    lane = lax.broadcasted_iota(jnp.int32, (x.shape[0], LANES), 1)
    for h in range(ML_HEADS):
        blk = proj[:, C_MQ + (h // 2) * LANES:C_MQ + (h // 2 + 1) * LANES]
        mq_ref[:, h * LANES:(h + 1) * LANES] = jnp.where((lane >> 6) == h % 2, blk, 0.0).astype(BF16)
    mkt_ref[...] = (proj[:, C_MK:C_MK + 256] * (ML_DK ** -0.5)).T.astype(BF16)
    mv_ref[...] = proj[:, C_MV:C_MV + 512].astype(BF16)
    mo_ref[...] = proj[:, C_MO:C_MO + 512]


def _pre_call(latent, l, x, mods, wts, tables):
    b, s, _ = x.shape
    ns = s // TM_PRE
    grid = (ns, b)

    def tok(width):
        return pl.BlockSpec((None, TM_PRE, width), lambda j, i: (i, j, 0))

    def mod(chunk):
        if latent:
            return pl.BlockSpec((None, 1, D_MODEL), lambda j, i: (l * 48 + (1 + i) * 6 + chunk, 0, 0))
        return pl.BlockSpec((None, 1, D_MODEL), lambda j, i: (l * 48 + chunk, 0, 0))

    def lw(*shape):
        nd = len(shape)
        return pl.BlockSpec((None,) + shape, lambda j, i: (l,) + (0,) * nd)

    in_specs = [tok(D_MODEL), mod(0), mod(1), lw(1, D_MODEL), lw(D_MODEL, NP_IN), lw(1, 256),
                lw(256, 512), lw(1, MLA_KV_RANK), lw(256, 512), lw(MLA_KV_RANK, 256), lw(1, LANES)]
    args = [x, mods, mods, wts["g_norm1"], wts["w_in"], wts["g_cq"], wts["w_uq"], wts["g_ckv"],
            wts["w_kk"], wts["w_v"], wts["b_gate"]]
    if latent:
        for t in tables:
            in_specs.append(pl.BlockSpec((TM_PRE, t.shape[1]), lambda j, i: (j, 0)))
            args.append(t)

    widths = [(512, BF16, False), (512, BF16, False), (256, BF16, False), (256, BF16, False),
              (256, BF16, False), (256, BF16, False), (512, BF16, False), (256, BF16, True),
              (512, BF16, False), (512, F32, False), (32, F32, True)]
    if not latent:
        widths += [(MLA_KV_RANK, F32, False), (MLA_ROPE, F32, False), (256, F32, False),
                   (256, F32, False)]
    out_shape = tuple(jax.ShapeDtypeStruct((b, w, s) if tr else (b, s, w), dt)
                      for w, dt, tr in widths)
    out_specs = tuple(pl.BlockSpec((None, w, TM_PRE), lambda j, i: (i, 0, j)) if tr else tok(w)
                      for w, _, tr in widths)
    return pl.pallas_call(
        functools.partial(_pre_kernel, latent),
        out_shape=out_shape, grid=grid, in_specs=in_specs, out_specs=out_specs,
        compiler_params=_cparams(("arbitrary", "arbitrary")),
        name="pre_lat" if latent else "pre_ctx",
    )(*args)


def _softmax_parts(s_list):
    m = functools.reduce(jnp.maximum, [jnp.max(s, axis=1, keepdims=True) for s in s_list])
    p_list = [jnp.exp(s - m) for s in s_list]
    l = functools.reduce(jnp.add, [jnp.sum(p, axis=1, keepdims=True) for p in p_list])
    return p_list, l


def _mla_kernel(latent, *refs):
    if latent:
        q_ref, k_ref, v_ref, kc_ref, vc_ref, o_ref = refs
        segs = [(kc_ref, vc_ref), (k_ref, v_ref)]
    else:
        q_ref, k_ref, v_ref, o_ref = refs
        segs = [(k_ref, v_ref)]
    lane = lax.broadcasted_iota(jnp.int32, (TQ, LANES), 1)
    outs = []
    for h in range(MLA_HEADS):
        hs = slice(h * LANES, (h + 1) * LANES)
        ps = slice((h // 2) * LANES, (h // 2 + 1) * LANES)
        qh = q_ref[:, hs]
        p_list, l = _softmax_parts([_dot_nt(qh, kr[:, hs]) for kr, _ in segs])
        pv = functools.reduce(jnp.add, [_dot(p.astype(BF16), vr[:, ps])
                                        for p, (_, vr) in zip(p_list, segs)])
        outs.append(pv / l)
    o_ref[:, 0:LANES] = jnp.where(lane < MLA_V, outs[0], outs[1]).astype(BF16)
    o_ref[:, LANES:2 * LANES] = jnp.where(lane < MLA_V, outs[2], outs[3]).astype(BF16)


def _mla_call(latent, l, q, k, v, kctx=None, vctx=None):
    b, s, _ = q.shape
    grid = (b, s // TQ)
    in_specs = [pl.BlockSpec((None, TQ, 512), lambda i, j: (i, j, 0)),
                pl.BlockSpec((None, s, 512), lambda i, j: (i, 0, 0)),
                pl.BlockSpec((None, s, 256), lambda i, j: (i, 0, 0))]
    args = [q, k, v]
    if latent:
        t = kctx.shape[2]
        in_specs += [pl.BlockSpec((None, None, t, 512), lambda i, j: (l, i, 0, 0)),
                     pl.BlockSpec((None, None, t, 256), lambda i, j: (l, i, 0, 0))]
        args += [kctx, vctx]
    return pl.pallas_call(
        functools.partial(_mla_kernel, latent),
        out_shape=jax.ShapeDtypeStruct((b, s, 256), BF16),
        grid=grid, in_specs=in_specs,
        out_specs=pl.BlockSpec((None, TQ, 256), lambda i, j: (i, j, 0)),
        compiler_params=_cparams(("arbitrary", "arbitrary")),
        name="mla_lat" if latent else "mla_ctx",
    )(*args)


def _diff_kernel(latent, lam_init, *refs):
    if latent:
        (q_ref, k_ref, v_ref, kc_ref, vc_ref, lq1, lk1, lq2, lk2, g_ref, o_ref) = refs
        segs = [(kc_ref, vc_ref), (k_ref, v_ref)]
    else:
        (q_ref, k_ref, v_ref, lq1, lk1, lq2, lk2, g_ref, o_ref) = refs
        segs = [(k_ref, v_ref)]
    lam = (jnp.exp(jnp.sum(lq1[...] * lk1[...], axis=1, keepdims=True))
           - jnp.exp(jnp.sum(lq2[...] * lk2[...], axis=1, keepdims=True)) + lam_init)
    lane = lax.broadcasted_iota(jnp.int32, (TQ, LANES), 1)
    grp = lane >> 5
    qf = q_ref[...].astype(F32)
    outs = []
    for h in range(DF_HEADS):
        ps = slice((h // 2) * LANES, (h // 2 + 1) * LANES)
        hh = h % 2
        qblk = qf[:, ps]
        parts = []
        for c in range(2):
            qm = jnp.where(grp == 2 * hh + c, qblk, 0.0).astype(BF16)
            parts.append(_softmax_parts([_dot_nt(qm, kr[:, ps].astype(BF16)) for kr, _ in segs]))
        (p0, l0), (p1, l1) = parts
        inv0 = 1.0 / l0
        inv1 = lam / l1
        pv = functools.reduce(jnp.add, [
            _dot((a0 * inv0 - a1 * inv1).astype(BF16), vr[:, ps].astype(BF16))
            for a0, a1, (_, vr) in zip(p0, p1, segs)])
        valid = (lane >> 6) == hh
        ms = jnp.sum(jnp.where(valid, pv * pv, 0.0), axis=1, keepdims=True) * (1.0 / (2 * DF_DIM))
        outs.append(pv * lax.rsqrt(ms + EPS) * g_ref[:, ps] * (1.0 - lam_init))
    o_ref[:, 0:LANES] = jnp.where(lane < 2 * DF_DIM, outs[0], outs[1]).astype(BF16)
    o_ref[:, LANES:2 * LANES] = jnp.where(lane < 2 * DF_DIM, outs[2], outs[3]).astype(BF16)


def _diff_call(latent, l, lam_init, dq, dk, dv, wts, cdk=None, cdv=None):
    b, s, _ = dq.shape
    grid = (b, s // TQ)
    in_specs = [pl.BlockSpec((None, TQ, 256), lambda i, j: (i, j, 0)),
                pl.BlockSpec((None, s, 256), lambda i, j: (i, 0, 0)),
                pl.BlockSpec((None, s, 256), lambda i, j: (i, 0, 0))]
    args = [dq, dk, dv]
    if latent:
        t = cdk.shape[2]
        in_specs += [pl.BlockSpec((None, None, t, 256), lambda i, j: (i, l, 0, 0)),
                     pl.BlockSpec((None, None, t, 256), lambda i, j: (i, l, 0, 0))]
        args += [cdk, cdv]
    for name in ("lam_q1", "lam_k1", "lam_q2", "lam_k2"):
        in_specs.append(pl.BlockSpec((None, 1, DF_DIM), lambda i, j: (l, 0, 0)))
        args.append(wts[name])
    in_specs.append(pl.BlockSpec((None, 1, 256), lambda i, j: (l, 0, 0)))
    args.append(wts["g_subln"])
    return pl.pallas_call(
        functools.partial(_diff_kernel, latent, lam_init),
        out_shape=jax.ShapeDtypeStruct((b, s, 256), BF16),
        grid=grid, in_specs=in_specs,
        out_specs=pl.BlockSpec((None, TQ, 256), lambda i, j: (i, j, 0)),
        compiler_params=_cparams(("arbitrary", "arbitrary")),
        name="diff_lat" if latent else "diff_ctx",
    )(*args)


def _log_sigmoid(x):
    return jnp.minimum(x, 0.0) - jnp.log1p(jnp.exp(-jnp.abs(x)))


def _mlstm_chunk(d, c, mq_ref, mkt_ref, mv_ref, gt_ref, s_ref, m_ref, h_ref):
    L = ML_CHUNK
    rows = pl.ds(pl.multiple_of(c * L, L), L)
    t_i = lax.broadcasted_iota(jnp.int32, (L, L), 0)
    s_i = lax.broadcasted_iota(jnp.int32, (L, L), 1)
    mask = (s_i <= t_i) if d == 0 else (s_i >= t_i)
    tri = jnp.where((t_i <= s_i) if d == 0 else (t_i >= s_i), 1.0, 0.0).astype(BF16)

    ig = gt_ref[16 * d:16 * d + 8, rows]
    lf = _log_sigmoid(gt_ref[16 * d + 8:16 * d + 16, rows])
    hi = lf.astype(BF16).astype(F32)
    r1 = lf - hi
    mid = r1.astype(BF16).astype(F32)
    parts = _dot(jnp.concatenate([hi, mid, r1 - mid], axis=0).astype(BF16), tri)
    bc = parts[0:8] + parts[8:16] + parts[16:24]
    rvec = ig - bc
    total = jnp.sum(lf, axis=1, keepdims=True)
    mm = m_ref[d]
    gvec = total + rvec
    m_new = jnp.maximum(total + mm, jnp.max(gvec, axis=1, keepdims=True))
    ws = jnp.exp(gvec - m_new).astype(BF16)
    cdec = jnp.exp(total + mm - m_new)
    m_ref[d] = m_new

    s_old = [s_ref[d, pair].astype(BF16) for pair in range(2)]
    ones = jnp.ones((L, LANES), BF16)
    upd = []
    for h in range(ML_HEADS):
        pair = h // 2
        hs = slice(h * ML_DV, (h + 1) * ML_DV)
        qh = mq_ref[rows, hs]
        ktp = mkt_ref[pair * LANES:(pair + 1) * LANES, rows]
        vh = mv_ref[rows, hs]
        mmh = mm[h:h + 1, 0:1]
        rm = jnp.where(mask, rvec[h:h + 1, :], -jnp.inf)
        a = jnp.maximum(jnp.max(rm, axis=1, keepdims=True), mmh)
        bcc = jnp.sum(jnp.where(mask, lf[h:h + 1, :], 0.0), axis=1, keepdims=True)
        wqk = jnp.exp(rm - a) * _dot(qh, ktp)
        dec = jnp.exp(mmh - a)
        qc = _dot(qh, s_old[pair])
        num = _dot(wqk.astype(BF16), vh) + dec * qc[:, :ML_DV]
        den = jnp.sum(wqk, axis=1, keepdims=True) + dec * qc[:, ML_DV:]
        h_ref[d, rows, hs] = num / jnp.maximum(jnp.abs(den), jnp.exp(-(a + bcc)))
        upd.append(_dot(ktp * ws[h:h + 1, :], jnp.concatenate([vh, ones], axis=1)))

    low = lax.broadcasted_iota(jnp.int32, (LANES, 2 * LANES), 0) < ML_DK
    for pair in range(2):
        h0, h1 = 2 * pair, 2 * pair + 1
        cd = jnp.where(low, cdec[h0:h0 + 1, 0:1], cdec[h1:h1 + 1, 0:1])
        s_ref[d, pair] = cd * s_ref[d, pair] + jnp.where(low, upd[h0], upd[h1])


def _mlstm_kernel(latent, seq, *refs):
    if latent:
        (mq_ref, mkt_ref, mv_ref, mo_ref, gt_ref, g_ref, s0_ref, m0_ref,
         o_ref, s_ref, m_ref, h_ref) = refs
        s_ref[...] = s0_ref[...]
        m_ref[...] = m0_ref[...]
    else:
        (mq_ref, mkt_ref, mv_ref, mo_ref, gt_ref, g_ref,
         o_ref, s_ref, mf_ref, m_ref, h_ref) = refs
        s_ref[...] = jnp.zeros(s_ref.shape, F32)
        m_ref[...] = jnp.zeros(m_ref.shape, F32)
    nc = seq // ML_CHUNK

    def body(j, carry):
        _mlstm_chunk(0, j, mq_ref, mkt_ref, mv_ref, gt_ref, s_ref, m_ref, h_ref)
        _mlstm_chunk(1, nc - 1 - j, mq_ref, mkt_ref, mv_ref, gt_ref, s_ref, m_ref, h_ref)
        return carry

    lax.fori_loop(0, nc, body, 0)

    for h in range(ML_HEADS):
        hs = slice(h * ML_DV, (h + 1) * ML_DV)
        hsum = h_ref[0, :, hs] + h_ref[1, :, hs]
        y = _rms_rows(hsum, g_ref[:, hs], ML_DV)
        o_ref[:, hs] = (jax.nn.sigmoid(mo_ref[:, hs]) * y).astype(BF16)
    if not latent:
        mf_ref[...] = m_ref[...]


def _mlstm_call(latent, l, mq, mkt, mv, mo, gt, wts, s0=None, m0=None):
    assert ML_CHUNK == LANES
    b, s, _ = mq.shape

    def tok(w):
        return pl.BlockSpec((None, s, w), lambda i: (i, 0, 0))

    in_specs = [tok(512), pl.BlockSpec((None, 256, s), lambda i: (i, 0, 0)), tok(512), tok(512),
                pl.BlockSpec((None, 32, s), lambda i: (i, 0, 0)),
                pl.BlockSpec((None, 1, 512), lambda i: (l, 0, 0))]
    args = [mq, mkt, mv, mo, gt, wts["g_mnorm"]]
    s_spec_shape = (2, 2, LANES, 2 * LANES)
    m_spec_shape = (2, 8, LANES)
    scratch = [pltpu.VMEM(m_spec_shape, F32), pltpu.VMEM((2, s, ML_HEADS * ML_DV), F32)]
    if latent:
        in_specs += [pl.BlockSpec((None, None) + s_spec_shape, lambda i: (i, l, 0, 0, 0, 0)),
                     pl.BlockSpec((None, None) + m_spec_shape, lambda i: (i, l, 0, 0, 0))]
        args += [s0, m0]
        out_shape = jax.ShapeDtypeStruct((b, s, 512), BF16)
        out_specs = tok(512)
        scratch = [pltpu.VMEM(s_spec_shape, F32)] + scratch
    else:
        out_shape = (jax.ShapeDtypeStruct((b, s, 512), BF16),
                     jax.ShapeDtypeStruct((b,) + s_spec_shape, F32),
                     jax.ShapeDtypeStruct((b,) + m_spec_shape, F32))
        out_specs = (tok(512),
                     pl.BlockSpec((None,) + s_spec_shape, lambda i: (i, 0, 0, 0, 0)),
                     pl.BlockSpec((None,) + m_spec_shape, lambda i: (i, 0, 0, 0)))
    return pl.pallas_call(
        functools.partial(_mlstm_kernel, latent, s),
        out_shape=out_shape, grid=(b,), in_specs=in_specs, out_specs=out_specs,
        scratch_shapes=scratch,
        compiler_params=_cparams(("arbitrary",)),
        name="mlstm_lat" if latent else "mlstm_ctx",
    )(*args)


def _post_kernel(seg, final, *refs):
    (x_ref, oa_ref, ob_ref, oc_ref, wo_ref, gt1_ref, sh2_ref, sc2_ref, gt2_ref, g2_ref,
     wu_ref, cw_ref, cb_ref, wd_ref, gf_ref, o_ref, x1_ref, h2_ref, act_ref) = refs

    mix = _dot(jnp.concatenate([oa_ref[...], ob_ref[...], oc_ref[...]], axis=1), wo_ref[...])
    x1 = x_ref[...] + gt1_ref[...] * mix
    x1_ref[...] = x1
    h2 = _rms_rows(x1, g2_ref[...], D_MODEL) * (1.0 + sc2_ref[...]) + sh2_ref[...]
    h2_ref[...] = h2.astype(BF16)

    row = lax.broadcasted_iota(jnp.int32, (TM_FFN, FC), 0) & (seg - 1)
    first = row == 0
    last = row == seg - 1

    def conv(u, cs):
        prev = jnp.where(first, 0.0, pltpu.roll(u, 1, 0))
        nxt = jnp.where(last, 0.0, pltpu.roll(u, TM_FFN - 1, 0))
        return (cb_ref[:, cs] + prev * cw_ref[0:1, cs] + u * cw_ref[1:2, cs]
                + nxt * cw_ref[2:3, cs])

    for j in range(N_FC):
        vs = slice(j * FC, (j + 1) * FC)
        gs = slice(D_FF + j * FC, D_FF + (j + 1) * FC)
        val = conv(_dot(h2_ref[...], wu_ref[:, vs]), vs)
        gate = conv(_dot(h2_ref[...], wu_ref[:, gs]), gs)
        act_ref[:, vs] = (gate * jax.nn.sigmoid(gate) * val).astype(BF16)

    x2 = x1_ref[...] + gt2_ref[...] * _dot(act_ref[...], wd_ref[...])
    if final:
        x2 = _rms_rows(x2, gf_ref[...], D_MODEL)
    o_ref[...] = x2


def _post_call(latent, l, final, x, oa, ob, oc, mods, wts):
    b, s, _ = x.shape
    n = b * s
    x2d = x.reshape(n, D_MODEL)
    nt = n // TM_FFN
    tiles_per_batch = s // TM_FFN if latent else 1
    seg = GRID_W if latent else s

    def tok(w):
        return pl.BlockSpec((TM_FFN, w), lambda i: (i, 0))

    def mod(chunk):
        if latent:
            return pl.BlockSpec((None, 1, D_MODEL),
                                lambda i: (l * 48 + (1 + i // tiles_per_batch) * 6 + chunk, 0, 0))
        return pl.BlockSpec((None, 1, D_MODEL), lambda i: (l * 48 + chunk, 0, 0))

    def resident(*shape):
        nd = len(shape)
        return pl.BlockSpec((None,) + shape, lambda i: (l,) + (0,) * nd,
                            pipeline_mode=pl.Buffered(1))

    in_specs = [
        tok(D_MODEL), tok(256), tok(256), tok(512),
        resident(D_MODEL, D_MODEL),
        mod(2), mod(3), mod(4), mod(5),
        resident(1, D_MODEL),
        resident(D_MODEL, 2 * D_FF), resident(CONV_W, 2 * D_FF), resident(1, 2 * D_FF),
        resident(D_FF, D_MODEL),
        pl.BlockSpec((1, D_MODEL), lambda i: (0, 0)),
    ]
    args = [x2d, oa.reshape(n, 256), ob.reshape(n, 256), oc.reshape(n, 512),
            wts["w_out"], mods, mods, mods, mods, wts["g_norm2"],
            wts["w_up"], wts["conv_w"], wts["conv_b"], wts["w_down"], wts["g_final"]]
    out = pl.pallas_call(
        functools.partial(_post_kernel, seg, final),
        out_shape=jax.ShapeDtypeStruct((n, D_MODEL), F32),
        grid=(nt,), in_specs=in_specs,
        out_specs=pl.BlockSpec((TM_FFN, D_MODEL), lambda i: (i, 0)),
        scratch_shapes=[pltpu.VMEM((TM_FFN, D_MODEL), F32), pltpu.VMEM((TM_FFN, D_MODEL), BF16),
                        pltpu.VMEM((TM_FFN, D_FF), BF16)],
        compiler_params=pltpu.CompilerParams(dimension_semantics=("arbitrary",),
                                             vmem_limit_bytes=VMEM_LIMIT_FFN),
        name="post_lat" if latent else "post_ctx",
    )(*args)
    return out.reshape(b, s, D_MODEL)


def _pack_weights(w_in, g_cq, w_uq, g_ckv, w_ukv, b_gate, g_subln, g_mnorm, g_norm1, g_norm2,
                  w_out, w_up, conv_w, conv_b, w_down, g_final, lam_q1, lam_k1, lam_q2, lam_k2):
    def cols(a, lo, n, pad=0):
        blk = a[..., lo:lo + n]
        if pad:
            blk = jnp.pad(blk, [(0, 0)] * (a.ndim - 1) + [(0, pad)])
        return blk

    o = 0
    parts = {}
    for name, n in (("cq", 192), ("ckv", 128), ("kr", 32), ("dq", 256), ("dk", 256), ("dv", 256),
                    ("mq", 256), ("mk", 256), ("mv", 512), ("mo", 512), ("mg", 16)):
        parts[name] = (o, n)
        o += n
    mg_lo = parts["mg"][0]
    gate_cols = [cols(w_in, mg_lo + 4 * g, ML_HEADS, pad=4) for g in range(4)]
    w_in_p = jnp.concatenate([
        cols(w_in, *parts["cq"], pad=64), cols(w_in, *parts["ckv"]),
        cols(w_in, *parts["kr"])] + gate_cols + [jnp.zeros(w_in.shape[:-1] + (64,), F32),
        cols(w_in, *parts["dq"]), cols(w_in, *parts["dk"]), cols(w_in, *parts["dv"]),
        cols(w_in, *parts["mq"]), cols(w_in, *parts["mk"]), cols(w_in, *parts["mv"]),
        cols(w_in, *parts["mo"])], axis=-1).astype(BF16)

    hd = MLA_NOPE + MLA_ROPE
    w_uq_p = jnp.pad(w_uq.reshape(DEPTH, MLA_Q_RANK, MLA_HEADS, hd),
                     [(0, 0), (0, 256 - MLA_Q_RANK), (0, 0), (0, LANES - hd)])
    w_uq_p = w_uq_p.reshape(DEPTH, 256, MLA_HEADS * LANES).astype(BF16)

    w_ukv4 = w_ukv.reshape(DEPTH, MLA_KV_RANK, MLA_HEADS, MLA_NOPE + MLA_V)
    w_k = jnp.pad(w_ukv4[..., :MLA_NOPE], [(0, 0), (0, 0), (0, 0), (0, LANES - MLA_NOPE)])
    w_k = w_k.reshape(DEPTH, MLA_KV_RANK, MLA_HEADS * LANES)
    j = jnp.arange(LANES)[:, None]
    cix = jnp.arange(MLA_HEADS * LANES)[None, :]
    place = ((j < MLA_ROPE) & ((cix % LANES) == MLA_NOPE + j)).astype(F32)
    w_kk = jnp.concatenate([w_k, jnp.broadcast_to(place, (DEPTH, LANES, MLA_HEADS * LANES))],
                           axis=1).astype(BF16)
    w_v = w_ukv4[..., MLA_NOPE:].reshape(DEPTH, MLA_KV_RANK, MLA_HEADS * MLA_V).astype(BF16)

    return dict(
        w_in=w_in_p, w_uq=w_uq_p, w_kk=w_kk, w_v=w_v,
        g_norm1=g_norm1[:, None, :], g_norm2=g_norm2[:, None, :],
        g_cq=jnp.pad(g_cq, [(0, 0), (0, 256 - MLA_Q_RANK)])[:, None, :],
        g_ckv=g_ckv[:, None, :],
        b_gate=jnp.pad(jnp.pad(b_gate.reshape(DEPTH, 4, ML_HEADS), [(0, 0), (0, 0), (0, 4)])
                       .reshape(DEPTH, 32), [(0, 0), (AUX_GATE, LANES - AUX_GATE - 32)])[:, None, :],
        g_subln=jnp.tile(g_subln, (1, DF_HEADS))[:, None, :],
        g_mnorm=g_mnorm[:, None, :],
        w_out=w_out.astype(BF16), w_up=w_up.astype(BF16), w_down=w_down.astype(BF16),
        conv_w=conv_w, conv_b=conv_b[:, None, :], g_final=g_final[None, :],
        lam_q1=lam_q1[:, None, :], lam_k1=lam_k1[:, None, :],
        lam_q2=lam_q2[:, None, :], lam_k2=lam_k2[:, None, :],
    )


def _rope_tables(n_tok):
    t = jnp.arange(n_tok)
    row = (t // GRID_W).astype(F32)
    col = (t % GRID_W).astype(F32)
    nf = MLA_ROPE // 4
    inv = ROPE_BASE ** (-jnp.arange(nf, dtype=F32) / nf)
    ar = row[:, None] * inv[None, :]
    ac = col[:, None] * inv[None, :]
    ang = jnp.concatenate([ar, ar, ac, ac], axis=-1)
    cos, sin = jnp.cos(ang), jnp.sin(ang)
    quarter = (jnp.arange(MLA_ROPE) // nf) % 2
    sin_up = jnp.where(quarter == 0, -sin, 0.0)
    sin_dn = jnp.where(quarter == 1, sin, 0.0)
    ones = jnp.ones((n_tok, 1), F32)
    zeros = jnp.zeros((n_tok, 1), F32)

    def head_q(t32, fill):
        blk = jnp.concatenate([jnp.tile(fill, (1, MLA_NOPE)), t32, jnp.tile(fill, (1, 32))], axis=1)
        return jnp.tile(blk, (1, MLA_HEADS))

    def aux_k(t32, fill):
        return jnp.concatenate([t32, jnp.tile(fill, (1, LANES - MLA_ROPE))], axis=1)

    tq = (head_q(cos, ones), head_q(sin_up, zeros), head_q(sin_dn, zeros))
    td = tuple(jnp.tile(a, (1, 256 // DF_DIM)) for a in (cos, sin_up, sin_dn))
    tk = (aux_k(cos, ones), aux_k(sin_up, zeros), aux_k(sin_dn, zeros))
    return tq + td + tk


def kernel(x_prompt, x_sample, cache_mla_ckv, cache_mla_krope, cache_diff_k, cache_diff_v,
           state_mlstm_C, state_mlstm_n, state_mlstm_m, c, c_ctx, w_ada, b_ada, g_norm1, w_in,
           g_cq, w_uq, g_ckv, w_ukv, lam_q1, lam_k1, lam_q2, lam_k2, g_subln, b_gate, g_mnorm,
           w_out, g_norm2, w_up, conv_w, conv_b, w_down, g_final):
    bp, sp, _ = x_prompt.shape
    bl, sl, _ = x_sample.shape
    t_len = cache_mla_ckv.shape[2]

    wts = _pack_weights(w_in, g_cq, w_uq, g_ckv, w_ukv, b_gate, g_subln, g_mnorm, g_norm1, g_norm2,
                        w_out, w_up, conv_w, conv_b, w_down, g_final, lam_q1, lam_k1, lam_q2, lam_k2)
    tables = _rope_tables(sl)

    cond = jnp.concatenate([c_ctx[None, :], c, jnp.zeros((8 - 1 - bl, D_MODEL), F32)], axis=0)
    mods = _ada_call(cond, w_ada, b_ada[:, None, :])
    mods = mods.reshape(DEPTH * 8 * N_MOD, 1, D_MODEL)

    kr_pad = jnp.pad(cache_mla_krope, [(0, 0), (0, 0), (0, 0), (0, LANES - MLA_ROPE)])
    kctx, vctx = _ctxkv_call(cache_mla_ckv, kr_pad, wts["w_kk"], wts["w_v"])
    cdk = cache_diff_k.reshape(bl, DEPTH, t_len, 256)
    cdv = cache_diff_v.reshape(bl, DEPTH, t_len, 256)
    s0 = jnp.concatenate([state_mlstm_C, jnp.broadcast_to(state_mlstm_n[..., None],
                                                           state_mlstm_n.shape + (LANES,))], axis=-1)
    s0 = s0.reshape(bl, DEPTH, 2, 2, LANES, 2 * LANES)
    m0 = jnp.broadcast_to(jnp.pad(state_mlstm_m, [(0, 0)] * 3 + [(0, 8 - ML_HEADS)])[..., None],
                          (bl, DEPTH, 2, 8, LANES))

    xp, xs = x_prompt, x_sample
    col = [[] for _ in range(7)]
    for l in range(DEPTH):
        lam_init = 0.8 - 0.6 * math.exp(-0.3 * l)
        final = l == DEPTH - 1
        (q, k, v, dq, dk, dv, mq, mkt, mv, mo, gt, ckv_s, kr_s, dk_s, dv_s) = _pre_call(
            False, l, xp, mods, wts, None)
        o_a = _mla_call(False, l, q, k, v)
        o_b = _diff_call(False, l, lam_init, dq, dk, dv, wts)
        o_c, s_fin, m_fin = _mlstm_call(False, l, mq, mkt, mv, mo, gt, wts)
        xp = _post_call(False, l, final, xp, o_a, o_b, o_c, mods, wts)
        col[0].append(ckv_s)
        col[1].append(kr_s)
        col[2].append(dk_s.reshape(bp, sp, DF_HEADS, 2 * DF_DIM))
        col[3].append(dv_s.reshape(bp, sp, DF_HEADS, 2 * DF_DIM))
        s_fin = s_fin.reshape(bp, 2, ML_HEADS, ML_DK, 2 * LANES)
        col[4].append(s_fin[..., :ML_DV])
        col[5].append(s_fin[..., ML_DV])
        col[6].append(m_fin[:, :, :ML_HEADS, 0])
        (q, k, v, dq, dk, dv, mq, mkt, mv, mo, gt) = _pre_call(True, l, xs, mods, wts, tables)
        o_a = _mla_call(True, l, q, k, v, kctx, vctx)
        o_b = _diff_call(True, l, lam_init, dq, dk, dv, wts, cdk, cdv)
        o_c = _mlstm_call(True, l, mq, mkt, mv, mo, gt, wts, s0, m0)
        xs = _post_call(True, l, final, xs, o_a, o_b, o_c, mods, wts)

    return (xp, xs) + tuple(jnp.stack(a, axis=1) for a in col)
```

```python
import functools
import math

import jax
import jax.numpy as jnp
import numpy as np
from jax import lax
from jax.experimental import pallas as pl
from jax.experimental.pallas import tpu as pltpu

F32 = jnp.float32
BF16 = jnp.bfloat16

D_MODEL = 1024
DEPTH = 4
GRID_W = 64
N_MOD = 6
EPS = 1e-6
ROPE_BASE = 10000.0
MLA_HEADS = 4
MLA_Q_RANK = 192
MLA_KV_RANK = 128
MLA_NOPE = 64
MLA_ROPE = 32
MLA_V = 64
DF_HEADS = 4
DF_DIM = 32
ML_HEADS = 4
ML_DK = 64
ML_DV = 128
D_FF = 2816
CONV_W = 3

LANES = 128
VMEM_LIMIT = 48 * 1024 * 1024
VMEM_LIMIT_FFN = 56 * 1024 * 1024

C_CQ, C_CKV, C_AUX, C_DQ, C_DK, C_DV, C_MQ, C_MK, C_MV, C_MO = (
    0, 256, 384, 512, 768, 1024, 1280, 1536, 1792, 2304)
NP_IN = 2816
AUX_GATE = 32

TM_PRE = 256
TQ = 256
ML_CHUNK = 128
TM_FFN = 512
FC = 256
N_FC = D_FF // FC

NT = (((1,), (1,)), ((), ()))
LOG2E = 1.4426950408889634


def _cparams(sem):
    return pltpu.CompilerParams(dimension_semantics=sem, vmem_limit_bytes=VMEM_LIMIT)


def _dot(a, b):
    return jnp.dot(a, b, preferred_element_type=F32)


def _dot_nt(a, b):
    return lax.dot_general(a, b, NT, preferred_element_type=F32)


def _rms_rows(x, g, n):
    ms = jnp.sum(x * x, axis=-1, keepdims=True) * (1.0 / n)
    return x * lax.rsqrt(ms + EPS) * g


def _rope(x, cos, sin_up, sin_dn):
    w = x.shape[-1]
    return x * cos + pltpu.roll(x, w - 8, 1) * sin_up + pltpu.roll(x, 8, 1) * sin_dn


def _ada_kernel(c_ref, w_ref, b_ref, o_ref):
    c = c_ref[...]
    s = (c * jax.nn.sigmoid(c)).astype(BF16)
    o_ref[...] = _dot(s, w_ref[...].astype(BF16)) + b_ref[...]


def _ada_call(cond, w_ada, b_ada):
    nt = 1024
    return pl.pallas_call(
        _ada_kernel,
        out_shape=jax.ShapeDtypeStruct((DEPTH, 8, N_MOD * D_MODEL), F32),
        grid=(DEPTH, N_MOD * D_MODEL // nt),
        in_specs=[pl.BlockSpec((8, D_MODEL), lambda l, j: (0, 0)),
                  pl.BlockSpec((None, D_MODEL, nt), lambda l, j: (l, 0, j)),
                  pl.BlockSpec((None, 1, nt), lambda l, j: (l, 0, j))],
        out_specs=pl.BlockSpec((None, 8, nt), lambda l, j: (l, 0, j)),
        compiler_params=_cparams(("arbitrary", "arbitrary")),
        name="ada_mod",
    )(cond, w_ada, b_ada)


def _ctxkv_kernel(ckv_ref, kr_ref, dk_ref, dv_ref, wkk_ref, wv_ref, k_ref, v_ref, dkb_ref, dvb_ref):
    ckv = ckv_ref[...].astype(BF16)
    kin = jnp.concatenate([ckv, kr_ref[...].astype(BF16)], axis=1)
    k_ref[...] = _dot(kin, wkk_ref[...]).astype(BF16)
    v_ref[...] = _dot(ckv, wv_ref[...]).astype(BF16)
    dkb_ref[...] = dk_ref[...].astype(BF16)
    dvb_ref[...] = dv_ref[...].astype(BF16)


def _ctxkv_call(cache_ckv, cache_kr_pad, cache_dk, cache_dv, wkk, wv):
    b, _, t, _ = cache_ckv.shape

    def cache(w):
        return pl.BlockSpec((None, None, t, w), lambda l, i: (i, l, 0, 0))

    def out(w):
        return pl.BlockSpec((None, None, t, w), lambda l, i: (l, i, 0, 0))

    return pl.pallas_call(
        _ctxkv_kernel,
        out_shape=tuple(jax.ShapeDtypeStruct((DEPTH, b, t, w), BF16) for w in (512, 256, 256, 256)),
        grid=(DEPTH, b),
        in_specs=[cache(MLA_KV_RANK), cache(LANES), cache(256), cache(256),
                  pl.BlockSpec((None, 256, 512), lambda l, i: (l, 0, 0)),
                  pl.BlockSpec((None, MLA_KV_RANK, 256), lambda l, i: (l, 0, 0))],
        out_specs=(out(512), out(256), out(256), out(256)),
        compiler_params=_cparams(("arbitrary", "arbitrary")),
        name="ctx_kv",
    )(cache_ckv, cache_kr_pad, cache_dk, cache_dv, wkk, wv)


def _pre_kernel(latent, *refs):
    (x_ref, sh_ref, sc_ref, g1_ref, win_ref, gcq_ref, wuq_ref, gckv_ref, wkk_ref, wv_ref,
     bg_ref) = refs[:11]
    refs = refs[11:]
    if latent:
        (cq_t, sqa_t, sqb_t, cd_t, sda_t, sdb_t, ck_t, ska_t, skb_t) = refs[:9]
        refs = refs[9:]
    (q_ref, k_ref, v_ref, dq_ref, dk_ref, dv_ref, mq_ref, mkt_ref, mv_ref, mo_ref,
     gt_ref) = refs[:11]
    refs = refs[11:]
    if not latent:
        ckv_out, kr_out, dk_out, dv_out = refs

    x = x_ref[...]
    h = _rms_rows(x, g1_ref[...], D_MODEL) * (1.0 + sc_ref[...]) + sh_ref[...]
    proj = _dot(h.astype(BF16), win_ref[...])

    cq = _rms_rows(proj[:, C_CQ:C_CQ + 256], gcq_ref[...], MLA_Q_RANK)
    q = _dot(cq.astype(BF16), wuq_ref[...])
    if latent:
        q = _rope(q, cq_t[...], sqa_t[...], sqb_t[...])
    q_ref[...] = (q * ((MLA_NOPE + MLA_ROPE) ** -0.5 * LOG2E)).astype(BF16)

    c_kv = _rms_rows(proj[:, C_CKV:C_CKV + MLA_KV_RANK], gckv_ref[...], MLA_KV_RANK)
    aux = proj[:, C_AUX:C_AUX + LANES] + bg_ref[...]
    if latent:
        aux = _rope(aux, ck_t[...], ska_t[...], skb_t[...])
    aux_t = aux.T
    gt_ref[...] = aux_t[AUX_GATE:AUX_GATE + 32, :]
    if not latent:
        ckv_out[...] = c_kv
        kr_out[...] = aux_t[:MLA_ROPE, :]
    ckv_b = c_kv.astype(BF16)
    kin = jnp.concatenate([ckv_b, aux.astype(BF16)], axis=1)
    k_ref[...] = _dot(kin, wkk_ref[...]).astype(BF16)
    v_ref[...] = _dot(ckv_b, wv_ref[...]).astype(BF16)

    dq = proj[:, C_DQ:C_DQ + 256]
    dk = proj[:, C_DK:C_DK + 256]
    dv = proj[:, C_DV:C_DV + 256]
    if not latent:
        dk_out[...] = dk.T
        dv_out[...] = dv.T
    else:
        dq = _rope(dq, cd_t[...], sda_t[...], sdb_t[...])
        dk = _rope(dk, cd_t[...], sda_t[...], sdb_t[...])
    dq_ref[...] = (dq * (DF_DIM ** -0.5 * LOG2E)).astype(BF16)
    dk_ref[...] = dk.astype(BF16)
    dv_ref[...] = dv.astype(BF16)

    lane = lax.broadcasted_iota(jnp.int32, (x.shape[0], LANES), 1)
    for h in range(ML_HEADS):
        blk = proj[:, C_MQ + (h // 2) * LANES:C_MQ + (h // 2 + 1) * LANES]
        mq_ref[:, h * LANES:(h + 1) * LANES] = jnp.where((lane >> 6) == h % 2, blk, 0.0).astype(BF16)
    mkt_ref[...] = (proj[:, C_MK:C_MK + 256] * (ML_DK ** -0.5)).T.astype(BF16)
    mv_ref[...] = proj[:, C_MV:C_MV + 512].astype(BF16)
    mo_ref[...] = proj[:, C_MO:C_MO + 512]


def _pre_call(latent, l, x, mods, wts, tables):
    b, s, _ = x.shape
    ns = s // TM_PRE
    grid = (ns, b)

    def tok(width):
        return pl.BlockSpec((None, TM_PRE, width), lambda j, i: (i, j, 0))

    def mod(chunk):
        if latent:
            return pl.BlockSpec((None, 1, D_MODEL), lambda j, i: (l * 48 + (1 + i) * 6 + chunk, 0, 0))
        return pl.BlockSpec((None, 1, D_MODEL), lambda j, i: (l * 48 + chunk, 0, 0))

    def lw(*shape):
        nd = len(shape)
        return pl.BlockSpec((None,) + shape, lambda j, i: (l,) + (0,) * nd)

    in_specs = [tok(D_MODEL), mod(0), mod(1), lw(1, D_MODEL), lw(D_MODEL, NP_IN), lw(1, 256),
                lw(256, 512), lw(1, MLA_KV_RANK), lw(256, 512), lw(MLA_KV_RANK, 256), lw(1, LANES)]
    args = [x, mods, mods, wts["g_norm1"], wts["w_in"], wts["g_cq"], wts["w_uq"], wts["g_ckv"],
            wts["w_kk"], wts["w_v"], wts["b_gate"]]
    if latent:
        for t in tables:
            in_specs.append(pl.BlockSpec((TM_PRE, t.shape[1]), lambda j, i: (j, 0)))
            args.append(t)

    widths = [(512, BF16, False), (512, BF16, False), (256, BF16, False), (256, BF16, False),
              (256, BF16, False), (256, BF16, False), (512, BF16, False), (256, BF16, True),
              (512, BF16, False), (512, F32, False), (32, F32, True)]
    if not latent:
        widths += [(MLA_KV_RANK, F32, False), (MLA_ROPE, F32, True), (256, F32, True),
                   (256, F32, True)]
    out_shape = tuple(jax.ShapeDtypeStruct((b, w, s) if tr else (b, s, w), dt)
                      for w, dt, tr in widths)
    out_specs = tuple(pl.BlockSpec((None, w, TM_PRE), lambda j, i: (i, 0, j)) if tr else tok(w)
                      for w, _, tr in widths)
    return pl.pallas_call(
        functools.partial(_pre_kernel, latent),
        out_shape=out_shape, grid=grid, in_specs=in_specs, out_specs=out_specs,
        compiler_params=_cparams(("arbitrary", "arbitrary")),
        name="pre_lat" if latent else "pre_ctx",
    )(*args)


def _softmax_parts(s_list):
    m = functools.reduce(jnp.maximum, [jnp.max(s, axis=1, keepdims=True) for s in s_list])
    p_list = [jnp.exp2(s - m) for s in s_list]
    l = functools.reduce(jnp.add, [jnp.sum(p, axis=1, keepdims=True) for p in p_list])
    return p_list, l


def _mla_kernel(latent, *refs):
    if latent:
        q_ref, k_ref, v_ref, kc_ref, vc_ref, o_ref = refs
        segs = [(kc_ref, vc_ref), (k_ref, v_ref)]
    else:
        q_ref, k_ref, v_ref, o_ref = refs
        segs = [(k_ref, v_ref)]
    lane = lax.broadcasted_iota(jnp.int32, (TQ, LANES), 1)

    def scores(h):
        hs = slice(h * LANES, (h + 1) * LANES)
        return [_dot_nt(q_ref[:, hs], kr[:, hs]) for kr, _ in segs]

    outs = []
    s_next = scores(0)
    for h in range(MLA_HEADS):
        ps = slice((h // 2) * LANES, (h // 2 + 1) * LANES)
        s_list = s_next
        if h + 1 < MLA_HEADS:
            s_next = scores(h + 1)
        p_list, l = _softmax_parts(s_list)
        pv = functools.reduce(jnp.add, [_dot(p.astype(BF16), vr[:, ps])
                                        for p, (_, vr) in zip(p_list, segs)])
        outs.append(pv / l)
    o_ref[:, 0:LANES] = jnp.where(lane < MLA_V, outs[0], outs[1]).astype(BF16)
    o_ref[:, LANES:2 * LANES] = jnp.where(lane < MLA_V, outs[2], outs[3]).astype(BF16)


def _mla_call(latent, l, q, k, v, kctx=None, vctx=None):
    b, s, _ = q.shape
    grid = (b, s // TQ)
    in_specs = [pl.BlockSpec((None, TQ, 512), lambda i, j: (i, j, 0)),
                pl.BlockSpec((None, s, 512), lambda i, j: (i, 0, 0)),
                pl.BlockSpec((None, s, 256), lambda i, j: (i, 0, 0))]
    args = [q, k, v]
    if latent:
        t = kctx.shape[2]
        in_specs += [pl.BlockSpec((None, None, t, 512), lambda i, j: (l, i, 0, 0)),
                     pl.BlockSpec((None, None, t, 256), lambda i, j: (l, i, 0, 0))]
        args += [kctx, vctx]
    return pl.pallas_call(
        functools.partial(_mla_kernel, latent),
        out_shape=jax.ShapeDtypeStruct((b, s, 256), BF16),
        grid=grid, in_specs=in_specs,
        out_specs=pl.BlockSpec((None, TQ, 256), lambda i, j: (i, j, 0)),
        compiler_params=_cparams(("arbitrary", "arbitrary")),
        name="mla_lat" if latent else "mla_ctx",
    )(*args)


def _diff_kernel(latent, lam_init, *refs):
    if latent:
        (q_ref, k_ref, v_ref, kc_ref, vc_ref, lq1, lk1, lq2, lk2, g_ref, o_ref) = refs
        segs = [(kc_ref, vc_ref), (k_ref, v_ref)]
    else:
        (q_ref, k_ref, v_ref, lq1, lk1, lq2, lk2, g_ref, o_ref) = refs
        segs = [(k_ref, v_ref)]
    lam = (jnp.exp(jnp.sum(lq1[...] * lk1[...], axis=1, keepdims=True))
           - jnp.exp(jnp.sum(lq2[...] * lk2[...], axis=1, keepdims=True)) + lam_init)
    lane = lax.broadcasted_iota(jnp.int32, (TQ, LANES), 1)
    grp = lane >> 5
    qf = q_ref[...].astype(F32)

    def scores(u):
        h, c = u // 2, u % 2
        ps = slice((h // 2) * LANES, (h // 2 + 1) * LANES)
        qm = jnp.where(grp == 2 * (h % 2) + c, qf[:, ps], 0.0).astype(BF16)
        return [_dot_nt(qm, kr[:, ps]) for kr, _ in segs]

    outs = []
    s_next = scores(0)
    for h in range(DF_HEADS):
        ps = slice((h // 2) * LANES, (h // 2 + 1) * LANES)
        hh = h % 2
        parts = []
        for c in range(2):
            s_list = s_next
            if 2 * h + c + 1 < 2 * DF_HEADS:
                s_next = scores(2 * h + c + 1)
            parts.append(_softmax_parts(s_list))
        (p0, l0), (p1, l1) = parts
        inv0 = 1.0 / l0
        inv1 = lam / l1
        pv = functools.reduce(jnp.add, [
            _dot((a0 * inv0 - a1 * inv1).astype(BF16), vr[:, ps])
            for a0, a1, (_, vr) in zip(p0, p1, segs)])
        valid = (lane >> 6) == hh
        ms = jnp.sum(jnp.where(valid, pv * pv, 0.0), axis=1, keepdims=True) * (1.0 / (2 * DF_DIM))
        outs.append(pv * lax.rsqrt(ms + EPS) * g_ref[:, ps] * (1.0 - lam_init))
    o_ref[:, 0:LANES] = jnp.where(lane < 2 * DF_DIM, outs[0], outs[1]).astype(BF16)
    o_ref[:, LANES:2 * LANES] = jnp.where(lane < 2 * DF_DIM, outs[2], outs[3]).astype(BF16)


def _diff_call(latent, l, lam_init, dq, dk, dv, wts, cdk=None, cdv=None):
    b, s, _ = dq.shape
    grid = (b, s // TQ)
    in_specs = [pl.BlockSpec((None, TQ, 256), lambda i, j: (i, j, 0)),
                pl.BlockSpec((None, s, 256), lambda i, j: (i, 0, 0)),
                pl.BlockSpec((None, s, 256), lambda i, j: (i, 0, 0))]
    args = [dq, dk, dv]
    if latent:
        t = cdk.shape[2]
        in_specs += [pl.BlockSpec((None, None, t, 256), lambda i, j: (l, i, 0, 0)),
                     pl.BlockSpec((None, None, t, 256), lambda i, j: (l, i, 0, 0))]
        args += [cdk, cdv]
    for name in ("lam_q1", "lam_k1", "lam_q2", "lam_k2"):
        in_specs.append(pl.BlockSpec((None, 1, DF_DIM), lambda i, j: (l, 0, 0)))
        args.append(wts[name])
    in_specs.append(pl.BlockSpec((None, 1, 256), lambda i, j: (l, 0, 0)))
    args.append(wts["g_subln"])
    return pl.pallas_call(
        functools.partial(_diff_kernel, latent, lam_init),
        out_shape=jax.ShapeDtypeStruct((b, s, 256), BF16),
        grid=grid, in_specs=in_specs,
        out_specs=pl.BlockSpec((None, TQ, 256), lambda i, j: (i, j, 0)),
        compiler_params=_cparams(("arbitrary", "arbitrary")),
        name="diff_lat" if latent else "diff_ctx",
    )(*args)


def _log_sigmoid(x):
    return jnp.minimum(x, 0.0) - jnp.log1p(jnp.exp(-jnp.abs(x)))


def _mlstm_chunk(d, c, mq_ref, mkt_ref, mv_ref, gt_ref, s_ref, m_ref, h_ref):
    L = ML_CHUNK
    rows = pl.ds(pl.multiple_of(c * L, L), L)
    t_i = lax.broadcasted_iota(jnp.int32, (L, L), 0)
    s_i = lax.broadcasted_iota(jnp.int32, (L, L), 1)
    mask = (s_i <= t_i) if d == 0 else (s_i >= t_i)
    tri = jnp.where((t_i <= s_i) if d == 0 else (t_i >= s_i), 1.0, 0.0).astype(BF16)

    ig = gt_ref[16 * d:16 * d + 8, rows]
    lf = _log_sigmoid(gt_ref[16 * d + 8:16 * d + 16, rows])
    hi = lf.astype(BF16).astype(F32)
    r1 = lf - hi
    mid = r1.astype(BF16).astype(F32)
    parts = _dot(jnp.concatenate([hi, mid, r1 - mid], axis=0).astype(BF16), tri)
    bc = parts[0:8] + parts[8:16] + parts[16:24]
    rvec = ig - bc
    total = jnp.sum(lf, axis=1, keepdims=True)
    mm = m_ref[d]
    gvec = total + rvec
    m_new = jnp.maximum(total + mm, jnp.max(gvec, axis=1, keepdims=True))
    ws = jnp.exp(gvec - m_new).astype(BF16)
    cdec = jnp.exp(total + mm - m_new)
    m_ref[d] = m_new

    s_old = [s_ref[d, pair].astype(BF16) for pair in range(2)]
    ones = jnp.ones((L, LANES), BF16)
    upd = []
    for h in range(ML_HEADS):
        pair = h // 2
        hs = slice(h * ML_DV, (h + 1) * ML_DV)
        qh = mq_ref[rows, hs]
        ktp = mkt_ref[pair * LANES:(pair + 1) * LANES, rows]
        vh = mv_ref[rows, hs]
        mmh = mm[h:h + 1, 0:1]
        rm = jnp.where(mask, rvec[h:h + 1, :], -jnp.inf)
        a = jnp.maximum(jnp.max(rm, axis=1, keepdims=True), mmh)
        bcc = jnp.sum(jnp.where(mask, lf[h:h + 1, :], 0.0), axis=1, keepdims=True)
        wqk = jnp.exp(rm - a) * _dot(qh, ktp)
        dec = jnp.exp(mmh - a)
        qc = _dot(qh, s_old[pair])
        num = _dot(wqk.astype(BF16), vh) + dec * qc[:, :ML_DV]
        den = jnp.sum(wqk, axis=1, keepdims=True) + dec * qc[:, ML_DV:]
        h_ref[d, rows, hs] = num / jnp.maximum(jnp.abs(den), jnp.exp(-(a + bcc)))
        upd.append(_dot(ktp * ws[h:h + 1, :], jnp.concatenate([vh, ones], axis=1)))

    low = lax.broadcasted_iota(jnp.int32, (LANES, 2 * LANES), 0) < ML_DK
    for pair in range(2):
        h0, h1 = 2 * pair, 2 * pair + 1
        cd = jnp.where(low, cdec[h0:h0 + 1, 0:1], cdec[h1:h1 + 1, 0:1])
        s_ref[d, pair] = cd * s_ref[d, pair] + jnp.where(low, upd[h0], upd[h1])


def _mlstm_kernel(latent, seq, *refs):
    if latent:
        (mq_ref, mkt_ref, mv_ref, mo_ref, gt_ref, g_ref, s0_ref, m0_ref,
         o_ref, s_ref, m_ref, h_ref) = refs
        s_ref[...] = s0_ref[...]
        m_ref[...] = m0_ref[...]
    else:
        (mq_ref, mkt_ref, mv_ref, mo_ref, gt_ref, g_ref,
         o_ref, cf_ref, nf_ref, mf_ref, s_ref, m_ref, h_ref) = refs
        s_ref[...] = jnp.zeros(s_ref.shape, F32)
        m_ref[...] = jnp.zeros(m_ref.shape, F32)
    nc = seq // ML_CHUNK

    def body(j, carry):
        _mlstm_chunk(0, j, mq_ref, mkt_ref, mv_ref, gt_ref, s_ref, m_ref, h_ref)
        _mlstm_chunk(1, nc - 1 - j, mq_ref, mkt_ref, mv_ref, gt_ref, s_ref, m_ref, h_ref)
        return carry

    lax.fori_loop(0, nc, body, 0)

    for h in range(ML_HEADS):
        hs = slice(h * ML_DV, (h + 1) * ML_DV)
        hsum = h_ref[0, :, hs] + h_ref[1, :, hs]
        y = _rms_rows(hsum, g_ref[:, hs], ML_DV)
        o_ref[:, hs] = (jax.nn.sigmoid(mo_ref[:, hs]) * y).astype(BF16)
    if not latent:
        cf_ref[...] = s_ref[:, :, :, :ML_DV]
        nf_ref[...] = s_ref[:, :, :, ML_DV:]
        mf_ref[...] = m_ref[...]


def _mlstm_call(latent, l, mq, mkt, mv, mo, gt, wts, s0=None, m0=None):
    assert ML_CHUNK == LANES
    b, s, _ = mq.shape

    def tok(w):
        return pl.BlockSpec((None, s, w), lambda i: (i, 0, 0))

    in_specs = [tok(512), pl.BlockSpec((None, 256, s), lambda i: (i, 0, 0)), tok(512), tok(512),
                pl.BlockSpec((None, 32, s), lambda i: (i, 0, 0)),
                pl.BlockSpec((None, 1, 512), lambda i: (l, 0, 0))]
    args = [mq, mkt, mv, mo, gt, wts["g_mnorm"]]
    s_spec_shape = (2, 2, LANES, 2 * LANES)
    m_spec_shape = (2, 8, LANES)
    scratch = [pltpu.VMEM(s_spec_shape, F32), pltpu.VMEM(m_spec_shape, F32),
               pltpu.VMEM((2, s, ML_HEADS * ML_DV), F32)]
    if latent:
        in_specs += [pl.BlockSpec((None, None) + s_spec_shape, lambda i: (i, l, 0, 0, 0, 0)),
                     pl.BlockSpec((None, None) + m_spec_shape, lambda i: (i, l, 0, 0, 0))]
        args += [s0, m0]
        out_shape = jax.ShapeDtypeStruct((b, s, 512), BF16)
        out_specs = tok(512)
    else:
        half = (2, 2, LANES, LANES)
        out_shape = (jax.ShapeDtypeStruct((b, s, 512), BF16),
                     jax.ShapeDtypeStruct((b,) + half, F32),
                     jax.ShapeDtypeStruct((b,) + half, F32),
                     jax.ShapeDtypeStruct((b,) + m_spec_shape, F32))
        out_specs = (tok(512),
                     pl.BlockSpec((None,) + half, lambda i: (i, 0, 0, 0, 0)),
                     pl.BlockSpec((None,) + half, lambda i: (i, 0, 0, 0, 0)),
                     pl.BlockSpec((None,) + m_spec_shape, lambda i: (i, 0, 0, 0)))
    return pl.pallas_call(
        functools.partial(_mlstm_kernel, latent, s),
        out_shape=out_shape, grid=(b,), in_specs=in_specs, out_specs=out_specs,
        scratch_shapes=scratch,
        compiler_params=_cparams(("arbitrary",)),
        name="mlstm_lat" if latent else "mlstm_ctx",
    )(*args)


def _post_kernel(seg, final, *refs):
    (x_ref, oa_ref, ob_ref, oc_ref, wo_ref, gt1_ref, sh2_ref, sc2_ref, gt2_ref, g2_ref,
     wu_ref, cw_ref, cb_ref, wd_ref, gf_ref, o_ref, x1_ref, h2_ref, act_ref) = refs

    mix = _dot(jnp.concatenate([oa_ref[...], ob_ref[...], oc_ref[...]], axis=1), wo_ref[...])
    x1 = x_ref[...] + gt1_ref[...] * mix
    x1_ref[...] = x1
    h2 = _rms_rows(x1, g2_ref[...], D_MODEL) * (1.0 + sc2_ref[...]) + sh2_ref[...]
    h2_ref[...] = h2.astype(BF16)

    row = lax.broadcasted_iota(jnp.int32, (TM_FFN, FC), 0) & (seg - 1)
    first = row == 0
    last = row == seg - 1

    def conv(u, cs):
        prev = jnp.where(first, 0.0, pltpu.roll(u, 1, 0))
        nxt = jnp.where(last, 0.0, pltpu.roll(u, TM_FFN - 1, 0))
        return (cb_ref[:, cs] + prev * cw_ref[0:1, cs] + u * cw_ref[1:2, cs]
                + nxt * cw_ref[2:3, cs])

    for j in range(N_FC):
        vs = slice(j * FC, (j + 1) * FC)
        gs = slice(D_FF + j * FC, D_FF + (j + 1) * FC)
        val = conv(_dot(h2_ref[...], wu_ref[:, vs]), vs)
        gate = conv(_dot(h2_ref[...], wu_ref[:, gs]), gs)
        act_ref[:, vs] = (gate * jax.nn.sigmoid(gate) * val).astype(BF16)

    x2 = x1_ref[...] + gt2_ref[...] * _dot(act_ref[...], wd_ref[...])
    if final:
        x2 = _rms_rows(x2, gf_ref[...], D_MODEL)
    o_ref[...] = x2


def _post_call(latent, l, final, x, oa, ob, oc, mods, wts):
    b, s, _ = x.shape
    n = b * s
    x2d = x.reshape(n, D_MODEL)
    nt = n // TM_FFN
    tiles_per_batch = s // TM_FFN if latent else 1
    seg = GRID_W if latent else s

    def tok(w):
        return pl.BlockSpec((TM_FFN, w), lambda i: (i, 0))

    def mod(chunk):
        if latent:
            return pl.BlockSpec((None, 1, D_MODEL),
                                lambda i: (l * 48 + (1 + i // tiles_per_batch) * 6 + chunk, 0, 0))
        return pl.BlockSpec((None, 1, D_MODEL), lambda i: (l * 48 + chunk, 0, 0))

    def resident(*shape):
        nd = len(shape)
        return pl.BlockSpec((None,) + shape, lambda i: (l,) + (0,) * nd,
                            pipeline_mode=pl.Buffered(1))

    in_specs = [
        tok(D_MODEL), tok(256), tok(256), tok(512),
        resident(D_MODEL, D_MODEL),
        mod(2), mod(3), mod(4), mod(5),
        resident(1, D_MODEL),
        resident(D_MODEL, 2 * D_FF), resident(CONV_W, 2 * D_FF), resident(1, 2 * D_FF),
        resident(D_FF, D_MODEL),
        pl.BlockSpec((1, D_MODEL), lambda i: (0, 0)),
    ]
    args = [x2d, oa.reshape(n, 256), ob.reshape(n, 256), oc.reshape(n, 512),
            wts["w_out"], mods, mods, mods, mods, wts["g_norm2"],
            wts["w_up"], wts["conv_w"], wts["conv_b"], wts["w_down"], wts["g_final"]]
    out = pl.pallas_call(
        functools.partial(_post_kernel, seg, final),
        out_shape=jax.ShapeDtypeStruct((n, D_MODEL), F32),
        grid=(nt,), in_specs=in_specs,
        out_specs=pl.BlockSpec((TM_FFN, D_MODEL), lambda i: (i, 0)),
        scratch_shapes=[pltpu.VMEM((TM_FFN, D_MODEL), F32), pltpu.VMEM((TM_FFN, D_MODEL), BF16),
                        pltpu.VMEM((TM_FFN, D_FF), BF16)],
        compiler_params=pltpu.CompilerParams(dimension_semantics=("arbitrary",),
                                             vmem_limit_bytes=VMEM_LIMIT_FFN),
        name="post_lat" if latent else "post_ctx",
    )(*args)
    return out.reshape(b, s, D_MODEL)


W_IN_BODY = (352, 2656)


def _pack_in_kernel(w_ref, o_ref):
    o_ref[...] = jnp.zeros(o_ref.shape, BF16)

    def put(dst, lo, n):
        o_ref[:, dst:dst + n] = w_ref[:, lo:lo + n].astype(BF16)

    put(C_CQ, 0, MLA_Q_RANK)
    put(C_CKV, MLA_Q_RANK, MLA_KV_RANK)
    put(C_AUX, MLA_Q_RANK + MLA_KV_RANK, MLA_ROPE)
    for g in range(4):
        put(C_AUX + AUX_GATE + 8 * g, W_IN_BODY[1] + ML_HEADS * g, ML_HEADS)
    put(C_DQ, W_IN_BODY[0], W_IN_BODY[1] - W_IN_BODY[0])


def _pack_in_call(w_in):
    tr = 256
    return pl.pallas_call(
        _pack_in_kernel,
        out_shape=jax.ShapeDtypeStruct((DEPTH, D_MODEL, NP_IN), BF16),
        grid=(DEPTH, D_MODEL // tr),
        in_specs=[pl.BlockSpec((None, tr, w_in.shape[-1]), lambda l, i: (l, i, 0))],
        out_specs=pl.BlockSpec((None, tr, NP_IN), lambda l, i: (l, i, 0)),
        compiler_params=_cparams(("arbitrary", "arbitrary")),
        name="pack_w_in",
    )(w_in)


def _pack_weights(w_in, g_cq, w_uq, g_ckv, w_ukv, b_gate, g_subln, g_mnorm, g_norm1, g_norm2,
                  w_out, w_up, conv_w, conv_b, w_down, g_final, lam_q1, lam_k1, lam_q2, lam_k2):
    def cols(a, lo, n, pad=0):
        blk = a[..., lo:lo + n]
        if pad:
            blk = jnp.pad(blk, [(0, 0)] * (a.ndim - 1) + [(0, pad)])
        return blk

    w_in_p = _pack_in_call(w_in)

    hd = MLA_NOPE + MLA_ROPE
    w_uq_p = jnp.pad(w_uq.reshape(DEPTH, MLA_Q_RANK, MLA_HEADS, hd),
                     [(0, 0), (0, 256 - MLA_Q_RANK), (0, 0), (0, LANES - hd)])
    w_uq_p = w_uq_p.reshape(DEPTH, 256, MLA_HEADS * LANES).astype(BF16)

    w_ukv4 = w_ukv.reshape(DEPTH, MLA_KV_RANK, MLA_HEADS, MLA_NOPE + MLA_V)
    w_k = jnp.pad(w_ukv4[..., :MLA_NOPE], [(0, 0), (0, 0), (0, 0), (0, LANES - MLA_NOPE)])
    w_k = w_k.reshape(DEPTH, MLA_KV_RANK, MLA_HEADS * LANES)
    j = jnp.arange(LANES)[:, None]
    cix = jnp.arange(MLA_HEADS * LANES)[None, :]
    place = ((j < MLA_ROPE) & ((cix % LANES) == MLA_NOPE + j)).astype(F32)
    w_kk = jnp.concatenate([w_k, jnp.broadcast_to(place, (DEPTH, LANES, MLA_HEADS * LANES))],
                           axis=1).astype(BF16)
    w_v = w_ukv4[..., MLA_NOPE:].reshape(DEPTH, MLA_KV_RANK, MLA_HEADS * MLA_V).astype(BF16)

    return dict(
        w_in=w_in_p, w_uq=w_uq_p, w_kk=w_kk, w_v=w_v,
        g_norm1=g_norm1[:, None, :], g_norm2=g_norm2[:, None, :],
        g_cq=jnp.pad(g_cq, [(0, 0), (0, 256 - MLA_Q_RANK)])[:, None, :],
        g_ckv=g_ckv[:, None, :],
        b_gate=jnp.pad(jnp.pad(b_gate.reshape(DEPTH, 4, ML_HEADS), [(0, 0), (0, 0), (0, 4)])
                       .reshape(DEPTH, 32), [(0, 0), (AUX_GATE, LANES - AUX_GATE - 32)])[:, None, :],
        g_subln=jnp.tile(g_subln, (1, DF_HEADS))[:, None, :],
        g_mnorm=g_mnorm[:, None, :],
        w_out=w_out.astype(BF16), w_up=w_up.astype(BF16), w_down=w_down.astype(BF16),
        conv_w=conv_w, conv_b=conv_b[:, None, :], g_final=g_final[None, :],
        lam_q1=lam_q1[:, None, :], lam_k1=lam_k1[:, None, :],
        lam_q2=lam_q2[:, None, :], lam_k2=lam_k2[:, None, :],
    )


def _rope_tables(n_tok):
    t = np.arange(n_tok)
    row = (t // GRID_W).astype(np.float64)
    col = (t % GRID_W).astype(np.float64)
    nf = MLA_ROPE // 4
    inv = ROPE_BASE ** (-np.arange(nf, dtype=np.float64) / nf)
    ar = row[:, None] * inv[None, :]
    ac = col[:, None] * inv[None, :]
    ang = np.concatenate([ar, ar, ac, ac], axis=-1)
    quarter = (np.arange(MLA_ROPE) // nf) % 2
    cos = jnp.asarray(np.cos(ang), F32)
    sin_up = jnp.asarray(np.where(quarter == 0, -np.sin(ang), 0.0), F32)
    sin_dn = jnp.asarray(np.where(quarter == 1, np.sin(ang), 0.0), F32)
    ones = jnp.ones((n_tok, 1), F32)
    zeros = jnp.zeros((n_tok, 1), F32)

    def head_q(t32, fill):
        blk = jnp.concatenate([jnp.tile(fill, (1, MLA_NOPE)), t32, jnp.tile(fill, (1, 32))], axis=1)
        return jnp.tile(blk, (1, MLA_HEADS))

    def aux_k(t32, fill):
        return jnp.concatenate([t32, jnp.tile(fill, (1, LANES - MLA_ROPE))], axis=1)

    tq = (head_q(cos, ones), head_q(sin_up, zeros), head_q(sin_dn, zeros))
    td = tuple(jnp.tile(a, (1, 256 // DF_DIM)) for a in (cos, sin_up, sin_dn))
    tk = (aux_k(cos, ones), aux_k(sin_up, zeros), aux_k(sin_dn, zeros))
    return tq + td + tk


def kernel(x_prompt, x_sample, cache_mla_ckv, cache_mla_krope, cache_diff_k, cache_diff_v,
           state_mlstm_C, state_mlstm_n, state_mlstm_m, c, c_ctx, w_ada, b_ada, g_norm1, w_in,
           g_cq, w_uq, g_ckv, w_ukv, lam_q1, lam_k1, lam_q2, lam_k2, g_subln, b_gate, g_mnorm,
           w_out, g_norm2, w_up, conv_w, conv_b, w_down, g_final):
    bp, sp, _ = x_prompt.shape
    bl, sl, _ = x_sample.shape
    t_len = cache_mla_ckv.shape[2]

    wts = _pack_weights(w_in, g_cq, w_uq, g_ckv, w_ukv, b_gate, g_subln, g_mnorm, g_norm1, g_norm2,
                        w_out, w_up, conv_w, conv_b, w_down, g_final, lam_q1, lam_k1, lam_q2, lam_k2)
    tables = _rope_tables(sl)

    cond = jnp.concatenate([c_ctx[None, :], c, jnp.zeros((8 - 1 - bl, D_MODEL), F32)], axis=0)
    mods = _ada_call(cond, w_ada, b_ada[:, None, :])
    mods = mods.reshape(DEPTH * 8 * N_MOD, 1, D_MODEL)

    kr_pad = jnp.pad(cache_mla_krope, [(0, 0), (0, 0), (0, 0), (0, LANES - MLA_ROPE)])
    kctx, vctx, cdk, cdv = _ctxkv_call(
        cache_mla_ckv, kr_pad, cache_diff_k.reshape(bl, DEPTH, t_len, 256),
        cache_diff_v.reshape(bl, DEPTH, t_len, 256), wts["w_kk"], wts["w_v"])
    s0 = jnp.concatenate([state_mlstm_C, jnp.broadcast_to(state_mlstm_n[..., None],
                                                           state_mlstm_n.shape + (LANES,))], axis=-1)
    s0 = s0.reshape(bl, DEPTH, 2, 2, LANES, 2 * LANES)
    m0 = jnp.broadcast_to(jnp.pad(state_mlstm_m, [(0, 0)] * 3 + [(0, 8 - ML_HEADS)])[..., None],
                          (bl, DEPTH, 2, 8, LANES))

    xp, xs = x_prompt, x_sample
    col = [[] for _ in range(7)]
    for l in range(DEPTH):
        lam_init = 0.8 - 0.6 * math.exp(-0.3 * l)
        final = l == DEPTH - 1
        (q, k, v, dq, dk, dv, mq, mkt, mv, mo, gt, ckv_s, kr_s, dk_s, dv_s) = _pre_call(
            False, l, xp, mods, wts, None)
        o_a = _mla_call(False, l, q, k, v)
        o_b = _diff_call(False, l, lam_init, dq, dk, dv, wts)
        o_c, c_fin, n_fin, m_fin = _mlstm_call(False, l, mq, mkt, mv, mo, gt, wts)
        xp = _post_call(False, l, final, xp, o_a, o_b, o_c, mods, wts)
        col[0].append(ckv_s)
        col[1].append(kr_s)
        col[2].append(dk_s)
        col[3].append(dv_s)
        col[4].append(c_fin.reshape(bp, 2, ML_HEADS, ML_DK, ML_DV))
        col[5].append(n_fin[..., 0].reshape(bp, 2, ML_HEADS, ML_DK))
        col[6].append(m_fin[:, :, :ML_HEADS, 0])
        (q, k, v, dq, dk, dv, mq, mkt, mv, mo, gt) = _pre_call(True, l, xs, mods, wts, tables)
        o_a = _mla_call(True, l, q, k, v, kctx, vctx)
        o_b = _diff_call(True, l, lam_init, dq, dk, dv, wts, cdk, cdv)
        o_c = _mlstm_call(True, l, mq, mkt, mv, mo, gt, wts, s0, m0)
        xs = _post_call(True, l, final, xs, o_a, o_b, o_c, mods, wts)

    st = [jnp.stack(a, axis=1) for a in col]
    st[1] = jnp.swapaxes(st[1], 2, 3)
    for i in (2, 3):
        st[i] = jnp.transpose(st[i].reshape(bp, DEPTH, DF_HEADS, 2 * DF_DIM, sp), (0, 1, 4, 2, 3))
    return (xp, xs) + tuple(st)
```

```python
import functools
import math

import jax
import jax.numpy as jnp
import numpy as np
from jax import lax
from jax.experimental import pallas as pl
from jax.experimental.pallas import tpu as pltpu

F32 = jnp.float32
BF16 = jnp.bfloat16

D_MODEL = 1024
DEPTH = 4
GRID_W = 64
N_MOD = 6
EPS = 1e-6
ROPE_BASE = 10000.0
MLA_HEADS = 4
MLA_Q_RANK = 192
MLA_KV_RANK = 128
MLA_NOPE = 64
MLA_ROPE = 32
MLA_V = 64
DF_HEADS = 4
DF_DIM = 32
ML_HEADS = 4
ML_DK = 64
ML_DV = 128
D_FF = 2816
CONV_W = 3

LANES = 128
VMEM_LIMIT = 48 * 1024 * 1024
VMEM_LIMIT_FFN = 56 * 1024 * 1024

C_CQ, C_CKV, C_AUX, C_DQ, C_DK, C_DV, C_MQ, C_MK, C_MV, C_MO = (
    0, 256, 384, 512, 768, 1024, 1280, 1536, 1792, 2304)
NP_IN = 2816
AUX_GATE = 32

TM_PRE = 256
TQ = 256
NB_CTX = 4
ML_CHUNK = 128
TM_FFN = 512
FC = 256
N_FC = D_FF // FC

NT = (((1,), (1,)), ((), ()))
LOG2E = 1.4426950408889634


def _cparams(sem):
    return pltpu.CompilerParams(dimension_semantics=sem, vmem_limit_bytes=VMEM_LIMIT)


def _dot(a, b):
    return jnp.dot(a, b, preferred_element_type=F32)


def _dot_nt(a, b):
    return lax.dot_general(a, b, NT, preferred_element_type=F32)


def _rms_rows(x, g, n):
    ms = jnp.sum(x * x, axis=-1, keepdims=True) * (1.0 / n)
    return x * lax.rsqrt(ms + EPS) * g


def _rope(x, cos, sin_up, sin_dn):
    w = x.shape[-1]
    return x * cos + pltpu.roll(x, w - 8, 1) * sin_up + pltpu.roll(x, 8, 1) * sin_dn


def _ada_kernel(c_ref, w_ref, b_ref, o_ref):
    c = c_ref[...]
    s = (c * jax.nn.sigmoid(c)).astype(BF16)
    o_ref[...] = _dot(s, w_ref[...].astype(BF16)) + b_ref[...]


def _ada_call(cond, w_ada, b_ada):
    nt = 1024
    return pl.pallas_call(
        _ada_kernel,
        out_shape=jax.ShapeDtypeStruct((DEPTH, 8, N_MOD * D_MODEL), F32),
        grid=(DEPTH, N_MOD * D_MODEL // nt),
        in_specs=[pl.BlockSpec((8, D_MODEL), lambda l, j: (0, 0)),
                  pl.BlockSpec((None, D_MODEL, nt), lambda l, j: (l, 0, j)),
                  pl.BlockSpec((None, 1, nt), lambda l, j: (l, 0, j))],
        out_specs=pl.BlockSpec((None, 8, nt), lambda l, j: (l, 0, j)),
        compiler_params=_cparams(("arbitrary", "arbitrary")),
        name="ada_mod",
    )(cond, w_ada, b_ada)


def _ctxkv_kernel(ckv_ref, krt_ref, dkt_ref, dvt_ref, wkk_ref, wv_ref, k_ref, v_ref, dkb_ref, dvb_ref):
    t = ckv_ref.shape[0]
    ckv = ckv_ref[...].astype(BF16)
    kr = jnp.concatenate([krt_ref[...], jnp.zeros((LANES - MLA_ROPE, t), F32)], axis=0).T
    kin = jnp.concatenate([ckv, kr.astype(BF16)], axis=1)
    k_ref[...] = _dot(kin, wkk_ref[...]).astype(BF16)
    v_ref[...] = _dot(ckv, wv_ref[...]).astype(BF16)
    dkb_ref[...] = dkt_ref[...].T.astype(BF16)
    dvb_ref[...] = dvt_ref[...].T.astype(BF16)


def _ctxkv_call(cache_ckv, cache_kr_t, cache_dk_t, cache_dv_t, wkk, wv):
    b, _, t, _ = cache_ckv.shape

    def cache_t(w):
        return pl.BlockSpec((None, None, w, t), lambda l, i: (i, l, 0, 0))

    def out(w):
        return pl.BlockSpec((None, None, t, w), lambda l, i: (l, i, 0, 0))

    return pl.pallas_call(
        _ctxkv_kernel,
        out_shape=tuple(jax.ShapeDtypeStruct((DEPTH, b, t, w), BF16) for w in (512, 256, 256, 256)),
        grid=(DEPTH, b),
        in_specs=[pl.BlockSpec((None, None, t, MLA_KV_RANK), lambda l, i: (i, l, 0, 0)),
                  cache_t(MLA_ROPE), cache_t(256), cache_t(256),
                  pl.BlockSpec((None, 256, 512), lambda l, i: (l, 0, 0)),
                  pl.BlockSpec((None, MLA_KV_RANK, 256), lambda l, i: (l, 0, 0))],
        out_specs=(out(512), out(256), out(256), out(256)),
        compiler_params=_cparams(("arbitrary", "arbitrary")),
        name="ctx_kv",
    )(cache_ckv, cache_kr_t, cache_dk_t, cache_dv_t, wkk, wv)


def _pre_kernel(latent, *refs):
    (x_ref, sh_ref, sc_ref, g1_ref, win_ref, gcq_ref, wuq_ref, gckv_ref, wkk_ref, wv_ref,
     bg_ref) = refs[:11]
    refs = refs[11:]
    if latent:
        (cq_t, sqa_t, sqb_t, cd_t, sda_t, sdb_t, ck_t, ska_t, skb_t) = refs[:9]
        refs = refs[9:]
    (q_ref, k_ref, v_ref, dq_ref, dk_ref, dv_ref, mq_ref, mk_ref, mv_ref, mo_ref,
     gt_ref) = refs[:11]
    refs = refs[11:]
    if not latent:
        ckv_out, kr_out, dk_out, dv_out = refs

    x = x_ref[...]
    h = _rms_rows(x, g1_ref[...], D_MODEL) * (1.0 + sc_ref[...]) + sh_ref[...]
    proj = _dot_nt(h.astype(BF16), win_ref[...])

    cq = _rms_rows(proj[:, C_CQ:C_CQ + 256], gcq_ref[...], MLA_Q_RANK)
    q = _dot(cq.astype(BF16), wuq_ref[...])
    if latent:
        q = _rope(q, cq_t[...], sqa_t[...], sqb_t[...])
    q_ref[...] = (q * ((MLA_NOPE + MLA_ROPE) ** -0.5 * LOG2E)).astype(BF16)

    c_kv = _rms_rows(proj[:, C_CKV:C_CKV + MLA_KV_RANK], gckv_ref[...], MLA_KV_RANK)
    aux = proj[:, C_AUX:C_AUX + LANES] + bg_ref[...]
    if latent:
        aux = _rope(aux, ck_t[...], ska_t[...], skb_t[...])
    aux_t = aux.T
    gt_ref[...] = aux_t[AUX_GATE:AUX_GATE + 32, :]
    if not latent:
        ckv_out[...] = c_kv
        kr_out[...] = aux_t[:MLA_ROPE, :]
    ckv_b = c_kv.astype(BF16)
    kin = jnp.concatenate([ckv_b, aux.astype(BF16)], axis=1)
    k_ref[...] = _dot(kin, wkk_ref[...]).astype(BF16)
    v_ref[...] = _dot(ckv_b, wv_ref[...]).astype(BF16)

    dq = proj[:, C_DQ:C_DQ + 256]
    dk = proj[:, C_DK:C_DK + 256]
    dv = proj[:, C_DV:C_DV + 256]
    if not latent:
        dk_out[...] = dk.T
        dv_out[...] = dv.T
    else:
        dq = _rope(dq, cd_t[...], sda_t[...], sdb_t[...])
        dk = _rope(dk, cd_t[...], sda_t[...], sdb_t[...])
    dq_ref[...] = (dq * (DF_DIM ** -0.5 * LOG2E)).astype(BF16)
    dk_ref[...] = dk.astype(BF16)
    dv_ref[...] = dv.astype(BF16)

    lane = lax.broadcasted_iota(jnp.int32, (x.shape[0], LANES), 1)
    for h in range(ML_HEADS):
        blk = proj[:, C_MQ + (h // 2) * LANES:C_MQ + (h // 2 + 1) * LANES]
        mq_ref[h * LANES:(h + 1) * LANES, :] = (
            jnp.where((lane >> 6) == h % 2, blk, 0.0).T.astype(BF16))
    mk_ref[...] = (proj[:, C_MK:C_MK + 256] * (ML_DK ** -0.5)).astype(BF16)
    mv_ref[...] = proj[:, C_MV:C_MV + 512].T.astype(BF16)
    mo_ref[...] = proj[:, C_MO:C_MO + 512].T


def _pre_call(latent, l, x, mods, wts, tables):
    b, s, _ = x.shape
    ns = s // TM_PRE
    grid = (ns, b)

    def tok(width):
        return pl.BlockSpec((None, TM_PRE, width), lambda j, i: (i, j, 0))

    def mod(chunk):
        if latent:
            return pl.BlockSpec((None, 1, D_MODEL), lambda j, i: (l * 48 + (1 + i) * 6 + chunk, 0, 0))
        return pl.BlockSpec((None, 1, D_MODEL), lambda j, i: (l * 48 + chunk, 0, 0))

    def lw(*shape):
        nd = len(shape)
        return pl.BlockSpec((None,) + shape, lambda j, i: (l,) + (0,) * nd)

    in_specs = [tok(D_MODEL), mod(0), mod(1), lw(1, D_MODEL), lw(NP_IN, D_MODEL), lw(1, 256),
                lw(256, 512), lw(1, MLA_KV_RANK), lw(256, 512), lw(MLA_KV_RANK, 256), lw(1, LANES)]
    args = [x, mods, mods, wts["g_norm1"], wts["w_in"], wts["g_cq"], wts["w_uq"], wts["g_ckv"],
            wts["w_kk"], wts["w_v"], wts["b_gate"]]
    if latent:
        for t in tables:
            in_specs.append(pl.BlockSpec((TM_PRE, t.shape[1]), lambda j, i: (j, 0)))
            args.append(t)

    widths = [(512, BF16, False), (512, BF16, False), (256, BF16, False), (256, BF16, False),
              (256, BF16, False), (256, BF16, False), (512, BF16, True), (256, BF16, False),
              (512, BF16, True), (512, F32, True), (32, F32, True)]
    if not latent:
        widths += [(MLA_KV_RANK, F32, False), (MLA_ROPE, F32, True), (256, F32, True),
                   (256, F32, True)]
    out_shape = tuple(jax.ShapeDtypeStruct((b, w, s) if tr else (b, s, w), dt)
                      for w, dt, tr in widths)
    out_specs = tuple(pl.BlockSpec((None, w, TM_PRE), lambda j, i: (i, 0, j)) if tr else tok(w)
                      for w, _, tr in widths)
    return pl.pallas_call(
        functools.partial(_pre_kernel, latent),
        out_shape=out_shape, grid=grid, in_specs=in_specs, out_specs=out_specs,
        compiler_params=_cparams(("arbitrary", "arbitrary")),
        name="pre_lat" if latent else "pre_ctx",
    )(*args)


def _softmax_parts(s_list):
    m = functools.reduce(jnp.maximum, [jnp.max(s, axis=1, keepdims=True) for s in s_list])
    p_list = [jnp.exp2(s - m) for s in s_list]
    l = functools.reduce(jnp.add, [jnp.sum(p, axis=1, keepdims=True) for p in p_list])
    return p_list, l


def _per_batch(body, n_batched_in, nb, refs):
    for n in range(nb):
        body(*[r.at[n] for r in refs[:n_batched_in]], *refs[n_batched_in:-1], refs[-1].at[n])


def _mla_kernel(latent, nb, *refs):
    _per_batch(functools.partial(_mla_body, latent), len(refs) - 1, nb, refs)


def _mla_body(latent, *refs):
    if latent:
        q_ref, k_ref, v_ref, kc_ref, vc_ref, o_ref = refs
        segs = [(kc_ref, vc_ref), (k_ref, v_ref)]
    else:
        q_ref, k_ref, v_ref, o_ref = refs
        segs = [(k_ref, v_ref)]
    lane = lax.broadcasted_iota(jnp.int32, (TQ, LANES), 1)

    def scores(h):
        hs = slice(h * LANES, (h + 1) * LANES)
        return [_dot_nt(q_ref[:, hs], kr[:, hs]) for kr, _ in segs]

    outs = []
    s_next = scores(0)
    for h in range(MLA_HEADS):
        ps = slice((h // 2) * LANES, (h // 2 + 1) * LANES)
        s_list = s_next
        if h + 1 < MLA_HEADS:
            s_next = scores(h + 1)
        p_list, l = _softmax_parts(s_list)
        pv = functools.reduce(jnp.add, [_dot(p.astype(BF16), vr[:, ps])
                                        for p, (_, vr) in zip(p_list, segs)])
        outs.append(pv / l)
    o_ref[:, 0:LANES] = jnp.where(lane < MLA_V, outs[0], outs[1]).astype(BF16)
    o_ref[:, LANES:2 * LANES] = jnp.where(lane < MLA_V, outs[2], outs[3]).astype(BF16)


def _mla_call(latent, l, q, k, v, kctx=None, vctx=None):
    b, s, _ = q.shape
    nb = 1 if latent else NB_CTX
    grid = (b // nb, s // TQ)
    in_specs = [pl.BlockSpec((nb, TQ, 512), lambda i, j: (i, j, 0)),
                pl.BlockSpec((nb, s, 512), lambda i, j: (i, 0, 0)),
                pl.BlockSpec((nb, s, 256), lambda i, j: (i, 0, 0))]
    args = [q, k, v]
    if latent:
        t = kctx.shape[2]
        in_specs += [pl.BlockSpec((None, nb, t, 512), lambda i, j: (l, i, 0, 0)),
                     pl.BlockSpec((None, nb, t, 256), lambda i, j: (l, i, 0, 0))]
        args += [kctx, vctx]
    return pl.pallas_call(
        functools.partial(_mla_kernel, latent, nb),
        out_shape=jax.ShapeDtypeStruct((b, s, 256), BF16),
        grid=grid, in_specs=in_specs,
        out_specs=pl.BlockSpec((nb, TQ, 256), lambda i, j: (i, j, 0)),
        compiler_params=_cparams(("arbitrary", "arbitrary")),
        name="mla_lat" if latent else "mla_ctx",
    )(*args)


def _diff_kernel(latent, lam_init, nb, *refs):
    _per_batch(functools.partial(_diff_body, latent, lam_init), len(refs) - 6, nb, refs)


def _diff_body(latent, lam_init, *refs):
    if latent:
        (q_ref, k_ref, v_ref, kc_ref, vc_ref, lq1, lk1, lq2, lk2, g_ref, o_ref) = refs
        segs = [(kc_ref, vc_ref), (k_ref, v_ref)]
    else:
        (q_ref, k_ref, v_ref, lq1, lk1, lq2, lk2, g_ref, o_ref) = refs
        segs = [(k_ref, v_ref)]
    lam = (jnp.exp(jnp.sum(lq1[...] * lk1[...], axis=1, keepdims=True))
           - jnp.exp(jnp.sum(lq2[...] * lk2[...], axis=1, keepdims=True)) + lam_init)
    lane = lax.broadcasted_iota(jnp.int32, (TQ, LANES), 1)
    grp = lane >> 5
    qf = q_ref[...].astype(F32)

    def scores(u):
        h, c = u // 2, u % 2
        ps = slice((h // 2) * LANES, (h // 2 + 1) * LANES)
        qm = jnp.where(grp == 2 * (h % 2) + c, qf[:, ps], 0.0).astype(BF16)
        return [_dot_nt(qm, kr[:, ps]) for kr, _ in segs]

    outs = []
    s_next = scores(0)
    for h in range(DF_HEADS):
        ps = slice((h // 2) * LANES, (h // 2 + 1) * LANES)
        hh = h % 2
        parts = []
        for c in range(2):
            s_list = s_next
            if 2 * h + c + 1 < 2 * DF_HEADS:
                s_next = scores(2 * h + c + 1)
            parts.append(_softmax_parts(s_list))
        (p0, l0), (p1, l1) = parts
        inv0 = 1.0 / l0
        inv1 = lam / l1
        pv = functools.reduce(jnp.add, [
            _dot((a0 * inv0 - a1 * inv1).astype(BF16), vr[:, ps])
            for a0, a1, (_, vr) in zip(p0, p1, segs)])
        valid = (lane >> 6) == hh
        ms = jnp.sum(jnp.where(valid, pv * pv, 0.0), axis=1, keepdims=True) * (1.0 / (2 * DF_DIM))
        outs.append(pv * lax.rsqrt(ms + EPS) * g_ref[:, ps] * (1.0 - lam_init))
    o_ref[:, 0:LANES] = jnp.where(lane < 2 * DF_DIM, outs[0], outs[1]).astype(BF16)
    o_ref[:, LANES:2 * LANES] = jnp.where(lane < 2 * DF_DIM, outs[2], outs[3]).astype(BF16)


def _diff_call(latent, l, lam_init, dq, dk, dv, wts, cdk=None, cdv=None):
    b, s, _ = dq.shape
    nb = 1 if latent else NB_CTX
    grid = (b // nb, s // TQ)
    in_specs = [pl.BlockSpec((nb, TQ, 256), lambda i, j: (i, j, 0)),
                pl.BlockSpec((nb, s, 256), lambda i, j: (i, 0, 0)),
                pl.BlockSpec((nb, s, 256), lambda i, j: (i, 0, 0))]
    args = [dq, dk, dv]
    if latent:
        t = cdk.shape[2]
        in_specs += [pl.BlockSpec((None, nb, t, 256), lambda i, j: (l, i, 0, 0)),
                     pl.BlockSpec((None, nb, t, 256), lambda i, j: (l, i, 0, 0))]
        args += [cdk, cdv]
    for name in ("lam_q1", "lam_k1", "lam_q2", "lam_k2"):
        in_specs.append(pl.BlockSpec((None, 1, DF_DIM), lambda i, j: (l, 0, 0)))
        args.append(wts[name])
    in_specs.append(pl.BlockSpec((None, 1, 256), lambda i, j: (l, 0, 0)))
    args.append(wts["g_subln"])
    return pl.pallas_call(
        functools.partial(_diff_kernel, latent, lam_init, nb),
        out_shape=jax.ShapeDtypeStruct((b, s, 256), BF16),
        grid=grid, in_specs=in_specs,
        out_specs=pl.BlockSpec((nb, TQ, 256), lambda i, j: (i, j, 0)),
        compiler_params=_cparams(("arbitrary", "arbitrary")),
        name="diff_lat" if latent else "diff_ctx",
    )(*args)


def _log_sigmoid(x):
    return jnp.minimum(x, 0.0) - jnp.log1p(jnp.exp(-jnp.abs(x)))


def _mlstm_chunk(d, c, mq_ref, mk_ref, mv_ref, gt_ref, s_ref, m_ref, h_ref):
    L = ML_CHUNK
    rows = pl.ds(pl.multiple_of(c * L, L), L)
    s_i = lax.broadcasted_iota(jnp.int32, (L, L), 0)
    t_i = lax.broadcasted_iota(jnp.int32, (L, L), 1)
    mask = (s_i <= t_i) if d == 0 else (s_i >= t_i)
    tri = jnp.where(mask, 1.0, 0.0).astype(BF16)

    ig = gt_ref[16 * d:16 * d + 8, rows]
    lf = _log_sigmoid(gt_ref[16 * d + 8:16 * d + 16, rows])
    hi = lf.astype(BF16).astype(F32)
    r1 = lf - hi
    mid = r1.astype(BF16).astype(F32)
    parts = _dot(jnp.concatenate([hi, mid, r1 - mid], axis=0).astype(BF16), tri)
    bc = parts[0:8] + parts[8:16] + parts[16:24]
    rvec = ig - bc
    total = jnp.sum(lf, axis=1, keepdims=True)
    mm = m_ref[d]
    gvec = total + rvec
    m_new = jnp.maximum(total + mm, jnp.max(gvec, axis=1, keepdims=True))
    ws = jnp.exp(gvec - m_new).astype(BF16)
    cdec = jnp.exp(total + mm - m_new)
    m_ref[d] = m_new

    rv_t = jnp.concatenate([rvec, jnp.zeros((LANES - 8, L), F32)], axis=0).T

    s_old = [s_ref[d, pair].astype(BF16) for pair in range(2)]
    upd = []
    for h in range(ML_HEADS):
        pair = h // 2
        hs = slice(h * ML_DV, (h + 1) * ML_DV)
        qt = mq_ref[hs, rows]
        kp = mk_ref[rows, pair * LANES:(pair + 1) * LANES]
        vt = mv_ref[hs, rows]
        mmh = mm[h:h + 1, 0:1]
        rm = jnp.where(mask, rv_t[:, h:h + 1], -jnp.inf)
        a = jnp.maximum(jnp.max(rm, axis=0, keepdims=True), mmh)
        wqk = jnp.exp(rm - a) * _dot(kp, qt)
        dec = jnp.exp(mmh - a)
        qc = _dot(s_old[pair], qt)
        num = _dot(vt, wqk.astype(BF16)) + dec * qc[:ML_DV]
        den = jnp.sum(wqk, axis=0, keepdims=True) + dec * qc[ML_DV:ML_DV + 1]
        inv = 1.0 / jnp.maximum(jnp.abs(den), jnp.exp(-(a + bc[h:h + 1, :])))
        h_ref[d, hs, rows] = num * inv
        wsr = ws[h:h + 1, :]
        vaug = jnp.concatenate([vt * wsr, jnp.broadcast_to(wsr, (LANES, L))], axis=0)
        upd.append(_dot(vaug, kp))

    low = lax.broadcasted_iota(jnp.int32, (2 * LANES, LANES), 1) < ML_DK
    for pair in range(2):
        h0, h1 = 2 * pair, 2 * pair + 1
        cd = jnp.where(low, cdec[h0:h0 + 1, 0:1], cdec[h1:h1 + 1, 0:1])
        s_ref[d, pair] = cd * s_ref[d, pair] + jnp.where(low, upd[h0], upd[h1])


def _mlstm_kernel(latent, seq, *refs):
    if latent:
        (mq_ref, mk_ref, mv_ref, mo_ref, gt_ref, g_ref, s0_ref, m0_ref,
         o_ref, s_ref, m_ref, h_ref) = refs
        s_ref[...] = s0_ref[...]
        m_ref[...] = m0_ref[...]
    else:
        (mq_ref, mk_ref, mv_ref, mo_ref, gt_ref, g_ref,
         o_ref, cf_ref, nf_ref, mf_ref, s_ref, m_ref, h_ref) = refs
        s_ref[...] = jnp.zeros(s_ref.shape, F32)
        m_ref[...] = jnp.zeros(m_ref.shape, F32)
    nc = seq // ML_CHUNK

    def body(j, carry):
        _mlstm_chunk(0, j, mq_ref, mk_ref, mv_ref, gt_ref, s_ref, m_ref, h_ref)
        _mlstm_chunk(1, nc - 1 - j, mq_ref, mk_ref, mv_ref, gt_ref, s_ref, m_ref, h_ref)
        return carry

    lax.fori_loop(0, nc, body, 0, unroll=2)

    for h in range(ML_HEADS):
        hs = slice(h * ML_DV, (h + 1) * ML_DV)
        for j in range(nc):
            ts = slice(j * LANES, (j + 1) * LANES)
            hsum = h_ref[0, hs, ts] + h_ref[1, hs, ts]
            ms = jnp.sum(hsum * hsum, axis=0, keepdims=True) * (1.0 / ML_DV)
            y = hsum * lax.rsqrt(ms + EPS) * g_ref[hs, :]
            o_ref[hs, ts] = (jax.nn.sigmoid(mo_ref[hs, ts]) * y).astype(BF16)
    if not latent:
        for d in range(2):
            for pair in range(2):
                cf_ref[d, pair] = s_ref[d, pair, :ML_DV, :].T
        nf_ref[...] = s_ref[:, :, ML_DV:, :]
        mf_ref[...] = m_ref[...]


def _mlstm_call(latent, l, mqt, mk, mvt, mot, gt, wts, s0=None, m0=None):
    assert ML_CHUNK == LANES
    b, _, s = mqt.shape

    def feat(w):
        return pl.BlockSpec((None, w, s), lambda i: (i, 0, 0))

    in_specs = [feat(512), pl.BlockSpec((None, s, 256), lambda i: (i, 0, 0)), feat(512), feat(512),
                feat(32), pl.BlockSpec((None, 512, LANES), lambda i: (l, 0, 0))]
    args = [mqt, mk, mvt, mot, gt, wts["g_mnorm"]]
    s_spec_shape = (2, 2, 2 * LANES, LANES)
    m_spec_shape = (2, 8, LANES)
    scratch = [pltpu.VMEM(s_spec_shape, F32), pltpu.VMEM(m_spec_shape, F32),
               pltpu.VMEM((2, ML_HEADS * ML_DV, s), F32)]
    o_shape = jax.ShapeDtypeStruct((ML_HEADS * ML_DV, b * s), BF16)
    o_spec = pl.BlockSpec((ML_HEADS * ML_DV, s), lambda i: (0, i))
    if latent:
        in_specs += [pl.BlockSpec((None, None) + s_spec_shape, lambda i: (i, l, 0, 0, 0, 0)),
                     pl.BlockSpec((None, None) + m_spec_shape, lambda i: (i, l, 0, 0, 0))]
        args += [s0, m0]
        out_shape = o_shape
        out_specs = o_spec
    else:
        half = (2, 2, LANES, LANES)
        out_shape = (o_shape,
                     jax.ShapeDtypeStruct((b,) + half, F32),
                     jax.ShapeDtypeStruct((b,) + half, F32),
                     jax.ShapeDtypeStruct((b,) + m_spec_shape, F32))
        out_specs = (o_spec,
                     pl.BlockSpec((None,) + half, lambda i: (i, 0, 0, 0, 0)),
                     pl.BlockSpec((None,) + half, lambda i: (i, 0, 0, 0, 0)),
                     pl.BlockSpec((None,) + m_spec_shape, lambda i: (i, 0, 0, 0)))
    return pl.pallas_call(
        functools.partial(_mlstm_kernel, latent, s),
        out_shape=out_shape, grid=(b,), in_specs=in_specs, out_specs=out_specs,
        scratch_shapes=scratch,
        compiler_params=_cparams(("arbitrary",)),
        name="mlstm_lat" if latent else "mlstm_ctx",
    )(*args)


def _post_kernel(seg, final, *refs):
    (x_ref, oa_ref, ob_ref, oc_ref, wo_ref, gt1_ref, sh2_ref, sc2_ref, gt2_ref, g2_ref,
     wu_ref, cw_ref, cb_ref, wd_ref, gf_ref, o_ref, x1_ref, h2_ref, act_ref) = refs

    mix = (_dot(jnp.concatenate([oa_ref[...], ob_ref[...]], axis=1), wo_ref[:2 * 256, :])
           + lax.dot_general(oc_ref[...], wo_ref[2 * 256:, :], (((0,), (0,)), ((), ())),
                             preferred_element_type=F32))
    x1 = x_ref[...] + gt1_ref[...] * mix
    x1_ref[...] = x1
    h2 = _rms_rows(x1, g2_ref[...], D_MODEL) * (1.0 + sc2_ref[...]) + sh2_ref[...]
    h2_ref[...] = h2.astype(BF16)

    row = lax.broadcasted_iota(jnp.int32, (TM_FFN, FC), 0) & (seg - 1)
    first = row == 0
    last = row == seg - 1

    def conv(u, cs):
        prev = jnp.where(first, 0.0, pltpu.roll(u, 1, 0))
        nxt = jnp.where(last, 0.0, pltpu.roll(u, TM_FFN - 1, 0))
        return (cb_ref[:, cs] + prev * cw_ref[0:1, cs] + u * cw_ref[1:2, cs]
                + nxt * cw_ref[2:3, cs])

    for j in range(N_FC):
        vs = slice(j * FC, (j + 1) * FC)
        gs = slice(D_FF + j * FC, D_FF + (j + 1) * FC)
        val = conv(_dot(h2_ref[...], wu_ref[:, vs]), vs)
        gate = conv(_dot(h2_ref[...], wu_ref[:, gs]), gs)
        act_ref[:, vs] = (gate * jax.nn.sigmoid(gate) * val).astype(BF16)

    x2 = x1_ref[...] + gt2_ref[...] * _dot(act_ref[...], wd_ref[...])
    if final:
        x2 = _rms_rows(x2, gf_ref[...], D_MODEL)
    o_ref[...] = x2


def _post_call(latent, l, final, x, oa, ob, oc, mods, wts):
    b, s, _ = x.shape
    n = b * s
    x2d = x.reshape(n, D_MODEL)
    nt = n // TM_FFN
    tiles_per_batch = s // TM_FFN if latent else 1
    seg = GRID_W if latent else s

    def tok(w):
        return pl.BlockSpec((TM_FFN, w), lambda i: (i, 0))

    def mod(chunk):
        if latent:
            return pl.BlockSpec((None, 1, D_MODEL),
                                lambda i: (l * 48 + (1 + i // tiles_per_batch) * 6 + chunk, 0, 0))
        return pl.BlockSpec((None, 1, D_MODEL), lambda i: (l * 48 + chunk, 0, 0))

    def resident(*shape):
        nd = len(shape)
        return pl.BlockSpec((None,) + shape, lambda i: (l,) + (0,) * nd,
                            pipeline_mode=pl.Buffered(1))

    in_specs = [
        tok(D_MODEL), tok(256), tok(256),
        pl.BlockSpec((ML_HEADS * ML_DV, TM_FFN), lambda i: (0, i)),
        resident(D_MODEL, D_MODEL),
        mod(2), mod(3), mod(4), mod(5),
        resident(1, D_MODEL),
        resident(D_MODEL, 2 * D_FF), resident(CONV_W, 2 * D_FF), resident(1, 2 * D_FF),
        resident(D_FF, D_MODEL),
        pl.BlockSpec((1, D_MODEL), lambda i: (0, 0)),
    ]
    args = [x2d, oa.reshape(n, 256), ob.reshape(n, 256), oc,
            wts["w_out"], mods, mods, mods, mods, wts["g_norm2"],
            wts["w_up"], wts["conv_w"], wts["conv_b"], wts["w_down"], wts["g_final"]]
    out = pl.pallas_call(
        functools.partial(_post_kernel, seg, final),
        out_shape=jax.ShapeDtypeStruct((n, D_MODEL), F32),
        grid=(nt,), in_specs=in_specs,
        out_specs=pl.BlockSpec((TM_FFN, D_MODEL), lambda i: (i, 0)),
        scratch_shapes=[pltpu.VMEM((TM_FFN, D_MODEL), F32), pltpu.VMEM((TM_FFN, D_MODEL), BF16),
                        pltpu.VMEM((TM_FFN, D_FF), BF16)],
        compiler_params=pltpu.CompilerParams(dimension_semantics=("arbitrary",),
                                             vmem_limit_bytes=VMEM_LIMIT_FFN),
        name="post_lat" if latent else "post_ctx",
    )(*args)
    return out.reshape(b, s, D_MODEL)


W_IN_BODY = (352, 2656)


def _pack_in_kernel(w_ref, o_ref):
    tc = w_ref.shape[1]

    def rows(lo, n):
        return w_ref[lo:lo + n, :]

    def zeros(n):
        return jnp.zeros((n, tc), F32)

    o_ref[C_CQ:C_CQ + MLA_Q_RANK, :] = rows(0, MLA_Q_RANK).astype(BF16)
    o_ref[C_CQ + MLA_Q_RANK:C_CKV, :] = zeros(C_CKV - C_CQ - MLA_Q_RANK).astype(BF16)
    o_ref[C_CKV:C_AUX, :] = rows(MLA_Q_RANK, MLA_KV_RANK).astype(BF16)
    aux = [rows(MLA_Q_RANK + MLA_KV_RANK, MLA_ROPE)]
    for g in range(4):
        aux += [rows(W_IN_BODY[1] + ML_HEADS * g, ML_HEADS), zeros(8 - ML_HEADS)]
    aux.append(zeros(LANES - AUX_GATE - 32))
    o_ref[C_AUX:C_DQ, :] = jnp.concatenate(aux, axis=0).astype(BF16)
    o_ref[C_DQ:, :] = rows(W_IN_BODY[0], W_IN_BODY[1] - W_IN_BODY[0]).astype(BF16)


def _pack_in_call(w_in):
    w_in_t = jnp.swapaxes(w_in, 1, 2)
    tc = 256
    return pl.pallas_call(
        _pack_in_kernel,
        out_shape=jax.ShapeDtypeStruct((DEPTH, NP_IN, D_MODEL), BF16),
        grid=(DEPTH, D_MODEL // tc),
        in_specs=[pl.BlockSpec((None, w_in_t.shape[1], tc), lambda l, i: (l, 0, i))],
        out_specs=pl.BlockSpec((None, NP_IN, tc), lambda l, i: (l, 0, i)),
        compiler_params=_cparams(("arbitrary", "arbitrary")),
        name="pack_w_in",
    )(w_in_t)


def _pack_weights(w_in, g_cq, w_uq, g_ckv, w_ukv, b_gate, g_subln, g_mnorm, g_norm1, g_norm2,
                  w_out, w_up, conv_w, conv_b, w_down, g_final, lam_q1, lam_k1, lam_q2, lam_k2):
    def cols(a, lo, n, pad=0):
        blk = a[..., lo:lo + n]
        if pad:
            blk = jnp.pad(blk, [(0, 0)] * (a.ndim - 1) + [(0, pad)])
        return blk

    w_in_p = _pack_in_call(w_in)

    hd = MLA_NOPE + MLA_ROPE
    w_uq_p = jnp.pad(w_uq.reshape(DEPTH, MLA_Q_RANK, MLA_HEADS, hd),
                     [(0, 0), (0, 256 - MLA_Q_RANK), (0, 0), (0, LANES - hd)])
    w_uq_p = w_uq_p.reshape(DEPTH, 256, MLA_HEADS * LANES).astype(BF16)

    w_ukv4 = w_ukv.reshape(DEPTH, MLA_KV_RANK, MLA_HEADS, MLA_NOPE + MLA_V)
    w_k = jnp.pad(w_ukv4[..., :MLA_NOPE], [(0, 0), (0, 0), (0, 0), (0, LANES - MLA_NOPE)])
    w_k = w_k.reshape(DEPTH, MLA_KV_RANK, MLA_HEADS * LANES)
    j = jnp.arange(LANES)[:, None]
    cix = jnp.arange(MLA_HEADS * LANES)[None, :]
    place = ((j < MLA_ROPE) & ((cix % LANES) == MLA_NOPE + j)).astype(F32)
    w_kk = jnp.concatenate([w_k, jnp.broadcast_to(place, (DEPTH, LANES, MLA_HEADS * LANES))],
                           axis=1).astype(BF16)
    w_v = w_ukv4[..., MLA_NOPE:].reshape(DEPTH, MLA_KV_RANK, MLA_HEADS * MLA_V).astype(BF16)

    return dict(
        w_in=w_in_p, w_uq=w_uq_p, w_kk=w_kk, w_v=w_v,
        g_norm1=g_norm1[:, None, :], g_norm2=g_norm2[:, None, :],
        g_cq=jnp.pad(g_cq, [(0, 0), (0, 256 - MLA_Q_RANK)])[:, None, :],
        g_ckv=g_ckv[:, None, :],
        b_gate=jnp.pad(jnp.pad(b_gate.reshape(DEPTH, 4, ML_HEADS), [(0, 0), (0, 0), (0, 4)])
                       .reshape(DEPTH, 32), [(0, 0), (AUX_GATE, LANES - AUX_GATE - 32)])[:, None, :],
        g_subln=jnp.tile(g_subln, (1, DF_HEADS))[:, None, :],
        g_mnorm=jnp.broadcast_to(g_mnorm[:, :, None], (DEPTH, ML_HEADS * ML_DV, LANES)),
        w_out=w_out.astype(BF16), w_up=w_up.astype(BF16), w_down=w_down.astype(BF16),
        conv_w=conv_w, conv_b=conv_b[:, None, :], g_final=g_final[None, :],
        lam_q1=lam_q1[:, None, :], lam_k1=lam_k1[:, None, :],
        lam_q2=lam_q2[:, None, :], lam_k2=lam_k2[:, None, :],
    )


def _rope_tables(n_tok):
    t = np.arange(n_tok)
    row = (t // GRID_W).astype(np.float64)
    col = (t % GRID_W).astype(np.float64)
    nf = MLA_ROPE // 4
    inv = ROPE_BASE ** (-np.arange(nf, dtype=np.float64) / nf)
    ar = row[:, None] * inv[None, :]
    ac = col[:, None] * inv[None, :]
    ang = np.concatenate([ar, ar, ac, ac], axis=-1)
    quarter = (np.arange(MLA_ROPE) // nf) % 2
    cos = jnp.asarray(np.cos(ang), F32)
    sin_up = jnp.asarray(np.where(quarter == 0, -np.sin(ang), 0.0), F32)
    sin_dn = jnp.asarray(np.where(quarter == 1, np.sin(ang), 0.0), F32)
    ones = jnp.ones((n_tok, 1), F32)
    zeros = jnp.zeros((n_tok, 1), F32)

    def head_q(t32, fill):
        blk = jnp.concatenate([jnp.tile(fill, (1, MLA_NOPE)), t32, jnp.tile(fill, (1, 32))], axis=1)
        return jnp.tile(blk, (1, MLA_HEADS))

    def aux_k(t32, fill):
        return jnp.concatenate([t32, jnp.tile(fill, (1, LANES - MLA_ROPE))], axis=1)

    tq = (head_q(cos, ones), head_q(sin_up, zeros), head_q(sin_dn, zeros))
    td = tuple(jnp.tile(a, (1, 256 // DF_DIM)) for a in (cos, sin_up, sin_dn))
    tk = (aux_k(cos, ones), aux_k(sin_up, zeros), aux_k(sin_dn, zeros))
    return tq + td + tk


def kernel(x_prompt, x_sample, cache_mla_ckv, cache_mla_krope, cache_diff_k, cache_diff_v,
           state_mlstm_C, state_mlstm_n, state_mlstm_m, c, c_ctx, w_ada, b_ada, g_norm1, w_in,
           g_cq, w_uq, g_ckv, w_ukv, lam_q1, lam_k1, lam_q2, lam_k2, g_subln, b_gate, g_mnorm,
           w_out, g_norm2, w_up, conv_w, conv_b, w_down, g_final):
    bp, sp, _ = x_prompt.shape
    bl, sl, _ = x_sample.shape
    t_len = cache_mla_ckv.shape[2]

    wts = _pack_weights(w_in, g_cq, w_uq, g_ckv, w_ukv, b_gate, g_subln, g_mnorm, g_norm1, g_norm2,
                        w_out, w_up, conv_w, conv_b, w_down, g_final, lam_q1, lam_k1, lam_q2, lam_k2)
    tables = _rope_tables(sl)

    cond = jnp.concatenate([c_ctx[None, :], c, jnp.zeros((8 - 1 - bl, D_MODEL), F32)], axis=0)
    mods = _ada_call(cond, w_ada, b_ada[:, None, :])
    mods = mods.reshape(DEPTH * 8 * N_MOD, 1, D_MODEL)

    def feat_major(a):
        return jnp.transpose(a, (0, 1, 3, 4, 2)).reshape(bl, DEPTH, 256, t_len)

    kctx, vctx, cdk, cdv = _ctxkv_call(
        cache_mla_ckv, jnp.swapaxes(cache_mla_krope, 2, 3), feat_major(cache_diff_k),
        feat_major(cache_diff_v), wts["w_kk"], wts["w_v"])
    c0_t = jnp.swapaxes(state_mlstm_C.reshape(bl, DEPTH, 2, 2, LANES, ML_DV), -1, -2)
    n0_r = jnp.broadcast_to(state_mlstm_n.reshape(bl, DEPTH, 2, 2, 1, LANES),
                            (bl, DEPTH, 2, 2, LANES, LANES))
    s0 = jnp.concatenate([c0_t, n0_r], axis=-2)
    m0 = jnp.broadcast_to(jnp.pad(state_mlstm_m, [(0, 0)] * 3 + [(0, 8 - ML_HEADS)])[..., None],
                          (bl, DEPTH, 2, 8, LANES))

    xp, xs = x_prompt, x_sample
    col = [[] for _ in range(7)]
    for l in range(DEPTH):
        lam_init = 0.8 - 0.6 * math.exp(-0.3 * l)
        final = l == DEPTH - 1
        (q, k, v, dq, dk, dv, mq, mkt, mv, mo, gt, ckv_s, kr_s, dk_s, dv_s) = _pre_call(
            False, l, xp, mods, wts, None)
        o_a = _mla_call(False, l, q, k, v)
        o_b = _diff_call(False, l, lam_init, dq, dk, dv, wts)
        o_c, c_fin, n_fin, m_fin = _mlstm_call(False, l, mq, mkt, mv, mo, gt, wts)
        xp = _post_call(False, l, final, xp, o_a, o_b, o_c, mods, wts)
        col[0].append(ckv_s)
        col[1].append(kr_s)
        col[2].append(dk_s)
        col[3].append(dv_s)
        col[4].append(c_fin.reshape(bp, 2, ML_HEADS, ML_DK, ML_DV))
        col[5].append(n_fin[..., 0, :].reshape(bp, 2, ML_HEADS, ML_DK))
        col[6].append(m_fin[:, :, :ML_HEADS, 0])
        (q, k, v, dq, dk, dv, mq, mkt, mv, mo, gt) = _pre_call(True, l, xs, mods, wts, tables)
        o_a = _mla_call(True, l, q, k, v, kctx, vctx)
        o_b = _diff_call(True, l, lam_init, dq, dk, dv, wts, cdk, cdv)
        o_c = _mlstm_call(True, l, mq, mkt, mv, mo, gt, wts, s0, m0)
        xs = _post_call(True, l, final, xs, o_a, o_b, o_c, mods, wts)

    st = [jnp.stack(a, axis=1) for a in col]
    st[1] = jnp.swapaxes(st[1], 2, 3)
    for i in (2, 3):
        st[i] = jnp.transpose(st[i].reshape(bp, DEPTH, DF_HEADS, 2 * DF_DIM, sp), (0, 1, 4, 2, 3))
    return (xp, xs) + tuple(st)
```

```python
import functools
import math

import jax
import jax.numpy as jnp
import numpy as np
from jax import lax
from jax.experimental import pallas as pl
from jax.experimental.pallas import tpu as pltpu

F32 = jnp.float32
BF16 = jnp.bfloat16

D_MODEL = 1024
DEPTH = 4
GRID_W = 64
N_MOD = 6
EPS = 1e-6
ROPE_BASE = 10000.0
MLA_HEADS = 4
MLA_Q_RANK = 192
MLA_KV_RANK = 128
MLA_NOPE = 64
MLA_ROPE = 32
MLA_V = 64
DF_HEADS = 4
DF_DIM = 32
ML_HEADS = 4
ML_DK = 64
ML_DV = 128
D_FF = 2816
CONV_W = 3

LANES = 128
VMEM_LIMIT = 48 * 1024 * 1024
VMEM_LIMIT_FFN = 56 * 1024 * 1024

C_CQ, C_CKV, C_AUX, C_DQ, C_DK, C_DV, C_MQ, C_MK, C_MV, C_MO = (
    0, 256, 384, 512, 768, 1024, 1280, 1536, 1792, 2304)
NP_IN = 2816
AUX_GATE = 32

TM_PRE = 256
TQ = 256
NB_CTX = 4
ML_CHUNK = 128
TM_FFN = 512
FC = 256
N_FC = D_FF // FC

NT = (((1,), (1,)), ((), ()))
LOG2E = 1.4426950408889634


def _cparams(sem):
    return pltpu.CompilerParams(dimension_semantics=sem, vmem_limit_bytes=VMEM_LIMIT)


def _dot(a, b):
    return jnp.dot(a, b, preferred_element_type=F32)


def _dot_nt(a, b):
    return lax.dot_general(a, b, NT, preferred_element_type=F32)


def _rms_rows(x, g, n):
    ms = jnp.sum(x * x, axis=-1, keepdims=True) * (1.0 / n)
    return x * lax.rsqrt(ms + EPS) * g


def _rope(x, cos, sin_up, sin_dn):
    w = x.shape[-1]
    return x * cos + pltpu.roll(x, w - 8, 1) * sin_up + pltpu.roll(x, 8, 1) * sin_dn


def _ada_kernel(c_ref, w_ref, b_ref, o_ref):
    c = c_ref[...]
    s = (c * jax.nn.sigmoid(c)).astype(BF16)
    o_ref[...] = _dot(s, w_ref[...].astype(BF16)) + b_ref[...]


def _ada_call(cond, w_ada, b_ada):
    nt = 1024
    return pl.pallas_call(
        _ada_kernel,
        out_shape=jax.ShapeDtypeStruct((DEPTH, 8, N_MOD * D_MODEL), F32),
        grid=(DEPTH, N_MOD * D_MODEL // nt),
        in_specs=[pl.BlockSpec((8, D_MODEL), lambda l, j: (0, 0)),
                  pl.BlockSpec((None, D_MODEL, nt), lambda l, j: (l, 0, j)),
                  pl.BlockSpec((None, 1, nt), lambda l, j: (l, 0, j))],
        out_specs=pl.BlockSpec((None, 8, nt), lambda l, j: (l, 0, j)),
        compiler_params=_cparams(("arbitrary", "arbitrary")),
        name="ada_mod",
    )(cond, w_ada, b_ada)


def _ctxkv_kernel(ckv_ref, krt_ref, dkt_ref, dvt_ref, wkk_ref, wv_ref, k_ref, v_ref, dkb_ref, dvb_ref):
    t = ckv_ref.shape[0]
    ckv = ckv_ref[...].astype(BF16)
    kr = jnp.concatenate([krt_ref[...], jnp.zeros((LANES - MLA_ROPE, t), F32)], axis=0).T
    kin = jnp.concatenate([ckv, kr.astype(BF16)], axis=1)
    k_ref[...] = _dot(kin, wkk_ref[...]).astype(BF16)
    v_ref[...] = _dot(ckv, wv_ref[...]).astype(BF16)
    dkb_ref[...] = dkt_ref[...].T.astype(BF16)
    dvb_ref[...] = dvt_ref[...].T.astype(BF16)


def _ctxkv_call(cache_ckv, cache_kr_t, cache_dk_t, cache_dv_t, wkk, wv):
    b, _, t, _ = cache_ckv.shape

    def cache_t(w):
        return pl.BlockSpec((None, None, w, t), lambda l, i: (i, l, 0, 0))

    def out(w):
        return pl.BlockSpec((None, None, t, w), lambda l, i: (l, i, 0, 0))

    return pl.pallas_call(
        _ctxkv_kernel,
        out_shape=tuple(jax.ShapeDtypeStruct((DEPTH, b, t, w), BF16) for w in (512, 256, 256, 256)),
        grid=(DEPTH, b),
        in_specs=[pl.BlockSpec((None, None, t, MLA_KV_RANK), lambda l, i: (i, l, 0, 0)),
                  cache_t(MLA_ROPE), cache_t(256), cache_t(256),
                  pl.BlockSpec((None, 256, 512), lambda l, i: (l, 0, 0)),
                  pl.BlockSpec((None, MLA_KV_RANK, 256), lambda l, i: (l, 0, 0))],
        out_specs=(out(512), out(256), out(256), out(256)),
        compiler_params=_cparams(("arbitrary", "arbitrary")),
        name="ctx_kv",
    )(cache_ckv, cache_kr_t, cache_dk_t, cache_dv_t, wkk, wv)


def _pre_kernel(latent, *refs):
    (x_ref, sh_ref, sc_ref, g1_ref, win_ref, gcq_ref, wuq_ref, gckv_ref, wkk_ref, wv_ref,
     bg_ref) = refs[:11]
    refs = refs[11:]
    if latent:
        (cq_t, sqa_t, sqb_t, cd_t, sda_t, sdb_t, ck_t, ska_t, skb_t) = refs[:9]
        refs = refs[9:]
    (q_ref, k_ref, v_ref, dq_ref, dk_ref, dv_ref, mq_ref, mk_ref, mv_ref, mo_ref,
     gt_ref) = refs[:11]
    refs = refs[11:]
    if not latent:
        ckv_out, kr_out, dk_out, dv_out = refs

    x = x_ref[...]
    h = _rms_rows(x, g1_ref[...], D_MODEL) * (1.0 + sc_ref[...]) + sh_ref[...]
    proj = _dot_nt(h.astype(BF16), win_ref[...])

    cq = _rms_rows(proj[:, C_CQ:C_CQ + 256], gcq_ref[...], MLA_Q_RANK)
    q = _dot(cq.astype(BF16), wuq_ref[...])
    if latent:
        q = _rope(q, cq_t[...], sqa_t[...], sqb_t[...])
    q_ref[...] = (q * ((MLA_NOPE + MLA_ROPE) ** -0.5 * LOG2E)).astype(BF16)

    c_kv = _rms_rows(proj[:, C_CKV:C_CKV + MLA_KV_RANK], gckv_ref[...], MLA_KV_RANK)
    aux = proj[:, C_AUX:C_AUX + LANES] + bg_ref[...]
    if latent:
        aux = _rope(aux, ck_t[...], ska_t[...], skb_t[...])
    aux_t = aux.T
    gt_ref[...] = aux_t[AUX_GATE:AUX_GATE + 32, :]
    if not latent:
        ckv_out[...] = c_kv
        kr_out[...] = aux_t[:MLA_ROPE, :]
    ckv_b = c_kv.astype(BF16)
    kin = jnp.concatenate([ckv_b, aux.astype(BF16)], axis=1)
    k_ref[...] = _dot(kin, wkk_ref[...]).astype(BF16)
    v_ref[...] = _dot(ckv_b, wv_ref[...]).astype(BF16)

    dq = proj[:, C_DQ:C_DQ + 256]
    dk = proj[:, C_DK:C_DK + 256]
    dv = proj[:, C_DV:C_DV + 256]
    if not latent:
        dk_out[...] = dk.T
        dv_out[...] = dv.T
    else:
        dq = _rope(dq, cd_t[...], sda_t[...], sdb_t[...])
        dk = _rope(dk, cd_t[...], sda_t[...], sdb_t[...])
    dq_ref[...] = (dq * (DF_DIM ** -0.5 * LOG2E)).astype(BF16)
    dk_ref[...] = dk.astype(BF16)
    dv_ref[...] = dv.astype(BF16)

    lane = lax.broadcasted_iota(jnp.int32, (x.shape[0], LANES), 1)
    for h in range(ML_HEADS):
        blk = proj[:, C_MQ + (h // 2) * LANES:C_MQ + (h // 2 + 1) * LANES]
        mq_ref[h * LANES:(h + 1) * LANES, :] = (
            jnp.where((lane >> 6) == h % 2, blk, 0.0).T.astype(BF16))
    mk_ref[...] = (proj[:, C_MK:C_MK + 256] * (ML_DK ** -0.5)).astype(BF16)
    mv_ref[...] = proj[:, C_MV:C_MV + 512].T.astype(BF16)
    mo_ref[...] = proj[:, C_MO:C_MO + 512].T


def _pre_call(latent, l, x, mods, wts, tables):
    b, s, _ = x.shape
    ns = s // TM_PRE
    grid = (ns, b)

    def tok(width):
        return pl.BlockSpec((None, TM_PRE, width), lambda j, i: (i, j, 0))

    def mod(chunk):
        if latent:
            return pl.BlockSpec((None, 1, D_MODEL), lambda j, i: (l * 48 + (1 + i) * 6 + chunk, 0, 0))
        return pl.BlockSpec((None, 1, D_MODEL), lambda j, i: (l * 48 + chunk, 0, 0))

    def lw(*shape):
        nd = len(shape)
        return pl.BlockSpec((None,) + shape, lambda j, i: (l,) + (0,) * nd)

    in_specs = [tok(D_MODEL), mod(0), mod(1), lw(1, D_MODEL), lw(NP_IN, D_MODEL), lw(1, 256),
                lw(256, 512), lw(1, MLA_KV_RANK), lw(256, 512), lw(MLA_KV_RANK, 256), lw(1, LANES)]
    args = [x, mods, mods, wts["g_norm1"], wts["w_in"], wts["g_cq"], wts["w_uq"], wts["g_ckv"],
            wts["w_kk"], wts["w_v"], wts["b_gate"]]
    if latent:
        for t in tables:
            in_specs.append(pl.BlockSpec((TM_PRE, t.shape[1]), lambda j, i: (j, 0)))
            args.append(t)

    widths = [(512, BF16, False), (512, BF16, False), (256, BF16, False), (256, BF16, False),
              (256, BF16, False), (256, BF16, False), (512, BF16, True), (256, BF16, False),
              (512, BF16, True), (512, F32, True), (32, F32, True)]
    if not latent:
        widths += [(MLA_KV_RANK, F32, False), (MLA_ROPE, F32, True), (256, F32, True),
                   (256, F32, True)]
    out_shape = tuple(jax.ShapeDtypeStruct((b, w, s) if tr else (b, s, w), dt)
                      for w, dt, tr in widths)
    out_specs = tuple(pl.BlockSpec((None, w, TM_PRE), lambda j, i: (i, 0, j)) if tr else tok(w)
                      for w, _, tr in widths)
    return pl.pallas_call(
        functools.partial(_pre_kernel, latent),
        out_shape=out_shape, grid=grid, in_specs=in_specs, out_specs=out_specs,
        compiler_params=_cparams(("arbitrary", "arbitrary")),
        name="pre_lat" if latent else "pre_ctx",
    )(*args)


def _softmax_parts(s_list):
    m = functools.reduce(jnp.maximum, [jnp.max(s, axis=1, keepdims=True) for s in s_list])
    p_list = [jnp.exp2(s - m) for s in s_list]
    l = functools.reduce(jnp.add, [jnp.sum(p, axis=1, keepdims=True) for p in p_list])
    return p_list, l


def _attn_kernel(latent, lam_init, nb, *refs):
    n_seg = 5 if latent else 3
    o_ref = refs[-1]
    mla_in, diff_in, shared = refs[:n_seg], refs[n_seg:2 * n_seg], refs[2 * n_seg:-1]
    for n in range(nb):
        _mla_body(latent, *[r.at[n] for r in mla_in], o_ref.at[n])
        _diff_body(latent, lam_init, *[r.at[n] for r in diff_in], *shared, o_ref.at[n])


def _mla_body(latent, *refs):
    if latent:
        q_ref, k_ref, v_ref, kc_ref, vc_ref, o_ref = refs
        segs = [(kc_ref, vc_ref), (k_ref, v_ref)]
    else:
        q_ref, k_ref, v_ref, o_ref = refs
        segs = [(k_ref, v_ref)]
    lane = lax.broadcasted_iota(jnp.int32, (TQ, LANES), 1)

    def scores(h):
        hs = slice(h * LANES, (h + 1) * LANES)
        return [_dot_nt(q_ref[:, hs], kr[:, hs]) for kr, _ in segs]

    outs = []
    s_next = scores(0)
    for h in range(MLA_HEADS):
        ps = slice((h // 2) * LANES, (h // 2 + 1) * LANES)
        s_list = s_next
        if h + 1 < MLA_HEADS:
            s_next = scores(h + 1)
        p_list, l = _softmax_parts(s_list)
        pv = functools.reduce(jnp.add, [_dot(p.astype(BF16), vr[:, ps])
                                        for p, (_, vr) in zip(p_list, segs)])
        outs.append(pv / l)
    o_ref[:, 0:LANES] = jnp.where(lane < MLA_V, outs[0], outs[1]).astype(BF16)
    o_ref[:, LANES:2 * LANES] = jnp.where(lane < MLA_V, outs[2], outs[3]).astype(BF16)


def _diff_body(latent, lam_init, *refs):
    if latent:
        (q_ref, k_ref, v_ref, kc_ref, vc_ref, lq1, lk1, lq2, lk2, g_ref, o_ref) = refs
        segs = [(kc_ref, vc_ref), (k_ref, v_ref)]
    else:
        (q_ref, k_ref, v_ref, lq1, lk1, lq2, lk2, g_ref, o_ref) = refs
        segs = [(k_ref, v_ref)]
    lam = (jnp.exp(jnp.sum(lq1[...] * lk1[...], axis=1, keepdims=True))
           - jnp.exp(jnp.sum(lq2[...] * lk2[...], axis=1, keepdims=True)) + lam_init)
    lane = lax.broadcasted_iota(jnp.int32, (TQ, LANES), 1)
    grp = lane >> 5
    qf = q_ref[...].astype(F32)

    def scores(u):
        h, c = u // 2, u % 2
        ps = slice((h // 2) * LANES, (h // 2 + 1) * LANES)
        qm = jnp.where(grp == 2 * (h % 2) + c, qf[:, ps], 0.0).astype(BF16)
        return [_dot_nt(qm, kr[:, ps]) for kr, _ in segs]

    outs = []
    s_next = scores(0)
    for h in range(DF_HEADS):
        ps = slice((h // 2) * LANES, (h // 2 + 1) * LANES)
        hh = h % 2
        parts = []
        for c in range(2):
            s_list = s_next
            if 2 * h + c + 1 < 2 * DF_HEADS:
                s_next = scores(2 * h + c + 1)
            parts.append(_softmax_parts(s_list))
        (p0, l0), (p1, l1) = parts
        ratio = lam * l0 / l1
        pv = functools.reduce(jnp.add, [
            _dot((a0 - a1 * ratio).astype(BF16), vr[:, ps])
            for a0, a1, (_, vr) in zip(p0, p1, segs)]) * (1.0 / l0)
        valid = (lane >> 6) == hh
        ms = jnp.sum(jnp.where(valid, pv * pv, 0.0), axis=1, keepdims=True) * (1.0 / (2 * DF_DIM))
        outs.append(pv * lax.rsqrt(ms + EPS) * g_ref[:, ps] * (1.0 - lam_init))
    o_ref[:, 2 * LANES:3 * LANES] = jnp.where(lane < 2 * DF_DIM, outs[0], outs[1]).astype(BF16)
    o_ref[:, 3 * LANES:4 * LANES] = jnp.where(lane < 2 * DF_DIM, outs[2], outs[3]).astype(BF16)


def _attn_call(latent, l, lam_init, qkv, dqkv, wts, ctx=None, dctx=None):
    b, s, _ = qkv[0].shape
    nb = 1 if latent else NB_CTX
    grid = (b // nb, s // TQ)

    def group(wq, wk, wv, ctx_pair):
        specs = [pl.BlockSpec((nb, TQ, wq), lambda i, j: (i, j, 0)),
                 pl.BlockSpec((nb, s, wk), lambda i, j: (i, 0, 0)),
                 pl.BlockSpec((nb, s, wv), lambda i, j: (i, 0, 0))]
        if latent:
            t = ctx_pair[0].shape[2]
            specs += [pl.BlockSpec((None, nb, t, wk), lambda i, j: (l, i, 0, 0)),
                      pl.BlockSpec((None, nb, t, wv), lambda i, j: (l, i, 0, 0))]
        return specs

    in_specs = group(512, 512, 256, ctx) + group(256, 256, 256, dctx)
    args = list(qkv) + (list(ctx) if latent else []) + list(dqkv) + (list(dctx) if latent else [])
    for name in ("lam_q1", "lam_k1", "lam_q2", "lam_k2"):
        in_specs.append(pl.BlockSpec((None, 1, DF_DIM), lambda i, j: (l, 0, 0)))
        args.append(wts[name])
    in_specs.append(pl.BlockSpec((None, 1, 256), lambda i, j: (l, 0, 0)))
    args.append(wts["g_subln"])
    return pl.pallas_call(
        functools.partial(_attn_kernel, latent, lam_init, nb),
        out_shape=jax.ShapeDtypeStruct((b, s, 512), BF16),
        grid=grid, in_specs=in_specs,
        out_specs=pl.BlockSpec((nb, TQ, 512), lambda i, j: (i, j, 0)),
        compiler_params=_cparams(("arbitrary", "arbitrary")),
        name="attn_lat" if latent else "attn_ctx",
    )(*args)


def _log_sigmoid(x):
    return jnp.minimum(x, 0.0) - jnp.log1p(jnp.exp(-jnp.abs(x)))


def _mlstm_chunk(d, c, mq_ref, mk_ref, mv_ref, gt_ref, s_ref, m_ref, h_ref):
    L = ML_CHUNK
    rows = pl.ds(pl.multiple_of(c * L, L), L)
    s_i = lax.broadcasted_iota(jnp.int32, (L, L), 0)
    t_i = lax.broadcasted_iota(jnp.int32, (L, L), 1)
    mask = (s_i <= t_i) if d == 0 else (s_i >= t_i)
    tri = jnp.where(mask, 1.0, 0.0).astype(BF16)

    ig = gt_ref[16 * d:16 * d + 8, rows]
    lf = _log_sigmoid(gt_ref[16 * d + 8:16 * d + 16, rows])
    hi = lf.astype(BF16).astype(F32)
    r1 = lf - hi
    mid = r1.astype(BF16).astype(F32)
    parts = _dot(jnp.concatenate([hi, mid, r1 - mid], axis=0).astype(BF16), tri)
    bc = parts[0:8] + parts[8:16] + parts[16:24]
    rvec = ig - bc
    total = jnp.sum(lf, axis=1, keepdims=True)
    mm = m_ref[d]
    gvec = total + rvec
    m_new = jnp.maximum(total + mm, jnp.max(gvec, axis=1, keepdims=True))
    ws = jnp.exp(gvec - m_new).astype(BF16)
    cdec = jnp.exp(total + mm - m_new)
    m_ref[d] = m_new

    rv_t = jnp.concatenate([rvec, jnp.zeros((LANES - 8, L), F32)], axis=0).T

    s_old = [s_ref[d, pair].astype(BF16) for pair in range(2)]
    upd = []
    for h in range(ML_HEADS):
        pair = h // 2
        hs = slice(h * ML_DV, (h + 1) * ML_DV)
        qt = mq_ref[hs, rows]
        kp = mk_ref[rows, pair * LANES:(pair + 1) * LANES]
        vt = mv_ref[hs, rows]
        mmh = mm[h:h + 1, 0:1]
        rm = jnp.where(mask, rv_t[:, h:h + 1], -jnp.inf)
        a = jnp.maximum(jnp.max(rm, axis=0, keepdims=True), mmh)
        wqk = jnp.exp(rm - a) * _dot(kp, qt)
        dec = jnp.exp(mmh - a)
        qc = _dot(s_old[pair], qt)
        num = _dot(vt, wqk.astype(BF16)) + dec * qc[:ML_DV]
        den = jnp.sum(wqk, axis=0, keepdims=True) + dec * qc[ML_DV:ML_DV + 1]
        inv = 1.0 / jnp.maximum(jnp.abs(den), jnp.exp(-(a + bc[h:h + 1, :])))
        h_ref[d, hs, rows] = num * inv
        wsr = ws[h:h + 1, :]
        vaug = jnp.concatenate([vt * wsr, jnp.broadcast_to(wsr, (LANES, L))], axis=0)
        upd.append(_dot(vaug, kp))

    low = lax.broadcasted_iota(jnp.int32, (2 * LANES, LANES), 1) < ML_DK
    for pair in range(2):
        h0, h1 = 2 * pair, 2 * pair + 1
        cd = jnp.where(low, cdec[h0:h0 + 1, 0:1], cdec[h1:h1 + 1, 0:1])
        s_ref[d, pair] = cd * s_ref[d, pair] + jnp.where(low, upd[h0], upd[h1])


def _mlstm_kernel(latent, seq, *refs):
    if latent:
        (mq_ref, mk_ref, mv_ref, mo_ref, gt_ref, g_ref, s0_ref, m0_ref,
         o_ref, s_ref, m_ref, h_ref) = refs
        s_ref[...] = s0_ref[...]
        m_ref[...] = m0_ref[...]
    else:
        (mq_ref, mk_ref, mv_ref, mo_ref, gt_ref, g_ref,
         o_ref, cf_ref, nf_ref, mf_ref, s_ref, m_ref, h_ref) = refs
        s_ref[...] = jnp.zeros(s_ref.shape, F32)
        m_ref[...] = jnp.zeros(m_ref.shape, F32)
    nc = seq // ML_CHUNK

    def body(j, carry):
        _mlstm_chunk(0, j, mq_ref, mk_ref, mv_ref, gt_ref, s_ref, m_ref, h_ref)
        _mlstm_chunk(1, nc - 1 - j, mq_ref, mk_ref, mv_ref, gt_ref, s_ref, m_ref, h_ref)
        return carry

    lax.fori_loop(0, nc, body, 0, unroll=min(nc, 4))

    for h in range(ML_HEADS):
        hs = slice(h * ML_DV, (h + 1) * ML_DV)
        for j in range(nc):
            ts = slice(j * LANES, (j + 1) * LANES)
            hsum = h_ref[0, hs, ts] + h_ref[1, hs, ts]
            ms = jnp.sum(hsum * hsum, axis=0, keepdims=True) * (1.0 / ML_DV)
            y = hsum * lax.rsqrt(ms + EPS) * g_ref[hs, :]
            o_ref[hs, ts] = (jax.nn.sigmoid(mo_ref[hs, ts]) * y).astype(BF16)
    if not latent:
        for d in range(2):
            for pair in range(2):
                cf_ref[d, pair] = s_ref[d, pair, :ML_DV, :].T
        nf_ref[...] = s_ref[:, :, ML_DV:, :]
        mf_ref[...] = m_ref[...]


def _mlstm_call(latent, l, mqt, mk, mvt, mot, gt, wts, s0=None, m0=None):
    assert ML_CHUNK == LANES
    b, _, s = mqt.shape

    def feat(w):
        return pl.BlockSpec((None, w, s), lambda i: (i, 0, 0))

    in_specs = [feat(512), pl.BlockSpec((None, s, 256), lambda i: (i, 0, 0)), feat(512), feat(512),
                feat(32), pl.BlockSpec((None, 512, LANES), lambda i: (l, 0, 0))]
    args = [mqt, mk, mvt, mot, gt, wts["g_mnorm"]]
    s_spec_shape = (2, 2, 2 * LANES, LANES)
    m_spec_shape = (2, 8, LANES)
    scratch = [pltpu.VMEM(s_spec_shape, F32), pltpu.VMEM(m_spec_shape, F32),
               pltpu.VMEM((2, ML_HEADS * ML_DV, s), F32)]
    o_shape = jax.ShapeDtypeStruct((ML_HEADS * ML_DV, b * s), BF16)
    o_spec = pl.BlockSpec((ML_HEADS * ML_DV, s), lambda i: (0, i))
    if latent:
        in_specs += [pl.BlockSpec((None, None) + s_spec_shape, lambda i: (i, l, 0, 0, 0, 0)),
                     pl.BlockSpec((None, None) + m_spec_shape, lambda i: (i, l, 0, 0, 0))]
        args += [s0, m0]
        out_shape = o_shape
        out_specs = o_spec
    else:
        half = (2, 2, LANES, LANES)
        out_shape = (o_shape,
                     jax.ShapeDtypeStruct((b,) + half, F32),
                     jax.ShapeDtypeStruct((b,) + half, F32),
                     jax.ShapeDtypeStruct((b,) + m_spec_shape, F32))
        out_specs = (o_spec,
                     pl.BlockSpec((None,) + half, lambda i: (i, 0, 0, 0, 0)),
                     pl.BlockSpec((None,) + half, lambda i: (i, 0, 0, 0, 0)),
                     pl.BlockSpec((None,) + m_spec_shape, lambda i: (i, 0, 0, 0)))
    return pl.pallas_call(
        functools.partial(_mlstm_kernel, latent, s),
        out_shape=out_shape, grid=(b,), in_specs=in_specs, out_specs=out_specs,
        scratch_shapes=scratch,
        compiler_params=_cparams(("arbitrary",)),
        name="mlstm_lat" if latent else "mlstm_ctx",
    )(*args)


def _post_kernel(seg, final, *refs):
    (x_ref, oab_ref, oc_ref, wo_ref, gt1_ref, sh2_ref, sc2_ref, gt2_ref, g2_ref,
     wu_ref, cw_ref, cb_ref, wd_ref, gf_ref, o_ref, x1_ref, h2_ref, act_ref) = refs

    mix = (_dot(oab_ref[...], wo_ref[:2 * 256, :])
           + lax.dot_general(oc_ref[...], wo_ref[2 * 256:, :], (((0,), (0,)), ((), ())),
                             preferred_element_type=F32))
    x1 = x_ref[...] + gt1_ref[...] * mix
    x1_ref[...] = x1
    h2 = _rms_rows(x1, g2_ref[...], D_MODEL) * (1.0 + sc2_ref[...]) + sh2_ref[...]
    h2_ref[...] = h2.astype(BF16)

    row = lax.broadcasted_iota(jnp.int32, (TM_FFN, FC), 0) & (seg - 1)
    first = row == 0
    last = row == seg - 1

    def conv(u, cs):
        prev = jnp.where(first, 0.0, pltpu.roll(u, 1, 0))
        nxt = jnp.where(last, 0.0, pltpu.roll(u, TM_FFN - 1, 0))
        return (cb_ref[:, cs] + prev * cw_ref[0:1, cs] + u * cw_ref[1:2, cs]
                + nxt * cw_ref[2:3, cs])

    for j in range(N_FC):
        vs = slice(j * FC, (j + 1) * FC)
        gs = slice(D_FF + j * FC, D_FF + (j + 1) * FC)
        val = conv(_dot(h2_ref[...], wu_ref[:, vs]), vs)
        gate = conv(_dot(h2_ref[...], wu_ref[:, gs]), gs)
        act_ref[:, vs] = (gate * jax.nn.sigmoid(gate) * val).astype(BF16)

    x2 = x1_ref[...] + gt2_ref[...] * _dot(act_ref[...], wd_ref[...])
    if final:
        x2 = _rms_rows(x2, gf_ref[...], D_MODEL)
    o_ref[...] = x2


def _post_call(latent, l, final, x, oab, oc, mods, wts):
    b, s, _ = x.shape
    n = b * s
    x2d = x.reshape(n, D_MODEL)
    nt = n // TM_FFN
    tiles_per_batch = s // TM_FFN if latent else 1
    seg = GRID_W if latent else s

    def tok(w):
        return pl.BlockSpec((TM_FFN, w), lambda i: (i, 0))

    def mod(chunk):
        if latent:
            return pl.BlockSpec((None, 1, D_MODEL),
                                lambda i: (l * 48 + (1 + i // tiles_per_batch) * 6 + chunk, 0, 0))
        return pl.BlockSpec((None, 1, D_MODEL), lambda i: (l * 48 + chunk, 0, 0))

    def resident(*shape):
        nd = len(shape)
        return pl.BlockSpec((None,) + shape, lambda i: (l,) + (0,) * nd,
                            pipeline_mode=pl.Buffered(1))

    in_specs = [
        tok(D_MODEL), tok(512),
        pl.BlockSpec((ML_HEADS * ML_DV, TM_FFN), lambda i: (0, i)),
        resident(D_MODEL, D_MODEL),
        mod(2), mod(3), mod(4), mod(5),
        resident(1, D_MODEL),
        resident(D_MODEL, 2 * D_FF), resident(CONV_W, 2 * D_FF), resident(1, 2 * D_FF),
        resident(D_FF, D_MODEL),
        pl.BlockSpec((1, D_MODEL), lambda i: (0, 0)),
    ]
    args = [x2d, oab.reshape(n, 512), oc,
            wts["w_out"], mods, mods, mods, mods, wts["g_norm2"],
            wts["w_up"], wts["conv_w"], wts["conv_b"], wts["w_down"], wts["g_final"]]
    out = pl.pallas_call(
        functools.partial(_post_kernel, seg, final),
        out_shape=jax.ShapeDtypeStruct((n, D_MODEL), F32),
        grid=(nt,), in_specs=in_specs,
        out_specs=pl.BlockSpec((TM_FFN, D_MODEL), lambda i: (i, 0)),
        scratch_shapes=[pltpu.VMEM((TM_FFN, D_MODEL), F32), pltpu.VMEM((TM_FFN, D_MODEL), BF16),
                        pltpu.VMEM((TM_FFN, D_FF), BF16)],
        compiler_params=pltpu.CompilerParams(dimension_semantics=("arbitrary",),
                                             vmem_limit_bytes=VMEM_LIMIT_FFN),
        name="post_lat" if latent else "post_ctx",
    )(*args)
    return out.reshape(b, s, D_MODEL)


W_IN_BODY = (352, 2656)


def _pack_in_kernel(w_ref, o_ref):
    tc = w_ref.shape[1]

    def rows(lo, n):
        return w_ref[lo:lo + n, :]

    def zeros(n):
        return jnp.zeros((n, tc), F32)

    o_ref[C_CQ:C_CQ + MLA_Q_RANK, :] = rows(0, MLA_Q_RANK).astype(BF16)
    o_ref[C_CQ + MLA_Q_RANK:C_CKV, :] = zeros(C_CKV - C_CQ - MLA_Q_RANK).astype(BF16)
    o_ref[C_CKV:C_AUX, :] = rows(MLA_Q_RANK, MLA_KV_RANK).astype(BF16)
    aux = [rows(MLA_Q_RANK + MLA_KV_RANK, MLA_ROPE)]
    for g in range(4):
        aux += [rows(W_IN_BODY[1] + ML_HEADS * g, ML_HEADS), zeros(8 - ML_HEADS)]
    aux.append(zeros(LANES - AUX_GATE - 32))
    o_ref[C_AUX:C_DQ, :] = jnp.concatenate(aux, axis=0).astype(BF16)
    o_ref[C_DQ:, :] = rows(W_IN_BODY[0], W_IN_BODY[1] - W_IN_BODY[0]).astype(BF16)


def _pack_in_call(w_in):
    w_in_t = jnp.swapaxes(w_in, 1, 2)
    tc = 256
    return pl.pallas_call(
        _pack_in_kernel,
        out_shape=jax.ShapeDtypeStruct((DEPTH, NP_IN, D_MODEL), BF16),
        grid=(DEPTH, D_MODEL // tc),
        in_specs=[pl.BlockSpec((None, w_in_t.shape[1], tc), lambda l, i: (l, 0, i))],
        out_specs=pl.BlockSpec((None, NP_IN, tc), lambda l, i: (l, 0, i)),
        compiler_params=_cparams(("arbitrary", "arbitrary")),
        name="pack_w_in",
    )(w_in_t)


def _pack_weights(w_in, g_cq, w_uq, g_ckv, w_ukv, b_gate, g_subln, g_mnorm, g_norm1, g_norm2,
                  w_out, w_up, conv_w, conv_b, w_down, g_final, lam_q1, lam_k1, lam_q2, lam_k2):
    def cols(a, lo, n, pad=0):
        blk = a[..., lo:lo + n]
        if pad:
            blk = jnp.pad(blk, [(0, 0)] * (a.ndim - 1) + [(0, pad)])
        return blk

    w_in_p = _pack_in_call(w_in)

    hd = MLA_NOPE + MLA_ROPE
    w_uq_p = jnp.pad(w_uq.reshape(DEPTH, MLA_Q_RANK, MLA_HEADS, hd),
                     [(0, 0), (0, 256 - MLA_Q_RANK), (0, 0), (0, LANES - hd)])
    w_uq_p = w_uq_p.reshape(DEPTH, 256, MLA_HEADS * LANES).astype(BF16)

    w_ukv4 = w_ukv.reshape(DEPTH, MLA_KV_RANK, MLA_HEADS, MLA_NOPE + MLA_V)
    w_k = jnp.pad(w_ukv4[..., :MLA_NOPE], [(0, 0), (0, 0), (0, 0), (0, LANES - MLA_NOPE)])
    w_k = w_k.reshape(DEPTH, MLA_KV_RANK, MLA_HEADS * LANES)
    j = jnp.arange(LANES)[:, None]
    cix = jnp.arange(MLA_HEADS * LANES)[None, :]
    place = ((j < MLA_ROPE) & ((cix % LANES) == MLA_NOPE + j)).astype(F32)
    w_kk = jnp.concatenate([w_k, jnp.broadcast_to(place, (DEPTH, LANES, MLA_HEADS * LANES))],
                           axis=1).astype(BF16)
    w_v = w_ukv4[..., MLA_NOPE:].reshape(DEPTH, MLA_KV_RANK, MLA_HEADS * MLA_V).astype(BF16)

    return dict(
        w_in=w_in_p, w_uq=w_uq_p, w_kk=w_kk, w_v=w_v,
        g_norm1=g_norm1[:, None, :], g_norm2=g_norm2[:, None, :],
        g_cq=jnp.pad(g_cq, [(0, 0), (0, 256 - MLA_Q_RANK)])[:, None, :],
        g_ckv=g_ckv[:, None, :],
        b_gate=jnp.pad(jnp.pad(b_gate.reshape(DEPTH, 4, ML_HEADS), [(0, 0), (0, 0), (0, 4)])
                       .reshape(DEPTH, 32), [(0, 0), (AUX_GATE, LANES - AUX_GATE - 32)])[:, None, :],
        g_subln=jnp.tile(g_subln, (1, DF_HEADS))[:, None, :],
        g_mnorm=jnp.broadcast_to(g_mnorm[:, :, None], (DEPTH, ML_HEADS * ML_DV, LANES)),
        w_out=w_out.astype(BF16), w_up=w_up.astype(BF16), w_down=w_down.astype(BF16),
        conv_w=conv_w, conv_b=conv_b[:, None, :], g_final=g_final[None, :],
        lam_q1=lam_q1[:, None, :], lam_k1=lam_k1[:, None, :],
        lam_q2=lam_q2[:, None, :], lam_k2=lam_k2[:, None, :],
    )


def _rope_tables(n_tok):
    t = np.arange(n_tok)
    row = (t // GRID_W).astype(np.float64)
    col = (t % GRID_W).astype(np.float64)
    nf = MLA_ROPE // 4
    inv = ROPE_BASE ** (-np.arange(nf, dtype=np.float64) / nf)
    ar = row[:, None] * inv[None, :]
    ac = col[:, None] * inv[None, :]
    ang = np.concatenate([ar, ar, ac, ac], axis=-1)
    quarter = (np.arange(MLA_ROPE) // nf) % 2
    cos = jnp.asarray(np.cos(ang), F32)
    sin_up = jnp.asarray(np.where(quarter == 0, -np.sin(ang), 0.0), F32)
    sin_dn = jnp.asarray(np.where(quarter == 1, np.sin(ang), 0.0), F32)
    ones = jnp.ones((n_tok, 1), F32)
    zeros = jnp.zeros((n_tok, 1), F32)

    def head_q(t32, fill):
        blk = jnp.concatenate([jnp.tile(fill, (1, MLA_NOPE)), t32, jnp.tile(fill, (1, 32))], axis=1)
        return jnp.tile(blk, (1, MLA_HEADS))

    def aux_k(t32, fill):
        return jnp.concatenate([t32, jnp.tile(fill, (1, LANES - MLA_ROPE))], axis=1)

    tq = (head_q(cos, ones), head_q(sin_up, zeros), head_q(sin_dn, zeros))
    td = tuple(jnp.tile(a, (1, 256 // DF_DIM)) for a in (cos, sin_up, sin_dn))
    tk = (aux_k(cos, ones), aux_k(sin_up, zeros), aux_k(sin_dn, zeros))
    return tq + td + tk


def kernel(x_prompt, x_sample, cache_mla_ckv, cache_mla_krope, cache_diff_k, cache_diff_v,
           state_mlstm_C, state_mlstm_n, state_mlstm_m, c, c_ctx, w_ada, b_ada, g_norm1, w_in,
           g_cq, w_uq, g_ckv, w_ukv, lam_q1, lam_k1, lam_q2, lam_k2, g_subln, b_gate, g_mnorm,
           w_out, g_norm2, w_up, conv_w, conv_b, w_down, g_final):
    bp, sp, _ = x_prompt.shape
    bl, sl, _ = x_sample.shape
    t_len = cache_mla_ckv.shape[2]

    wts = _pack_weights(w_in, g_cq, w_uq, g_ckv, w_ukv, b_gate, g_subln, g_mnorm, g_norm1, g_norm2,
                        w_out, w_up, conv_w, conv_b, w_down, g_final, lam_q1, lam_k1, lam_q2, lam_k2)
    tables = _rope_tables(sl)

    cond = jnp.concatenate([c_ctx[None, :], c, jnp.zeros((8 - 1 - bl, D_MODEL), F32)], axis=0)
    mods = _ada_call(cond, w_ada, b_ada[:, None, :])
    mods = mods.reshape(DEPTH * 8 * N_MOD, 1, D_MODEL)

    def feat_major(a):
        return jnp.transpose(a, (0, 1, 3, 4, 2)).reshape(bl, DEPTH, 256, t_len)

    kctx, vctx, cdk, cdv = _ctxkv_call(
        cache_mla_ckv, jnp.swapaxes(cache_mla_krope, 2, 3), feat_major(cache_diff_k),
        feat_major(cache_diff_v), wts["w_kk"], wts["w_v"])
    c0_t = jnp.swapaxes(state_mlstm_C.reshape(bl, DEPTH, 2, 2, LANES, ML_DV), -1, -2)
    n0_r = jnp.broadcast_to(state_mlstm_n.reshape(bl, DEPTH, 2, 2, 1, LANES),
                            (bl, DEPTH, 2, 2, LANES, LANES))
    s0 = jnp.concatenate([c0_t, n0_r], axis=-2)
    m0 = jnp.broadcast_to(jnp.pad(state_mlstm_m, [(0, 0)] * 3 + [(0, 8 - ML_HEADS)])[..., None],
                          (bl, DEPTH, 2, 8, LANES))

    xp, xs = x_prompt, x_sample
    col = [[] for _ in range(7)]
    for l in range(DEPTH):
        lam_init = 0.8 - 0.6 * math.exp(-0.3 * l)
        final = l == DEPTH - 1
        (q, k, v, dq, dk, dv, mq, mkt, mv, mo, gt, ckv_s, kr_s, dk_s, dv_s) = _pre_call(
            False, l, xp, mods, wts, None)
        o_ab = _attn_call(False, l, lam_init, (q, k, v), (dq, dk, dv), wts)
        o_c, c_fin, n_fin, m_fin = _mlstm_call(False, l, mq, mkt, mv, mo, gt, wts)
        xp = _post_call(False, l, final, xp, o_ab, o_c, mods, wts)
        col[0].append(ckv_s)
        col[1].append(kr_s)
        col[2].append(dk_s)
        col[3].append(dv_s)
        col[4].append(c_fin.reshape(bp, 2, ML_HEADS, ML_DK, ML_DV))
        col[5].append(n_fin[..., 0, :].reshape(bp, 2, ML_HEADS, ML_DK))
        col[6].append(m_fin[:, :, :ML_HEADS, 0])
        (q, k, v, dq, dk, dv, mq, mkt, mv, mo, gt) = _pre_call(True, l, xs, mods, wts, tables)
        o_ab = _attn_call(True, l, lam_init, (q, k, v), (dq, dk, dv), wts, (kctx, vctx), (cdk, cdv))
        o_c = _mlstm_call(True, l, mq, mkt, mv, mo, gt, wts, s0, m0)
        xs = _post_call(True, l, final, xs, o_ab, o_c, mods, wts)

    st = [jnp.stack(a, axis=1) for a in col]
    st[1] = jnp.swapaxes(st[1], 2, 3)
    for i in (2, 3):
        st[i] = jnp.transpose(st[i].reshape(bp, DEPTH, DF_HEADS, 2 * DF_DIM, sp), (0, 1, 4, 2, 3))
    return (xp, xs) + tuple(st)
```

```python
import functools
import math

import jax
import jax.numpy as jnp
import numpy as np
from jax import lax
from jax.experimental import pallas as pl
from jax.experimental.pallas import tpu as pltpu

F32 = jnp.float32
BF16 = jnp.bfloat16

D_MODEL = 1024
DEPTH = 4
GRID_W = 64
N_MOD = 6
EPS = 1e-6
ROPE_BASE = 10000.0
MLA_HEADS = 4
MLA_Q_RANK = 192
MLA_KV_RANK = 128
MLA_NOPE = 64
MLA_ROPE = 32
MLA_V = 64
DF_HEADS = 4
DF_DIM = 32
ML_HEADS = 4
ML_DK = 64
ML_DV = 128
D_FF = 2816
CONV_W = 3

LANES = 128
VMEM_LIMIT = 48 * 1024 * 1024
VMEM_LIMIT_FFN = 56 * 1024 * 1024

C_CQ, C_CKV, C_AUX, C_DQ, C_DK, C_DV, C_MQ, C_MK, C_MV, C_MO = (
    0, 256, 384, 512, 768, 1024, 1280, 1536, 1792, 2304)
NP_IN = 2816
AUX_GATE = 32

TM_PRE = 256
TQ = 256
NB_CTX = 4
ML_CHUNK = 128
TM_FFN = 512
FC = 256
N_FC = D_FF // FC
N_CAST = 22
CAST_UP = 2 * D_FF // N_CAST
CAST_DOWN = D_FF // N_CAST
CAST_OUT = 128
N_CAST_OUT = D_MODEL // CAST_OUT

NT = (((1,), (1,)), ((), ()))
LOG2E = 1.4426950408889634


def _cparams(sem):
    return pltpu.CompilerParams(dimension_semantics=sem, vmem_limit_bytes=VMEM_LIMIT)


def _dot(a, b):
    return jnp.dot(a, b, preferred_element_type=F32)


def _dot_nt(a, b):
    return lax.dot_general(a, b, NT, preferred_element_type=F32)


def _rms_rows(x, g, n):
    ms = jnp.sum(x * x, axis=-1, keepdims=True) * (1.0 / n)
    return x * lax.rsqrt(ms + EPS) * g


def _rope(x, cos, sin_up, sin_dn):
    w = x.shape[-1]
    return x * cos + pltpu.roll(x, w - 8, 1) * sin_up + pltpu.roll(x, 8, 1) * sin_dn


def _ada_kernel(c_ref, w_ref, b_ref, o_ref):
    c = c_ref[...]
    s = (c * jax.nn.sigmoid(c)).astype(BF16)
    o_ref[...] = _dot(s, w_ref[...].astype(BF16)) + b_ref[...]


def _ada_call(cond, w_ada, b_ada):
    nt = 1024
    return pl.pallas_call(
        _ada_kernel,
        out_shape=jax.ShapeDtypeStruct((DEPTH, 8, N_MOD * D_MODEL), F32),
        grid=(DEPTH, N_MOD * D_MODEL // nt),
        in_specs=[pl.BlockSpec((8, D_MODEL), lambda l, j: (0, 0)),
                  pl.BlockSpec((None, D_MODEL, nt), lambda l, j: (l, 0, j)),
                  pl.BlockSpec((None, 1, nt), lambda l, j: (l, 0, j))],
        out_specs=pl.BlockSpec((None, 8, nt), lambda l, j: (l, 0, j)),
        compiler_params=_cparams(("arbitrary", "arbitrary")),
        name="ada_mod",
    )(cond, w_ada, b_ada)


def _ctxkv_kernel(ckv_ref, krt_ref, dkt_ref, dvt_ref, wkk_ref, wv_ref, k_ref, v_ref, dkb_ref, dvb_ref):
    t = ckv_ref.shape[0]
    ckv = ckv_ref[...].astype(BF16)
    kr = jnp.concatenate([krt_ref[...], jnp.zeros((LANES - MLA_ROPE, t), F32)], axis=0).T
    kin = jnp.concatenate([ckv, kr.astype(BF16)], axis=1)
    k_ref[...] = _dot(kin, wkk_ref[...]).astype(BF16)
    v_ref[...] = _dot(ckv, wv_ref[...]).astype(BF16)
    dkb_ref[...] = dkt_ref[...].T.astype(BF16)
    dvb_ref[...] = dvt_ref[...].T.astype(BF16)


def _ctxkv_call(cache_ckv, cache_kr_t, cache_dk_t, cache_dv_t, wkk, wv):
    b, _, t, _ = cache_ckv.shape

    def cache_t(w):
        return pl.BlockSpec((None, None, w, t), lambda l, i: (i, l, 0, 0))

    def out(w):
        return pl.BlockSpec((None, None, t, w), lambda l, i: (l, i, 0, 0))

    return pl.pallas_call(
        _ctxkv_kernel,
        out_shape=tuple(jax.ShapeDtypeStruct((DEPTH, b, t, w), BF16) for w in (512, 256, 256, 256)),
        grid=(DEPTH, b),
        in_specs=[pl.BlockSpec((None, None, t, MLA_KV_RANK), lambda l, i: (i, l, 0, 0)),
                  cache_t(MLA_ROPE), cache_t(256), cache_t(256),
                  pl.BlockSpec((None, 256, 512), lambda l, i: (l, 0, 0)),
                  pl.BlockSpec((None, MLA_KV_RANK, 256), lambda l, i: (l, 0, 0))],
        out_specs=(out(512), out(256), out(256), out(256)),
        compiler_params=_cparams(("arbitrary", "arbitrary")),
        name="ctx_kv",
    )(cache_ckv, cache_kr_t, cache_dk_t, cache_dv_t, wkk, wv)


def _pre_kernel(latent, *refs):
    (x_ref, sh_ref, sc_ref, g1_ref, win_ref, gcq_ref, wuq_ref, gckv_ref, wkk_ref, wv_ref,
     bg_ref) = refs[:11]
    refs = refs[11:]
    if latent:
        (cq_t, sqa_t, sqb_t, cd_t, sda_t, sdb_t, ck_t, ska_t, skb_t) = refs[:9]
        refs = refs[9:]
    (q_ref, k_ref, v_ref, dq_ref, dk_ref, dv_ref, mq_ref, mk_ref, mv_ref, mo_ref,
     gt_ref) = refs[:11]
    refs = refs[11:]
    if not latent:
        ckv_out, kr_out, dk_out, dv_out = refs

    x = x_ref[...]
    h = _rms_rows(x, g1_ref[...], D_MODEL) * (1.0 + sc_ref[...]) + sh_ref[...]
    proj = _dot_nt(h.astype(BF16), win_ref[...])

    cq = _rms_rows(proj[:, C_CQ:C_CQ + 256], gcq_ref[...], MLA_Q_RANK)
    q = _dot(cq.astype(BF16), wuq_ref[...])
    if latent:
        q = _rope(q, cq_t[...], sqa_t[...], sqb_t[...])
    q_ref[...] = (q * ((MLA_NOPE + MLA_ROPE) ** -0.5 * LOG2E)).astype(BF16)

    c_kv = _rms_rows(proj[:, C_CKV:C_CKV + MLA_KV_RANK], gckv_ref[...], MLA_KV_RANK)
    aux = proj[:, C_AUX:C_AUX + LANES] + bg_ref[...]
    if latent:
        aux = _rope(aux, ck_t[...], ska_t[...], skb_t[...])
    aux_t = aux.T
    gt_ref[...] = aux_t[AUX_GATE:AUX_GATE + 32, :]
    if not latent:
        ckv_out[...] = c_kv
        kr_out[...] = aux_t[:MLA_ROPE, :]
    ckv_b = c_kv.astype(BF16)
    kin = jnp.concatenate([ckv_b, aux.astype(BF16)], axis=1)
    k_ref[...] = _dot(kin, wkk_ref[...]).astype(BF16)
    v_ref[...] = _dot(ckv_b, wv_ref[...]).astype(BF16)

    dq = proj[:, C_DQ:C_DQ + 256]
    dk = proj[:, C_DK:C_DK + 256]
    dv = proj[:, C_DV:C_DV + 256]
    if not latent:
        dk_out[...] = dk.T
        dv_out[...] = dv.T
    else:
        dq = _rope(dq, cd_t[...], sda_t[...], sdb_t[...])
        dk = _rope(dk, cd_t[...], sda_t[...], sdb_t[...])
    dq_ref[...] = (dq * (DF_DIM ** -0.5 * LOG2E)).astype(BF16)
    dk_ref[...] = dk.astype(BF16)
    dv_ref[...] = dv.astype(BF16)

    lane = lax.broadcasted_iota(jnp.int32, (x.shape[0], LANES), 1)
    for h in range(ML_HEADS):
        blk = proj[:, C_MQ + (h // 2) * LANES:C_MQ + (h // 2 + 1) * LANES]
        mq_ref[h * LANES:(h + 1) * LANES, :] = (
            jnp.where((lane >> 6) == h % 2, blk, 0.0).T.astype(BF16))
    mk_ref[...] = (proj[:, C_MK:C_MK + 256] * (ML_DK ** -0.5)).astype(BF16)
    mv_ref[...] = proj[:, C_MV:C_MV + 512].T.astype(BF16)
    mo_ref[...] = proj[:, C_MO:C_MO + 512].T


def _pre_call(latent, l, x2d, tok_off, b, s, mods, wts, tables):
    ns = s // TM_PRE
    tile_off = tok_off // TM_PRE
    grid = (ns, b)

    def tok(width):
        return pl.BlockSpec((None, TM_PRE, width), lambda j, i: (i, j, 0))

    def mod(chunk):
        if latent:
            return pl.BlockSpec((None, 1, D_MODEL), lambda j, i: (l * 48 + (1 + i) * 6 + chunk, 0, 0))
        return pl.BlockSpec((None, 1, D_MODEL), lambda j, i: (l * 48 + chunk, 0, 0))

    def lw(*shape):
        nd = len(shape)
        return pl.BlockSpec((None,) + shape, lambda j, i: (l,) + (0,) * nd)

    x_spec = pl.BlockSpec((TM_PRE, D_MODEL), lambda j, i: (tile_off + i * ns + j, 0))
    in_specs = [x_spec, mod(0), mod(1), lw(1, D_MODEL), lw(NP_IN, D_MODEL), lw(1, 256),
                lw(256, 512), lw(1, MLA_KV_RANK), lw(256, 512), lw(MLA_KV_RANK, 256), lw(1, LANES)]
    args = [x2d, mods, mods, wts["g_norm1"], wts["w_in"], wts["g_cq"], wts["w_uq"], wts["g_ckv"],
            wts["w_kk"], wts["w_v"], wts["b_gate"]]
    if latent:
        for t in tables:
            in_specs.append(pl.BlockSpec((TM_PRE, t.shape[1]), lambda j, i: (j, 0)))
            args.append(t)

    widths = [(512, BF16, False), (512, BF16, False), (256, BF16, False), (256, BF16, False),
              (256, BF16, False), (256, BF16, False), (512, BF16, True), (256, BF16, False),
              (512, BF16, True), (512, F32, True), (32, F32, True)]
    if not latent:
        widths += [(MLA_KV_RANK, F32, False), (MLA_ROPE, F32, True), (256, F32, True),
                   (256, F32, True)]
    out_shape = tuple(jax.ShapeDtypeStruct((b, w, s) if tr else (b, s, w), dt)
                      for w, dt, tr in widths)
    out_specs = tuple(pl.BlockSpec((None, w, TM_PRE), lambda j, i: (i, 0, j)) if tr else tok(w)
                      for w, _, tr in widths)
    return pl.pallas_call(
        functools.partial(_pre_kernel, latent),
        out_shape=out_shape, grid=grid, in_specs=in_specs, out_specs=out_specs,
        compiler_params=_cparams(("arbitrary", "arbitrary")),
        name="pre_lat" if latent else "pre_ctx",
    )(*args)


def _softmax_parts(s_list):
    m = functools.reduce(jnp.maximum, [jnp.max(s, axis=1, keepdims=True) for s in s_list])
    p_list = [jnp.exp2(s - m) for s in s_list]
    l = functools.reduce(jnp.add, [jnp.sum(p, axis=1, keepdims=True) for p in p_list])
    return p_list, l


def _attn_kernel(latent, lam_init, nb, *refs):
    n_seg = 5 if latent else 3
    o_ref = refs[-1]
    mla_in, diff_in, shared = refs[:n_seg], refs[n_seg:2 * n_seg], refs[2 * n_seg:-1]
    for n in range(nb):
        _mla_body(latent, *[r.at[n] for r in mla_in], o_ref.at[n])
        _diff_body(latent, lam_init, *[r.at[n] for r in diff_in], *shared, o_ref.at[n])


def _mla_body(latent, *refs):
    if latent:
        q_ref, k_ref, v_ref, kc_ref, vc_ref, o_ref = refs
        segs = [(kc_ref, vc_ref), (k_ref, v_ref)]
    else:
        q_ref, k_ref, v_ref, o_ref = refs
        segs = [(k_ref, v_ref)]
    lane = lax.broadcasted_iota(jnp.int32, (TQ, LANES), 1)

    def scores(h):
        hs = slice(h * LANES, (h + 1) * LANES)
        return [_dot_nt(q_ref[:, hs], kr[:, hs]) for kr, _ in segs]

    outs = []
    s_next = scores(0)
    for h in range(MLA_HEADS):
        ps = slice((h // 2) * LANES, (h // 2 + 1) * LANES)
        s_list = s_next
        if h + 1 < MLA_HEADS:
            s_next = scores(h + 1)
        p_list, l = _softmax_parts(s_list)
        pv = functools.reduce(jnp.add, [_dot(p.astype(BF16), vr[:, ps])
                                        for p, (_, vr) in zip(p_list, segs)])
        outs.append(pv / l)
    o_ref[:, 0:LANES] = jnp.where(lane < MLA_V, outs[0], outs[1]).astype(BF16)
    o_ref[:, LANES:2 * LANES] = jnp.where(lane < MLA_V, outs[2], outs[3]).astype(BF16)


def _diff_body(latent, lam_init, *refs):
    if latent:
        (q_ref, k_ref, v_ref, kc_ref, vc_ref, lq1, lk1, lq2, lk2, g_ref, o_ref) = refs
        segs = [(kc_ref, vc_ref), (k_ref, v_ref)]
    else:
        (q_ref, k_ref, v_ref, lq1, lk1, lq2, lk2, g_ref, o_ref) = refs
        segs = [(k_ref, v_ref)]
    lam = (jnp.exp(jnp.sum(lq1[...] * lk1[...], axis=1, keepdims=True))
           - jnp.exp(jnp.sum(lq2[...] * lk2[...], axis=1, keepdims=True)) + lam_init)
    lane = lax.broadcasted_iota(jnp.int32, (TQ, LANES), 1)
    grp = lane >> 5
    qf = q_ref[...].astype(F32)

    def scores(u):
        h, c = u // 2, u % 2
        ps = slice((h // 2) * LANES, (h // 2 + 1) * LANES)
        qm = jnp.where(grp == 2 * (h % 2) + c, qf[:, ps], 0.0).astype(BF16)
        return [_dot_nt(qm, kr[:, ps]) for kr, _ in segs]

    outs = []
    s_next = scores(0)
    for h in range(DF_HEADS):
        ps = slice((h // 2) * LANES, (h // 2 + 1) * LANES)
        hh = h % 2
        parts = []
        for c in range(2):
            s_list = s_next
            if 2 * h + c + 1 < 2 * DF_HEADS:
                s_next = scores(2 * h + c + 1)
            parts.append(_softmax_parts(s_list))
        (p0, l0), (p1, l1) = parts
        ratio = lam * l0 / l1
        pv = functools.reduce(jnp.add, [
            _dot((a0 - a1 * ratio).astype(BF16), vr[:, ps])
            for a0, a1, (_, vr) in zip(p0, p1, segs)]) * (1.0 / l0)
        valid = (lane >> 6) == hh
        ms = jnp.sum(jnp.where(valid, pv * pv, 0.0), axis=1, keepdims=True) * (1.0 / (2 * DF_DIM))
        outs.append(pv * lax.rsqrt(ms + EPS) * g_ref[:, ps] * (1.0 - lam_init))
    o_ref[:, 2 * LANES:3 * LANES] = jnp.where(lane < 2 * DF_DIM, outs[0], outs[1]).astype(BF16)
    o_ref[:, 3 * LANES:4 * LANES] = jnp.where(lane < 2 * DF_DIM, outs[2], outs[3]).astype(BF16)


def _attn_call(latent, l, lam_init, qkv, dqkv, wts, ctx=None, dctx=None):
    b, s, _ = qkv[0].shape
    nb = 1 if latent else NB_CTX
    grid = (b // nb, s // TQ)

    def group(wq, wk, wv, ctx_pair):
        specs = [pl.BlockSpec((nb, TQ, wq), lambda i, j: (i, j, 0)),
                 pl.BlockSpec((nb, s, wk), lambda i, j: (i, 0, 0)),
                 pl.BlockSpec((nb, s, wv), lambda i, j: (i, 0, 0))]
        if latent:
            t = ctx_pair[0].shape[2]
            specs += [pl.BlockSpec((None, nb, t, wk), lambda i, j: (l, i, 0, 0)),
                      pl.BlockSpec((None, nb, t, wv), lambda i, j: (l, i, 0, 0))]
        return specs

    in_specs = group(512, 512, 256, ctx) + group(256, 256, 256, dctx)
    args = list(qkv) + (list(ctx) if latent else []) + list(dqkv) + (list(dctx) if latent else [])
    for name in ("lam_q1", "lam_k1", "lam_q2", "lam_k2"):
        in_specs.append(pl.BlockSpec((None, 1, DF_DIM), lambda i, j: (l, 0, 0)))
        args.append(wts[name])
    in_specs.append(pl.BlockSpec((None, 1, 256), lambda i, j: (l, 0, 0)))
    args.append(wts["g_subln"])
    return pl.pallas_call(
        functools.partial(_attn_kernel, latent, lam_init, nb),
        out_shape=jax.ShapeDtypeStruct((b, s, 512), BF16),
        grid=grid, in_specs=in_specs,
        out_specs=pl.BlockSpec((nb, TQ, 512), lambda i, j: (i, j, 0)),
        compiler_params=_cparams(("arbitrary", "arbitrary")),
        name="attn_lat" if latent else "attn_ctx",
    )(*args)


def _log_sigmoid(x):
    return jnp.minimum(x, 0.0) - jnp.log1p(jnp.exp(-jnp.abs(x)))


def _mlstm_chunk(d, c, mq_ref, mk_ref, mv_ref, gt_ref, s_ref, m_ref, h_ref):
    L = ML_CHUNK
    rows = pl.ds(pl.multiple_of(c * L, L), L)
    s_i = lax.broadcasted_iota(jnp.int32, (L, L), 0)
    t_i = lax.broadcasted_iota(jnp.int32, (L, L), 1)
    mask = (s_i <= t_i) if d == 0 else (s_i >= t_i)
    tri = jnp.where(mask, 1.0, 0.0).astype(BF16)

    ig = gt_ref[16 * d:16 * d + 8, rows]
    lf = _log_sigmoid(gt_ref[16 * d + 8:16 * d + 16, rows])
    hi = lf.astype(BF16).astype(F32)
    r1 = lf - hi
    mid = r1.astype(BF16).astype(F32)
    parts = _dot(jnp.concatenate([hi, mid, r1 - mid], axis=0).astype(BF16), tri)
    bc = parts[0:8] + parts[8:16] + parts[16:24]
    rvec = ig - bc
    total = jnp.sum(lf, axis=1, keepdims=True)
    mm = m_ref[d]
    gvec = total + rvec
    m_new = jnp.maximum(total + mm, jnp.max(gvec, axis=1, keepdims=True))
    ws = jnp.exp(gvec - m_new).astype(BF16)
    cdec = jnp.exp(total + mm - m_new)
    m_ref[d] = m_new

    rv_t = jnp.concatenate([rvec, jnp.zeros((LANES - 8, L), F32)], axis=0).T

    s_old = [s_ref[d, pair].astype(BF16) for pair in range(2)]
    upd = []
    for h in range(ML_HEADS):
        pair = h // 2
        hs = slice(h * ML_DV, (h + 1) * ML_DV)
        qt = mq_ref[hs, rows]
        kp = mk_ref[rows, pair * LANES:(pair + 1) * LANES]
        vt = mv_ref[hs, rows]
        mmh = mm[h:h + 1, 0:1]
        rm = jnp.where(mask, rv_t[:, h:h + 1], -jnp.inf)
        a = jnp.maximum(jnp.max(rm, axis=0, keepdims=True), mmh)
        wqk = jnp.exp(rm - a) * _dot(kp, qt)
        dec = jnp.exp(mmh - a)
        qc = _dot(s_old[pair], qt)
        num = _dot(vt, wqk.astype(BF16)) + dec * qc[:ML_DV]
        den = jnp.sum(wqk, axis=0, keepdims=True) + dec * qc[ML_DV:ML_DV + 1]
        inv = 1.0 / jnp.maximum(jnp.abs(den), jnp.exp(-(a + bc[h:h + 1, :])))
        h_ref[d, hs, rows] = num * inv
        wsr = ws[h:h + 1, :]
        vaug = jnp.concatenate([vt * wsr, jnp.broadcast_to(wsr, (LANES, L))], axis=0)
        upd.append(_dot(vaug, kp))

    low = lax.broadcasted_iota(jnp.int32, (2 * LANES, LANES), 1) < ML_DK
    for pair in range(2):
        h0, h1 = 2 * pair, 2 * pair + 1
        cd = jnp.where(low, cdec[h0:h0 + 1, 0:1], cdec[h1:h1 + 1, 0:1])
        s_ref[d, pair] = cd * s_ref[d, pair] + jnp.where(low, upd[h0], upd[h1])


def _mlstm_kernel(latent, seq, *refs):
    if latent:
        (mq_ref, mk_ref, mv_ref, mo_ref, gt_ref, g_ref, s0_ref, m0_ref,
         o_ref, s_ref, m_ref, h_ref) = refs
        s_ref[...] = s0_ref[...]
        m_ref[...] = m0_ref[...]
    else:
        (mq_ref, mk_ref, mv_ref, mo_ref, gt_ref, g_ref,
         o_ref, cf_ref, nf_ref, mf_ref, s_ref, m_ref, h_ref) = refs
        s_ref[...] = jnp.zeros(s_ref.shape, F32)
        m_ref[...] = jnp.zeros(m_ref.shape, F32)
    nc = seq // ML_CHUNK

    def body(j, carry):
        _mlstm_chunk(0, j, mq_ref, mk_ref, mv_ref, gt_ref, s_ref, m_ref, h_ref)
        _mlstm_chunk(1, nc - 1 - j, mq_ref, mk_ref, mv_ref, gt_ref, s_ref, m_ref, h_ref)
        return carry

    lax.fori_loop(0, nc, body, 0, unroll=min(nc, 4))

    for h in range(ML_HEADS):
        hs = slice(h * ML_DV, (h + 1) * ML_DV)
        for j in range(nc):
            ts = slice(j * LANES, (j + 1) * LANES)
            hsum = h_ref[0, hs, ts] + h_ref[1, hs, ts]
            ms = jnp.sum(hsum * hsum, axis=0, keepdims=True) * (1.0 / ML_DV)
            y = hsum * lax.rsqrt(ms + EPS) * g_ref[hs, :]
            o_ref[hs, ts] = (jax.nn.sigmoid(mo_ref[hs, ts]) * y).astype(BF16)
    if not latent:
        for d in range(2):
            for pair in range(2):
                cf_ref[d, pair] = s_ref[d, pair, :ML_DV, :].T
        nf_ref[...] = s_ref[:, :, ML_DV:, :]
        mf_ref[...] = m_ref[...]


def _mlstm_call(latent, l, mqt, mk, mvt, mot, gt, wts, s0=None, m0=None):
    assert ML_CHUNK == LANES
    b, _, s = mqt.shape

    def feat(w):
        return pl.BlockSpec((None, w, s), lambda i: (i, 0, 0))

    in_specs = [feat(512), pl.BlockSpec((None, s, 256), lambda i: (i, 0, 0)), feat(512), feat(512),
                feat(32), pl.BlockSpec((None, 512, LANES), lambda i: (l, 0, 0))]
    args = [mqt, mk, mvt, mot, gt, wts["g_mnorm"]]
    s_spec_shape = (2, 2, 2 * LANES, LANES)
    m_spec_shape = (2, 8, LANES)
    scratch = [pltpu.VMEM(s_spec_shape, F32), pltpu.VMEM(m_spec_shape, F32),
               pltpu.VMEM((2, ML_HEADS * ML_DV, s), F32)]
    o_shape = jax.ShapeDtypeStruct((ML_HEADS * ML_DV, b * s), BF16)
    o_spec = pl.BlockSpec((ML_HEADS * ML_DV, s), lambda i: (0, i))
    if latent:
        in_specs += [pl.BlockSpec((None, None) + s_spec_shape, lambda i: (i, l, 0, 0, 0, 0)),
                     pl.BlockSpec((None, None) + m_spec_shape, lambda i: (i, l, 0, 0, 0))]
        args += [s0, m0]
        out_shape = o_shape
        out_specs = o_spec
    else:
        half = (2, 2, LANES, LANES)
        out_shape = (o_shape,
                     jax.ShapeDtypeStruct((b,) + half, F32),
                     jax.ShapeDtypeStruct((b,) + half, F32),
                     jax.ShapeDtypeStruct((b,) + m_spec_shape, F32))
        out_specs = (o_spec,
                     pl.BlockSpec((None,) + half, lambda i: (i, 0, 0, 0, 0)),
                     pl.BlockSpec((None,) + half, lambda i: (i, 0, 0, 0, 0)),
                     pl.BlockSpec((None,) + m_spec_shape, lambda i: (i, 0, 0, 0)))
    return pl.pallas_call(
        functools.partial(_mlstm_kernel, latent, s),
        out_shape=out_shape, grid=(b,), in_specs=in_specs, out_specs=out_specs,
        scratch_shapes=scratch,
        compiler_params=_cparams(("arbitrary",)),
        name="mlstm_lat" if latent else "mlstm_ctx",
    )(*args)


def _post_kernel(nt_ctx, seg_ctx, split_x, final, *refs):
    n_x = 2 if split_x else 1
    n_o = 2 if final else 1
    x_refs, refs = refs[:n_x], refs[n_x:]
    (oab_c, oab_l, oc_c, oc_l, wo_ref, wu_ref, wd_ref, gt1_ref, sh2_ref, sc2_ref, gt2_ref, g2_ref,
     cw_ref, cb_ref, gf_ref) = refs[:15]
    o_refs = refs[15:15 + n_o]
    wo_s, wu_s, wd_s, x1_ref, h2_ref, act_ref = refs[15 + n_o:]
    i = pl.program_id(0)

    @pl.when(i < N_CAST)
    def _():
        wu_s[:, pl.ds(pl.multiple_of(i * CAST_UP, CAST_UP), CAST_UP)] = wu_ref[...].astype(BF16)
        wd_s[pl.ds(pl.multiple_of(i * CAST_DOWN, CAST_DOWN), CAST_DOWN), :] = wd_ref[...].astype(BF16)

    @pl.when(i < N_CAST_OUT)
    def _():
        wo_s[pl.ds(pl.multiple_of(i * CAST_OUT, CAST_OUT), CAST_OUT), :] = wo_ref[...].astype(BF16)

    @pl.when(i >= N_CAST)
    def _():
        t = i - N_CAST
        is_ctx = t < nt_ctx

        def pick(a_ref, b_ref):
            return jnp.where(is_ctx, a_ref[...], b_ref[...])

        x = pick(*x_refs) if split_x else x_refs[0][...]
        mix = (_dot(pick(oab_c, oab_l), wo_s[:2 * 256, :])
               + lax.dot_general(pick(oc_c, oc_l), wo_s[2 * 256:, :], (((0,), (0,)), ((), ())),
                                 preferred_element_type=F32))
        x1 = x + gt1_ref[...] * mix
        x1_ref[...] = x1
        h2 = _rms_rows(x1, g2_ref[...], D_MODEL) * (1.0 + sc2_ref[...]) + sh2_ref[...]
        h2_ref[...] = h2.astype(BF16)

        seg = jnp.where(is_ctx, seg_ctx, GRID_W)
        row = lax.broadcasted_iota(jnp.int32, (TM_FFN, FC), 0)
        first_w = (row & (GRID_W - 1)) == 0
        last_w = (row & (GRID_W - 1)) == GRID_W - 1
        seg_first = (row & (seg - 1)) == 0
        seg_last = (row & (seg - 1)) == seg - 1

        def conv(u, cs):
            prev = pltpu.roll(u, 1, 0)
            prev = jnp.where(first_w, jnp.where(seg_first, 0.0, prev), prev)
            nxt = pltpu.roll(u, TM_FFN - 1, 0)
            nxt = jnp.where(last_w, jnp.where(seg_last, 0.0, nxt), nxt)
            return (cb_ref[:, cs] + prev * cw_ref[0:1, cs] + u * cw_ref[1:2, cs]
                    + nxt * cw_ref[2:3, cs])

        for j in range(N_FC):
            vs = slice(j * FC, (j + 1) * FC)
            gs = slice(D_FF + j * FC, D_FF + (j + 1) * FC)
            val = conv(_dot(h2_ref[...], wu_s[:, vs]), vs)
            gate = conv(_dot(h2_ref[...], wu_s[:, gs]), gs)
            act_ref[:, vs] = (gate * jax.nn.sigmoid(gate) * val).astype(BF16)

        x2 = x1_ref[...] + gt2_ref[...] * _dot(act_ref[...], wd_s[...])
        if not final:
            o_refs[0][...] = x2
        else:
            y = _rms_rows(x2, gf_ref[...], D_MODEL)

            @pl.when(is_ctx)
            def _():
                o_refs[0][...] = y

            @pl.when(jnp.logical_not(is_ctx))
            def _():
                o_refs[1][...] = y


def _post_call(l, final, xs, n_ctx, s_ctx, s_lat, oab, oc, mods, wts):
    split_x = len(xs) == 2
    n_lat = oab[1].shape[0]
    nt_ctx, nt_lat = n_ctx // TM_FFN, n_lat // TM_FFN
    nt = nt_ctx + nt_lat
    tiles_per_batch = s_lat // TM_FFN
    assert s_ctx <= TM_FFN and TM_FFN % s_ctx == 0 and s_lat % TM_FFN == 0

    def tile(i):
        return jnp.maximum(i - N_CAST, 0)

    def ctx_t(i):
        return jnp.minimum(tile(i), nt_ctx - 1)

    def lat_t(i):
        return jnp.maximum(tile(i) - nt_ctx, 0)

    def mod(chunk):
        def index(i):
            row = jnp.where(tile(i) < nt_ctx, 0, 1 + lat_t(i) // tiles_per_batch)
            return (l * 48 + row * 6 + chunk, 0, 0)
        return pl.BlockSpec((None, 1, D_MODEL), index)

    def resident(*shape):
        nd = len(shape)
        return pl.BlockSpec((None,) + shape, lambda i: (l,) + (0,) * nd,
                            pipeline_mode=pl.Buffered(1))

    if split_x:
        x_specs = [pl.BlockSpec((TM_FFN, D_MODEL), lambda i: (ctx_t(i), 0)),
                   pl.BlockSpec((TM_FFN, D_MODEL), lambda i: (lat_t(i), 0))]
    else:
        x_specs = [pl.BlockSpec((TM_FFN, D_MODEL), lambda i: (tile(i), 0))]
    in_specs = x_specs + [
        pl.BlockSpec((TM_FFN, 512), lambda i: (ctx_t(i), 0)),
        pl.BlockSpec((TM_FFN, 512), lambda i: (lat_t(i), 0)),
        pl.BlockSpec((ML_HEADS * ML_DV, TM_FFN), lambda i: (0, ctx_t(i))),
        pl.BlockSpec((ML_HEADS * ML_DV, TM_FFN), lambda i: (0, lat_t(i))),
        pl.BlockSpec((None, CAST_OUT, D_MODEL), lambda i: (l, jnp.minimum(i, N_CAST_OUT - 1), 0)),
        pl.BlockSpec((None, D_MODEL, CAST_UP), lambda i: (l, 0, jnp.minimum(i, N_CAST - 1))),
        pl.BlockSpec((None, CAST_DOWN, D_MODEL), lambda i: (l, jnp.minimum(i, N_CAST - 1), 0)),
        mod(2), mod(3), mod(4), mod(5),
        resident(1, D_MODEL), resident(CONV_W, 2 * D_FF), resident(1, 2 * D_FF),
        pl.BlockSpec((1, D_MODEL), lambda i: (0, 0)),
    ]
    args = list(xs) + [oab[0], oab[1], oc[0], oc[1], wts["w_out"], wts["w_up"], wts["w_down"],
                       mods, mods, mods, mods, wts["g_norm2"], wts["conv_w"], wts["conv_b"],
                       wts["g_final"]]
    if final:
        out_shape = (jax.ShapeDtypeStruct((n_ctx, D_MODEL), F32),
                     jax.ShapeDtypeStruct((n_lat, D_MODEL), F32))
        out_specs = (pl.BlockSpec((TM_FFN, D_MODEL), lambda i: (ctx_t(i), 0)),
                     pl.BlockSpec((TM_FFN, D_MODEL), lambda i: (lat_t(i), 0)))
    else:
        out_shape = jax.ShapeDtypeStruct((n_ctx + n_lat, D_MODEL), F32)
        out_specs = pl.BlockSpec((TM_FFN, D_MODEL), lambda i: (tile(i), 0))
    return pl.pallas_call(
        functools.partial(_post_kernel, nt_ctx, s_ctx, split_x, final),
        out_shape=out_shape, grid=(N_CAST + nt,), in_specs=in_specs, out_specs=out_specs,
        scratch_shapes=[pltpu.VMEM((D_MODEL, D_MODEL), BF16), pltpu.VMEM((D_MODEL, 2 * D_FF), BF16),
                        pltpu.VMEM((D_FF, D_MODEL), BF16),
                        pltpu.VMEM((TM_FFN, D_MODEL), F32), pltpu.VMEM((TM_FFN, D_MODEL), BF16),
                        pltpu.VMEM((TM_FFN, D_FF), BF16)],
        compiler_params=pltpu.CompilerParams(dimension_semantics=("arbitrary",),
                                             vmem_limit_bytes=VMEM_LIMIT_FFN),
        name="post",
    )(*args)


W_IN_BODY = (352, 2656)


def _pack_in_kernel(w_ref, o_ref):
    tc = w_ref.shape[1]

    def rows(lo, n):
        return w_ref[lo:lo + n, :]

    def zeros(n):
        return jnp.zeros((n, tc), F32)

    o_ref[C_CQ:C_CQ + MLA_Q_RANK, :] = rows(0, MLA_Q_RANK).astype(BF16)
    o_ref[C_CQ + MLA_Q_RANK:C_CKV, :] = zeros(C_CKV - C_CQ - MLA_Q_RANK).astype(BF16)
    o_ref[C_CKV:C_AUX, :] = rows(MLA_Q_RANK, MLA_KV_RANK).astype(BF16)
    aux = [rows(MLA_Q_RANK + MLA_KV_RANK, MLA_ROPE)]
    for g in range(4):
        aux += [rows(W_IN_BODY[1] + ML_HEADS * g, ML_HEADS), zeros(8 - ML_HEADS)]
    aux.append(zeros(LANES - AUX_GATE - 32))
    o_ref[C_AUX:C_DQ, :] = jnp.concatenate(aux, axis=0).astype(BF16)
    o_ref[C_DQ:, :] = rows(W_IN_BODY[0], W_IN_BODY[1] - W_IN_BODY[0]).astype(BF16)


def _pack_in_call(w_in):
    w_in_t = jnp.swapaxes(w_in, 1, 2)
    tc = 256
    return pl.pallas_call(
        _pack_in_kernel,
        out_shape=jax.ShapeDtypeStruct((DEPTH, NP_IN, D_MODEL), BF16),
        grid=(DEPTH, D_MODEL // tc),
        in_specs=[pl.BlockSpec((None, w_in_t.shape[1], tc), lambda l, i: (l, 0, i))],
        out_specs=pl.BlockSpec((None, NP_IN, tc), lambda l, i: (l, 0, i)),
        compiler_params=_cparams(("arbitrary", "arbitrary")),
        name="pack_w_in",
    )(w_in_t)


def _pack_weights(w_in, g_cq, w_uq, g_ckv, w_ukv, b_gate, g_subln, g_mnorm, g_norm1, g_norm2,
                  w_out, w_up, conv_w, conv_b, w_down, g_final, lam_q1, lam_k1, lam_q2, lam_k2):
    def cols(a, lo, n, pad=0):
        blk = a[..., lo:lo + n]
        if pad:
            blk = jnp.pad(blk, [(0, 0)] * (a.ndim - 1) + [(0, pad)])
        return blk

    w_in_p = _pack_in_call(w_in)

    hd = MLA_NOPE + MLA_ROPE
    w_uq_p = jnp.pad(w_uq.reshape(DEPTH, MLA_Q_RANK, MLA_HEADS, hd),
                     [(0, 0), (0, 256 - MLA_Q_RANK), (0, 0), (0, LANES - hd)])
    w_uq_p = w_uq_p.reshape(DEPTH, 256, MLA_HEADS * LANES).astype(BF16)

    w_ukv4 = w_ukv.reshape(DEPTH, MLA_KV_RANK, MLA_HEADS, MLA_NOPE + MLA_V)
    w_k = jnp.pad(w_ukv4[..., :MLA_NOPE], [(0, 0), (0, 0), (0, 0), (0, LANES - MLA_NOPE)])
    w_k = w_k.reshape(DEPTH, MLA_KV_RANK, MLA_HEADS * LANES)
    j = jnp.arange(LANES)[:, None]
    cix = jnp.arange(MLA_HEADS * LANES)[None, :]
    place = ((j < MLA_ROPE) & ((cix % LANES) == MLA_NOPE + j)).astype(F32)
    w_kk = jnp.concatenate([w_k, jnp.broadcast_to(place, (DEPTH, LANES, MLA_HEADS * LANES))],
                           axis=1).astype(BF16)
    w_v = w_ukv4[..., MLA_NOPE:].reshape(DEPTH, MLA_KV_RANK, MLA_HEADS * MLA_V).astype(BF16)

    return dict(
        w_in=w_in_p, w_uq=w_uq_p, w_kk=w_kk, w_v=w_v,
        g_norm1=g_norm1[:, None, :], g_norm2=g_norm2[:, None, :],
        g_cq=jnp.pad(g_cq, [(0, 0), (0, 256 - MLA_Q_RANK)])[:, None, :],
        g_ckv=g_ckv[:, None, :],
        b_gate=jnp.pad(jnp.pad(b_gate.reshape(DEPTH, 4, ML_HEADS), [(0, 0), (0, 0), (0, 4)])
                       .reshape(DEPTH, 32), [(0, 0), (AUX_GATE, LANES - AUX_GATE - 32)])[:, None, :],
        g_subln=jnp.tile(g_subln, (1, DF_HEADS))[:, None, :],
        g_mnorm=jnp.broadcast_to(g_mnorm[:, :, None], (DEPTH, ML_HEADS * ML_DV, LANES)),
        w_out=w_out, w_up=w_up, w_down=w_down,
        conv_w=conv_w, conv_b=conv_b[:, None, :], g_final=g_final[None, :],
        lam_q1=lam_q1[:, None, :], lam_k1=lam_k1[:, None, :],
        lam_q2=lam_q2[:, None, :], lam_k2=lam_k2[:, None, :],
    )


def _rope_tables(n_tok):
    t = np.arange(n_tok)
    row = (t // GRID_W).astype(np.float64)
    col = (t % GRID_W).astype(np.float64)
    nf = MLA_ROPE // 4
    inv = ROPE_BASE ** (-np.arange(nf, dtype=np.float64) / nf)
    ar = row[:, None] * inv[None, :]
    ac = col[:, None] * inv[None, :]
    ang = np.concatenate([ar, ar, ac, ac], axis=-1)
    quarter = (np.arange(MLA_ROPE) // nf) % 2
    cos = jnp.asarray(np.cos(ang), F32)
    sin_up = jnp.asarray(np.where(quarter == 0, -np.sin(ang), 0.0), F32)
    sin_dn = jnp.asarray(np.where(quarter == 1, np.sin(ang), 0.0), F32)
    ones = jnp.ones((n_tok, 1), F32)
    zeros = jnp.zeros((n_tok, 1), F32)

    def head_q(t32, fill):
        blk = jnp.concatenate([jnp.tile(fill, (1, MLA_NOPE)), t32, jnp.tile(fill, (1, 32))], axis=1)
        return jnp.tile(blk, (1, MLA_HEADS))

    def aux_k(t32, fill):
        return jnp.concatenate([t32, jnp.tile(fill, (1, LANES - MLA_ROPE))], axis=1)

    tq = (head_q(cos, ones), head_q(sin_up, zeros), head_q(sin_dn, zeros))
    td = tuple(jnp.tile(a, (1, 256 // DF_DIM)) for a in (cos, sin_up, sin_dn))
    tk = (aux_k(cos, ones), aux_k(sin_up, zeros), aux_k(sin_dn, zeros))
    return tq + td + tk


def kernel(x_prompt, x_sample, cache_mla_ckv, cache_mla_krope, cache_diff_k, cache_diff_v,
           state_mlstm_C, state_mlstm_n, state_mlstm_m, c, c_ctx, w_ada, b_ada, g_norm1, w_in,
           g_cq, w_uq, g_ckv, w_ukv, lam_q1, lam_k1, lam_q2, lam_k2, g_subln, b_gate, g_mnorm,
           w_out, g_norm2, w_up, conv_w, conv_b, w_down, g_final):
    bp, sp, _ = x_prompt.shape
    bl, sl, _ = x_sample.shape
    t_len = cache_mla_ckv.shape[2]

    wts = _pack_weights(w_in, g_cq, w_uq, g_ckv, w_ukv, b_gate, g_subln, g_mnorm, g_norm1, g_norm2,
                        w_out, w_up, conv_w, conv_b, w_down, g_final, lam_q1, lam_k1, lam_q2, lam_k2)
    tables = _rope_tables(sl)

    cond = jnp.concatenate([c_ctx[None, :], c, jnp.zeros((8 - 1 - bl, D_MODEL), F32)], axis=0)
    mods = _ada_call(cond, w_ada, b_ada[:, None, :])
    mods = mods.reshape(DEPTH * 8 * N_MOD, 1, D_MODEL)

    def feat_major(a):
        return jnp.transpose(a, (0, 1, 3, 4, 2)).reshape(bl, DEPTH, 256, t_len)

    kctx, vctx, cdk, cdv = _ctxkv_call(
        cache_mla_ckv, jnp.swapaxes(cache_mla_krope, 2, 3), feat_major(cache_diff_k),
        feat_major(cache_diff_v), wts["w_kk"], wts["w_v"])
    c0_t = jnp.swapaxes(state_mlstm_C.reshape(bl, DEPTH, 2, 2, LANES, ML_DV), -1, -2)
    n0_r = jnp.broadcast_to(state_mlstm_n.reshape(bl, DEPTH, 2, 2, 1, LANES),
                            (bl, DEPTH, 2, 2, LANES, LANES))
    s0 = jnp.concatenate([c0_t, n0_r], axis=-2)
    m0 = jnp.broadcast_to(jnp.pad(state_mlstm_m, [(0, 0)] * 3 + [(0, 8 - ML_HEADS)])[..., None],
                          (bl, DEPTH, 2, 8, LANES))

    n_ctx, n_lat = bp * sp, bl * sl
    xs = (x_prompt.reshape(n_ctx, D_MODEL), x_sample.reshape(n_lat, D_MODEL))
    col = [[] for _ in range(7)]
    for l in range(DEPTH):
        lam_init = 0.8 - 0.6 * math.exp(-0.3 * l)
        final = l == DEPTH - 1
        x_ctx, x_lat, lat_off = (xs[0], xs[1], 0) if len(xs) == 2 else (xs[0], xs[0], n_ctx)
        (q, k, v, dq, dk, dv, mq, mkt, mv, mo, gt, ckv_s, kr_s, dk_s, dv_s) = _pre_call(
            False, l, x_ctx, 0, bp, sp, mods, wts, None)
        oab_c = _attn_call(False, l, lam_init, (q, k, v), (dq, dk, dv), wts)
        oc_c, c_fin, n_fin, m_fin = _mlstm_call(False, l, mq, mkt, mv, mo, gt, wts)
        col[0].append(ckv_s)
        col[1].append(kr_s)
        col[2].append(dk_s)
        col[3].append(dv_s)
        col[4].append(c_fin.reshape(bp, 2, ML_HEADS, ML_DK, ML_DV))
        col[5].append(n_fin[..., 0, :].reshape(bp, 2, ML_HEADS, ML_DK))
        col[6].append(m_fin[:, :, :ML_HEADS, 0])
        (q, k, v, dq, dk, dv, mq, mkt, mv, mo, gt) = _pre_call(
            True, l, x_lat, lat_off, bl, sl, mods, wts, tables)
        oab_l = _attn_call(True, l, lam_init, (q, k, v), (dq, dk, dv), wts, (kctx, vctx), (cdk, cdv))
        oc_l = _mlstm_call(True, l, mq, mkt, mv, mo, gt, wts, s0, m0)
        out = _post_call(l, final, xs, n_ctx, sp, sl,
                         (oab_c.reshape(n_ctx, 512), oab_l.reshape(n_lat, 512)), (oc_c, oc_l),
                         mods, wts)
        xs = out if final else (out,)

    xp = xs[0].reshape(bp, sp, D_MODEL)
    xs = xs[1].reshape(bl, sl, D_MODEL)
    st = [jnp.stack(a, axis=1) for a in col]
    st[1] = jnp.swapaxes(st[1], 2, 3)
    for i in (2, 3):
        st[i] = jnp.transpose(st[i].reshape(bp, DEPTH, DF_HEADS, 2 * DF_DIM, sp), (0, 1, 4, 2, 3))
    return (xp, xs) + tuple(st)
```

```python
import functools
import math

import jax
import jax.numpy as jnp
import numpy as np
from jax import lax
from jax.experimental import pallas as pl
from jax.experimental.pallas import tpu as pltpu

F32 = jnp.float32
BF16 = jnp.bfloat16

D_MODEL = 1024
DEPTH = 4
GRID_W = 64
N_MOD = 6
EPS = 1e-6
ROPE_BASE = 10000.0
MLA_HEADS = 4
MLA_Q_RANK = 192
MLA_KV_RANK = 128
MLA_NOPE = 64
MLA_ROPE = 32
MLA_V = 64
DF_HEADS = 4
DF_DIM = 32
ML_HEADS = 4
ML_DK = 64
ML_DV = 128
D_FF = 2816
CONV_W = 3

LANES = 128
VMEM_LIMIT = 48 * 1024 * 1024
VMEM_LIMIT_FFN = 56 * 1024 * 1024

C_CQ, C_CKV, C_AUX, C_DQ, C_DK, C_DV, C_MQ, C_MK, C_MV, C_MO = (
    0, 256, 384, 512, 768, 1024, 1280, 1536, 1792, 2304)
NP_IN = 2816
AUX_GATE = 32

TM_PRE = 256
PRE_SUB = 2
TQ = 256
NB_CTX = 4
ML_CHUNK = 128
ML_NROWS = 128
NB_ML_CTX = 4
NB_ML_LAT = 2
TM_FFN = 512
FC = 256
N_FC = D_FF // FC
N_CAST = 22
CAST_UP = 2 * D_FF // N_CAST
CAST_DOWN = D_FF // N_CAST
CAST_OUT = 128
N_CAST_OUT = D_MODEL // CAST_OUT

NT = (((1,), (1,)), ((), ()))
LOG2E = 1.4426950408889634


def _cparams(sem):
    return pltpu.CompilerParams(dimension_semantics=sem, vmem_limit_bytes=VMEM_LIMIT)


def _dot(a, b):
    return jnp.dot(a, b, preferred_element_type=F32)


def _dot_nt(a, b):
    return lax.dot_general(a, b, NT, preferred_element_type=F32)


def _rms_rows(x, g, n):
    ms = jnp.sum(x * x, axis=-1, keepdims=True) * (1.0 / n)
    return x * lax.rsqrt(ms + EPS) * g


def _rope(x, cos, sin_up, sin_dn):
    w = x.shape[-1]
    return x * cos + pltpu.roll(x, w - 8, 1) * sin_up + pltpu.roll(x, 8, 1) * sin_dn


def _ada_kernel(c_ref, w_ref, b_ref, o_ref):
    c = c_ref[...]
    s = (c * jax.nn.sigmoid(c)).astype(BF16)
    o_ref[...] = _dot(s, w_ref[...].astype(BF16)) + b_ref[...]


def _ada_call(cond, w_ada, b_ada):
    nt = 1024
    return pl.pallas_call(
        _ada_kernel,
        out_shape=jax.ShapeDtypeStruct((DEPTH, 8, N_MOD * D_MODEL), F32),
        grid=(DEPTH, N_MOD * D_MODEL // nt),
        in_specs=[pl.BlockSpec((8, D_MODEL), lambda l, j: (0, 0)),
                  pl.BlockSpec((None, D_MODEL, nt), lambda l, j: (l, 0, j)),
                  pl.BlockSpec((None, 1, nt), lambda l, j: (l, 0, j))],
        out_specs=pl.BlockSpec((None, 8, nt), lambda l, j: (l, 0, j)),
        compiler_params=_cparams(("arbitrary", "arbitrary")),
        name="ada_mod",
    )(cond, w_ada, b_ada)


def _ctxkv_kernel(ckv_ref, krt_ref, dkt_ref, dvt_ref, wkk_ref, wv_ref, k_ref, v_ref, dkb_ref, dvb_ref):
    t = ckv_ref.shape[0]
    ckv = ckv_ref[...].astype(BF16)
    kr = jnp.concatenate([krt_ref[...], jnp.zeros((LANES - MLA_ROPE, t), F32)], axis=0).T
    kin = jnp.concatenate([ckv, kr.astype(BF16)], axis=1)
    k_ref[...] = _dot(kin, wkk_ref[...]).astype(BF16)
    v_ref[...] = _dot(ckv, wv_ref[...]).astype(BF16)
    dkb_ref[...] = dkt_ref[...].T.astype(BF16)
    dvb_ref[...] = dvt_ref[...].T.astype(BF16)


def _ctxkv_call(cache_ckv, cache_kr_t, cache_dk_t, cache_dv_t, wkk, wv):
    b, _, t, _ = cache_ckv.shape

    def cache_t(w):
        return pl.BlockSpec((None, None, w, t), lambda l, i: (i, l, 0, 0))

    def out(w):
        return pl.BlockSpec((None, None, t, w), lambda l, i: (l, i, 0, 0))

    return pl.pallas_call(
        _ctxkv_kernel,
        out_shape=tuple(jax.ShapeDtypeStruct((DEPTH, b, t, w), BF16) for w in (512, 256, 256, 256)),
        grid=(DEPTH, b),
        in_specs=[pl.BlockSpec((None, None, t, MLA_KV_RANK), lambda l, i: (i, l, 0, 0)),
                  cache_t(MLA_ROPE), cache_t(256), cache_t(256),
                  pl.BlockSpec((None, 256, 512), lambda l, i: (l, 0, 0)),
                  pl.BlockSpec((None, MLA_KV_RANK, 256), lambda l, i: (l, 0, 0))],
        out_specs=(out(512), out(256), out(256), out(256)),
        compiler_params=_cparams(("arbitrary", "arbitrary")),
        name="ctx_kv",
    )(cache_ckv, cache_kr_t, cache_dk_t, cache_dv_t, wkk, wv)


def _pre_kernel(latent, kinds, *refs):
    for n in range(PRE_SUB):
        views = []
        for kind, r in zip(kinds, refs):
            if kind == "rows":
                r = r.at[n * TM_PRE:(n + 1) * TM_PRE]
            elif kind == "lanes":
                r = r.at[:, n * TM_PRE:(n + 1) * TM_PRE]
            elif kind == "batch":
                r = r.at[n]
            views.append(r)
        _pre_tile(latent, *views)


def _pre_tile(latent, *refs):
    (x_ref, sh_ref, sc_ref, g1_ref, win_ref, gcq_ref, wuq_ref, gckv_ref, wkk_ref, wv_ref,
     bg_ref) = refs[:11]
    refs = refs[11:]
    if latent:
        (cq_t, sqa_t, sqb_t, cd_t, sda_t, sdb_t, ck_t, ska_t, skb_t) = refs[:9]
        refs = refs[9:]
    (q_ref, k_ref, v_ref, dq_ref, dk_ref, dv_ref, mq_ref, mk_ref, mv_ref, mo_ref,
     gt_ref) = refs[:11]
    refs = refs[11:]
    if not latent:
        ckv_out, kr_out, dk_out, dv_out = refs

    x = x_ref[...]
    h = _rms_rows(x, g1_ref[...], D_MODEL) * (1.0 + sc_ref[...]) + sh_ref[...]
    proj = _dot_nt(h.astype(BF16), win_ref[...])

    cq = _rms_rows(proj[:, C_CQ:C_CQ + 256], gcq_ref[...], MLA_Q_RANK)
    q = _dot(cq.astype(BF16), wuq_ref[...])
    if latent:
        q = _rope(q, cq_t[...], sqa_t[...], sqb_t[...])
    q_ref[...] = (q * ((MLA_NOPE + MLA_ROPE) ** -0.5 * LOG2E)).astype(BF16)

    c_kv = _rms_rows(proj[:, C_CKV:C_CKV + MLA_KV_RANK], gckv_ref[...], MLA_KV_RANK)
    aux = proj[:, C_AUX:C_AUX + LANES] + bg_ref[...]
    if latent:
        aux = _rope(aux, ck_t[...], ska_t[...], skb_t[...])
    aux_t = aux.T
    gt_ref[...] = aux_t[AUX_GATE:AUX_GATE + 32, :]
    if not latent:
        ckv_out[...] = c_kv
        kr_out[...] = aux_t[:MLA_ROPE, :]
    ckv_b = c_kv.astype(BF16)
    kin = jnp.concatenate([ckv_b, aux.astype(BF16)], axis=1)
    k_ref[...] = _dot(kin, wkk_ref[...]).astype(BF16)
    v_ref[...] = _dot(ckv_b, wv_ref[...]).astype(BF16)

    dq = proj[:, C_DQ:C_DQ + 256]
    dk = proj[:, C_DK:C_DK + 256]
    dv = proj[:, C_DV:C_DV + 256]
    if not latent:
        dk_out[...] = dk.T
        dv_out[...] = dv.T
    else:
        dq = _rope(dq, cd_t[...], sda_t[...], sdb_t[...])
        dk = _rope(dk, cd_t[...], sda_t[...], sdb_t[...])
    dq_ref[...] = (dq * (DF_DIM ** -0.5 * LOG2E)).astype(BF16)
    dk_ref[...] = dk.astype(BF16)
    dv_ref[...] = dv.astype(BF16)

    lane = lax.broadcasted_iota(jnp.int32, (x.shape[0], LANES), 1)
    for h in range(ML_HEADS):
        blk = proj[:, C_MQ + (h // 2) * LANES:C_MQ + (h // 2 + 1) * LANES]
        mq_ref[h * LANES:(h + 1) * LANES, :] = (
            jnp.where((lane >> 6) == h % 2, blk, 0.0).T.astype(BF16))
    mk_ref[...] = (proj[:, C_MK:C_MK + 256] * (ML_DK ** -0.5)).astype(BF16)
    mv_ref[...] = proj[:, C_MV:C_MV + 512].T.astype(BF16)
    mo_ref[...] = proj[:, C_MO:C_MO + 512].T


def _pre_call(latent, l, x2d, tok_off, b, s, mods, wts, tables):
    ns = s // TM_PRE
    tile_off = tok_off // TM_PRE
    tm2 = PRE_SUB * TM_PRE
    seq_split = ns > 1
    assert (ns % PRE_SUB == 0) if seq_split else (b % PRE_SUB == 0 and ns == 1)
    assert tile_off % PRE_SUB == 0
    grid = (ns // PRE_SUB, b) if seq_split else (1, b // PRE_SUB)

    def tok(width):
        if seq_split:
            return pl.BlockSpec((None, tm2, width), lambda j, i: (i, j, 0)), "rows"
        return pl.BlockSpec((PRE_SUB, TM_PRE, width), lambda j, i: (i, j, 0)), "batch"

    def feat(width):
        if seq_split:
            return pl.BlockSpec((None, width, tm2), lambda j, i: (i, 0, j)), "lanes"
        return pl.BlockSpec((PRE_SUB, width, TM_PRE), lambda j, i: (i, 0, j)), "batch"

    def mod(chunk):
        if latent:
            return pl.BlockSpec((None, 1, D_MODEL), lambda j, i: (l * 48 + (1 + i) * 6 + chunk, 0, 0))
        return pl.BlockSpec((None, 1, D_MODEL), lambda j, i: (l * 48 + chunk, 0, 0))

    def lw(*shape):
        nd = len(shape)
        return pl.BlockSpec((None,) + shape, lambda j, i: (l,) + (0,) * nd)

    if seq_split:
        x_spec = pl.BlockSpec((tm2, D_MODEL), lambda j, i: ((tile_off + i * ns) // PRE_SUB + j, 0))
    else:
        x_spec = pl.BlockSpec((tm2, D_MODEL), lambda j, i: (tile_off // PRE_SUB + i, 0))
    in_specs = [x_spec, mod(0), mod(1), lw(1, D_MODEL), lw(NP_IN, D_MODEL), lw(1, 256),
                lw(256, 512), lw(1, MLA_KV_RANK), lw(256, 512), lw(MLA_KV_RANK, 256), lw(1, LANES)]
    kinds = ["rows"] + [None] * 10
    args = [x2d, mods, mods, wts["g_norm1"], wts["w_in"], wts["g_cq"], wts["w_uq"], wts["g_ckv"],
            wts["w_kk"], wts["w_v"], wts["b_gate"]]
    if latent:
        for t in tables:
            in_specs.append(pl.BlockSpec((tm2, t.shape[1]), lambda j, i: (j, 0)))
            kinds.append("rows")
            args.append(t)

    widths = [(512, BF16, False), (512, BF16, False), (256, BF16, False), (256, BF16, False),
              (256, BF16, False), (256, BF16, False), (512, BF16, True), (256, BF16, False),
              (512, BF16, True), (512, F32, True), (32, F32, True)]
    if not latent:
        widths += [(MLA_KV_RANK, F32, False), (MLA_ROPE, F32, True), (256, F32, True),
                   (256, F32, True)]
    out_shape = tuple(jax.ShapeDtypeStruct((b, w, s) if tr else (b, s, w), dt)
                      for w, dt, tr in widths)
    out_pairs = [feat(w) if tr else tok(w) for w, _, tr in widths]
    out_specs = tuple(p[0] for p in out_pairs)
    kinds += [p[1] for p in out_pairs]
    return pl.pallas_call(
        functools.partial(_pre_kernel, latent, tuple(kinds)),
        out_shape=out_shape, grid=grid, in_specs=in_specs, out_specs=out_specs,
        compiler_params=_cparams(("arbitrary", "arbitrary")),
        name="pre_lat" if latent else "pre_ctx",
    )(*args)


def _softmax_parts(s_list):
    m = functools.reduce(jnp.maximum, [jnp.max(s, axis=1, keepdims=True) for s in s_list])
    p_list = [jnp.exp2(s - m) for s in s_list]
    l = functools.reduce(jnp.add, [jnp.sum(p, axis=1, keepdims=True) for p in p_list])
    return p_list, l


def _attn_kernel(latent, lam_init, nb, *refs):
    n_seg = 5 if latent else 3
    o_ref = refs[-1]
    mla_in, diff_in, shared = refs[:n_seg], refs[n_seg:2 * n_seg], refs[2 * n_seg:-1]
    for n in range(nb):
        _mla_body(latent, *[r.at[n] for r in mla_in], o_ref.at[n])
        _diff_body(latent, lam_init, *[r.at[n] for r in diff_in], *shared, o_ref.at[n])


def _mla_body(latent, *refs):
    if latent:
        q_ref, k_ref, v_ref, kc_ref, vc_ref, o_ref = refs
        segs = [(kc_ref, vc_ref), (k_ref, v_ref)]
    else:
        q_ref, k_ref, v_ref, o_ref = refs
        segs = [(k_ref, v_ref)]
    lane = lax.broadcasted_iota(jnp.int32, (TQ, LANES), 1)

    def scores(h):
        hs = slice(h * LANES, (h + 1) * LANES)
        return [_dot_nt(q_ref[:, hs], kr[:, hs]) for kr, _ in segs]

    outs = []
    s_next = scores(0)
    for h in range(MLA_HEADS):
        ps = slice((h // 2) * LANES, (h // 2 + 1) * LANES)
        s_list = s_next
        if h + 1 < MLA_HEADS:
            s_next = scores(h + 1)
        p_list, l = _softmax_parts(s_list)
        pv = functools.reduce(jnp.add, [_dot(p.astype(BF16), vr[:, ps])
                                        for p, (_, vr) in zip(p_list, segs)])
        outs.append(pv / l)
    o_ref[:, 0:LANES] = jnp.where(lane < MLA_V, outs[0], outs[1]).astype(BF16)
    o_ref[:, LANES:2 * LANES] = jnp.where(lane < MLA_V, outs[2], outs[3]).astype(BF16)


def _diff_body(latent, lam_init, *refs):
    if latent:
        (q_ref, k_ref, v_ref, kc_ref, vc_ref, lq1, lk1, lq2, lk2, g_ref, o_ref) = refs
        segs = [(kc_ref, vc_ref), (k_ref, v_ref)]
    else:
        (q_ref, k_ref, v_ref, lq1, lk1, lq2, lk2, g_ref, o_ref) = refs
        segs = [(k_ref, v_ref)]
    lam = (jnp.exp(jnp.sum(lq1[...] * lk1[...], axis=1, keepdims=True))
           - jnp.exp(jnp.sum(lq2[...] * lk2[...], axis=1, keepdims=True)) + lam_init)
    lane = lax.broadcasted_iota(jnp.int32, (TQ, LANES), 1)
    grp = lane >> 5
    qf = q_ref[...].astype(F32)

    def scores(u):
        h, c = u // 2, u % 2
        ps = slice((h // 2) * LANES, (h // 2 + 1) * LANES)
        qm = jnp.where(grp == 2 * (h % 2) + c, qf[:, ps], 0.0).astype(BF16)
        return [_dot_nt(qm, kr[:, ps]) for kr, _ in segs]

    outs = []
    s_next = scores(0)
    for h in range(DF_HEADS):
        ps = slice((h // 2) * LANES, (h // 2 + 1) * LANES)
        hh = h % 2
        parts = []
        for c in range(2):
            s_list = s_next
            if 2 * h + c + 1 < 2 * DF_HEADS:
                s_next = scores(2 * h + c + 1)
            parts.append(_softmax_parts(s_list))
        (p0, l0), (p1, l1) = parts
        ratio = lam * l0 / l1
        pv = functools.reduce(jnp.add, [
            _dot((a0 - a1 * ratio).astype(BF16), vr[:, ps])
            for a0, a1, (_, vr) in zip(p0, p1, segs)]) * (1.0 / l0)
        valid = (lane >> 6) == hh
        ms = jnp.sum(jnp.where(valid, pv * pv, 0.0), axis=1, keepdims=True) * (1.0 / (2 * DF_DIM))
        outs.append(pv * lax.rsqrt(ms + EPS) * g_ref[:, ps] * (1.0 - lam_init))
    o_ref[:, 2 * LANES:3 * LANES] = jnp.where(lane < 2 * DF_DIM, outs[0], outs[1]).astype(BF16)
    o_ref[:, 3 * LANES:4 * LANES] = jnp.where(lane < 2 * DF_DIM, outs[2], outs[3]).astype(BF16)


def _attn_call(latent, l, lam_init, qkv, dqkv, wts, ctx=None, dctx=None):
    b, s, _ = qkv[0].shape
    nb = 1 if latent else NB_CTX
    grid = (b // nb, s // TQ)

    def group(wq, wk, wv, ctx_pair):
        specs = [pl.BlockSpec((nb, TQ, wq), lambda i, j: (i, j, 0)),
                 pl.BlockSpec((nb, s, wk), lambda i, j: (i, 0, 0)),
                 pl.BlockSpec((nb, s, wv), lambda i, j: (i, 0, 0))]
        if latent:
            t = ctx_pair[0].shape[2]
            specs += [pl.BlockSpec((None, nb, t, wk), lambda i, j: (l, i, 0, 0)),
                      pl.BlockSpec((None, nb, t, wv), lambda i, j: (l, i, 0, 0))]
        return specs

    in_specs = group(512, 512, 256, ctx) + group(256, 256, 256, dctx)
    args = list(qkv) + (list(ctx) if latent else []) + list(dqkv) + (list(dctx) if latent else [])
    for name in ("lam_q1", "lam_k1", "lam_q2", "lam_k2"):
        in_specs.append(pl.BlockSpec((None, 1, DF_DIM), lambda i, j: (l, 0, 0)))
        args.append(wts[name])
    in_specs.append(pl.BlockSpec((None, 1, 256), lambda i, j: (l, 0, 0)))
    args.append(wts["g_subln"])
    return pl.pallas_call(
        functools.partial(_attn_kernel, latent, lam_init, nb),
        out_shape=jax.ShapeDtypeStruct((b, s, 512), BF16),
        grid=grid, in_specs=in_specs,
        out_specs=pl.BlockSpec((nb, TQ, 512), lambda i, j: (i, j, 0)),
        compiler_params=_cparams(("arbitrary", "arbitrary")),
        name="attn_lat" if latent else "attn_ctx",
    )(*args)


def _log_sigmoid(x):
    return jnp.minimum(x, 0.0) - jnp.log1p(jnp.exp(-jnp.abs(x)))


def _mlstm_chunk(d, c, mq_ref, mk_ref, mv_ref, gt_ref, s_ref, m_ref, h_ref):
    L = ML_CHUNK
    rows = pl.ds(pl.multiple_of(c * L, L), L)
    s_i = lax.broadcasted_iota(jnp.int32, (L, L), 0)
    t_i = lax.broadcasted_iota(jnp.int32, (L, L), 1)
    mask = (s_i <= t_i) if d == 0 else (s_i >= t_i)
    tri = jnp.where(mask, 1.0, 0.0).astype(BF16)

    ig = gt_ref[16 * d:16 * d + 8, rows]
    lf = _log_sigmoid(gt_ref[16 * d + 8:16 * d + 16, rows])
    hi = lf.astype(BF16).astype(F32)
    r1 = lf - hi
    mid = r1.astype(BF16).astype(F32)
    parts = _dot(jnp.concatenate([hi, mid, r1 - mid], axis=0).astype(BF16), tri)
    bc = parts[0:8] + parts[8:16] + parts[16:24]
    rvec = ig - bc
    total = jnp.sum(lf, axis=1, keepdims=True)
    mm = m_ref[d]
    gvec = total + rvec
    m_new = jnp.maximum(total + mm, jnp.max(gvec, axis=1, keepdims=True))
    ws = jnp.exp(gvec - m_new).astype(BF16)
    cdec = jnp.exp(total + mm - m_new)
    m_ref[d] = m_new

    rv_t = jnp.concatenate([rvec, jnp.zeros((LANES - 8, L), F32)], axis=0).T

    s_old = [s_ref[d, pair].astype(BF16) for pair in range(2)]
    upd = []
    for h in range(ML_HEADS):
        pair = h // 2
        hs = slice(h * ML_DV, (h + 1) * ML_DV)
        qt = mq_ref[hs, rows]
        kp = mk_ref[rows, pair * LANES:(pair + 1) * LANES]
        vt = mv_ref[hs, rows]
        mmh = mm[h:h + 1, 0:1]
        rm = jnp.where(mask, rv_t[:, h:h + 1], -jnp.inf)
        a = jnp.maximum(jnp.max(rm, axis=0, keepdims=True), mmh)
        wqk = jnp.exp(rm - a) * _dot(kp, qt)
        dec = jnp.exp(mmh - a)
        qc = _dot(s_old[pair], qt)
        num = _dot(vt, wqk.astype(BF16)) + dec * qc[:ML_DV]
        den = jnp.sum(wqk, axis=0, keepdims=True) + dec * qc[ML_DV:ML_DV + 1]
        inv = 1.0 / jnp.maximum(jnp.abs(den), jnp.exp(-(a + bc[h:h + 1, :])))
        h_ref[d, hs, rows] = num * inv
        wsr = ws[h:h + 1, :]
        vaug = jnp.concatenate([vt * wsr, jnp.broadcast_to(wsr, (ML_NROWS, L))], axis=0)
        upd.append(_dot(vaug, kp))

    low = lax.broadcasted_iota(jnp.int32, (ML_DV + ML_NROWS, LANES), 1) < ML_DK
    for pair in range(2):
        h0, h1 = 2 * pair, 2 * pair + 1
        cd = jnp.where(low, cdec[h0:h0 + 1, 0:1], cdec[h1:h1 + 1, 0:1])
        s_ref[d, pair] = cd * s_ref[d, pair] + jnp.where(low, upd[h0], upd[h1])


def _mlstm_kernel(latent, seq, nb, *refs):
    if latent:
        (mq_ref, mk_ref, mv_ref, mo_ref, gt_ref, g_ref, s0_ref, m0_ref,
         o_ref, s_ref, m_ref, h_ref) = refs
        s_ref[...] = s0_ref[...]
        m_ref[...] = m0_ref[...]
    else:
        (mq_ref, mk_ref, mv_ref, mo_ref, gt_ref, g_ref,
         o_ref, cf_ref, nf_ref, mf_ref, s_ref, m_ref, h_ref) = refs
        s_ref[...] = jnp.zeros(s_ref.shape, F32)
        m_ref[...] = jnp.zeros(m_ref.shape, F32)
    nc = seq // ML_CHUNK

    def body(j, carry):
        for n in range(nb):
            views = (mq_ref.at[n], mk_ref.at[n], mv_ref.at[n], gt_ref.at[n], s_ref.at[n],
                     m_ref.at[n], h_ref.at[n])
            _mlstm_chunk(0, j, *views)
            _mlstm_chunk(1, nc - 1 - j, *views)
        return carry

    lax.fori_loop(0, nc, body, 0, unroll=min(nc, 4))

    for n in range(nb):
        for h in range(ML_HEADS):
            hs = slice(h * ML_DV, (h + 1) * ML_DV)
            for j in range(nc):
                ts = slice(j * LANES, (j + 1) * LANES)
                hsum = h_ref[n, 0, hs, ts] + h_ref[n, 1, hs, ts]
                ms = jnp.sum(hsum * hsum, axis=0, keepdims=True) * (1.0 / ML_DV)
                y = hsum * lax.rsqrt(ms + EPS) * g_ref[hs, :]
                o_ref[hs, n * seq + j * LANES:n * seq + (j + 1) * LANES] = (
                    jax.nn.sigmoid(mo_ref[n, hs, ts]) * y).astype(BF16)
    if not latent:
        for n in range(nb):
            for d in range(2):
                for pair in range(2):
                    cf_ref[n, d, pair] = s_ref[n, d, pair, :ML_DV, :].T
        nf_ref[...] = s_ref[:, :, :, ML_DV:, :]
        mf_ref[...] = m_ref[...]


def _mlstm_call(latent, l, mqt, mk, mvt, mot, gt, wts, s0=None, m0=None):
    assert ML_CHUNK == LANES
    b, _, s = mqt.shape
    nb = NB_ML_LAT if latent else NB_ML_CTX

    def feat(w):
        return pl.BlockSpec((nb, w, s), lambda i: (i, 0, 0))

    in_specs = [feat(512), pl.BlockSpec((nb, s, 256), lambda i: (i, 0, 0)), feat(512), feat(512),
                feat(32), pl.BlockSpec((None, 512, LANES), lambda i: (l, 0, 0))]
    args = [mqt, mk, mvt, mot, gt, wts["g_mnorm"]]
    s_spec_shape = (2, 2, ML_DV + ML_NROWS, LANES)
    m_spec_shape = (2, 8, LANES)
    scratch = [pltpu.VMEM((nb,) + s_spec_shape, F32), pltpu.VMEM((nb,) + m_spec_shape, F32),
               pltpu.VMEM((nb, 2, ML_HEADS * ML_DV, s), F32)]
    o_shape = jax.ShapeDtypeStruct((ML_HEADS * ML_DV, b * s), BF16)
    o_spec = pl.BlockSpec((ML_HEADS * ML_DV, nb * s), lambda i: (0, i))
    if latent:
        in_specs += [pl.BlockSpec((nb, None) + s_spec_shape, lambda i: (i, l, 0, 0, 0, 0)),
                     pl.BlockSpec((nb, None) + m_spec_shape, lambda i: (i, l, 0, 0, 0))]
        args += [s0, m0]
        out_shape = o_shape
        out_specs = o_spec
    else:
        half = (2, 2, LANES, LANES)
        out_shape = (o_shape,
                     jax.ShapeDtypeStruct((b,) + half, F32),
                     jax.ShapeDtypeStruct((b, 2, 2, ML_NROWS, LANES), F32),
                     jax.ShapeDtypeStruct((b,) + m_spec_shape, F32))
        out_specs = (o_spec,
                     pl.BlockSpec((nb,) + half, lambda i: (i, 0, 0, 0, 0)),
                     pl.BlockSpec((nb, 2, 2, ML_NROWS, LANES), lambda i: (i, 0, 0, 0, 0)),
                     pl.BlockSpec((nb,) + m_spec_shape, lambda i: (i, 0, 0, 0)))
    return pl.pallas_call(
        functools.partial(_mlstm_kernel, latent, s, nb),
        out_shape=out_shape, grid=(b // nb,), in_specs=in_specs, out_specs=out_specs,
        scratch_shapes=scratch,
        compiler_params=_cparams(("arbitrary",)),
        name="mlstm_lat" if latent else "mlstm_ctx",
    )(*args)


def _post_kernel(nt_ctx, seg_ctx, split_x, final, *refs):
    n_x = 2 if split_x else 1
    n_o = 2 if final else 1
    x_refs, refs = refs[:n_x], refs[n_x:]
    (oab_c, oab_l, oc_c, oc_l, wo_ref, wu_ref, wd_ref, gt1_ref, sh2_ref, sc2_ref, gt2_ref, g2_ref,
     cw_ref, cb_ref, gf_ref) = refs[:15]
    o_refs = refs[15:15 + n_o]
    wo_s, wu_s, wd_s, x1_ref, h2_ref, act_ref = refs[15 + n_o:]
    i = pl.program_id(0)

    @pl.when(i < N_CAST)
    def _():
        wu_s[:, pl.ds(pl.multiple_of(i * CAST_UP, CAST_UP), CAST_UP)] = wu_ref[...].astype(BF16)
        wd_s[pl.ds(pl.multiple_of(i * CAST_DOWN, CAST_DOWN), CAST_DOWN), :] = wd_ref[...].astype(BF16)

    @pl.when(i < N_CAST_OUT)
    def _():
        wo_s[pl.ds(pl.multiple_of(i * CAST_OUT, CAST_OUT), CAST_OUT), :] = wo_ref[...].astype(BF16)

    @pl.when(i >= N_CAST)
    def _():
        t = i - N_CAST
        is_ctx = t < nt_ctx

        def pick(a_ref, b_ref):
            return jnp.where(is_ctx, a_ref[...], b_ref[...])

        x = pick(*x_refs) if split_x else x_refs[0][...]
        mix = (_dot(pick(oab_c, oab_l), wo_s[:2 * 256, :])
               + lax.dot_general(pick(oc_c, oc_l), wo_s[2 * 256:, :], (((0,), (0,)), ((), ())),
                                 preferred_element_type=F32))
        x1 = x + gt1_ref[...] * mix
        x1_ref[...] = x1
        h2 = _rms_rows(x1, g2_ref[...], D_MODEL) * (1.0 + sc2_ref[...]) + sh2_ref[...]
        h2_ref[...] = h2.astype(BF16)

        seg = jnp.where(is_ctx, seg_ctx, GRID_W)
        row = lax.broadcasted_iota(jnp.int32, (TM_FFN, FC), 0)
        first_w = (row & (GRID_W - 1)) == 0
        last_w = (row & (GRID_W - 1)) == GRID_W - 1
        seg_first = (row & (seg - 1)) == 0
        seg_last = (row & (seg - 1)) == seg - 1

        def conv(u, cs):
            prev = pltpu.roll(u, 1, 0)
            prev = jnp.where(first_w, jnp.where(seg_first, 0.0, prev), prev)
            nxt = pltpu.roll(u, TM_FFN - 1, 0)
            nxt = jnp.where(last_w, jnp.where(seg_last, 0.0, nxt), nxt)
            return (cb_ref[:, cs] + prev * cw_ref[0:1, cs] + u * cw_ref[1:2, cs]
                    + nxt * cw_ref[2:3, cs])

        for j in range(N_FC):
            vs = slice(j * FC, (j + 1) * FC)
            gs = slice(D_FF + j * FC, D_FF + (j + 1) * FC)
            val = conv(_dot(h2_ref[...], wu_s[:, vs]), vs)
            gate = conv(_dot(h2_ref[...], wu_s[:, gs]), gs)
            act_ref[:, vs] = (gate * jax.nn.sigmoid(gate) * val).astype(BF16)

        x2 = x1_ref[...] + gt2_ref[...] * _dot(act_ref[...], wd_s[...])
        if not final:
            o_refs[0][...] = x2
        else:
            y = _rms_rows(x2, gf_ref[...], D_MODEL)

            @pl.when(is_ctx)
            def _():
                o_refs[0][...] = y

            @pl.when(jnp.logical_not(is_ctx))
            def _():
                o_refs[1][...] = y


def _post_call(l, final, xs, n_ctx, s_ctx, s_lat, oab, oc, mods, wts):
    split_x = len(xs) == 2
    n_lat = oab[1].shape[0]
    nt_ctx, nt_lat = n_ctx // TM_FFN, n_lat // TM_FFN
    nt = nt_ctx + nt_lat
    tiles_per_batch = s_lat // TM_FFN
    assert s_ctx <= TM_FFN and TM_FFN % s_ctx == 0 and s_lat % TM_FFN == 0

    def tile(i):
        return jnp.maximum(i - N_CAST, 0)

    def ctx_t(i):
        return jnp.minimum(tile(i), nt_ctx - 1)

    def lat_t(i):
        return jnp.maximum(tile(i) - nt_ctx, 0)

    def mod(chunk):
        def index(i):
            row = jnp.where(tile(i) < nt_ctx, 0, 1 + lat_t(i) // tiles_per_batch)
            return (l * 48 + row * 6 + chunk, 0, 0)
        return pl.BlockSpec((None, 1, D_MODEL), index)

    def resident(*shape):
        nd = len(shape)
        return pl.BlockSpec((None,) + shape, lambda i: (l,) + (0,) * nd,
                            pipeline_mode=pl.Buffered(1))

    if split_x:
        x_specs = [pl.BlockSpec((TM_FFN, D_MODEL), lambda i: (ctx_t(i), 0)),
                   pl.BlockSpec((TM_FFN, D_MODEL), lambda i: (lat_t(i), 0))]
    else:
        x_specs = [pl.BlockSpec((TM_FFN, D_MODEL), lambda i: (tile(i), 0))]
    in_specs = x_specs + [
        pl.BlockSpec((TM_FFN, 512), lambda i: (ctx_t(i), 0)),
        pl.BlockSpec((TM_FFN, 512), lambda i: (lat_t(i), 0)),
        pl.BlockSpec((ML_HEADS * ML_DV, TM_FFN), lambda i: (0, ctx_t(i))),
        pl.BlockSpec((ML_HEADS * ML_DV, TM_FFN), lambda i: (0, lat_t(i))),
        pl.BlockSpec((None, CAST_OUT, D_MODEL), lambda i: (l, jnp.minimum(i, N_CAST_OUT - 1), 0)),
        pl.BlockSpec((None, D_MODEL, CAST_UP), lambda i: (l, 0, jnp.minimum(i, N_CAST - 1))),
        pl.BlockSpec((None, CAST_DOWN, D_MODEL), lambda i: (l, jnp.minimum(i, N_CAST - 1), 0)),
        mod(2), mod(3), mod(4), mod(5),
        resident(1, D_MODEL), resident(CONV_W, 2 * D_FF), resident(1, 2 * D_FF),
        pl.BlockSpec((1, D_MODEL), lambda i: (0, 0)),
    ]
    args = list(xs) + [oab[0], oab[1], oc[0], oc[1], wts["w_out"], wts["w_up"], wts["w_down"],
                       mods, mods, mods, mods, wts["g_norm2"], wts["conv_w"], wts["conv_b"],
                       wts["g_final"]]
    if final:
        out_shape = (jax.ShapeDtypeStruct((n_ctx, D_MODEL), F32),
                     jax.ShapeDtypeStruct((n_lat, D_MODEL), F32))
        out_specs = (pl.BlockSpec((TM_FFN, D_MODEL), lambda i: (ctx_t(i), 0)),
                     pl.BlockSpec((TM_FFN, D_MODEL), lambda i: (lat_t(i), 0)))
    else:
        out_shape = jax.ShapeDtypeStruct((n_ctx + n_lat, D_MODEL), F32)
        out_specs = pl.BlockSpec((TM_FFN, D_MODEL), lambda i: (tile(i), 0))
    return pl.pallas_call(
        functools.partial(_post_kernel, nt_ctx, s_ctx, split_x, final),
        out_shape=out_shape, grid=(N_CAST + nt,), in_specs=in_specs, out_specs=out_specs,
        scratch_shapes=[pltpu.VMEM((D_MODEL, D_MODEL), BF16), pltpu.VMEM((D_MODEL, 2 * D_FF), BF16),
                        pltpu.VMEM((D_FF, D_MODEL), BF16),
                        pltpu.VMEM((TM_FFN, D_MODEL), F32), pltpu.VMEM((TM_FFN, D_MODEL), BF16),
                        pltpu.VMEM((TM_FFN, D_FF), BF16)],
        compiler_params=pltpu.CompilerParams(dimension_semantics=("arbitrary",),
                                             vmem_limit_bytes=VMEM_LIMIT_FFN),
        name="post",
    )(*args)


W_IN_BODY = (352, 2656)


def _pack_in_kernel(w_ref, o_ref):
    tc = w_ref.shape[1]

    def rows(lo, n):
        return w_ref[lo:lo + n, :]

    def zeros(n):
        return jnp.zeros((n, tc), F32)

    o_ref[C_CQ:C_CQ + MLA_Q_RANK, :] = rows(0, MLA_Q_RANK).astype(BF16)
    o_ref[C_CQ + MLA_Q_RANK:C_CKV, :] = zeros(C_CKV - C_CQ - MLA_Q_RANK).astype(BF16)
    o_ref[C_CKV:C_AUX, :] = rows(MLA_Q_RANK, MLA_KV_RANK).astype(BF16)
    aux = [rows(MLA_Q_RANK + MLA_KV_RANK, MLA_ROPE)]
    for g in range(4):
        aux += [rows(W_IN_BODY[1] + ML_HEADS * g, ML_HEADS), zeros(8 - ML_HEADS)]
    aux.append(zeros(LANES - AUX_GATE - 32))
    o_ref[C_AUX:C_DQ, :] = jnp.concatenate(aux, axis=0).astype(BF16)
    o_ref[C_DQ:, :] = rows(W_IN_BODY[0], W_IN_BODY[1] - W_IN_BODY[0]).astype(BF16)


def _pack_in_call(w_in):
    w_in_t = jnp.swapaxes(w_in, 1, 2)
    tc = 256
    return pl.pallas_call(
        _pack_in_kernel,
        out_shape=jax.ShapeDtypeStruct((DEPTH, NP_IN, D_MODEL), BF16),
        grid=(DEPTH, D_MODEL // tc),
        in_specs=[pl.BlockSpec((None, w_in_t.shape[1], tc), lambda l, i: (l, 0, i))],
        out_specs=pl.BlockSpec((None, NP_IN, tc), lambda l, i: (l, 0, i)),
        compiler_params=_cparams(("arbitrary", "arbitrary")),
        name="pack_w_in",
    )(w_in_t)


def _pack_weights(w_in, g_cq, w_uq, g_ckv, w_ukv, b_gate, g_subln, g_mnorm, g_norm1, g_norm2,
                  w_out, w_up, conv_w, conv_b, w_down, g_final, lam_q1, lam_k1, lam_q2, lam_k2):
    def cols(a, lo, n, pad=0):
        blk = a[..., lo:lo + n]
        if pad:
            blk = jnp.pad(blk, [(0, 0)] * (a.ndim - 1) + [(0, pad)])
        return blk

    w_in_p = _pack_in_call(w_in)

    hd = MLA_NOPE + MLA_ROPE
    w_uq_p = jnp.pad(w_uq.reshape(DEPTH, MLA_Q_RANK, MLA_HEADS, hd),
                     [(0, 0), (0, 256 - MLA_Q_RANK), (0, 0), (0, LANES - hd)])
    w_uq_p = w_uq_p.reshape(DEPTH, 256, MLA_HEADS * LANES).astype(BF16)

    w_ukv4 = w_ukv.reshape(DEPTH, MLA_KV_RANK, MLA_HEADS, MLA_NOPE + MLA_V)
    w_k = jnp.pad(w_ukv4[..., :MLA_NOPE], [(0, 0), (0, 0), (0, 0), (0, LANES - MLA_NOPE)])
    w_k = w_k.reshape(DEPTH, MLA_KV_RANK, MLA_HEADS * LANES)
    j = jnp.arange(LANES)[:, None]
    cix = jnp.arange(MLA_HEADS * LANES)[None, :]
    place = ((j < MLA_ROPE) & ((cix % LANES) == MLA_NOPE + j)).astype(F32)
    w_kk = jnp.concatenate([w_k, jnp.broadcast_to(place, (DEPTH, LANES, MLA_HEADS * LANES))],
                           axis=1).astype(BF16)
    w_v = w_ukv4[..., MLA_NOPE:].reshape(DEPTH, MLA_KV_RANK, MLA_HEADS * MLA_V).astype(BF16)

    return dict(
        w_in=w_in_p, w_uq=w_uq_p, w_kk=w_kk, w_v=w_v,
        g_norm1=g_norm1[:, None, :], g_norm2=g_norm2[:, None, :],
        g_cq=jnp.pad(g_cq, [(0, 0), (0, 256 - MLA_Q_RANK)])[:, None, :],
        g_ckv=g_ckv[:, None, :],
        b_gate=jnp.pad(jnp.pad(b_gate.reshape(DEPTH, 4, ML_HEADS), [(0, 0), (0, 0), (0, 4)])
                       .reshape(DEPTH, 32), [(0, 0), (AUX_GATE, LANES - AUX_GATE - 32)])[:, None, :],
        g_subln=jnp.tile(g_subln, (1, DF_HEADS))[:, None, :],
        g_mnorm=jnp.broadcast_to(g_mnorm[:, :, None], (DEPTH, ML_HEADS * ML_DV, LANES)),
        w_out=w_out, w_up=w_up, w_down=w_down,
        conv_w=conv_w, conv_b=conv_b[:, None, :], g_final=g_final[None, :],
        lam_q1=lam_q1[:, None, :], lam_k1=lam_k1[:, None, :],
        lam_q2=lam_q2[:, None, :], lam_k2=lam_k2[:, None, :],
    )


def _rope_tables(n_tok):
    t = np.arange(n_tok)
    row = (t // GRID_W).astype(np.float64)
    col = (t % GRID_W).astype(np.float64)
    nf = MLA_ROPE // 4
    inv = ROPE_BASE ** (-np.arange(nf, dtype=np.float64) / nf)
    ar = row[:, None] * inv[None, :]
    ac = col[:, None] * inv[None, :]
    ang = np.concatenate([ar, ar, ac, ac], axis=-1)
    quarter = (np.arange(MLA_ROPE) // nf) % 2
    cos = jnp.asarray(np.cos(ang), F32)
    sin_up = jnp.asarray(np.where(quarter == 0, -np.sin(ang), 0.0), F32)
    sin_dn = jnp.asarray(np.where(quarter == 1, np.sin(ang), 0.0), F32)
    ones = jnp.ones((n_tok, 1), F32)
    zeros = jnp.zeros((n_tok, 1), F32)

    def head_q(t32, fill):
        blk = jnp.concatenate([jnp.tile(fill, (1, MLA_NOPE)), t32, jnp.tile(fill, (1, 32))], axis=1)
        return jnp.tile(blk, (1, MLA_HEADS))

    def aux_k(t32, fill):
        return jnp.concatenate([t32, jnp.tile(fill, (1, LANES - MLA_ROPE))], axis=1)

    tq = (head_q(cos, ones), head_q(sin_up, zeros), head_q(sin_dn, zeros))
    td = tuple(jnp.tile(a, (1, 256 // DF_DIM)) for a in (cos, sin_up, sin_dn))
    tk = (aux_k(cos, ones), aux_k(sin_up, zeros), aux_k(sin_dn, zeros))
    return tq + td + tk


def kernel(x_prompt, x_sample, cache_mla_ckv, cache_mla_krope, cache_diff_k, cache_diff_v,
           state_mlstm_C, state_mlstm_n, state_mlstm_m, c, c_ctx, w_ada, b_ada, g_norm1, w_in,
           g_cq, w_uq, g_ckv, w_ukv, lam_q1, lam_k1, lam_q2, lam_k2, g_subln, b_gate, g_mnorm,
           w_out, g_norm2, w_up, conv_w, conv_b, w_down, g_final):
    bp, sp, _ = x_prompt.shape
    bl, sl, _ = x_sample.shape
    t_len = cache_mla_ckv.shape[2]

    wts = _pack_weights(w_in, g_cq, w_uq, g_ckv, w_ukv, b_gate, g_subln, g_mnorm, g_norm1, g_norm2,
                        w_out, w_up, conv_w, conv_b, w_down, g_final, lam_q1, lam_k1, lam_q2, lam_k2)
    tables = _rope_tables(sl)

    cond = jnp.concatenate([c_ctx[None, :], c, jnp.zeros((8 - 1 - bl, D_MODEL), F32)], axis=0)
    mods = _ada_call(cond, w_ada, b_ada[:, None, :])
    mods = mods.reshape(DEPTH * 8 * N_MOD, 1, D_MODEL)

    def feat_major(a):
        return jnp.transpose(a, (0, 1, 3, 4, 2)).reshape(bl, DEPTH, 256, t_len)

    kctx, vctx, cdk, cdv = _ctxkv_call(
        cache_mla_ckv, jnp.swapaxes(cache_mla_krope, 2, 3), feat_major(cache_diff_k),
        feat_major(cache_diff_v), wts["w_kk"], wts["w_v"])
    c0_t = jnp.swapaxes(state_mlstm_C.reshape(bl, DEPTH, 2, 2, LANES, ML_DV), -1, -2)
    n0_r = jnp.broadcast_to(state_mlstm_n.reshape(bl, DEPTH, 2, 2, 1, LANES),
                            (bl, DEPTH, 2, 2, ML_NROWS, LANES))
    s0 = jnp.concatenate([c0_t, n0_r], axis=-2)
    m0 = jnp.broadcast_to(jnp.pad(state_mlstm_m, [(0, 0)] * 3 + [(0, 8 - ML_HEADS)])[..., None],
                          (bl, DEPTH, 2, 8, LANES))

    n_ctx, n_lat = bp * sp, bl * sl
    xs = (x_prompt.reshape(n_ctx, D_MODEL), x_sample.reshape(n_lat, D_MODEL))
    col = [[] for _ in range(7)]
    for l in range(DEPTH):
        lam_init = 0.8 - 0.6 * math.exp(-0.3 * l)
        final = l == DEPTH - 1
        x_ctx, x_lat, lat_off = (xs[0], xs[1], 0) if len(xs) == 2 else (xs[0], xs[0], n_ctx)
        (q, k, v, dq, dk, dv, mq, mkt, mv, mo, gt, ckv_s, kr_s, dk_s, dv_s) = _pre_call(
            False, l, x_ctx, 0, bp, sp, mods, wts, None)
        oab_c = _attn_call(False, l, lam_init, (q, k, v), (dq, dk, dv), wts)
        oc_c, c_fin, n_fin, m_fin = _mlstm_call(False, l, mq, mkt, mv, mo, gt, wts)
        col[0].append(ckv_s)
        col[1].append(kr_s)
        col[2].append(dk_s)
        col[3].append(dv_s)
        col[4].append(c_fin.reshape(bp, 2, ML_HEADS, ML_DK, ML_DV))
        col[5].append(n_fin[..., 0, :].reshape(bp, 2, ML_HEADS, ML_DK))
        col[6].append(m_fin[:, :, :ML_HEADS, 0])
        (q, k, v, dq, dk, dv, mq, mkt, mv, mo, gt) = _pre_call(
            True, l, x_lat, lat_off, bl, sl, mods, wts, tables)
        oab_l = _attn_call(True, l, lam_init, (q, k, v), (dq, dk, dv), wts, (kctx, vctx), (cdk, cdv))
        oc_l = _mlstm_call(True, l, mq, mkt, mv, mo, gt, wts, s0, m0)
        out = _post_call(l, final, xs, n_ctx, sp, sl,
                         (oab_c.reshape(n_ctx, 512), oab_l.reshape(n_lat, 512)), (oc_c, oc_l),
                         mods, wts)
        xs = out if final else (out,)

    xp = xs[0].reshape(bp, sp, D_MODEL)
    xs = xs[1].reshape(bl, sl, D_MODEL)
    st = [jnp.stack(a, axis=1) for a in col]
    st[1] = jnp.swapaxes(st[1], 2, 3)
    for i in (2, 3):
        st[i] = jnp.transpose(st[i].reshape(bp, DEPTH, DF_HEADS, 2 * DF_DIM, sp), (0, 1, 4, 2, 3))
    return (xp, xs) + tuple(st)
```

```python
import functools
import math

import jax
import jax.numpy as jnp
import numpy as np
from jax import lax
from jax.experimental import pallas as pl
from jax.experimental.pallas import tpu as pltpu

F32 = jnp.float32
BF16 = jnp.bfloat16

D_MODEL = 1024
DEPTH = 4
GRID_W = 64
N_MOD = 6
EPS = 1e-6
ROPE_BASE = 10000.0
MLA_HEADS = 4
MLA_Q_RANK = 192
MLA_KV_RANK = 128
MLA_NOPE = 64
MLA_ROPE = 32
MLA_V = 64
DF_HEADS = 4
DF_DIM = 32
ML_HEADS = 4
ML_DK = 64
ML_DV = 128
D_FF = 2816
CONV_W = 3

LANES = 128
VMEM_LIMIT = 48 * 1024 * 1024
VMEM_LIMIT_FFN = 56 * 1024 * 1024

C_CQ, C_CKV, C_AUX, C_DQ, C_DK, C_DV, C_MQ, C_MK, C_MV, C_MO = (
    0, 256, 384, 512, 768, 1024, 1280, 1536, 1792, 2304)
NP_IN = 2816
AUX_GATE = 32

TM_PRE = 256
PRE_SUB = 2
TQ = 512
NB_CTX = 4
ML_CHUNK = 128
ML_NROWS = 128
NB_ML_CTX = 4
NB_ML_LAT = 2
TM_FFN = 512
FC = 256
N_FC = D_FF // FC
N_CAST = 22
CAST_UP = 2 * D_FF // N_CAST
CAST_DOWN = D_FF // N_CAST
CAST_OUT = 128
N_CAST_OUT = D_MODEL // CAST_OUT

NT = (((1,), (1,)), ((), ()))
LOG2E = 1.4426950408889634


def _cparams(sem):
    return pltpu.CompilerParams(dimension_semantics=sem, vmem_limit_bytes=VMEM_LIMIT)


def _dot(a, b):
    return jnp.dot(a, b, preferred_element_type=F32)


def _dot_nt(a, b):
    return lax.dot_general(a, b, NT, preferred_element_type=F32)


def _rms_rows(x, g, n):
    ms = jnp.sum(x * x, axis=-1, keepdims=True) * (1.0 / n)
    return x * lax.rsqrt(ms + EPS) * g


def _rope(x, cos, sin_up, sin_dn):
    w = x.shape[-1]
    return x * cos + pltpu.roll(x, w - 8, 1) * sin_up + pltpu.roll(x, 8, 1) * sin_dn


def _ada_kernel(c_ref, w_ref, b_ref, o_ref):
    c = c_ref[...]
    s = (c * jax.nn.sigmoid(c)).astype(BF16)
    o_ref[...] = _dot(s, w_ref[...].astype(BF16)) + b_ref[...]


def _ada_call(cond, w_ada, b_ada):
    nt = 1024
    return pl.pallas_call(
        _ada_kernel,
        out_shape=jax.ShapeDtypeStruct((DEPTH, 8, N_MOD * D_MODEL), F32),
        grid=(DEPTH, N_MOD * D_MODEL // nt),
        in_specs=[pl.BlockSpec((8, D_MODEL), lambda l, j: (0, 0)),
                  pl.BlockSpec((None, D_MODEL, nt), lambda l, j: (l, 0, j)),
                  pl.BlockSpec((None, 1, nt), lambda l, j: (l, 0, j))],
        out_specs=pl.BlockSpec((None, 8, nt), lambda l, j: (l, 0, j)),
        compiler_params=_cparams(("arbitrary", "arbitrary")),
        name="ada_mod",
    )(cond, w_ada, b_ada)


def _ctxkv_kernel(ckv_ref, krt_ref, dkt_ref, dvt_ref, wkk_ref, wv_ref, k_ref, v_ref, dkb_ref, dvb_ref):
    t = ckv_ref.shape[0]
    ckv = ckv_ref[...].astype(BF16)
    kr = jnp.concatenate([krt_ref[...], jnp.zeros((LANES - MLA_ROPE, t), F32)], axis=0).T
    kin = jnp.concatenate([ckv, kr.astype(BF16)], axis=1)
    k_ref[...] = _dot(kin, wkk_ref[...]).T.astype(BF16)
    v_ref[...] = _dot(ckv, wv_ref[...]).astype(BF16)
    dkb_ref[...] = dkt_ref[...].astype(BF16)
    dvb_ref[...] = dvt_ref[...].T.astype(BF16)


def _ctxkv_call(cache_ckv, cache_kr_t, cache_dk_t, cache_dv_t, wkk, wv):
    b, _, t, _ = cache_ckv.shape

    def cache_t(w):
        return pl.BlockSpec((None, None, w, t), lambda l, i: (i, l, 0, 0))

    def out(w):
        return pl.BlockSpec((None, None, t, w), lambda l, i: (l, i, 0, 0))

    def out_t(w):
        return pl.BlockSpec((None, None, w, t), lambda l, i: (l, i, 0, 0))

    return pl.pallas_call(
        _ctxkv_kernel,
        out_shape=(jax.ShapeDtypeStruct((DEPTH, b, 512, t), BF16),
                   jax.ShapeDtypeStruct((DEPTH, b, t, 256), BF16),
                   jax.ShapeDtypeStruct((DEPTH, b, 256, t), BF16),
                   jax.ShapeDtypeStruct((DEPTH, b, t, 256), BF16)),
        grid=(DEPTH, b),
        in_specs=[pl.BlockSpec((None, None, t, MLA_KV_RANK), lambda l, i: (i, l, 0, 0)),
                  cache_t(MLA_ROPE), cache_t(256), cache_t(256),
                  pl.BlockSpec((None, 256, 512), lambda l, i: (l, 0, 0)),
                  pl.BlockSpec((None, MLA_KV_RANK, 256), lambda l, i: (l, 0, 0))],
        out_specs=(out_t(512), out(256), out_t(256), out(256)),
        compiler_params=_cparams(("arbitrary", "arbitrary")),
        name="ctx_kv",
    )(cache_ckv, cache_kr_t, cache_dk_t, cache_dv_t, wkk, wv)


def _pre_kernel(latent, kinds, *refs):
    for n in range(PRE_SUB):
        views = []
        for kind, r in zip(kinds, refs):
            if kind == "rows":
                r = r.at[n * TM_PRE:(n + 1) * TM_PRE]
            elif kind == "lanes":
                r = r.at[:, n * TM_PRE:(n + 1) * TM_PRE]
            elif kind == "batch":
                r = r.at[n]
            views.append(r)
        _pre_tile(latent, *views)


def _pre_tile(latent, *refs):
    (x_ref, sh_ref, sc_ref, g1_ref, win_ref, gcq_ref, wuq_ref, gckv_ref, wkk_ref, wv_ref,
     bg_ref) = refs[:11]
    refs = refs[11:]
    if latent:
        (cq_t, sqa_t, sqb_t, cd_t, sda_t, sdb_t, ck_t, ska_t, skb_t) = refs[:9]
        refs = refs[9:]
    (q_ref, k_ref, v_ref, dq_ref, dk_ref, dv_ref, mq_ref, mk_ref, mv_ref, mo_ref,
     gt_ref) = refs[:11]
    refs = refs[11:]
    if not latent:
        ckv_out, kr_out, dk_out, dv_out = refs

    x = x_ref[...]
    h = _rms_rows(x, g1_ref[...], D_MODEL) * (1.0 + sc_ref[...]) + sh_ref[...]
    proj = _dot_nt(h.astype(BF16), win_ref[...])

    cq = _rms_rows(proj[:, C_CQ:C_CQ + 256], gcq_ref[...], MLA_Q_RANK)
    q = _dot(cq.astype(BF16), wuq_ref[...])
    if latent:
        q = _rope(q, cq_t[...], sqa_t[...], sqb_t[...])
    q_ref[...] = (q * ((MLA_NOPE + MLA_ROPE) ** -0.5 * LOG2E)).astype(BF16)

    c_kv = _rms_rows(proj[:, C_CKV:C_CKV + MLA_KV_RANK], gckv_ref[...], MLA_KV_RANK)
    aux = proj[:, C_AUX:C_AUX + LANES] + bg_ref[...]
    if latent:
        aux = _rope(aux, ck_t[...], ska_t[...], skb_t[...])
    aux_t = aux.T
    gt_ref[...] = aux_t[AUX_GATE:AUX_GATE + 32, :]
    if not latent:
        ckv_out[...] = c_kv
        kr_out[...] = aux_t[:MLA_ROPE, :]
    ckv_b = c_kv.astype(BF16)
    kin = jnp.concatenate([ckv_b, aux.astype(BF16)], axis=1)
    k_ref[...] = _dot(kin, wkk_ref[...]).T.astype(BF16)
    v_ref[...] = _dot(ckv_b, wv_ref[...]).astype(BF16)

    dq = proj[:, C_DQ:C_DQ + 256]
    dk = proj[:, C_DK:C_DK + 256]
    dv = proj[:, C_DV:C_DV + 256]
    if not latent:
        dk_t = dk.T
        dk_out[...] = dk_t
        dv_out[...] = dv.T
    else:
        dq = _rope(dq, cd_t[...], sda_t[...], sdb_t[...])
        dk_t = _rope(dk, cd_t[...], sda_t[...], sdb_t[...]).T
    dq_ref[...] = (dq * (DF_DIM ** -0.5 * LOG2E)).astype(BF16)
    dk_ref[...] = dk_t.astype(BF16)
    dv_ref[...] = dv.astype(BF16)

    lane = lax.broadcasted_iota(jnp.int32, (x.shape[0], LANES), 1)
    for h in range(ML_HEADS):
        blk = proj[:, C_MQ + (h // 2) * LANES:C_MQ + (h // 2 + 1) * LANES]
        mq_ref[h * LANES:(h + 1) * LANES, :] = (
            jnp.where((lane >> 6) == h % 2, blk, 0.0).T.astype(BF16))
    mk_ref[...] = (proj[:, C_MK:C_MK + 256] * (ML_DK ** -0.5)).astype(BF16)
    mv_ref[...] = proj[:, C_MV:C_MV + 512].T.astype(BF16)
    mo_ref[...] = proj[:, C_MO:C_MO + 512].T


def _pre_call(latent, l, x2d, tok_off, b, s, mods, wts, tables):
    ns = s // TM_PRE
    tile_off = tok_off // TM_PRE
    tm2 = PRE_SUB * TM_PRE
    seq_split = ns > 1
    assert (ns % PRE_SUB == 0) if seq_split else (b % PRE_SUB == 0 and ns == 1)
    assert tile_off % PRE_SUB == 0
    grid = (ns // PRE_SUB, b) if seq_split else (1, b // PRE_SUB)

    def tok(width):
        if seq_split:
            return pl.BlockSpec((None, tm2, width), lambda j, i: (i, j, 0)), "rows"
        return pl.BlockSpec((PRE_SUB, TM_PRE, width), lambda j, i: (i, j, 0)), "batch"

    def feat(width):
        if seq_split:
            return pl.BlockSpec((None, width, tm2), lambda j, i: (i, 0, j)), "lanes"
        return pl.BlockSpec((PRE_SUB, width, TM_PRE), lambda j, i: (i, 0, j)), "batch"

    def mod(chunk):
        if latent:
            return pl.BlockSpec((None, 1, D_MODEL), lambda j, i: (l * 48 + (1 + i) * 6 + chunk, 0, 0))
        return pl.BlockSpec((None, 1, D_MODEL), lambda j, i: (l * 48 + chunk, 0, 0))

    def lw(*shape):
        nd = len(shape)
        return pl.BlockSpec((None,) + shape, lambda j, i: (l,) + (0,) * nd)

    if seq_split:
        x_spec = pl.BlockSpec((tm2, D_MODEL), lambda j, i: ((tile_off + i * ns) // PRE_SUB + j, 0))
    else:
        x_spec = pl.BlockSpec((tm2, D_MODEL), lambda j, i: (tile_off // PRE_SUB + i, 0))
    in_specs = [x_spec, mod(0), mod(1), lw(1, D_MODEL), lw(NP_IN, D_MODEL), lw(1, 256),
                lw(256, 512), lw(1, MLA_KV_RANK), lw(256, 512), lw(MLA_KV_RANK, 256), lw(1, LANES)]
    kinds = ["rows"] + [None] * 10
    args = [x2d, mods, mods, wts["g_norm1"], wts["w_in"], wts["g_cq"], wts["w_uq"], wts["g_ckv"],
            wts["w_kk"], wts["w_v"], wts["b_gate"]]
    if latent:
        for t in tables:
            in_specs.append(pl.BlockSpec((tm2, t.shape[1]), lambda j, i: (j, 0)))
            kinds.append("rows")
            args.append(t)

    widths = [(512, BF16, False), (512, BF16, True), (256, BF16, False), (256, BF16, False),
              (256, BF16, True), (256, BF16, False), (512, BF16, True), (256, BF16, False),
              (512, BF16, True), (512, F32, True), (32, F32, True)]
    if not latent:
        widths += [(MLA_KV_RANK, F32, False), (MLA_ROPE, F32, True), (256, F32, True),
                   (256, F32, True)]
    out_shape = tuple(jax.ShapeDtypeStruct((b, w, s) if tr else (b, s, w), dt)
                      for w, dt, tr in widths)
    out_pairs = [feat(w) if tr else tok(w) for w, _, tr in widths]
    out_specs = tuple(p[0] for p in out_pairs)
    kinds += [p[1] for p in out_pairs]
    return pl.pallas_call(
        functools.partial(_pre_kernel, latent, tuple(kinds)),
        out_shape=out_shape, grid=grid, in_specs=in_specs, out_specs=out_specs,
        compiler_params=_cparams(("arbitrary", "arbitrary")),
        name="pre_lat" if latent else "pre_ctx",
    )(*args)


def _softmax_parts(s_list):
    m = functools.reduce(jnp.maximum, [jnp.max(s, axis=1, keepdims=True) for s in s_list])
    p_list = [jnp.exp2(s - m) for s in s_list]
    l = functools.reduce(jnp.add, [jnp.sum(p, axis=1, keepdims=True) for p in p_list])
    return p_list, l


def _attn_kernel(latent, lam_init, nb, *refs):
    n_seg = 5 if latent else 3
    o_ref = refs[-1]
    mla_in, diff_in, shared = refs[:n_seg], refs[n_seg:2 * n_seg], refs[2 * n_seg:-1]
    for n in range(nb):
        _mla_body(latent, *[r.at[n] for r in mla_in], o_ref.at[n])
        _diff_body(latent, lam_init, *[r.at[n] for r in diff_in], *shared, o_ref.at[n])


def _mla_body(latent, *refs):
    if latent:
        q_ref, k_ref, v_ref, kc_ref, vc_ref, o_ref = refs
        segs = [(kc_ref, vc_ref), (k_ref, v_ref)]
    else:
        q_ref, k_ref, v_ref, o_ref = refs
        segs = [(k_ref, v_ref)]
    lane = lax.broadcasted_iota(jnp.int32, (q_ref.shape[0], LANES), 1)

    def scores(h):
        hs = slice(h * LANES, (h + 1) * LANES)
        return [_dot(q_ref[:, hs], kr[hs, :]) for kr, _ in segs]

    outs = []
    s_next = scores(0)
    for h in range(MLA_HEADS):
        ps = slice((h // 2) * LANES, (h // 2 + 1) * LANES)
        s_list = s_next
        if h + 1 < MLA_HEADS:
            s_next = scores(h + 1)
        p_list, l = _softmax_parts(s_list)
        pv = functools.reduce(jnp.add, [_dot(p.astype(BF16), vr[:, ps])
                                        for p, (_, vr) in zip(p_list, segs)])
        outs.append(pv / l)
    o_ref[:, 0:LANES] = jnp.where(lane < MLA_V, outs[0], outs[1]).astype(BF16)
    o_ref[:, LANES:2 * LANES] = jnp.where(lane < MLA_V, outs[2], outs[3]).astype(BF16)


def _diff_body(latent, lam_init, *refs):
    if latent:
        (q_ref, k_ref, v_ref, kc_ref, vc_ref, lq1, lk1, lq2, lk2, g_ref, o_ref) = refs
        segs = [(kc_ref, vc_ref), (k_ref, v_ref)]
    else:
        (q_ref, k_ref, v_ref, lq1, lk1, lq2, lk2, g_ref, o_ref) = refs
        segs = [(k_ref, v_ref)]
    lam = (jnp.exp(jnp.sum(lq1[...] * lk1[...], axis=1, keepdims=True))
           - jnp.exp(jnp.sum(lq2[...] * lk2[...], axis=1, keepdims=True)) + lam_init)
    lane = lax.broadcasted_iota(jnp.int32, (q_ref.shape[0], LANES), 1)
    grp = lane >> 5
    qf = q_ref[...].astype(F32)

    def scores(u):
        h, c = u // 2, u % 2
        ps = slice((h // 2) * LANES, (h // 2 + 1) * LANES)
        qm = jnp.where(grp == 2 * (h % 2) + c, qf[:, ps], 0.0).astype(BF16)
        return [_dot(qm, kr[ps, :]) for kr, _ in segs]

    outs = []
    s_next = scores(0)
    for h in range(DF_HEADS):
        ps = slice((h // 2) * LANES, (h // 2 + 1) * LANES)
        hh = h % 2
        parts = []
        for c in range(2):
            s_list = s_next
            if 2 * h + c + 1 < 2 * DF_HEADS:
                s_next = scores(2 * h + c + 1)
            parts.append(_softmax_parts(s_list))
        (p0, l0), (p1, l1) = parts
        ratio = lam * l0 / l1
        pv = functools.reduce(jnp.add, [
            _dot((a0 - a1 * ratio).astype(BF16), vr[:, ps])
            for a0, a1, (_, vr) in zip(p0, p1, segs)]) * (1.0 / l0)
        valid = (lane >> 6) == hh
        ms = jnp.sum(jnp.where(valid, pv * pv, 0.0), axis=1, keepdims=True) * (1.0 / (2 * DF_DIM))
        outs.append(pv * lax.rsqrt(ms + EPS) * g_ref[:, ps] * (1.0 - lam_init))
    o_ref[:, 2 * LANES:3 * LANES] = jnp.where(lane < 2 * DF_DIM, outs[0], outs[1]).astype(BF16)
    o_ref[:, 3 * LANES:4 * LANES] = jnp.where(lane < 2 * DF_DIM, outs[2], outs[3]).astype(BF16)


def _attn_call(latent, l, lam_init, qkv, dqkv, wts, ctx=None, dctx=None):
    b, s, _ = qkv[0].shape
    nb = 1 if latent else NB_CTX
    tq = min(TQ, s)
    grid = (b // nb, s // tq)

    def group(wq, wk, wv, ctx_pair):
        specs = [pl.BlockSpec((nb, tq, wq), lambda i, j: (i, j, 0)),
                 pl.BlockSpec((nb, wk, s), lambda i, j: (i, 0, 0)),
                 pl.BlockSpec((nb, s, wv), lambda i, j: (i, 0, 0))]
        if latent:
            t = ctx_pair[1].shape[2]
            specs += [pl.BlockSpec((None, nb, wk, t), lambda i, j: (l, i, 0, 0)),
                      pl.BlockSpec((None, nb, t, wv), lambda i, j: (l, i, 0, 0))]
        return specs

    in_specs = group(512, 512, 256, ctx) + group(256, 256, 256, dctx)
    args = list(qkv) + (list(ctx) if latent else []) + list(dqkv) + (list(dctx) if latent else [])
    for name in ("lam_q1", "lam_k1", "lam_q2", "lam_k2"):
        in_specs.append(pl.BlockSpec((None, 1, DF_DIM), lambda i, j: (l, 0, 0)))
        args.append(wts[name])
    in_specs.append(pl.BlockSpec((None, 1, 256), lambda i, j: (l, 0, 0)))
    args.append(wts["g_subln"])
    return pl.pallas_call(
        functools.partial(_attn_kernel, latent, lam_init, nb),
        out_shape=jax.ShapeDtypeStruct((b, s, 512), BF16),
        grid=grid, in_specs=in_specs,
        out_specs=pl.BlockSpec((nb, tq, 512), lambda i, j: (i, j, 0)),
        compiler_params=_cparams(("arbitrary", "arbitrary")),
        name="attn_lat" if latent else "attn_ctx",
    )(*args)


def _log_sigmoid(x):
    return jnp.minimum(x, 0.0) - jnp.log1p(jnp.exp(-jnp.abs(x)))


def _mlstm_chunk(d, c, mq_ref, mk_ref, mv_ref, gt_ref, s_ref, m_ref, h_ref):
    L = ML_CHUNK
    rows = pl.ds(pl.multiple_of(c * L, L), L)
    s_i = lax.broadcasted_iota(jnp.int32, (L, L), 0)
    t_i = lax.broadcasted_iota(jnp.int32, (L, L), 1)
    mask = (s_i <= t_i) if d == 0 else (s_i >= t_i)
    tri = jnp.where(mask, 1.0, 0.0).astype(BF16)

    ig = gt_ref[16 * d:16 * d + 8, rows]
    lf = _log_sigmoid(gt_ref[16 * d + 8:16 * d + 16, rows])
    hi = lf.astype(BF16).astype(F32)
    r1 = lf - hi
    mid = r1.astype(BF16).astype(F32)
    parts = _dot(jnp.concatenate([hi, mid, r1 - mid], axis=0).astype(BF16), tri)
    bc = parts[0:8] + parts[8:16] + parts[16:24]
    rvec = ig - bc
    total = jnp.sum(lf, axis=1, keepdims=True)
    mm = m_ref[d]
    gvec = total + rvec
    m_new = jnp.maximum(total + mm, jnp.max(gvec, axis=1, keepdims=True))
    ws = jnp.exp(gvec - m_new).astype(BF16)
    cdec = jnp.exp(total + mm - m_new)
    m_ref[d] = m_new

    rv_t = jnp.concatenate([rvec, jnp.zeros((LANES - 8, L), F32)], axis=0).T

    s_old = [s_ref[d, pair].astype(BF16) for pair in range(2)]
    upd = []
    for h in range(ML_HEADS):
        pair = h // 2
        hs = slice(h * ML_DV, (h + 1) * ML_DV)
        qt = mq_ref[hs, rows]
        kp = mk_ref[rows, pair * LANES:(pair + 1) * LANES]
        vt = mv_ref[hs, rows]
        mmh = mm[h:h + 1, 0:1]
        rm = jnp.where(mask, rv_t[:, h:h + 1], -jnp.inf)
        a = jnp.maximum(jnp.max(rm, axis=0, keepdims=True), mmh)
        wqk = jnp.exp(rm - a) * _dot(kp, qt)
        dec = jnp.exp(mmh - a)
        qc = _dot(s_old[pair], qt)
        num = _dot(vt, wqk.astype(BF16)) + dec * qc[:ML_DV]
        den = jnp.sum(wqk, axis=0, keepdims=True) + dec * qc[ML_DV:ML_DV + 1]
        inv = 1.0 / jnp.maximum(jnp.abs(den), jnp.exp(-(a + bc[h:h + 1, :])))
        h_ref[d, hs, rows] = num * inv
        wsr = ws[h:h + 1, :]
        vaug = jnp.concatenate([vt * wsr, jnp.broadcast_to(wsr, (ML_NROWS, L))], axis=0)
        upd.append(_dot(vaug, kp))

    low = lax.broadcasted_iota(jnp.int32, (ML_DV + ML_NROWS, LANES), 1) < ML_DK
    for pair in range(2):
        h0, h1 = 2 * pair, 2 * pair + 1
        cd = jnp.where(low, cdec[h0:h0 + 1, 0:1], cdec[h1:h1 + 1, 0:1])
        s_ref[d, pair] = cd * s_ref[d, pair] + jnp.where(low, upd[h0], upd[h1])


def _mlstm_kernel(latent, seq, nb, *refs):
    if latent:
        (mq_ref, mk_ref, mv_ref, mo_ref, gt_ref, g_ref, s0_ref, m0_ref,
         o_ref, s_ref, m_ref, h_ref) = refs
        s_ref[...] = s0_ref[...]
        m_ref[...] = m0_ref[...]
    else:
        (mq_ref, mk_ref, mv_ref, mo_ref, gt_ref, g_ref,
         o_ref, cf_ref, nf_ref, mf_ref, s_ref, m_ref, h_ref) = refs
        s_ref[...] = jnp.zeros(s_ref.shape, F32)
        m_ref[...] = jnp.zeros(m_ref.shape, F32)
    nc = seq // ML_CHUNK

    def body(j, carry):
        for n in range(nb):
            views = (mq_ref.at[n], mk_ref.at[n], mv_ref.at[n], gt_ref.at[n], s_ref.at[n],
                     m_ref.at[n], h_ref.at[n])
            _mlstm_chunk(0, j, *views)
            _mlstm_chunk(1, nc - 1 - j, *views)
        return carry

    lax.fori_loop(0, nc, body, 0, unroll=min(nc, 4))

    for n in range(nb):
        for h in range(ML_HEADS):
            hs = slice(h * ML_DV, (h + 1) * ML_DV)
            for j in range(nc):
                ts = slice(j * LANES, (j + 1) * LANES)
                hsum = h_ref[n, 0, hs, ts] + h_ref[n, 1, hs, ts]
                ms = jnp.sum(hsum * hsum, axis=0, keepdims=True) * (1.0 / ML_DV)
                y = hsum * lax.rsqrt(ms + EPS) * g_ref[hs, :]
                o_ref[hs, n * seq + j * LANES:n * seq + (j + 1) * LANES] = (
                    jax.nn.sigmoid(mo_ref[n, hs, ts]) * y).astype(BF16)
    if not latent:
        for n in range(nb):
            for d in range(2):
                for pair in range(2):
                    cf_ref[n, d, pair] = s_ref[n, d, pair, :ML_DV, :].T
        nf_ref[...] = s_ref[:, :, :, ML_DV:, :]
        mf_ref[...] = m_ref[...]


def _mlstm_call(latent, l, mqt, mk, mvt, mot, gt, wts, s0=None, m0=None):
    assert ML_CHUNK == LANES
    b, _, s = mqt.shape
    nb = NB_ML_LAT if latent else NB_ML_CTX

    def feat(w):
        return pl.BlockSpec((nb, w, s), lambda i: (i, 0, 0))

    in_specs = [feat(512), pl.BlockSpec((nb, s, 256), lambda i: (i, 0, 0)), feat(512), feat(512),
                feat(32), pl.BlockSpec((None, 512, LANES), lambda i: (l, 0, 0))]
    args = [mqt, mk, mvt, mot, gt, wts["g_mnorm"]]
    s_spec_shape = (2, 2, ML_DV + ML_NROWS, LANES)
    m_spec_shape = (2, 8, LANES)
    scratch = [pltpu.VMEM((nb,) + s_spec_shape, F32), pltpu.VMEM((nb,) + m_spec_shape, F32),
               pltpu.VMEM((nb, 2, ML_HEADS * ML_DV, s), F32)]
    o_shape = jax.ShapeDtypeStruct((ML_HEADS * ML_DV, b * s), BF16)
    o_spec = pl.BlockSpec((ML_HEADS * ML_DV, nb * s), lambda i: (0, i))
    if latent:
        in_specs += [pl.BlockSpec((nb, None) + s_spec_shape, lambda i: (i, l, 0, 0, 0, 0)),
                     pl.BlockSpec((nb, None) + m_spec_shape, lambda i: (i, l, 0, 0, 0))]
        args += [s0, m0]
        out_shape = o_shape
        out_specs = o_spec
    else:
        half = (2, 2, LANES, LANES)
        out_shape = (o_shape,
                     jax.ShapeDtypeStruct((b,) + half, F32),
                     jax.ShapeDtypeStruct((b, 2, 2, ML_NROWS, LANES), F32),
                     jax.ShapeDtypeStruct((b,) + m_spec_shape, F32))
        out_specs = (o_spec,
                     pl.BlockSpec((nb,) + half, lambda i: (i, 0, 0, 0, 0)),
                     pl.BlockSpec((nb, 2, 2, ML_NROWS, LANES), lambda i: (i, 0, 0, 0, 0)),
                     pl.BlockSpec((nb,) + m_spec_shape, lambda i: (i, 0, 0, 0)))
    return pl.pallas_call(
        functools.partial(_mlstm_kernel, latent, s, nb),
        out_shape=out_shape, grid=(b // nb,), in_specs=in_specs, out_specs=out_specs,
        scratch_shapes=scratch,
        compiler_params=_cparams(("arbitrary",)),
        name="mlstm_lat" if latent else "mlstm_ctx",
    )(*args)


def _post_kernel(nt_ctx, seg_ctx, split_x, final, *refs):
    n_x = 2 if split_x else 1
    n_o = 2 if final else 1
    x_refs, refs = refs[:n_x], refs[n_x:]
    (oab_c, oab_l, oc_c, oc_l, wo_ref, wu_ref, wd_ref, gt1_ref, sh2_ref, sc2_ref, gt2_ref, g2_ref,
     cw_ref, cb_ref, gf_ref) = refs[:15]
    o_refs = refs[15:15 + n_o]
    wo_s, wu_s, wd_s, x1_ref, h2_ref, act_ref = refs[15 + n_o:]
    i = pl.program_id(0)

    @pl.when(i < N_CAST)
    def _():
        wu_s[:, pl.ds(pl.multiple_of(i * CAST_UP, CAST_UP), CAST_UP)] = wu_ref[...].astype(BF16)
        wd_s[pl.ds(pl.multiple_of(i * CAST_DOWN, CAST_DOWN), CAST_DOWN), :] = wd_ref[...].astype(BF16)

    @pl.when(i < N_CAST_OUT)
    def _():
        wo_s[pl.ds(pl.multiple_of(i * CAST_OUT, CAST_OUT), CAST_OUT), :] = wo_ref[...].astype(BF16)

    @pl.when(i >= N_CAST)
    def _():
        t = i - N_CAST
        is_ctx = t < nt_ctx

        def pick(a_ref, b_ref):
            return jnp.where(is_ctx, a_ref[...], b_ref[...])

        x = pick(*x_refs) if split_x else x_refs[0][...]
        mix = (_dot(pick(oab_c, oab_l), wo_s[:2 * 256, :])
               + lax.dot_general(pick(oc_c, oc_l), wo_s[2 * 256:, :], (((0,), (0,)), ((), ())),
                                 preferred_element_type=F32))
        x1 = x + gt1_ref[...] * mix
        x1_ref[...] = x1
        h2 = _rms_rows(x1, g2_ref[...], D_MODEL) * (1.0 + sc2_ref[...]) + sh2_ref[...]
        h2_ref[...] = h2.astype(BF16)

        seg = jnp.where(is_ctx, seg_ctx, GRID_W)
        row = lax.broadcasted_iota(jnp.int32, (TM_FFN, FC), 0)
        first_w = (row & (GRID_W - 1)) == 0
        last_w = (row & (GRID_W - 1)) == GRID_W - 1
        seg_first = (row & (seg - 1)) == 0
        seg_last = (row & (seg - 1)) == seg - 1

        def conv(u, cs):
            prev = pltpu.roll(u, 1, 0)
            prev = jnp.where(first_w, jnp.where(seg_first, 0.0, prev), prev)
            nxt = pltpu.roll(u, TM_FFN - 1, 0)
            nxt = jnp.where(last_w, jnp.where(seg_last, 0.0, nxt), nxt)
            return (cb_ref[:, cs] + prev * cw_ref[0:1, cs] + u * cw_ref[1:2, cs]
                    + nxt * cw_ref[2:3, cs])

        for j in range(N_FC):
            vs = slice(j * FC, (j + 1) * FC)
            gs = slice(D_FF + j * FC, D_FF + (j + 1) * FC)
            val = conv(_dot(h2_ref[...], wu_s[:, vs]), vs)
            gate = conv(_dot(h2_ref[...], wu_s[:, gs]), gs)
            act_ref[:, vs] = (gate * jax.nn.sigmoid(gate) * val).astype(BF16)

        x2 = x1_ref[...] + gt2_ref[...] * _dot(act_ref[...], wd_s[...])
        if not final:
            o_refs[0][...] = x2
        else:
            y = _rms_rows(x2, gf_ref[...], D_MODEL)

            @pl.when(is_ctx)
            def _():
                o_refs[0][...] = y

            @pl.when(jnp.logical_not(is_ctx))
            def _():
                o_refs[1][...] = y


def _post_call(l, final, xs, n_ctx, s_ctx, s_lat, oab, oc, mods, wts):
    split_x = len(xs) == 2
    n_lat = oab[1].shape[0]
    nt_ctx, nt_lat = n_ctx // TM_FFN, n_lat // TM_FFN
    nt = nt_ctx + nt_lat
    tiles_per_batch = s_lat // TM_FFN
    assert s_ctx <= TM_FFN and TM_FFN % s_ctx == 0 and s_lat % TM_FFN == 0

    def tile(i):
        return jnp.maximum(i - N_CAST, 0)

    def ctx_t(i):
        return jnp.minimum(tile(i), nt_ctx - 1)

    def lat_t(i):
        return jnp.maximum(tile(i) - nt_ctx, 0)

    def mod(chunk):
        def index(i):
            row = jnp.where(tile(i) < nt_ctx, 0, 1 + lat_t(i) // tiles_per_batch)
            return (l * 48 + row * 6 + chunk, 0, 0)
        return pl.BlockSpec((None, 1, D_MODEL), index)

    def resident(*shape):
        nd = len(shape)
        return pl.BlockSpec((None,) + shape, lambda i: (l,) + (0,) * nd,
                            pipeline_mode=pl.Buffered(1))

    if split_x:
        x_specs = [pl.BlockSpec((TM_FFN, D_MODEL), lambda i: (ctx_t(i), 0)),
                   pl.BlockSpec((TM_FFN, D_MODEL), lambda i: (lat_t(i), 0))]
    else:
        x_specs = [pl.BlockSpec((TM_FFN, D_MODEL), lambda i: (tile(i), 0))]
    in_specs = x_specs + [
        pl.BlockSpec((TM_FFN, 512), lambda i: (ctx_t(i), 0)),
        pl.BlockSpec((TM_FFN, 512), lambda i: (lat_t(i), 0)),
        pl.BlockSpec((ML_HEADS * ML_DV, TM_FFN), lambda i: (0, ctx_t(i))),
        pl.BlockSpec((ML_HEADS * ML_DV, TM_FFN), lambda i: (0, lat_t(i))),
        pl.BlockSpec((None, CAST_OUT, D_MODEL), lambda i: (l, jnp.minimum(i, N_CAST_OUT - 1), 0)),
        pl.BlockSpec((None, D_MODEL, CAST_UP), lambda i: (l, 0, jnp.minimum(i, N_CAST - 1))),
        pl.BlockSpec((None, CAST_DOWN, D_MODEL), lambda i: (l, jnp.minimum(i, N_CAST - 1), 0)),
        mod(2), mod(3), mod(4), mod(5),
        resident(1, D_MODEL), resident(CONV_W, 2 * D_FF), resident(1, 2 * D_FF),
        pl.BlockSpec((1, D_MODEL), lambda i: (0, 0)),
    ]
    args = list(xs) + [oab[0], oab[1], oc[0], oc[1], wts["w_out"], wts["w_up"], wts["w_down"],
                       mods, mods, mods, mods, wts["g_norm2"], wts["conv_w"], wts["conv_b"],
                       wts["g_final"]]
    if final:
        out_shape = (jax.ShapeDtypeStruct((n_ctx, D_MODEL), F32),
                     jax.ShapeDtypeStruct((n_lat, D_MODEL), F32))
        out_specs = (pl.BlockSpec((TM_FFN, D_MODEL), lambda i: (ctx_t(i), 0)),
                     pl.BlockSpec((TM_FFN, D_MODEL), lambda i: (lat_t(i), 0)))
    else:
        out_shape = jax.ShapeDtypeStruct((n_ctx + n_lat, D_MODEL), F32)
        out_specs = pl.BlockSpec((TM_FFN, D_MODEL), lambda i: (tile(i), 0))
    return pl.pallas_call(
        functools.partial(_post_kernel, nt_ctx, s_ctx, split_x, final),
        out_shape=out_shape, grid=(N_CAST + nt,), in_specs=in_specs, out_specs=out_specs,
        scratch_shapes=[pltpu.VMEM((D_MODEL, D_MODEL), BF16), pltpu.VMEM((D_MODEL, 2 * D_FF), BF16),
                        pltpu.VMEM((D_FF, D_MODEL), BF16),
                        pltpu.VMEM((TM_FFN, D_MODEL), F32), pltpu.VMEM((TM_FFN, D_MODEL), BF16),
                        pltpu.VMEM((TM_FFN, D_FF), BF16)],
        compiler_params=pltpu.CompilerParams(dimension_semantics=("arbitrary",),
                                             vmem_limit_bytes=VMEM_LIMIT_FFN),
        name="post",
    )(*args)


W_IN_BODY = (352, 2656)


def _pack_in_kernel(w_ref, o_ref):
    tc = w_ref.shape[1]

    def rows(lo, n):
        return w_ref[lo:lo + n, :]

    def zeros(n):
        return jnp.zeros((n, tc), F32)

    o_ref[C_CQ:C_CQ + MLA_Q_RANK, :] = rows(0, MLA_Q_RANK).astype(BF16)
    o_ref[C_CQ + MLA_Q_RANK:C_CKV, :] = zeros(C_CKV - C_CQ - MLA_Q_RANK).astype(BF16)
    o_ref[C_CKV:C_AUX, :] = rows(MLA_Q_RANK, MLA_KV_RANK).astype(BF16)
    aux = [rows(MLA_Q_RANK + MLA_KV_RANK, MLA_ROPE)]
    for g in range(4):
        aux += [rows(W_IN_BODY[1] + ML_HEADS * g, ML_HEADS), zeros(8 - ML_HEADS)]
    aux.append(zeros(LANES - AUX_GATE - 32))
    o_ref[C_AUX:C_DQ, :] = jnp.concatenate(aux, axis=0).astype(BF16)
    o_ref[C_DQ:, :] = rows(W_IN_BODY[0], W_IN_BODY[1] - W_IN_BODY[0]).astype(BF16)


def _pack_in_call(w_in):
    w_in_t = jnp.swapaxes(w_in, 1, 2)
    tc = 256
    return pl.pallas_call(
        _pack_in_kernel,
        out_shape=jax.ShapeDtypeStruct((DEPTH, NP_IN, D_MODEL), BF16),
        grid=(DEPTH, D_MODEL // tc),
        in_specs=[pl.BlockSpec((None, w_in_t.shape[1], tc), lambda l, i: (l, 0, i))],
        out_specs=pl.BlockSpec((None, NP_IN, tc), lambda l, i: (l, 0, i)),
        compiler_params=_cparams(("arbitrary", "arbitrary")),
        name="pack_w_in",
    )(w_in_t)


def _pack_weights(w_in, g_cq, w_uq, g_ckv, w_ukv, b_gate, g_subln, g_mnorm, g_norm1, g_norm2,
                  w_out, w_up, conv_w, conv_b, w_down, g_final, lam_q1, lam_k1, lam_q2, lam_k2):
    def cols(a, lo, n, pad=0):
        blk = a[..., lo:lo + n]
        if pad:
            blk = jnp.pad(blk, [(0, 0)] * (a.ndim - 1) + [(0, pad)])
        return blk

    w_in_p = _pack_in_call(w_in)

    hd = MLA_NOPE + MLA_ROPE
    w_uq_p = jnp.pad(w_uq.reshape(DEPTH, MLA_Q_RANK, MLA_HEADS, hd),
                     [(0, 0), (0, 256 - MLA_Q_RANK), (0, 0), (0, LANES - hd)])
    w_uq_p = w_uq_p.reshape(DEPTH, 256, MLA_HEADS * LANES).astype(BF16)

    w_ukv4 = w_ukv.reshape(DEPTH, MLA_KV_RANK, MLA_HEADS, MLA_NOPE + MLA_V)
    w_k = jnp.pad(w_ukv4[..., :MLA_NOPE], [(0, 0), (0, 0), (0, 0), (0, LANES - MLA_NOPE)])
    w_k = w_k.reshape(DEPTH, MLA_KV_RANK, MLA_HEADS * LANES)
    j = jnp.arange(LANES)[:, None]
    cix = jnp.arange(MLA_HEADS * LANES)[None, :]
    place = ((j < MLA_ROPE) & ((cix % LANES) == MLA_NOPE + j)).astype(F32)
    w_kk = jnp.concatenate([w_k, jnp.broadcast_to(place, (DEPTH, LANES, MLA_HEADS * LANES))],
                           axis=1).astype(BF16)
    w_v = w_ukv4[..., MLA_NOPE:].reshape(DEPTH, MLA_KV_RANK, MLA_HEADS * MLA_V).astype(BF16)

    return dict(
        w_in=w_in_p, w_uq=w_uq_p, w_kk=w_kk, w_v=w_v,
        g_norm1=g_norm1[:, None, :], g_norm2=g_norm2[:, None, :],
        g_cq=jnp.pad(g_cq, [(0, 0), (0, 256 - MLA_Q_RANK)])[:, None, :],
        g_ckv=g_ckv[:, None, :],
        b_gate=jnp.pad(jnp.pad(b_gate.reshape(DEPTH, 4, ML_HEADS), [(0, 0), (0, 0), (0, 4)])
                       .reshape(DEPTH, 32), [(0, 0), (AUX_GATE, LANES - AUX_GATE - 32)])[:, None, :],
        g_subln=jnp.tile(g_subln, (1, DF_HEADS))[:, None, :],
        g_mnorm=jnp.broadcast_to(g_mnorm[:, :, None], (DEPTH, ML_HEADS * ML_DV, LANES)),
        w_out=w_out, w_up=w_up, w_down=w_down,
        conv_w=conv_w, conv_b=conv_b[:, None, :], g_final=g_final[None, :],
        lam_q1=lam_q1[:, None, :], lam_k1=lam_k1[:, None, :],
        lam_q2=lam_q2[:, None, :], lam_k2=lam_k2[:, None, :],
    )


def _rope_tables(n_tok):
    t = np.arange(n_tok)
    row = (t // GRID_W).astype(np.float64)
    col = (t % GRID_W).astype(np.float64)
    nf = MLA_ROPE // 4
    inv = ROPE_BASE ** (-np.arange(nf, dtype=np.float64) / nf)
    ar = row[:, None] * inv[None, :]
    ac = col[:, None] * inv[None, :]
    ang = np.concatenate([ar, ar, ac, ac], axis=-1)
    quarter = (np.arange(MLA_ROPE) // nf) % 2
    cos = jnp.asarray(np.cos(ang), F32)
    sin_up = jnp.asarray(np.where(quarter == 0, -np.sin(ang), 0.0), F32)
    sin_dn = jnp.asarray(np.where(quarter == 1, np.sin(ang), 0.0), F32)
    ones = jnp.ones((n_tok, 1), F32)
    zeros = jnp.zeros((n_tok, 1), F32)

    def head_q(t32, fill):
        blk = jnp.concatenate([jnp.tile(fill, (1, MLA_NOPE)), t32, jnp.tile(fill, (1, 32))], axis=1)
        return jnp.tile(blk, (1, MLA_HEADS))

    def aux_k(t32, fill):
        return jnp.concatenate([t32, jnp.tile(fill, (1, LANES - MLA_ROPE))], axis=1)

    tq = (head_q(cos, ones), head_q(sin_up, zeros), head_q(sin_dn, zeros))
    td = tuple(jnp.tile(a, (1, 256 // DF_DIM)) for a in (cos, sin_up, sin_dn))
    tk = (aux_k(cos, ones), aux_k(sin_up, zeros), aux_k(sin_dn, zeros))
    return tq + td + tk


def kernel(x_prompt, x_sample, cache_mla_ckv, cache_mla_krope, cache_diff_k, cache_diff_v,
           state_mlstm_C, state_mlstm_n, state_mlstm_m, c, c_ctx, w_ada, b_ada, g_norm1, w_in,
           g_cq, w_uq, g_ckv, w_ukv, lam_q1, lam_k1, lam_q2, lam_k2, g_subln, b_gate, g_mnorm,
           w_out, g_norm2, w_up, conv_w, conv_b, w_down, g_final):
    bp, sp, _ = x_prompt.shape
    bl, sl, _ = x_sample.shape
    t_len = cache_mla_ckv.shape[2]

    wts = _pack_weights(w_in, g_cq, w_uq, g_ckv, w_ukv, b_gate, g_subln, g_mnorm, g_norm1, g_norm2,
                        w_out, w_up, conv_w, conv_b, w_down, g_final, lam_q1, lam_k1, lam_q2, lam_k2)
    tables = _rope_tables(sl)

    cond = jnp.concatenate([c_ctx[None, :], c, jnp.zeros((8 - 1 - bl, D_MODEL), F32)], axis=0)
    mods = _ada_call(cond, w_ada, b_ada[:, None, :])
    mods = mods.reshape(DEPTH * 8 * N_MOD, 1, D_MODEL)

    def feat_major(a):
        return jnp.transpose(a, (0, 1, 3, 4, 2)).reshape(bl, DEPTH, 256, t_len)

    kctx, vctx, cdk, cdv = _ctxkv_call(
        cache_mla_ckv, jnp.swapaxes(cache_mla_krope, 2, 3), feat_major(cache_diff_k),
        feat_major(cache_diff_v), wts["w_kk"], wts["w_v"])
    c0_t = jnp.swapaxes(state_mlstm_C.reshape(bl, DEPTH, 2, 2, LANES, ML_DV), -1, -2)
    n0_r = jnp.broadcast_to(state_mlstm_n.reshape(bl, DEPTH, 2, 2, 1, LANES),
                            (bl, DEPTH, 2, 2, ML_NROWS, LANES))
    s0 = jnp.concatenate([c0_t, n0_r], axis=-2)
    m0 = jnp.broadcast_to(jnp.pad(state_mlstm_m, [(0, 0)] * 3 + [(0, 8 - ML_HEADS)])[..., None],
                          (bl, DEPTH, 2, 8, LANES))

    n_ctx, n_lat = bp * sp, bl * sl
    xs = (x_prompt.reshape(n_ctx, D_MODEL), x_sample.reshape(n_lat, D_MODEL))
    col = [[] for _ in range(7)]
    for l in range(DEPTH):
        lam_init = 0.8 - 0.6 * math.exp(-0.3 * l)
        final = l == DEPTH - 1
        x_ctx, x_lat, lat_off = (xs[0], xs[1], 0) if len(xs) == 2 else (xs[0], xs[0], n_ctx)
        (q, k, v, dq, dk, dv, mq, mkt, mv, mo, gt, ckv_s, kr_s, dk_s, dv_s) = _pre_call(
            False, l, x_ctx, 0, bp, sp, mods, wts, None)
        oab_c = _attn_call(False, l, lam_init, (q, k, v), (dq, dk, dv), wts)
        oc_c, c_fin, n_fin, m_fin = _mlstm_call(False, l, mq, mkt, mv, mo, gt, wts)
        col[0].append(ckv_s)
        col[1].append(kr_s)
        col[2].append(dk_s)
        col[3].append(dv_s)
        col[4].append(c_fin.reshape(bp, 2, ML_HEADS, ML_DK, ML_DV))
        col[5].append(n_fin[..., 0, :].reshape(bp, 2, ML_HEADS, ML_DK))
        col[6].append(m_fin[:, :, :ML_HEADS, 0])
        (q, k, v, dq, dk, dv, mq, mkt, mv, mo, gt) = _pre_call(
            True, l, x_lat, lat_off, bl, sl, mods, wts, tables)
        oab_l = _attn_call(True, l, lam_init, (q, k, v), (dq, dk, dv), wts, (kctx, vctx), (cdk, cdv))
        oc_l = _mlstm_call(True, l, mq, mkt, mv, mo, gt, wts, s0, m0)
        out = _post_call(l, final, xs, n_ctx, sp, sl,
                         (oab_c.reshape(n_ctx, 512), oab_l.reshape(n_lat, 512)), (oc_c, oc_l),
                         mods, wts)
        xs = out if final else (out,)

    xp = xs[0].reshape(bp, sp, D_MODEL)
    xs = xs[1].reshape(bl, sl, D_MODEL)
    st = [jnp.stack(a, axis=1) for a in col]
    st[1] = jnp.swapaxes(st[1], 2, 3)
    for i in (2, 3):
        st[i] = jnp.transpose(st[i].reshape(bp, DEPTH, DF_HEADS, 2 * DF_DIM, sp), (0, 1, 4, 2, 3))
    return (xp, xs) + tuple(st)
```

```python
import functools
import math

import jax
import jax.numpy as jnp
import numpy as np
from jax import lax
from jax.experimental import pallas as pl
from jax.experimental.pallas import tpu as pltpu

F32 = jnp.float32
BF16 = jnp.bfloat16

D_MODEL = 1024
DEPTH = 4
GRID_W = 64
N_MOD = 6
EPS = 1e-6
ROPE_BASE = 10000.0
MLA_HEADS = 4
MLA_Q_RANK = 192
MLA_KV_RANK = 128
MLA_NOPE = 64
MLA_ROPE = 32
MLA_V = 64
DF_HEADS = 4
DF_DIM = 32
ML_HEADS = 4
ML_DK = 64
ML_DV = 128
D_FF = 2816
CONV_W = 3

LANES = 128
VMEM_LIMIT = 48 * 1024 * 1024
VMEM_LIMIT_FFN = 56 * 1024 * 1024

C_CQ, C_CKV, C_AUX, C_DQ, C_DK, C_DV, C_MQ, C_MK, C_MV, C_MO = (
    0, 256, 384, 512, 768, 1024, 1280, 1536, 1792, 2304)
NP_IN = 2816
AUX_GATE = 32

TM_PRE = 256
PRE_SUB = 2
TQ = 512
NB_CTX = 4
ML_CHUNK = 128
ML_NROWS = 128
NB_ML_CTX = 4
NB_ML_LAT = 2
TM_FFN = 512
POST_SUB = 1
FC = 256
N_FC = D_FF // FC
N_CAST = 22
CAST_UP = 2 * D_FF // N_CAST
CAST_DOWN = D_FF // N_CAST
CAST_OUT = 128
N_CAST_OUT = D_MODEL // CAST_OUT

NT = (((1,), (1,)), ((), ()))
LOG2E = 1.4426950408889634


def _cparams(sem):
    return pltpu.CompilerParams(dimension_semantics=sem, vmem_limit_bytes=VMEM_LIMIT)


def _dot(a, b):
    return jnp.dot(a, b, preferred_element_type=F32)


def _dot_nt(a, b):
    return lax.dot_general(a, b, NT, preferred_element_type=F32)


def _rms_rows(x, g, n):
    ms = jnp.sum(x * x, axis=-1, keepdims=True) * (1.0 / n)
    return x * lax.rsqrt(ms + EPS) * g


def _rope(x, cos, sin_up, sin_dn):
    w = x.shape[-1]
    return x * cos + pltpu.roll(x, w - 8, 1) * sin_up + pltpu.roll(x, 8, 1) * sin_dn


def _ada_kernel(c_ref, w_ref, b_ref, o_ref):
    c = c_ref[...]
    s = (c * jax.nn.sigmoid(c)).astype(BF16)
    o_ref[...] = _dot(s, w_ref[...].astype(BF16)) + b_ref[...]


def _ada_call(cond, w_ada, b_ada):
    nt = 1024
    return pl.pallas_call(
        _ada_kernel,
        out_shape=jax.ShapeDtypeStruct((DEPTH, 8, N_MOD * D_MODEL), F32),
        grid=(DEPTH, N_MOD * D_MODEL // nt),
        in_specs=[pl.BlockSpec((8, D_MODEL), lambda l, j: (0, 0)),
                  pl.BlockSpec((None, D_MODEL, nt), lambda l, j: (l, 0, j)),
                  pl.BlockSpec((None, 1, nt), lambda l, j: (l, 0, j))],
        out_specs=pl.BlockSpec((None, 8, nt), lambda l, j: (l, 0, j)),
        compiler_params=_cparams(("arbitrary", "arbitrary")),
        name="ada_mod",
    )(cond, w_ada, b_ada)


def _ctxkv_kernel(ckv_ref, krt_ref, dkt_ref, dvt_ref, wkk_ref, wv_ref, k_ref, v_ref, dkb_ref, dvb_ref):
    t = ckv_ref.shape[0]
    ckv = ckv_ref[...].astype(BF16)
    kr = jnp.concatenate([krt_ref[...], jnp.zeros((LANES - MLA_ROPE, t), F32)], axis=0).T
    kin = jnp.concatenate([ckv, kr.astype(BF16)], axis=1)
    k_ref[...] = _dot(kin, wkk_ref[...]).T.astype(BF16)
    v_ref[...] = _dot(ckv, wv_ref[...]).astype(BF16)
    dkb_ref[...] = dkt_ref[...].astype(BF16)
    dvb_ref[...] = dvt_ref[...].T.astype(BF16)


def _ctxkv_call(cache_ckv, cache_kr_t, cache_dk_t, cache_dv_t, wkk, wv):
    b, _, t, _ = cache_ckv.shape

    def cache_t(w):
        return pl.BlockSpec((None, None, w, t), lambda l, i: (i, l, 0, 0))

    def out(w):
        return pl.BlockSpec((None, None, t, w), lambda l, i: (l, i, 0, 0))

    def out_t(w):
        return pl.BlockSpec((None, None, w, t), lambda l, i: (l, i, 0, 0))

    return pl.pallas_call(
        _ctxkv_kernel,
        out_shape=(jax.ShapeDtypeStruct((DEPTH, b, 512, t), BF16),
                   jax.ShapeDtypeStruct((DEPTH, b, t, 256), BF16),
                   jax.ShapeDtypeStruct((DEPTH, b, 256, t), BF16),
                   jax.ShapeDtypeStruct((DEPTH, b, t, 256), BF16)),
        grid=(DEPTH, b),
        in_specs=[pl.BlockSpec((None, None, t, MLA_KV_RANK), lambda l, i: (i, l, 0, 0)),
                  cache_t(MLA_ROPE), cache_t(256), cache_t(256),
                  pl.BlockSpec((None, 256, 512), lambda l, i: (l, 0, 0)),
                  pl.BlockSpec((None, MLA_KV_RANK, 256), lambda l, i: (l, 0, 0))],
        out_specs=(out_t(512), out(256), out_t(256), out(256)),
        compiler_params=_cparams(("arbitrary", "arbitrary")),
        name="ctx_kv",
    )(cache_ckv, cache_kr_t, cache_dk_t, cache_dv_t, wkk, wv)


def _pre_kernel(latent, kinds, *refs):
    for n in range(PRE_SUB):
        views = []
        for kind, r in zip(kinds, refs):
            if kind == "alias":
                continue
            if kind == "rows":
                r = r.at[n * TM_PRE:(n + 1) * TM_PRE]
            elif kind == "lanes":
                r = r.at[:, n * TM_PRE:(n + 1) * TM_PRE]
            elif kind == "batch":
                r = r.at[n]
            views.append(r)
        _pre_tile(latent, *views)


def _pre_tile(latent, *refs):
    (x_ref, sh_ref, sc_ref, g1_ref, win_ref, gcq_ref, wuq_ref, gckv_ref, wkk_ref, wv_ref,
     bg_ref) = refs[:11]
    refs = refs[11:]
    if latent:
        (cq_t, sqa_t, sqb_t, cd_t, sda_t, sdb_t, ck_t, ska_t, skb_t) = refs[:9]
        refs = refs[9:]
    (q_ref, k_ref, v_ref, dq_ref, dk_ref, dv_ref, mq_ref, mk_ref, mv_ref, mo_ref,
     gt_ref) = refs[:11]
    refs = refs[11:]
    if not latent:
        ckv_out, kr_out, dk_out, dv_out = refs

    x = x_ref[...]
    h = _rms_rows(x, g1_ref[...], D_MODEL) * (1.0 + sc_ref[...]) + sh_ref[...]
    proj = _dot_nt(h.astype(BF16), win_ref[...])

    cq = _rms_rows(proj[:, C_CQ:C_CQ + 256], gcq_ref[...], MLA_Q_RANK)
    q = _dot(cq.astype(BF16), wuq_ref[...])
    if latent:
        q = _rope(q, cq_t[...], sqa_t[...], sqb_t[...])
    q_ref[...] = (q * ((MLA_NOPE + MLA_ROPE) ** -0.5 * LOG2E)).astype(BF16)

    c_kv = _rms_rows(proj[:, C_CKV:C_CKV + MLA_KV_RANK], gckv_ref[...], MLA_KV_RANK)
    aux = proj[:, C_AUX:C_AUX + LANES] + bg_ref[...]
    if latent:
        aux = _rope(aux, ck_t[...], ska_t[...], skb_t[...])
    aux_t = aux.T
    gt_ref[...] = aux_t[AUX_GATE:AUX_GATE + 32, :]
    if not latent:
        ckv_out[...] = c_kv
        kr_out[...] = aux_t[:MLA_ROPE, :]
    ckv_b = c_kv.astype(BF16)
    kin = jnp.concatenate([ckv_b, aux.astype(BF16)], axis=1)
    k_ref[...] = _dot(kin, wkk_ref[...]).T.astype(BF16)
    v_ref[...] = _dot(ckv_b, wv_ref[...]).astype(BF16)

    dq = proj[:, C_DQ:C_DQ + 256]
    dk = proj[:, C_DK:C_DK + 256]
    dv = proj[:, C_DV:C_DV + 256]
    if not latent:
        dk_t = dk.T
        dk_out[...] = dk_t
        dv_out[...] = dv.T
    else:
        dq = _rope(dq, cd_t[...], sda_t[...], sdb_t[...])
        dk_t = _rope(dk, cd_t[...], sda_t[...], sdb_t[...]).T
    dq_ref[...] = (dq * (DF_DIM ** -0.5 * LOG2E)).astype(BF16)
    dk_ref[...] = dk_t.astype(BF16)
    dv_ref[...] = dv.astype(BF16)

    lane = lax.broadcasted_iota(jnp.int32, (x.shape[0], LANES), 1)
    for h in range(ML_HEADS):
        blk = proj[:, C_MQ + (h // 2) * LANES:C_MQ + (h // 2 + 1) * LANES]
        mq_ref[h * LANES:(h + 1) * LANES, :] = (
            jnp.where((lane >> 6) == h % 2, blk, 0.0).T.astype(BF16))
    mk_ref[...] = (proj[:, C_MK:C_MK + 256] * (ML_DK ** -0.5)).astype(BF16)
    mv_ref[...] = proj[:, C_MV:C_MV + 512].T.astype(BF16)
    mo_ref[...] = proj[:, C_MO:C_MO + 512].T


def _pre_call(latent, l, x2d, tok_off, b, s, mods, wts, tables, state_bufs=None):
    ns = s // TM_PRE
    tile_off = tok_off // TM_PRE
    tm2 = PRE_SUB * TM_PRE
    seq_split = ns > 1
    assert (ns % PRE_SUB == 0) if seq_split else (b % PRE_SUB == 0 and ns == 1)
    assert tile_off % PRE_SUB == 0
    grid = (ns // PRE_SUB, b) if seq_split else (1, b // PRE_SUB)

    def tok(width):
        if seq_split:
            return pl.BlockSpec((None, tm2, width), lambda j, i: (i, j, 0)), "rows"
        return pl.BlockSpec((PRE_SUB, TM_PRE, width), lambda j, i: (i, j, 0)), "batch"

    def feat(width):
        if seq_split:
            return pl.BlockSpec((None, width, tm2), lambda j, i: (i, 0, j)), "lanes"
        return pl.BlockSpec((PRE_SUB, width, TM_PRE), lambda j, i: (i, 0, j)), "batch"

    def mod(chunk):
        if latent:
            return pl.BlockSpec((None, 1, D_MODEL), lambda j, i: (l * 48 + (1 + i) * 6 + chunk, 0, 0))
        return pl.BlockSpec((None, 1, D_MODEL), lambda j, i: (l * 48 + chunk, 0, 0))

    def lw(*shape):
        nd = len(shape)
        return pl.BlockSpec((None,) + shape, lambda j, i: (l,) + (0,) * nd)

    if seq_split:
        x_spec = pl.BlockSpec((tm2, D_MODEL), lambda j, i: ((tile_off + i * ns) // PRE_SUB + j, 0))
    else:
        x_spec = pl.BlockSpec((tm2, D_MODEL), lambda j, i: (tile_off // PRE_SUB + i, 0))
    in_specs = [x_spec, mod(0), mod(1), lw(1, D_MODEL), lw(NP_IN, D_MODEL), lw(1, 256),
                lw(256, 512), lw(1, MLA_KV_RANK), lw(256, 512), lw(MLA_KV_RANK, 256), lw(1, LANES)]
    kinds = ["rows"] + [None] * 10
    args = [x2d, mods, mods, wts["g_norm1"], wts["w_in"], wts["g_cq"], wts["w_uq"], wts["g_ckv"],
            wts["w_kk"], wts["w_v"], wts["b_gate"]]
    if latent:
        for t in tables:
            in_specs.append(pl.BlockSpec((tm2, t.shape[1]), lambda j, i: (j, 0)))
            kinds.append("rows")
            args.append(t)

    widths = [(512, BF16, False), (512, BF16, True), (256, BF16, False), (256, BF16, False),
              (256, BF16, True), (256, BF16, False), (512, BF16, True), (256, BF16, False),
              (512, BF16, True), (512, F32, True), (32, F32, True)]
    out_shape = [jax.ShapeDtypeStruct((b, w, s) if tr else (b, s, w), dt) for w, dt, tr in widths]
    out_pairs = [feat(w) if tr else tok(w) for w, _, tr in widths]
    aliases = {}
    if not latent:
        state = [(MLA_KV_RANK, False), (MLA_ROPE, True), (256, True), (256, True)]
        for k, (w, tr) in enumerate(state):
            if tr:
                spec = pl.BlockSpec((PRE_SUB, None, w, TM_PRE), lambda j, i: (i, l, 0, j))
                shape = (b, DEPTH, w, s)
            else:
                spec = pl.BlockSpec((PRE_SUB, None, TM_PRE, w), lambda j, i: (i, l, j, 0))
                shape = (b, DEPTH, s, w)
            out_shape.append(jax.ShapeDtypeStruct(shape, F32))
            out_pairs.append((spec, "batch"))
            if state_bufs is not None:
                aliases[len(args)] = len(widths) + k
                in_specs.append(pl.BlockSpec(memory_space=pl.ANY))
                kinds.append("alias")
                args.append(state_bufs[k])
    out_specs = tuple(p[0] for p in out_pairs)
    kinds += [p[1] for p in out_pairs]
    return pl.pallas_call(
        functools.partial(_pre_kernel, latent, tuple(kinds)),
        out_shape=tuple(out_shape), grid=grid, in_specs=in_specs, out_specs=out_specs,
        input_output_aliases=aliases,
        compiler_params=_cparams(("arbitrary", "arbitrary")),
        name="pre_lat" if latent else "pre_ctx",
    )(*args)


def _softmax_parts(s_list):
    m = functools.reduce(jnp.maximum, [jnp.max(s, axis=1, keepdims=True) for s in s_list])
    p_list = [jnp.exp2(s - m) for s in s_list]
    l = functools.reduce(jnp.add, [jnp.sum(p, axis=1, keepdims=True) for p in p_list])
    return p_list, l


def _attn_kernel(latent, lam_init, nb, *refs):
    n_seg = 5 if latent else 3
    o_ref = refs[-1]
    mla_in, diff_in, shared = refs[:n_seg], refs[n_seg:2 * n_seg], refs[2 * n_seg:-1]
    for n in range(nb):
        _mla_body(latent, *[r.at[n] for r in mla_in], o_ref.at[n])
        _diff_body(latent, lam_init, *[r.at[n] for r in diff_in], *shared, o_ref.at[n])


def _mla_body(latent, *refs):
    if latent:
        q_ref, k_ref, v_ref, kc_ref, vc_ref, o_ref = refs
        segs = [(kc_ref, vc_ref), (k_ref, v_ref)]
    else:
        q_ref, k_ref, v_ref, o_ref = refs
        segs = [(k_ref, v_ref)]
    lane = lax.broadcasted_iota(jnp.int32, (q_ref.shape[0], LANES), 1)

    def scores(h):
        hs = slice(h * LANES, (h + 1) * LANES)
        return [_dot(q_ref[:, hs], kr[hs, :]) for kr, _ in segs]

    outs = []
    s_next = scores(0)
    for h in range(MLA_HEADS):
        ps = slice((h // 2) * LANES, (h // 2 + 1) * LANES)
        s_list = s_next
        if h + 1 < MLA_HEADS:
            s_next = scores(h + 1)
        p_list, l = _softmax_parts(s_list)
        pv = functools.reduce(jnp.add, [_dot(p.astype(BF16), vr[:, ps])
                                        for p, (_, vr) in zip(p_list, segs)])
        outs.append(pv / l)
    o_ref[:, 0:LANES] = jnp.where(lane < MLA_V, outs[0], outs[1]).astype(BF16)
    o_ref[:, LANES:2 * LANES] = jnp.where(lane < MLA_V, outs[2], outs[3]).astype(BF16)


def _diff_body(latent, lam_init, *refs):
    if latent:
        (q_ref, k_ref, v_ref, kc_ref, vc_ref, lq1, lk1, lq2, lk2, g_ref, o_ref) = refs
        segs = [(kc_ref, vc_ref), (k_ref, v_ref)]
    else:
        (q_ref, k_ref, v_ref, lq1, lk1, lq2, lk2, g_ref, o_ref) = refs
        segs = [(k_ref, v_ref)]
    lam = (jnp.exp(jnp.sum(lq1[...] * lk1[...], axis=1, keepdims=True))
           - jnp.exp(jnp.sum(lq2[...] * lk2[...], axis=1, keepdims=True)) + lam_init)
    lane = lax.broadcasted_iota(jnp.int32, (q_ref.shape[0], LANES), 1)
    grp = lane >> 5
    qf = q_ref[...].astype(F32)

    def scores(u):
        h, c = u // 2, u % 2
        ps = slice((h // 2) * LANES, (h // 2 + 1) * LANES)
        qm = jnp.where(grp == 2 * (h % 2) + c, qf[:, ps], 0.0).astype(BF16)
        return [_dot(qm, kr[ps, :]) for kr, _ in segs]

    outs = []
    s_next = scores(0)
    for h in range(DF_HEADS):
        ps = slice((h // 2) * LANES, (h // 2 + 1) * LANES)
        hh = h % 2
        parts = []
        for c in range(2):
            s_list = s_next
            if 2 * h + c + 1 < 2 * DF_HEADS:
                s_next = scores(2 * h + c + 1)
            parts.append(_softmax_parts(s_list))
        (p0, l0), (p1, l1) = parts
        ratio = lam * l0 / l1
        pv = functools.reduce(jnp.add, [
            _dot((a0 - a1 * ratio).astype(BF16), vr[:, ps])
            for a0, a1, (_, vr) in zip(p0, p1, segs)]) * (1.0 / l0)
        valid = (lane >> 6) == hh
        ms = jnp.sum(jnp.where(valid, pv * pv, 0.0), axis=1, keepdims=True) * (1.0 / (2 * DF_DIM))
        outs.append(pv * lax.rsqrt(ms + EPS) * g_ref[:, ps] * (1.0 - lam_init))
    o_ref[:, 2 * LANES:3 * LANES] = jnp.where(lane < 2 * DF_DIM, outs[0], outs[1]).astype(BF16)
    o_ref[:, 3 * LANES:4 * LANES] = jnp.where(lane < 2 * DF_DIM, outs[2], outs[3]).astype(BF16)


def _attn_call(latent, l, lam_init, qkv, dqkv, wts, ctx=None, dctx=None):
    b, s, _ = qkv[0].shape
    nb = 1 if latent else NB_CTX
    tq = min(TQ, s)
    grid = (b // nb, s // tq)

    def group(wq, wk, wv, ctx_pair):
        specs = [pl.BlockSpec((nb, tq, wq), lambda i, j: (i, j, 0)),
                 pl.BlockSpec((nb, wk, s), lambda i, j: (i, 0, 0)),
                 pl.BlockSpec((nb, s, wv), lambda i, j: (i, 0, 0))]
        if latent:
            t = ctx_pair[1].shape[2]
            specs += [pl.BlockSpec((None, nb, wk, t), lambda i, j: (l, i, 0, 0)),
                      pl.BlockSpec((None, nb, t, wv), lambda i, j: (l, i, 0, 0))]
        return specs

    in_specs = group(512, 512, 256, ctx) + group(256, 256, 256, dctx)
    args = list(qkv) + (list(ctx) if latent else []) + list(dqkv) + (list(dctx) if latent else [])
    for name in ("lam_q1", "lam_k1", "lam_q2", "lam_k2"):
        in_specs.append(pl.BlockSpec((None, 1, DF_DIM), lambda i, j: (l, 0, 0)))
        args.append(wts[name])
    in_specs.append(pl.BlockSpec((None, 1, 256), lambda i, j: (l, 0, 0)))
    args.append(wts["g_subln"])
    return pl.pallas_call(
        functools.partial(_attn_kernel, latent, lam_init, nb),
        out_shape=jax.ShapeDtypeStruct((b, s, 512), BF16),
        grid=grid, in_specs=in_specs,
        out_specs=pl.BlockSpec((nb, tq, 512), lambda i, j: (i, j, 0)),
        compiler_params=_cparams(("arbitrary", "arbitrary")),
        name="attn_lat" if latent else "attn_ctx",
    )(*args)


def _log_sigmoid(x):
    return jnp.minimum(x, 0.0) - jnp.log1p(jnp.exp(-jnp.abs(x)))


def _mlstm_chunk(d, c, mq_ref, mk_ref, mv_ref, gt_ref, s_ref, m_ref, h_ref):
    L = ML_CHUNK
    rows = pl.ds(pl.multiple_of(c * L, L), L)
    s_i = lax.broadcasted_iota(jnp.int32, (L, L), 0)
    t_i = lax.broadcasted_iota(jnp.int32, (L, L), 1)
    mask = (s_i <= t_i) if d == 0 else (s_i >= t_i)
    tri = jnp.where(mask, 1.0, 0.0).astype(BF16)

    ig = gt_ref[16 * d:16 * d + 8, rows]
    lf = _log_sigmoid(gt_ref[16 * d + 8:16 * d + 16, rows])
    hi = lf.astype(BF16).astype(F32)
    r1 = lf - hi
    mid = r1.astype(BF16).astype(F32)
    parts = _dot(jnp.concatenate([hi, mid, r1 - mid], axis=0).astype(BF16), tri)
    bc = parts[0:8] + parts[8:16] + parts[16:24]
    rvec = ig - bc
    total = jnp.sum(lf, axis=1, keepdims=True)
    mm = m_ref[d]
    gvec = total + rvec
    m_new = jnp.maximum(total + mm, jnp.max(gvec, axis=1, keepdims=True))
    ws = jnp.exp(gvec - m_new).astype(BF16)
    cdec = jnp.exp(total + mm - m_new)
    m_ref[d] = m_new

    rv_t = jnp.concatenate([rvec, jnp.zeros((LANES - 8, L), F32)], axis=0).T

    s_old = [s_ref[d, pair].astype(BF16) for pair in range(2)]
    upd = []
    for h in range(ML_HEADS):
        pair = h // 2
        hs = slice(h * ML_DV, (h + 1) * ML_DV)
        qt = mq_ref[hs, rows]
        kp = mk_ref[rows, pair * LANES:(pair + 1) * LANES]
        vt = mv_ref[hs, rows]
        mmh = mm[h:h + 1, 0:1]
        rm = jnp.where(mask, rv_t[:, h:h + 1], -jnp.inf)
        a = jnp.maximum(jnp.max(rm, axis=0, keepdims=True), mmh)
        wqk = jnp.exp(rm - a) * _dot(kp, qt)
        dec = jnp.exp(mmh - a)
        qc = _dot(s_old[pair], qt)
        num = _dot(vt, wqk.astype(BF16)) + dec * qc[:ML_DV]
        den = jnp.sum(wqk, axis=0, keepdims=True) + dec * qc[ML_DV:ML_DV + 1]
        inv = 1.0 / jnp.maximum(jnp.abs(den), jnp.exp(-(a + bc[h:h + 1, :])))
        h_ref[d, hs, rows] = num * inv
        wsr = ws[h:h + 1, :]
        vaug = jnp.concatenate([vt * wsr, jnp.broadcast_to(wsr, (ML_NROWS, L))], axis=0)
        upd.append(_dot(vaug, kp))

    low = lax.broadcasted_iota(jnp.int32, (ML_DV + ML_NROWS, LANES), 1) < ML_DK
    for pair in range(2):
        h0, h1 = 2 * pair, 2 * pair + 1
        cd = jnp.where(low, cdec[h0:h0 + 1, 0:1], cdec[h1:h1 + 1, 0:1])
        s_ref[d, pair] = cd * s_ref[d, pair] + jnp.where(low, upd[h0], upd[h1])


def _mlstm_kernel(latent, seq, nb, has_alias, *refs):
    if latent:
        (mq_ref, mk_ref, mv_ref, mo_ref, gt_ref, g_ref, s0_ref, m0_ref,
         o_ref, s_ref, m_ref, h_ref) = refs
        s_ref[...] = s0_ref[...]
        m_ref[...] = m0_ref[...]
    else:
        if has_alias:
            refs = refs[:6] + refs[7:]
        (mq_ref, mk_ref, mv_ref, mo_ref, gt_ref, g_ref,
         o_ref, cf_ref, nf_ref, mf_ref, s_ref, m_ref, h_ref) = refs
        s_ref[...] = jnp.zeros(s_ref.shape, F32)
        m_ref[...] = jnp.zeros(m_ref.shape, F32)
    nc = seq // ML_CHUNK

    def body(j, carry):
        for n in range(nb):
            views = (mq_ref.at[n], mk_ref.at[n], mv_ref.at[n], gt_ref.at[n], s_ref.at[n],
                     m_ref.at[n], h_ref.at[n])
            _mlstm_chunk(0, j, *views)
            _mlstm_chunk(1, nc - 1 - j, *views)
        return carry

    lax.fori_loop(0, nc, body, 0, unroll=min(nc, 4))

    for n in range(nb):
        for h in range(ML_HEADS):
            hs = slice(h * ML_DV, (h + 1) * ML_DV)
            for j in range(nc):
                ts = slice(j * LANES, (j + 1) * LANES)
                hsum = h_ref[n, 0, hs, ts] + h_ref[n, 1, hs, ts]
                ms = jnp.sum(hsum * hsum, axis=0, keepdims=True) * (1.0 / ML_DV)
                y = hsum * lax.rsqrt(ms + EPS) * g_ref[hs, :]
                o_ref[hs, n * seq + j * LANES:n * seq + (j + 1) * LANES] = (
                    jax.nn.sigmoid(mo_ref[n, hs, ts]) * y).astype(BF16)
    if not latent:
        for n in range(nb):
            for d in range(2):
                for pair in range(2):
                    cf_ref[n, d, pair] = s_ref[n, d, pair, :ML_DV, :].T
        nf_ref[...] = s_ref[:, :, :, ML_DV:ML_DV + 8, :]
        mf_ref[...] = m_ref[...]


def _mlstm_call(latent, l, mqt, mk, mvt, mot, gt, wts, s0=None, m0=None, c_buf=None):
    assert ML_CHUNK == LANES
    b, _, s = mqt.shape
    nb = NB_ML_LAT if latent else NB_ML_CTX

    def feat(w):
        return pl.BlockSpec((nb, w, s), lambda i: (i, 0, 0))

    in_specs = [feat(512), pl.BlockSpec((nb, s, 256), lambda i: (i, 0, 0)), feat(512), feat(512),
                feat(32), pl.BlockSpec((None, 512, LANES), lambda i: (l, 0, 0))]
    args = [mqt, mk, mvt, mot, gt, wts["g_mnorm"]]
    s_spec_shape = (2, 2, ML_DV + ML_NROWS, LANES)
    m_spec_shape = (2, 8, LANES)
    scratch = [pltpu.VMEM((nb,) + s_spec_shape, F32), pltpu.VMEM((nb,) + m_spec_shape, F32),
               pltpu.VMEM((nb, 2, ML_HEADS * ML_DV, s), F32)]
    o_shape = jax.ShapeDtypeStruct((ML_HEADS * ML_DV, b * s), BF16)
    o_spec = pl.BlockSpec((ML_HEADS * ML_DV, nb * s), lambda i: (0, i))
    if latent:
        in_specs += [pl.BlockSpec((nb, None) + s_spec_shape, lambda i: (i, l, 0, 0, 0, 0)),
                     pl.BlockSpec((nb, None) + m_spec_shape, lambda i: (i, l, 0, 0, 0))]
        args += [s0, m0]
        out_shape = o_shape
        out_specs = o_spec
    aliases = {}
    if not latent:
        half = (2, 2, LANES, LANES)
        out_shape = (o_shape,
                     jax.ShapeDtypeStruct((b, DEPTH) + half, F32),
                     jax.ShapeDtypeStruct((b, 2, 2, 8, LANES), F32),
                     jax.ShapeDtypeStruct((b,) + m_spec_shape, F32))
        out_specs = (o_spec,
                     pl.BlockSpec((nb, None) + half, lambda i: (i, l, 0, 0, 0, 0)),
                     pl.BlockSpec((nb, 2, 2, 8, LANES), lambda i: (i, 0, 0, 0, 0)),
                     pl.BlockSpec((nb,) + m_spec_shape, lambda i: (i, 0, 0, 0)))
        if c_buf is not None:
            aliases[len(args)] = 1
            in_specs.append(pl.BlockSpec(memory_space=pl.ANY))
            args.append(c_buf)
    return pl.pallas_call(
        functools.partial(_mlstm_kernel, latent, s, nb, c_buf is not None),
        out_shape=out_shape, grid=(b // nb,), in_specs=in_specs, out_specs=out_specs,
        input_output_aliases=aliases,
        scratch_shapes=scratch,
        compiler_params=_cparams(("arbitrary",)),
        name="mlstm_lat" if latent else "mlstm_ctx",
    )(*args)


def _post_kernel(nt_ctx, seg_ctx, split_x, final, *refs):
    n_x = 2 if split_x else 1
    n_o = 2 if final else 1
    x_refs, refs = refs[:n_x], refs[n_x:]
    (oab_c, oab_l, oc_c, oc_l, wo_ref, wu_ref, wd_ref, gt1_ref, sh2_ref, sc2_ref, gt2_ref, g2_ref,
     cw_ref, cb_ref, gf_ref) = refs[:15]
    o_refs = refs[15:15 + n_o]
    wo_s, wu_s, wd_s, x1_ref, h2_ref, act_ref = refs[15 + n_o:]
    i = pl.program_id(0)

    @pl.when(i < N_CAST)
    def _():
        wu_s[:, pl.ds(pl.multiple_of(i * CAST_UP, CAST_UP), CAST_UP)] = wu_ref[...].astype(BF16)
        wd_s[pl.ds(pl.multiple_of(i * CAST_DOWN, CAST_DOWN), CAST_DOWN), :] = wd_ref[...].astype(BF16)

    @pl.when(i < N_CAST_OUT)
    def _():
        wo_s[pl.ds(pl.multiple_of(i * CAST_OUT, CAST_OUT), CAST_OUT), :] = wo_ref[...].astype(BF16)

    @pl.when(i >= N_CAST)
    def _():
        t = i - N_CAST
        is_ctx = t < nt_ctx

        th = TM_FFN // POST_SUB
        seg = jnp.where(is_ctx, seg_ctx, GRID_W)
        row = lax.broadcasted_iota(jnp.int32, (th, FC), 0)
        first_w = (row & (GRID_W - 1)) == 0
        last_w = (row & (GRID_W - 1)) == GRID_W - 1
        seg_first = (row & (seg - 1)) == 0
        seg_last = (row & (seg - 1)) == seg - 1

        def conv(u, cs):
            prev = pltpu.roll(u, 1, 0)
            prev = jnp.where(first_w, jnp.where(seg_first, 0.0, prev), prev)
            nxt = pltpu.roll(u, th - 1, 0)
            nxt = jnp.where(last_w, jnp.where(seg_last, 0.0, nxt), nxt)
            return (cb_ref[:, cs] + prev * cw_ref[0:1, cs] + u * cw_ref[1:2, cs]
                    + nxt * cw_ref[2:3, cs])

        outs = []
        for hf in range(POST_SUB):
            rs = slice(hf * th, (hf + 1) * th)

            def pick(a, b):
                return jnp.where(is_ctx, a, b)

            x = pick(x_refs[0][rs, :], x_refs[1][rs, :]) if split_x else x_refs[0][rs, :]
            mix = (_dot(pick(oab_c[rs, :], oab_l[rs, :]), wo_s[:2 * 256, :])
                   + lax.dot_general(pick(oc_c[:, rs], oc_l[:, rs]), wo_s[2 * 256:, :],
                                     (((0,), (0,)), ((), ())), preferred_element_type=F32))
            x1 = x + gt1_ref[...] * mix
            x1_ref[rs, :] = x1
            h2 = _rms_rows(x1, g2_ref[...], D_MODEL) * (1.0 + sc2_ref[...]) + sh2_ref[...]
            h2_ref[rs, :] = h2.astype(BF16)

            for j in range(N_FC):
                vs = slice(j * FC, (j + 1) * FC)
                gs = slice(D_FF + j * FC, D_FF + (j + 1) * FC)
                val = conv(_dot(h2_ref[rs, :], wu_s[:, vs]), vs)
                gate = conv(_dot(h2_ref[rs, :], wu_s[:, gs]), gs)
                act_ref[rs, vs] = (gate * jax.nn.sigmoid(gate) * val).astype(BF16)

            x2 = x1_ref[rs, :] + gt2_ref[...] * _dot(act_ref[rs, :], wd_s[...])
            if not final:
                o_refs[0][rs, :] = x2
            else:
                outs.append(_rms_rows(x2, gf_ref[...], D_MODEL))

        if final:
            y = jnp.concatenate(outs, axis=0)

            @pl.when(is_ctx)
            def _():
                o_refs[0][...] = y

            @pl.when(jnp.logical_not(is_ctx))
            def _():
                o_refs[1][...] = y


def _post_call(l, final, xs, n_ctx, s_ctx, s_lat, oab, oc, mods, wts):
    split_x = len(xs) == 2
    n_lat = oab[1].shape[0]
    nt_ctx, nt_lat = n_ctx // TM_FFN, n_lat // TM_FFN
    nt = nt_ctx + nt_lat
    tiles_per_batch = s_lat // TM_FFN
    assert s_ctx <= TM_FFN and TM_FFN % s_ctx == 0 and s_lat % TM_FFN == 0

    def tile(i):
        return jnp.maximum(i - N_CAST, 0)

    def ctx_t(i):
        return jnp.minimum(tile(i), nt_ctx - 1)

    def lat_t(i):
        return jnp.maximum(tile(i) - nt_ctx, 0)

    def mod(chunk):
        def index(i):
            row = jnp.where(tile(i) < nt_ctx, 0, 1 + lat_t(i) // tiles_per_batch)
            return (l * 48 + row * 6 + chunk, 0, 0)
        return pl.BlockSpec((None, 1, D_MODEL), index)

    def resident(*shape):
        nd = len(shape)
        return pl.BlockSpec((None,) + shape, lambda i: (l,) + (0,) * nd,
                            pipeline_mode=pl.Buffered(1))

    if split_x:
        x_specs = [pl.BlockSpec((TM_FFN, D_MODEL), lambda i: (ctx_t(i), 0)),
                   pl.BlockSpec((TM_FFN, D_MODEL), lambda i: (lat_t(i), 0))]
    else:
        x_specs = [pl.BlockSpec((TM_FFN, D_MODEL), lambda i: (tile(i), 0))]
    in_specs = x_specs + [
        pl.BlockSpec((TM_FFN, 512), lambda i: (ctx_t(i), 0)),
        pl.BlockSpec((TM_FFN, 512), lambda i: (lat_t(i), 0)),
        pl.BlockSpec((ML_HEADS * ML_DV, TM_FFN), lambda i: (0, ctx_t(i))),
        pl.BlockSpec((ML_HEADS * ML_DV, TM_FFN), lambda i: (0, lat_t(i))),
        pl.BlockSpec((None, CAST_OUT, D_MODEL), lambda i: (l, jnp.minimum(i, N_CAST_OUT - 1), 0)),
        pl.BlockSpec((None, D_MODEL, CAST_UP), lambda i: (l, 0, jnp.minimum(i, N_CAST - 1))),
        pl.BlockSpec((None, CAST_DOWN, D_MODEL), lambda i: (l, jnp.minimum(i, N_CAST - 1), 0)),
        mod(2), mod(3), mod(4), mod(5),
        resident(1, D_MODEL), resident(CONV_W, 2 * D_FF), resident(1, 2 * D_FF),
        pl.BlockSpec((1, D_MODEL), lambda i: (0, 0)),
    ]
    args = list(xs) + [oab[0], oab[1], oc[0], oc[1], wts["w_out"], wts["w_up"], wts["w_down"],
                       mods, mods, mods, mods, wts["g_norm2"], wts["conv_w"], wts["conv_b"],
                       wts["g_final"]]
    if final:
        out_shape = (jax.ShapeDtypeStruct((n_ctx, D_MODEL), F32),
                     jax.ShapeDtypeStruct((n_lat, D_MODEL), F32))
        out_specs = (pl.BlockSpec((TM_FFN, D_MODEL), lambda i: (ctx_t(i), 0)),
                     pl.BlockSpec((TM_FFN, D_MODEL), lambda i: (lat_t(i), 0)))
    else:
        out_shape = jax.ShapeDtypeStruct((n_ctx + n_lat, D_MODEL), F32)
        out_specs = pl.BlockSpec((TM_FFN, D_MODEL), lambda i: (tile(i), 0))
    return pl.pallas_call(
        functools.partial(_post_kernel, nt_ctx, s_ctx, split_x, final),
        out_shape=out_shape, grid=(N_CAST + nt,), in_specs=in_specs, out_specs=out_specs,
        scratch_shapes=[pltpu.VMEM((D_MODEL, D_MODEL), BF16), pltpu.VMEM((D_MODEL, 2 * D_FF), BF16),
                        pltpu.VMEM((D_FF, D_MODEL), BF16),
                        pltpu.VMEM((TM_FFN, D_MODEL), F32), pltpu.VMEM((TM_FFN, D_MODEL), BF16),
                        pltpu.VMEM((TM_FFN, D_FF), BF16)],
        compiler_params=pltpu.CompilerParams(dimension_semantics=("arbitrary",),
                                             vmem_limit_bytes=VMEM_LIMIT_FFN),
        name="post",
    )(*args)


W_IN_BODY = (352, 2656)


def _pack_in_kernel(w_ref, o_ref):
    tc = w_ref.shape[1]

    def rows(lo, n):
        return w_ref[lo:lo + n, :]

    def zeros(n):
        return jnp.zeros((n, tc), F32)

    o_ref[C_CQ:C_CQ + MLA_Q_RANK, :] = rows(0, MLA_Q_RANK).astype(BF16)
    o_ref[C_CQ + MLA_Q_RANK:C_CKV, :] = zeros(C_CKV - C_CQ - MLA_Q_RANK).astype(BF16)
    o_ref[C_CKV:C_AUX, :] = rows(MLA_Q_RANK, MLA_KV_RANK).astype(BF16)
    aux = [rows(MLA_Q_RANK + MLA_KV_RANK, MLA_ROPE)]
    for g in range(4):
        aux += [rows(W_IN_BODY[1] + ML_HEADS * g, ML_HEADS), zeros(8 - ML_HEADS)]
    aux.append(zeros(LANES - AUX_GATE - 32))
    o_ref[C_AUX:C_DQ, :] = jnp.concatenate(aux, axis=0).astype(BF16)
    o_ref[C_DQ:, :] = rows(W_IN_BODY[0], W_IN_BODY[1] - W_IN_BODY[0]).astype(BF16)


def _pack_in_call(w_in):
    w_in_t = jnp.swapaxes(w_in, 1, 2)
    tc = 256
    return pl.pallas_call(
        _pack_in_kernel,
        out_shape=jax.ShapeDtypeStruct((DEPTH, NP_IN, D_MODEL), BF16),
        grid=(DEPTH, D_MODEL // tc),
        in_specs=[pl.BlockSpec((None, w_in_t.shape[1], tc), lambda l, i: (l, 0, i))],
        out_specs=pl.BlockSpec((None, NP_IN, tc), lambda l, i: (l, 0, i)),
        compiler_params=_cparams(("arbitrary", "arbitrary")),
        name="pack_w_in",
    )(w_in_t)


def _pack_weights(w_in, g_cq, w_uq, g_ckv, w_ukv, b_gate, g_subln, g_mnorm, g_norm1, g_norm2,
                  w_out, w_up, conv_w, conv_b, w_down, g_final, lam_q1, lam_k1, lam_q2, lam_k2):
    def cols(a, lo, n, pad=0):
        blk = a[..., lo:lo + n]
        if pad:
            blk = jnp.pad(blk, [(0, 0)] * (a.ndim - 1) + [(0, pad)])
        return blk

    w_in_p = _pack_in_call(w_in)

    hd = MLA_NOPE + MLA_ROPE
    w_uq_p = jnp.pad(w_uq.reshape(DEPTH, MLA_Q_RANK, MLA_HEADS, hd),
                     [(0, 0), (0, 256 - MLA_Q_RANK), (0, 0), (0, LANES - hd)])
    w_uq_p = w_uq_p.reshape(DEPTH, 256, MLA_HEADS * LANES).astype(BF16)

    w_ukv4 = w_ukv.reshape(DEPTH, MLA_KV_RANK, MLA_HEADS, MLA_NOPE + MLA_V)
    w_k = jnp.pad(w_ukv4[..., :MLA_NOPE], [(0, 0), (0, 0), (0, 0), (0, LANES - MLA_NOPE)])
    w_k = w_k.reshape(DEPTH, MLA_KV_RANK, MLA_HEADS * LANES)
    j = jnp.arange(LANES)[:, None]
    cix = jnp.arange(MLA_HEADS * LANES)[None, :]
    place = ((j < MLA_ROPE) & ((cix % LANES) == MLA_NOPE + j)).astype(F32)
    w_kk = jnp.concatenate([w_k, jnp.broadcast_to(place, (DEPTH, LANES, MLA_HEADS * LANES))],
                           axis=1).astype(BF16)
    w_v = w_ukv4[..., MLA_NOPE:].reshape(DEPTH, MLA_KV_RANK, MLA_HEADS * MLA_V).astype(BF16)

    return dict(
        w_in=w_in_p, w_uq=w_uq_p, w_kk=w_kk, w_v=w_v,
        g_norm1=g_norm1[:, None, :], g_norm2=g_norm2[:, None, :],
        g_cq=jnp.pad(g_cq, [(0, 0), (0, 256 - MLA_Q_RANK)])[:, None, :],
        g_ckv=g_ckv[:, None, :],
        b_gate=jnp.pad(jnp.pad(b_gate.reshape(DEPTH, 4, ML_HEADS), [(0, 0), (0, 0), (0, 4)])
                       .reshape(DEPTH, 32), [(0, 0), (AUX_GATE, LANES - AUX_GATE - 32)])[:, None, :],
        g_subln=jnp.tile(g_subln, (1, DF_HEADS))[:, None, :],
        g_mnorm=jnp.broadcast_to(g_mnorm[:, :, None], (DEPTH, ML_HEADS * ML_DV, LANES)),
        w_out=w_out, w_up=w_up, w_down=w_down,
        conv_w=conv_w, conv_b=conv_b[:, None, :], g_final=g_final[None, :],
        lam_q1=lam_q1[:, None, :], lam_k1=lam_k1[:, None, :],
        lam_q2=lam_q2[:, None, :], lam_k2=lam_k2[:, None, :],
    )


def _rope_tables(n_tok):
    t = np.arange(n_tok)
    row = (t // GRID_W).astype(np.float64)
    col = (t % GRID_W).astype(np.float64)
    nf = MLA_ROPE // 4
    inv = ROPE_BASE ** (-np.arange(nf, dtype=np.float64) / nf)
    ar = row[:, None] * inv[None, :]
    ac = col[:, None] * inv[None, :]
    ang = np.concatenate([ar, ar, ac, ac], axis=-1)
    quarter = (np.arange(MLA_ROPE) // nf) % 2
    cos = jnp.asarray(np.cos(ang), F32)
    sin_up = jnp.asarray(np.where(quarter == 0, -np.sin(ang), 0.0), F32)
    sin_dn = jnp.asarray(np.where(quarter == 1, np.sin(ang), 0.0), F32)
    ones = jnp.ones((n_tok, 1), F32)
    zeros = jnp.zeros((n_tok, 1), F32)

    def head_q(t32, fill):
        blk = jnp.concatenate([jnp.tile(fill, (1, MLA_NOPE)), t32, jnp.tile(fill, (1, 32))], axis=1)
        return jnp.tile(blk, (1, MLA_HEADS))

    def aux_k(t32, fill):
        return jnp.concatenate([t32, jnp.tile(fill, (1, LANES - MLA_ROPE))], axis=1)

    tq = (head_q(cos, ones), head_q(sin_up, zeros), head_q(sin_dn, zeros))
    td = tuple(jnp.tile(a, (1, 256 // DF_DIM)) for a in (cos, sin_up, sin_dn))
    tk = (aux_k(cos, ones), aux_k(sin_up, zeros), aux_k(sin_dn, zeros))
    return tq + td + tk


def kernel(x_prompt, x_sample, cache_mla_ckv, cache_mla_krope, cache_diff_k, cache_diff_v,
           state_mlstm_C, state_mlstm_n, state_mlstm_m, c, c_ctx, w_ada, b_ada, g_norm1, w_in,
           g_cq, w_uq, g_ckv, w_ukv, lam_q1, lam_k1, lam_q2, lam_k2, g_subln, b_gate, g_mnorm,
           w_out, g_norm2, w_up, conv_w, conv_b, w_down, g_final):
    bp, sp, _ = x_prompt.shape
    bl, sl, _ = x_sample.shape
    t_len = cache_mla_ckv.shape[2]

    wts = _pack_weights(w_in, g_cq, w_uq, g_ckv, w_ukv, b_gate, g_subln, g_mnorm, g_norm1, g_norm2,
                        w_out, w_up, conv_w, conv_b, w_down, g_final, lam_q1, lam_k1, lam_q2, lam_k2)
    tables = _rope_tables(sl)

    cond = jnp.concatenate([c_ctx[None, :], c, jnp.zeros((8 - 1 - bl, D_MODEL), F32)], axis=0)
    mods = _ada_call(cond, w_ada, b_ada[:, None, :])
    mods = mods.reshape(DEPTH * 8 * N_MOD, 1, D_MODEL)

    def feat_major(a):
        return jnp.transpose(a, (0, 1, 3, 4, 2)).reshape(bl, DEPTH, 256, t_len)

    kctx, vctx, cdk, cdv = _ctxkv_call(
        cache_mla_ckv, jnp.swapaxes(cache_mla_krope, 2, 3), feat_major(cache_diff_k),
        feat_major(cache_diff_v), wts["w_kk"], wts["w_v"])
    c0_t = jnp.swapaxes(state_mlstm_C.reshape(bl, DEPTH, 2, 2, LANES, ML_DV), -1, -2)
    n0_r = jnp.broadcast_to(state_mlstm_n.reshape(bl, DEPTH, 2, 2, 1, LANES),
                            (bl, DEPTH, 2, 2, ML_NROWS, LANES))
    s0 = jnp.concatenate([c0_t, n0_r], axis=-2)
    m0 = jnp.broadcast_to(jnp.pad(state_mlstm_m, [(0, 0)] * 3 + [(0, 8 - ML_HEADS)])[..., None],
                          (bl, DEPTH, 2, 8, LANES))

    n_ctx, n_lat = bp * sp, bl * sl
    xs = (x_prompt.reshape(n_ctx, D_MODEL), x_sample.reshape(n_lat, D_MODEL))
    state_bufs, c_buf = None, None
    n_col, m_col = [], []
    for l in range(DEPTH):
        lam_init = 0.8 - 0.6 * math.exp(-0.3 * l)
        final = l == DEPTH - 1
        x_ctx, x_lat, lat_off = (xs[0], xs[1], 0) if len(xs) == 2 else (xs[0], xs[0], n_ctx)
        (q, k, v, dq, dk, dv, mq, mkt, mv, mo, gt, *state_bufs) = _pre_call(
            False, l, x_ctx, 0, bp, sp, mods, wts, None, state_bufs)
        oab_c = _attn_call(False, l, lam_init, (q, k, v), (dq, dk, dv), wts)
        oc_c, c_buf, n_fin, m_fin = _mlstm_call(False, l, mq, mkt, mv, mo, gt, wts, c_buf=c_buf)
        n_col.append(n_fin[..., 0, :].reshape(bp, 2, ML_HEADS, ML_DK))
        m_col.append(m_fin[:, :, :ML_HEADS, 0])
        (q, k, v, dq, dk, dv, mq, mkt, mv, mo, gt) = _pre_call(
            True, l, x_lat, lat_off, bl, sl, mods, wts, tables)
        oab_l = _attn_call(True, l, lam_init, (q, k, v), (dq, dk, dv), wts, (kctx, vctx), (cdk, cdv))
        oc_l = _mlstm_call(True, l, mq, mkt, mv, mo, gt, wts, s0, m0)
        out = _post_call(l, final, xs, n_ctx, sp, sl,
                         (oab_c.reshape(n_ctx, 512), oab_l.reshape(n_lat, 512)), (oc_c, oc_l),
                         mods, wts)
        xs = out if final else (out,)

    xp = xs[0].reshape(bp, sp, D_MODEL)
    xs = xs[1].reshape(bl, sl, D_MODEL)
    ckv_all, kr_all, dk_all, dv_all = state_bufs

    def token_major(a):
        return jnp.transpose(a.reshape(bp, DEPTH, DF_HEADS, 2 * DF_DIM, sp), (0, 1, 4, 2, 3))

    return (xp, xs, ckv_all, jnp.swapaxes(kr_all, 2, 3), token_major(dk_all), token_major(dv_all),
            c_buf.reshape(bp, DEPTH, 2, ML_HEADS, ML_DK, ML_DV),
            jnp.stack(n_col, axis=1), jnp.stack(m_col, axis=1))
```

```python
import functools
import math

import jax
import jax.numpy as jnp
import numpy as np
from jax import lax
from jax.experimental import pallas as pl
from jax.experimental.pallas import tpu as pltpu

F32 = jnp.float32
BF16 = jnp.bfloat16

D_MODEL = 1024
DEPTH = 4
GRID_W = 64
N_MOD = 6
EPS = 1e-6
ROPE_BASE = 10000.0
MLA_HEADS = 4
MLA_Q_RANK = 192
MLA_KV_RANK = 128
MLA_NOPE = 64
MLA_ROPE = 32
MLA_V = 64
DF_HEADS = 4
DF_DIM = 32
ML_HEADS = 4
ML_DK = 64
ML_DV = 128
D_FF = 2816
CONV_W = 3

LANES = 128
VMEM_LIMIT = 48 * 1024 * 1024
VMEM_LIMIT_FFN = 56 * 1024 * 1024

C_CQ, C_CKV, C_AUX, C_DQ, C_DK, C_DV, C_MQ, C_MK, C_MV, C_MO = (
    0, 256, 384, 512, 768, 1024, 1280, 1536, 1792, 2304)
NP_IN = 2816
AUX_GATE = 32

TM_PRE = 256
PRE_SUB = 2
TQ = 512
NB_CTX = 4
ML_CHUNK = 128
ML_NROWS = 128
NB_ML_CTX = 4
NB_ML_LAT = 2
TM_FFN = 512
POST_SUB = 1
FC = 256
N_FC = D_FF // FC
N_CAST = 22
CAST_UP = 2 * D_FF // N_CAST
CAST_DOWN = D_FF // N_CAST
CAST_OUT = 128
N_CAST_OUT = D_MODEL // CAST_OUT

NT = (((1,), (1,)), ((), ()))
LOG2E = 1.4426950408889634


def _cparams(sem):
    return pltpu.CompilerParams(dimension_semantics=sem, vmem_limit_bytes=VMEM_LIMIT)


def _dot(a, b):
    return jnp.dot(a, b, preferred_element_type=F32)


def _dot_nt(a, b):
    return lax.dot_general(a, b, NT, preferred_element_type=F32)


def _rms_rows(x, g, n):
    ms = jnp.sum(x * x, axis=-1, keepdims=True) * (1.0 / n)
    return x * lax.rsqrt(ms + EPS) * g


def _rope(x, cos, sin_up, sin_dn):
    w = x.shape[-1]
    return x * cos + pltpu.roll(x, w - 8, 1) * sin_up + pltpu.roll(x, 8, 1) * sin_dn


def _ada_block(c_ref, w_ref, b_ref):
    c = c_ref[...]
    s = (c * jax.nn.sigmoid(c)).astype(BF16)
    return _dot(s, w_ref[...].astype(BF16)) + b_ref[...]


def _ada_kernel(c_ref, w_ref, b_ref, o_ref):
    o_ref[...] = _ada_block(c_ref, w_ref, b_ref)


def _ada_call(cond, w_ada, b_ada):
    nt = 1024
    return pl.pallas_call(
        _ada_kernel,
        out_shape=jax.ShapeDtypeStruct((8, N_MOD * D_MODEL), F32),
        grid=(N_MOD * D_MODEL // nt,),
        in_specs=[pl.BlockSpec((8, D_MODEL), lambda j: (0, 0)),
                  pl.BlockSpec((None, D_MODEL, nt), lambda j: (0, 0, j)),
                  pl.BlockSpec((None, 1, nt), lambda j: (0, 0, j))],
        out_specs=pl.BlockSpec((8, nt), lambda j: (0, j)),
        compiler_params=_cparams(("arbitrary",)),
        name="ada_mod",
    )(cond, w_ada, b_ada)


def _ctxkv_kernel(ckv_ref, krt_ref, dkt_ref, dvt_ref, wkk_ref, wv_ref, k_ref, v_ref, dkb_ref, dvb_ref):
    t = ckv_ref.shape[0]
    ckv = ckv_ref[...].astype(BF16)
    kr = jnp.concatenate([krt_ref[...], jnp.zeros((LANES - MLA_ROPE, t), F32)], axis=0).T
    kin = jnp.concatenate([ckv, kr.astype(BF16)], axis=1)
    k_ref[...] = _dot(kin, wkk_ref[...]).T.astype(BF16)
    v_ref[...] = _dot(ckv, wv_ref[...]).astype(BF16)
    dkb_ref[...] = dkt_ref[...].astype(BF16)
    dvb_ref[...] = dvt_ref[...].T.astype(BF16)


def _ctxkv_call(cache_ckv, cache_kr_t, cache_dk_t, cache_dv_t, wkk, wv):
    b, _, t, _ = cache_ckv.shape

    def cache_t(w):
        return pl.BlockSpec((None, None, w, t), lambda l, i: (i, l, 0, 0))

    def out(w):
        return pl.BlockSpec((None, None, t, w), lambda l, i: (l, i, 0, 0))

    def out_t(w):
        return pl.BlockSpec((None, None, w, t), lambda l, i: (l, i, 0, 0))

    return pl.pallas_call(
        _ctxkv_kernel,
        out_shape=(jax.ShapeDtypeStruct((DEPTH, b, 512, t), BF16),
                   jax.ShapeDtypeStruct((DEPTH, b, t, 256), BF16),
                   jax.ShapeDtypeStruct((DEPTH, b, 256, t), BF16),
                   jax.ShapeDtypeStruct((DEPTH, b, t, 256), BF16)),
        grid=(DEPTH, b),
        in_specs=[pl.BlockSpec((None, None, t, MLA_KV_RANK), lambda l, i: (i, l, 0, 0)),
                  cache_t(MLA_ROPE), cache_t(256), cache_t(256),
                  pl.BlockSpec((None, 256, 512), lambda l, i: (l, 0, 0)),
                  pl.BlockSpec((None, MLA_KV_RANK, 256), lambda l, i: (l, 0, 0))],
        out_specs=(out_t(512), out(256), out_t(256), out(256)),
        compiler_params=_cparams(("arbitrary", "arbitrary")),
        name="ctx_kv",
    )(cache_ckv, cache_kr_t, cache_dk_t, cache_dv_t, wkk, wv)


def _pre_kernel(latent, kinds, *refs):
    for n in range(PRE_SUB):
        views = []
        for kind, r in zip(kinds, refs):
            if kind == "alias":
                continue
            if kind == "rows":
                r = r.at[n * TM_PRE:(n + 1) * TM_PRE]
            elif kind == "lanes":
                r = r.at[:, n * TM_PRE:(n + 1) * TM_PRE]
            elif kind == "batch":
                r = r.at[n]
            views.append(r)
        _pre_tile(latent, *views)


def _pre_tile(latent, *refs):
    (x_ref, sh_ref, sc_ref, g1_ref, win_ref, gcq_ref, wuq_ref, gckv_ref, wkk_ref, wv_ref,
     bg_ref) = refs[:11]
    refs = refs[11:]
    if latent:
        (cq_t, sqa_t, sqb_t, cd_t, sda_t, sdb_t, ck_t, ska_t, skb_t) = refs[:9]
        refs = refs[9:]
    (q_ref, k_ref, v_ref, dq_ref, dk_ref, dv_ref, mq_ref, mk_ref, mv_ref, mo_ref,
     gt_ref) = refs[:11]
    refs = refs[11:]
    if not latent:
        ckv_out, kr_out, dk_out, dv_out = refs

    x = x_ref[...]
    h = _rms_rows(x, g1_ref[...], D_MODEL) * (1.0 + sc_ref[...]) + sh_ref[...]
    proj = _dot(h.astype(BF16), win_ref[...])

    cq = _rms_rows(proj[:, C_CQ:C_CQ + 256], gcq_ref[...], MLA_Q_RANK)
    q = _dot(cq.astype(BF16), wuq_ref[...])
    if latent:
        q = _rope(q, cq_t[...], sqa_t[...], sqb_t[...])
    q_ref[...] = (q * ((MLA_NOPE + MLA_ROPE) ** -0.5 * LOG2E)).astype(BF16)

    c_kv = _rms_rows(proj[:, C_CKV:C_CKV + MLA_KV_RANK], gckv_ref[...], MLA_KV_RANK)
    aux = proj[:, C_AUX:C_AUX + LANES] + bg_ref[...]
    if latent:
        aux = _rope(aux, ck_t[...], ska_t[...], skb_t[...])
    aux_t = aux.T
    gt_ref[...] = aux_t[AUX_GATE:AUX_GATE + 32, :]
    if not latent:
        ckv_out[...] = c_kv
        kr_out[...] = aux_t[:MLA_ROPE, :]
    ckv_b = c_kv.astype(BF16)
    kin = jnp.concatenate([ckv_b, aux.astype(BF16)], axis=1)
    k_ref[...] = _dot(kin, wkk_ref[...]).T.astype(BF16)
    v_ref[...] = _dot(ckv_b, wv_ref[...]).astype(BF16)

    dq = proj[:, C_DQ:C_DQ + 256]
    dk = proj[:, C_DK:C_DK + 256]
    dv = proj[:, C_DV:C_DV + 256]
    if not latent:
        dk_t = dk.T
        dk_out[...] = dk_t
        dv_out[...] = dv.T
    else:
        dq = _rope(dq, cd_t[...], sda_t[...], sdb_t[...])
        dk_t = _rope(dk, cd_t[...], sda_t[...], sdb_t[...]).T
    dq_ref[...] = (dq * (DF_DIM ** -0.5 * LOG2E)).astype(BF16)
    dk_ref[...] = dk_t.astype(BF16)
    dv_ref[...] = dv.astype(BF16)

    lane = lax.broadcasted_iota(jnp.int32, (x.shape[0], LANES), 1)
    for h in range(ML_HEADS):
        blk = proj[:, C_MQ + (h // 2) * LANES:C_MQ + (h // 2 + 1) * LANES]
        mq_ref[h * LANES:(h + 1) * LANES, :] = (
            jnp.where((lane >> 6) == h % 2, blk, 0.0).T.astype(BF16))
    mk_ref[...] = (proj[:, C_MK:C_MK + 256] * (ML_DK ** -0.5)).astype(BF16)
    mv_ref[...] = proj[:, C_MV:C_MV + 512].T.astype(BF16)
    mo_ref[...] = proj[:, C_MO:C_MO + 512].T


def _pre_call(latent, l, x2d, tok_off, b, s, mods, wts, tables, state_bufs=None):
    ns = s // TM_PRE
    tile_off = tok_off // TM_PRE
    tm2 = PRE_SUB * TM_PRE
    seq_split = ns > 1
    assert (ns % PRE_SUB == 0) if seq_split else (b % PRE_SUB == 0 and ns == 1)
    assert tile_off % PRE_SUB == 0
    grid = (ns // PRE_SUB, b) if seq_split else (1, b // PRE_SUB)

    def tok(width):
        if seq_split:
            return pl.BlockSpec((None, tm2, width), lambda j, i: (i, j, 0)), "rows"
        return pl.BlockSpec((PRE_SUB, TM_PRE, width), lambda j, i: (i, j, 0)), "batch"

    def feat(width):
        if seq_split:
            return pl.BlockSpec((None, width, tm2), lambda j, i: (i, 0, j)), "lanes"
        return pl.BlockSpec((PRE_SUB, width, TM_PRE), lambda j, i: (i, 0, j)), "batch"

    def mod(chunk):
        if latent:
            return pl.BlockSpec((None, 1, D_MODEL), lambda j, i: ((1 + i) * 6 + chunk, 0, 0))
        return pl.BlockSpec((None, 1, D_MODEL), lambda j, i: (chunk, 0, 0))

    def lw(*shape):
        nd = len(shape)
        return pl.BlockSpec((None,) + shape, lambda j, i: (l,) + (0,) * nd)

    if seq_split:
        x_spec = pl.BlockSpec((tm2, D_MODEL), lambda j, i: ((tile_off + i * ns) // PRE_SUB + j, 0))
    else:
        x_spec = pl.BlockSpec((tm2, D_MODEL), lambda j, i: (tile_off // PRE_SUB + i, 0))
    in_specs = [x_spec, mod(0), mod(1), lw(1, D_MODEL), lw(D_MODEL, NP_IN), lw(1, 256),
                lw(256, 512), lw(1, MLA_KV_RANK), lw(256, 512), lw(MLA_KV_RANK, 256), lw(1, LANES)]
    kinds = ["rows"] + [None] * 10
    args = [x2d, mods, mods, wts["g_norm1"], wts["w_in"], wts["g_cq"], wts["w_uq"], wts["g_ckv"],
            wts["w_kk"], wts["w_v"], wts["b_gate"]]
    if latent:
        for t in tables:
            in_specs.append(pl.BlockSpec((tm2, t.shape[1]), lambda j, i: (j, 0)))
            kinds.append("rows")
            args.append(t)

    widths = [(512, BF16, False), (512, BF16, True), (256, BF16, False), (256, BF16, False),
              (256, BF16, True), (256, BF16, False), (512, BF16, True), (256, BF16, False),
              (512, BF16, True), (512, F32, True), (32, F32, True)]
    out_shape = [jax.ShapeDtypeStruct((b, w, s) if tr else (b, s, w), dt) for w, dt, tr in widths]
    out_pairs = [feat(w) if tr else tok(w) for w, _, tr in widths]
    aliases = {}
    if not latent:
        state = [(MLA_KV_RANK, False), (MLA_ROPE, True), (256, True), (256, True)]
        for k, (w, tr) in enumerate(state):
            if tr:
                spec = pl.BlockSpec((PRE_SUB, None, w, TM_PRE), lambda j, i: (i, l, 0, j))
                shape = (b, DEPTH, w, s)
            else:
                spec = pl.BlockSpec((PRE_SUB, None, TM_PRE, w), lambda j, i: (i, l, j, 0))
                shape = (b, DEPTH, s, w)
            out_shape.append(jax.ShapeDtypeStruct(shape, F32))
            out_pairs.append((spec, "batch"))
            if state_bufs is not None:
                aliases[len(args)] = len(widths) + k
                in_specs.append(pl.BlockSpec(memory_space=pl.ANY))
                kinds.append("alias")
                args.append(state_bufs[k])
    out_specs = tuple(p[0] for p in out_pairs)
    kinds += [p[1] for p in out_pairs]
    return pl.pallas_call(
        functools.partial(_pre_kernel, latent, tuple(kinds)),
        out_shape=tuple(out_shape), grid=grid, in_specs=in_specs, out_specs=out_specs,
        input_output_aliases=aliases,
        compiler_params=_cparams(("arbitrary", "arbitrary")),
        name="pre_lat" if latent else "pre_ctx",
    )(*args)


def _softmax_parts(s_list):
    m = functools.reduce(jnp.maximum, [jnp.max(s, axis=1, keepdims=True) for s in s_list])
    p_list = [jnp.exp2(s - m) for s in s_list]
    l = functools.reduce(jnp.add, [jnp.sum(p, axis=1, keepdims=True) for p in p_list])
    return p_list, l


def _attn_kernel(latent, lam_init, nb, *refs):
    n_seg = 5 if latent else 3
    o_ref = refs[-1]
    mla_in, diff_in, shared = refs[:n_seg], refs[n_seg:2 * n_seg], refs[2 * n_seg:-1]
    for n in range(nb):
        _mla_body(latent, *[r.at[n] for r in mla_in], o_ref.at[n])
        _diff_body(latent, lam_init, *[r.at[n] for r in diff_in], *shared, o_ref.at[n])


def _mla_body(latent, *refs):
    if latent:
        q_ref, k_ref, v_ref, kc_ref, vc_ref, o_ref = refs
        segs = [(kc_ref, vc_ref), (k_ref, v_ref)]
    else:
        q_ref, k_ref, v_ref, o_ref = refs
        segs = [(k_ref, v_ref)]
    lane = lax.broadcasted_iota(jnp.int32, (q_ref.shape[0], LANES), 1)

    def scores(h):
        hs = slice(h * LANES, (h + 1) * LANES)
        return [_dot(q_ref[:, hs], kr[hs, :]) for kr, _ in segs]

    outs = []
    s_next = scores(0)
    for h in range(MLA_HEADS):
        ps = slice((h // 2) * LANES, (h // 2 + 1) * LANES)
        s_list = s_next
        if h + 1 < MLA_HEADS:
            s_next = scores(h + 1)
        p_list, l = _softmax_parts(s_list)
        pv = functools.reduce(jnp.add, [_dot(p.astype(BF16), vr[:, ps])
                                        for p, (_, vr) in zip(p_list, segs)])
        outs.append(pv / l)
    o_ref[:, 0:LANES] = jnp.where(lane < MLA_V, outs[0], outs[1]).astype(BF16)
    o_ref[:, LANES:2 * LANES] = jnp.where(lane < MLA_V, outs[2], outs[3]).astype(BF16)


def _diff_body(latent, lam_init, *refs):
    if latent:
        (q_ref, k_ref, v_ref, kc_ref, vc_ref, lq1, lk1, lq2, lk2, g_ref, o_ref) = refs
        segs = [(kc_ref, vc_ref), (k_ref, v_ref)]
    else:
        (q_ref, k_ref, v_ref, lq1, lk1, lq2, lk2, g_ref, o_ref) = refs
        segs = [(k_ref, v_ref)]
    lam = (jnp.exp(jnp.sum(lq1[...] * lk1[...], axis=1, keepdims=True))
           - jnp.exp(jnp.sum(lq2[...] * lk2[...], axis=1, keepdims=True)) + lam_init)
    lane = lax.broadcasted_iota(jnp.int32, (q_ref.shape[0], LANES), 1)
    grp = lane >> 5
    qf = q_ref[...].astype(F32)

    def scores(u):
        h, c = u // 2, u % 2
        ps = slice((h // 2) * LANES, (h // 2 + 1) * LANES)
        qm = jnp.where(grp == 2 * (h % 2) + c, qf[:, ps], 0.0).astype(BF16)
        return [_dot(qm, kr[ps, :]) for kr, _ in segs]

    outs = []
    s_next = scores(0)
    for h in range(DF_HEADS):
        ps = slice((h // 2) * LANES, (h // 2 + 1) * LANES)
        hh = h % 2
        parts = []
        for c in range(2):
            s_list = s_next
            if 2 * h + c + 1 < 2 * DF_HEADS:
                s_next = scores(2 * h + c + 1)
            parts.append(_softmax_parts(s_list))
        (p0, l0), (p1, l1) = parts
        ratio = lam * l0 / l1
        pv = functools.reduce(jnp.add, [
            _dot((a0 - a1 * ratio).astype(BF16), vr[:, ps])
            for a0, a1, (_, vr) in zip(p0, p1, segs)]) * (1.0 / l0)
        valid = (lane >> 6) == hh
        ms = jnp.sum(jnp.where(valid, pv * pv, 0.0), axis=1, keepdims=True) * (1.0 / (2 * DF_DIM))
        outs.append(pv * lax.rsqrt(ms + EPS) * g_ref[:, ps] * (1.0 - lam_init))
    o_ref[:, 2 * LANES:3 * LANES] = jnp.where(lane < 2 * DF_DIM, outs[0], outs[1]).astype(BF16)
    o_ref[:, 3 * LANES:4 * LANES] = jnp.where(lane < 2 * DF_DIM, outs[2], outs[3]).astype(BF16)


def _attn_call(latent, l, lam_init, qkv, dqkv, wts, ctx=None, dctx=None):
    b, s, _ = qkv[0].shape
    nb = 1 if latent else NB_CTX
    tq = min(TQ, s)
    grid = (b // nb, s // tq)

    def group(wq, wk, wv, ctx_pair):
        specs = [pl.BlockSpec((nb, tq, wq), lambda i, j: (i, j, 0)),
                 pl.BlockSpec((nb, wk, s), lambda i, j: (i, 0, 0)),
                 pl.BlockSpec((nb, s, wv), lambda i, j: (i, 0, 0))]
        if latent:
            t = ctx_pair[1].shape[2]
            specs += [pl.BlockSpec((None, nb, wk, t), lambda i, j: (l, i, 0, 0)),
                      pl.BlockSpec((None, nb, t, wv), lambda i, j: (l, i, 0, 0))]
        return specs

    in_specs = group(512, 512, 256, ctx) + group(256, 256, 256, dctx)
    args = list(qkv) + (list(ctx) if latent else []) + list(dqkv) + (list(dctx) if latent else [])
    for name in ("lam_q1", "lam_k1", "lam_q2", "lam_k2"):
        in_specs.append(pl.BlockSpec((None, 1, DF_DIM), lambda i, j: (l, 0, 0)))
        args.append(wts[name])
    in_specs.append(pl.BlockSpec((None, 1, 256), lambda i, j: (l, 0, 0)))
    args.append(wts["g_subln"])
    return pl.pallas_call(
        functools.partial(_attn_kernel, latent, lam_init, nb),
        out_shape=jax.ShapeDtypeStruct((b, s, 512), BF16),
        grid=grid, in_specs=in_specs,
        out_specs=pl.BlockSpec((nb, tq, 512), lambda i, j: (i, j, 0)),
        compiler_params=_cparams(("arbitrary", "arbitrary")),
        name="attn_lat" if latent else "attn_ctx",
    )(*args)


def _log_sigmoid(x):
    return jnp.minimum(x, 0.0) - jnp.log1p(jnp.exp(-jnp.abs(x)))


def _mlstm_chunk(d, c, mq_ref, mk_ref, mv_ref, gt_ref, s_ref, m_ref, h_ref):
    L = ML_CHUNK
    rows = pl.ds(pl.multiple_of(c * L, L), L)
    s_i = lax.broadcasted_iota(jnp.int32, (L, L), 0)
    t_i = lax.broadcasted_iota(jnp.int32, (L, L), 1)
    mask = (s_i <= t_i) if d == 0 else (s_i >= t_i)
    tri = jnp.where(mask, 1.0, 0.0).astype(BF16)

    ig = gt_ref[16 * d:16 * d + 8, rows]
    lf = _log_sigmoid(gt_ref[16 * d + 8:16 * d + 16, rows])
    hi = lf.astype(BF16).astype(F32)
    r1 = lf - hi
    mid = r1.astype(BF16).astype(F32)
    parts = _dot(jnp.concatenate([hi, mid, r1 - mid], axis=0).astype(BF16), tri)
    bc = parts[0:8] + parts[8:16] + parts[16:24]
    rvec = ig - bc
    total = jnp.sum(lf, axis=1, keepdims=True)
    mm = m_ref[d]
    gvec = total + rvec
    m_new = jnp.maximum(total + mm, jnp.max(gvec, axis=1, keepdims=True))
    ws = jnp.exp(gvec - m_new).astype(BF16)
    cdec = jnp.exp(total + mm - m_new)
    m_ref[d] = m_new

    rv_t = jnp.concatenate([rvec, jnp.zeros((LANES - 8, L), F32)], axis=0).T

    s_old = [s_ref[d, pair].astype(BF16) for pair in range(2)]
    upd = []
    for h in range(ML_HEADS):
        pair = h // 2
        hs = slice(h * ML_DV, (h + 1) * ML_DV)
        qt = mq_ref[hs, rows]
        kp = mk_ref[rows, pair * LANES:(pair + 1) * LANES]
        vt = mv_ref[hs, rows]
        mmh = mm[h:h + 1, 0:1]
        rm = jnp.where(mask, rv_t[:, h:h + 1], -jnp.inf)
        a = jnp.maximum(jnp.max(rm, axis=0, keepdims=True), mmh)
        wqk = jnp.exp(rm - a) * _dot(kp, qt)
        dec = jnp.exp(mmh - a)
        qc = _dot(s_old[pair], qt)
        num = _dot(vt, wqk.astype(BF16)) + dec * qc[:ML_DV]
        den = jnp.sum(wqk, axis=0, keepdims=True) + dec * qc[ML_DV:ML_DV + 1]
        inv = 1.0 / jnp.maximum(jnp.abs(den), jnp.exp(-(a + bc[h:h + 1, :])))
        h_ref[d, hs, rows] = num * inv
        wsr = ws[h:h + 1, :]
        vaug = jnp.concatenate([vt * wsr, jnp.broadcast_to(wsr, (ML_NROWS, L))], axis=0)
        upd.append(_dot(vaug, kp))

    low = lax.broadcasted_iota(jnp.int32, (ML_DV + ML_NROWS, LANES), 1) < ML_DK
    for pair in range(2):
        h0, h1 = 2 * pair, 2 * pair + 1
        cd = jnp.where(low, cdec[h0:h0 + 1, 0:1], cdec[h1:h1 + 1, 0:1])
        s_ref[d, pair] = cd * s_ref[d, pair] + jnp.where(low, upd[h0], upd[h1])


def _mlstm_kernel(latent, seq, nb, has_alias, *refs):
    if latent:
        (mq_ref, mk_ref, mv_ref, mo_ref, gt_ref, g_ref, s0_ref, m0_ref,
         o_ref, s_ref, m_ref, h_ref) = refs
        s_ref[...] = s0_ref[...]
        m_ref[...] = m0_ref[...]
    else:
        if has_alias:
            refs = refs[:6] + refs[7:]
        (mq_ref, mk_ref, mv_ref, mo_ref, gt_ref, g_ref,
         o_ref, cf_ref, nf_ref, mf_ref, s_ref, m_ref, h_ref) = refs
        s_ref[...] = jnp.zeros(s_ref.shape, F32)
        m_ref[...] = jnp.zeros(m_ref.shape, F32)
    nc = seq // ML_CHUNK

    def body(j, carry):
        for n in range(nb):
            views = (mq_ref.at[n], mk_ref.at[n], mv_ref.at[n], gt_ref.at[n], s_ref.at[n],
                     m_ref.at[n], h_ref.at[n])
            _mlstm_chunk(0, j, *views)
            _mlstm_chunk(1, nc - 1 - j, *views)
        return carry

    lax.fori_loop(0, nc, body, 0, unroll=min(nc, 4))

    for n in range(nb):
        for h in range(ML_HEADS):
            hs = slice(h * ML_DV, (h + 1) * ML_DV)
            for j in range(nc):
                ts = slice(j * LANES, (j + 1) * LANES)
                hsum = h_ref[n, 0, hs, ts] + h_ref[n, 1, hs, ts]
                ms = jnp.sum(hsum * hsum, axis=0, keepdims=True) * (1.0 / ML_DV)
                y = hsum * lax.rsqrt(ms + EPS) * g_ref[hs, :]
                o_ref[hs, n * seq + j * LANES:n * seq + (j + 1) * LANES] = (
                    jax.nn.sigmoid(mo_ref[n, hs, ts]) * y).astype(BF16)
    if not latent:
        for n in range(nb):
            for d in range(2):
                for pair in range(2):
                    cf_ref[n, d, pair] = s_ref[n, d, pair, :ML_DV, :].T
        nf_ref[...] = s_ref[:, :, :, ML_DV:ML_DV + 8, :]
        mf_ref[...] = m_ref[...]


def _mlstm_call(latent, l, mqt, mk, mvt, mot, gt, wts, s0=None, m0=None, c_buf=None):
    assert ML_CHUNK == LANES
    b, _, s = mqt.shape
    nb = NB_ML_LAT if latent else NB_ML_CTX

    def feat(w):
        return pl.BlockSpec((nb, w, s), lambda i: (i, 0, 0))

    in_specs = [feat(512), pl.BlockSpec((nb, s, 256), lambda i: (i, 0, 0)), feat(512), feat(512),
                feat(32), pl.BlockSpec((None, 512, LANES), lambda i: (l, 0, 0))]
    args = [mqt, mk, mvt, mot, gt, wts["g_mnorm"]]
    s_spec_shape = (2, 2, ML_DV + ML_NROWS, LANES)
    m_spec_shape = (2, 8, LANES)
    scratch = [pltpu.VMEM((nb,) + s_spec_shape, F32), pltpu.VMEM((nb,) + m_spec_shape, F32),
               pltpu.VMEM((nb, 2, ML_HEADS * ML_DV, s), F32)]
    o_shape = jax.ShapeDtypeStruct((ML_HEADS * ML_DV, b * s), BF16)
    o_spec = pl.BlockSpec((ML_HEADS * ML_DV, nb * s), lambda i: (0, i))
    if latent:
        in_specs += [pl.BlockSpec((nb, None) + s_spec_shape, lambda i: (i, l, 0, 0, 0, 0)),
                     pl.BlockSpec((nb, None) + m_spec_shape, lambda i: (i, l, 0, 0, 0))]
        args += [s0, m0]
        out_shape = o_shape
        out_specs = o_spec
    aliases = {}
    if not latent:
        half = (2, 2, LANES, LANES)
        out_shape = (o_shape,
                     jax.ShapeDtypeStruct((b, DEPTH) + half, F32),
                     jax.ShapeDtypeStruct((b, 2, 2, 8, LANES), F32),
                     jax.ShapeDtypeStruct((b,) + m_spec_shape, F32))
        out_specs = (o_spec,
                     pl.BlockSpec((nb, None) + half, lambda i: (i, l, 0, 0, 0, 0)),
                     pl.BlockSpec((nb, 2, 2, 8, LANES), lambda i: (i, 0, 0, 0, 0)),
                     pl.BlockSpec((nb,) + m_spec_shape, lambda i: (i, 0, 0, 0)))
        if c_buf is not None:
            aliases[len(args)] = 1
            in_specs.append(pl.BlockSpec(memory_space=pl.ANY))
            args.append(c_buf)
    return pl.pallas_call(
        functools.partial(_mlstm_kernel, latent, s, nb, c_buf is not None),
        out_shape=out_shape, grid=(b // nb,), in_specs=in_specs, out_specs=out_specs,
        input_output_aliases=aliases,
        scratch_shapes=scratch,
        compiler_params=_cparams(("arbitrary",)),
        name="mlstm_lat" if latent else "mlstm_ctx",
    )(*args)


def _post_kernel(nt_ctx, seg_ctx, split_x, final, *refs):
    n_x = 2 if split_x else 1
    x_refs, refs = refs[:n_x], refs[n_x:]
    (oab_c, oab_l, oc_c, oc_l, wo_ref, wu_ref, wd_ref, gt1_ref, sh2_ref, sc2_ref, gt2_ref, g2_ref,
     cw_ref, cb_ref, gf_ref) = refs[:15]
    refs = refs[15:]
    if final:
        o_refs, refs = refs[:2], refs[2:]
    else:
        (cond_ref, wada_ref, bada_ref), o_refs, mods_next_ref, refs = (
            refs[:3], refs[3:4], refs[4], refs[5:])
    wo_s, wu_s, wd_s, x1_ref, h2_ref, act_ref = refs
    i = pl.program_id(0)

    @pl.when(i < N_CAST)
    def _():
        wu_s[:, pl.ds(pl.multiple_of(i * CAST_UP, CAST_UP), CAST_UP)] = wu_ref[...].astype(BF16)
        wd_s[pl.ds(pl.multiple_of(i * CAST_DOWN, CAST_DOWN), CAST_DOWN), :] = wd_ref[...].astype(BF16)

    @pl.when(i < N_CAST_OUT)
    def _():
        wo_s[pl.ds(pl.multiple_of(i * CAST_OUT, CAST_OUT), CAST_OUT), :] = wo_ref[...].astype(BF16)

    @pl.when(i >= N_CAST)
    def _():
        t = i - N_CAST
        is_ctx = t < nt_ctx
        if not final:
            mods_next_ref[...] = _ada_block(cond_ref, wada_ref, bada_ref)

        th = TM_FFN // POST_SUB
        seg = jnp.where(is_ctx, seg_ctx, GRID_W)
        row = lax.broadcasted_iota(jnp.int32, (th, FC), 0)
        first_w = (row & (GRID_W - 1)) == 0
        last_w = (row & (GRID_W - 1)) == GRID_W - 1
        seg_first = (row & (seg - 1)) == 0
        seg_last = (row & (seg - 1)) == seg - 1

        def conv(u, cs):
            prev = pltpu.roll(u, 1, 0)
            prev = jnp.where(first_w, jnp.where(seg_first, 0.0, prev), prev)
            nxt = pltpu.roll(u, th - 1, 0)
            nxt = jnp.where(last_w, jnp.where(seg_last, 0.0, nxt), nxt)
            return (cb_ref[:, cs] + prev * cw_ref[0:1, cs] + u * cw_ref[1:2, cs]
                    + nxt * cw_ref[2:3, cs])

        outs = []
        for hf in range(POST_SUB):
            rs = slice(hf * th, (hf + 1) * th)

            def pick(a, b):
                return jnp.where(is_ctx, a, b)

            x = pick(x_refs[0][rs, :], x_refs[1][rs, :]) if split_x else x_refs[0][rs, :]
            mix = (_dot(pick(oab_c[rs, :], oab_l[rs, :]), wo_s[:2 * 256, :])
                   + lax.dot_general(pick(oc_c[:, rs], oc_l[:, rs]), wo_s[2 * 256:, :],
                                     (((0,), (0,)), ((), ())), preferred_element_type=F32))
            x1 = x + gt1_ref[...] * mix
            x1_ref[rs, :] = x1
            h2 = _rms_rows(x1, g2_ref[...], D_MODEL) * (1.0 + sc2_ref[...]) + sh2_ref[...]
            h2_ref[rs, :] = h2.astype(BF16)

            for j in range(N_FC):
                vs = slice(j * FC, (j + 1) * FC)
                gs = slice(D_FF + j * FC, D_FF + (j + 1) * FC)
                val = conv(_dot(h2_ref[rs, :], wu_s[:, vs]), vs)
                gate = conv(_dot(h2_ref[rs, :], wu_s[:, gs]), gs)
                act_ref[rs, vs] = (gate * jax.nn.sigmoid(gate) * val).astype(BF16)

            x2 = x1_ref[rs, :] + gt2_ref[...] * _dot(act_ref[rs, :], wd_s[...])
            if not final:
                o_refs[0][rs, :] = x2
            else:
                outs.append(_rms_rows(x2, gf_ref[...], D_MODEL))

        if final:
            y = jnp.concatenate(outs, axis=0)

            @pl.when(is_ctx)
            def _():
                o_refs[0][...] = y

            @pl.when(jnp.logical_not(is_ctx))
            def _():
                o_refs[1][...] = y


def _post_call(l, final, xs, n_ctx, s_ctx, s_lat, oab, oc, mods, wts, ada):
    split_x = len(xs) == 2
    n_lat = oab[1].shape[0]
    nt_ctx, nt_lat = n_ctx // TM_FFN, n_lat // TM_FFN
    nt = nt_ctx + nt_lat
    tiles_per_batch = s_lat // TM_FFN
    assert s_ctx <= TM_FFN and TM_FFN % s_ctx == 0 and s_lat % TM_FFN == 0

    def tile(i):
        return jnp.maximum(i - N_CAST, 0)

    def ctx_t(i):
        return jnp.minimum(tile(i), nt_ctx - 1)

    def lat_t(i):
        return jnp.maximum(tile(i) - nt_ctx, 0)

    def mod(chunk):
        def index(i):
            row = jnp.where(tile(i) < nt_ctx, 0, 1 + lat_t(i) // tiles_per_batch)
            return (row * 6 + chunk, 0, 0)
        return pl.BlockSpec((None, 1, D_MODEL), index)

    def resident(*shape):
        nd = len(shape)
        return pl.BlockSpec((None,) + shape, lambda i: (l,) + (0,) * nd,
                            pipeline_mode=pl.Buffered(1))

    if split_x:
        x_specs = [pl.BlockSpec((TM_FFN, D_MODEL), lambda i: (ctx_t(i), 0)),
                   pl.BlockSpec((TM_FFN, D_MODEL), lambda i: (lat_t(i), 0))]
    else:
        x_specs = [pl.BlockSpec((TM_FFN, D_MODEL), lambda i: (tile(i), 0))]
    in_specs = x_specs + [
        pl.BlockSpec((TM_FFN, 512), lambda i: (ctx_t(i), 0)),
        pl.BlockSpec((TM_FFN, 512), lambda i: (lat_t(i), 0)),
        pl.BlockSpec((ML_HEADS * ML_DV, TM_FFN), lambda i: (0, ctx_t(i))),
        pl.BlockSpec((ML_HEADS * ML_DV, TM_FFN), lambda i: (0, lat_t(i))),
        pl.BlockSpec((None, CAST_OUT, D_MODEL), lambda i: (l, jnp.minimum(i, N_CAST_OUT - 1), 0)),
        pl.BlockSpec((None, D_MODEL, CAST_UP), lambda i: (l, 0, jnp.minimum(i, N_CAST - 1))),
        pl.BlockSpec((None, CAST_DOWN, D_MODEL), lambda i: (l, jnp.minimum(i, N_CAST - 1), 0)),
        mod(2), mod(3), mod(4), mod(5),
        resident(1, D_MODEL), resident(CONV_W, 2 * D_FF), resident(1, 2 * D_FF),
        pl.BlockSpec((1, D_MODEL), lambda i: (0, 0)),
    ]
    args = list(xs) + [oab[0], oab[1], oc[0], oc[1], wts["w_out"], wts["w_up"], wts["w_down"],
                       mods, mods, mods, mods, wts["g_norm2"], wts["conv_w"], wts["conv_b"],
                       wts["g_final"]]
    if final:
        out_shape = (jax.ShapeDtypeStruct((n_ctx, D_MODEL), F32),
                     jax.ShapeDtypeStruct((n_lat, D_MODEL), F32))
        out_specs = (pl.BlockSpec((TM_FFN, D_MODEL), lambda i: (ctx_t(i), 0)),
                     pl.BlockSpec((TM_FFN, D_MODEL), lambda i: (lat_t(i), 0)))
    else:
        ada_w = N_MOD * D_MODEL // nt
        assert ada_w % LANES == 0
        in_specs += [pl.BlockSpec((8, D_MODEL), lambda i: (0, 0)),
                     pl.BlockSpec((None, D_MODEL, ada_w), lambda i: (l + 1, 0, tile(i))),
                     pl.BlockSpec((None, 1, ada_w), lambda i: (l + 1, 0, tile(i)))]
        args += [ada[0], ada[1], ada[2]]
        out_shape = (jax.ShapeDtypeStruct((n_ctx + n_lat, D_MODEL), F32),
                     jax.ShapeDtypeStruct((8, N_MOD * D_MODEL), F32))
        out_specs = (pl.BlockSpec((TM_FFN, D_MODEL), lambda i: (tile(i), 0)),
                     pl.BlockSpec((8, ada_w), lambda i: (0, tile(i))))
    return pl.pallas_call(
        functools.partial(_post_kernel, nt_ctx, s_ctx, split_x, final),
        out_shape=out_shape, grid=(N_CAST + nt,), in_specs=in_specs, out_specs=out_specs,
        scratch_shapes=[pltpu.VMEM((D_MODEL, D_MODEL), BF16), pltpu.VMEM((D_MODEL, 2 * D_FF), BF16),
                        pltpu.VMEM((D_FF, D_MODEL), BF16),
                        pltpu.VMEM((TM_FFN, D_MODEL), F32), pltpu.VMEM((TM_FFN, D_MODEL), BF16),
                        pltpu.VMEM((TM_FFN, D_FF), BF16)],
        compiler_params=pltpu.CompilerParams(dimension_semantics=("arbitrary",),
                                             vmem_limit_bytes=VMEM_LIMIT_FFN),
        name="post",
    )(*args)


W_IN_BODY = (352, 2656)


def _pack_in_kernel(w_ref, o_ref):
    tc = w_ref.shape[1]

    def put(dst, blk):
        o_ref[:, dst:dst + blk.shape[0]] = blk.T.astype(BF16)

    def rows(lo, n):
        return w_ref[lo:lo + n, :]

    def zeros(n):
        return jnp.zeros((n, tc), F32)

    put(C_CQ, jnp.concatenate([rows(0, MLA_Q_RANK), zeros(C_CKV - C_CQ - MLA_Q_RANK)], axis=0))
    put(C_CKV, rows(MLA_Q_RANK, MLA_KV_RANK))
    aux = [rows(MLA_Q_RANK + MLA_KV_RANK, MLA_ROPE)]
    for g in range(4):
        aux += [rows(W_IN_BODY[1] + ML_HEADS * g, ML_HEADS), zeros(8 - ML_HEADS)]
    aux.append(zeros(LANES - AUX_GATE - 32))
    put(C_AUX, jnp.concatenate(aux, axis=0))
    for lo in range(W_IN_BODY[0], W_IN_BODY[1], 256):
        put(C_DQ + lo - W_IN_BODY[0], rows(lo, 256))


def _pack_in_call(w_in):
    w_in_t = jnp.swapaxes(w_in, 1, 2)
    tc = 256
    return pl.pallas_call(
        _pack_in_kernel,
        out_shape=jax.ShapeDtypeStruct((DEPTH, D_MODEL, NP_IN), BF16),
        grid=(DEPTH, D_MODEL // tc),
        in_specs=[pl.BlockSpec((None, w_in_t.shape[1], tc), lambda l, i: (l, 0, i))],
        out_specs=pl.BlockSpec((None, tc, NP_IN), lambda l, i: (l, i, 0)),
        compiler_params=_cparams(("arbitrary", "arbitrary")),
        name="pack_w_in",
    )(w_in_t)


def _pack_weights(w_in, g_cq, w_uq, g_ckv, w_ukv, b_gate, g_subln, g_mnorm, g_norm1, g_norm2,
                  w_out, w_up, conv_w, conv_b, w_down, g_final, lam_q1, lam_k1, lam_q2, lam_k2):
    def cols(a, lo, n, pad=0):
        blk = a[..., lo:lo + n]
        if pad:
            blk = jnp.pad(blk, [(0, 0)] * (a.ndim - 1) + [(0, pad)])
        return blk

    w_in_p = _pack_in_call(w_in)

    hd = MLA_NOPE + MLA_ROPE
    w_uq_p = jnp.pad(w_uq.reshape(DEPTH, MLA_Q_RANK, MLA_HEADS, hd),
                     [(0, 0), (0, 256 - MLA_Q_RANK), (0, 0), (0, LANES - hd)])
    w_uq_p = w_uq_p.reshape(DEPTH, 256, MLA_HEADS * LANES).astype(BF16)

    w_ukv4 = w_ukv.reshape(DEPTH, MLA_KV_RANK, MLA_HEADS, MLA_NOPE + MLA_V)
    w_k = jnp.pad(w_ukv4[..., :MLA_NOPE], [(0, 0), (0, 0), (0, 0), (0, LANES - MLA_NOPE)])
    w_k = w_k.reshape(DEPTH, MLA_KV_RANK, MLA_HEADS * LANES)
    j = jnp.arange(LANES)[:, None]
    cix = jnp.arange(MLA_HEADS * LANES)[None, :]
    place = ((j < MLA_ROPE) & ((cix % LANES) == MLA_NOPE + j)).astype(F32)
    w_kk = jnp.concatenate([w_k, jnp.broadcast_to(place, (DEPTH, LANES, MLA_HEADS * LANES))],
                           axis=1).astype(BF16)
    w_v = w_ukv4[..., MLA_NOPE:].reshape(DEPTH, MLA_KV_RANK, MLA_HEADS * MLA_V).astype(BF16)

    return dict(
        w_in=w_in_p, w_uq=w_uq_p, w_kk=w_kk, w_v=w_v,
        g_norm1=g_norm1[:, None, :], g_norm2=g_norm2[:, None, :],
        g_cq=jnp.pad(g_cq, [(0, 0), (0, 256 - MLA_Q_RANK)])[:, None, :],
        g_ckv=g_ckv[:, None, :],
        b_gate=jnp.pad(jnp.pad(b_gate.reshape(DEPTH, 4, ML_HEADS), [(0, 0), (0, 0), (0, 4)])
                       .reshape(DEPTH, 32), [(0, 0), (AUX_GATE, LANES - AUX_GATE - 32)])[:, None, :],
        g_subln=jnp.tile(g_subln, (1, DF_HEADS))[:, None, :],
        g_mnorm=jnp.broadcast_to(g_mnorm[:, :, None], (DEPTH, ML_HEADS * ML_DV, LANES)),
        w_out=w_out, w_up=w_up, w_down=w_down,
        conv_w=conv_w, conv_b=conv_b[:, None, :], g_final=g_final[None, :],
        lam_q1=lam_q1[:, None, :], lam_k1=lam_k1[:, None, :],
        lam_q2=lam_q2[:, None, :], lam_k2=lam_k2[:, None, :],
    )


def _rope_tables(n_tok):
    t = np.arange(n_tok)
    row = (t // GRID_W).astype(np.float64)
    col = (t % GRID_W).astype(np.float64)
    nf = MLA_ROPE // 4
    inv = ROPE_BASE ** (-np.arange(nf, dtype=np.float64) / nf)
    ar = row[:, None] * inv[None, :]
    ac = col[:, None] * inv[None, :]
    ang = np.concatenate([ar, ar, ac, ac], axis=-1)
    quarter = (np.arange(MLA_ROPE) // nf) % 2
    cos = jnp.asarray(np.cos(ang), F32)
    sin_up = jnp.asarray(np.where(quarter == 0, -np.sin(ang), 0.0), F32)
    sin_dn = jnp.asarray(np.where(quarter == 1, np.sin(ang), 0.0), F32)
    ones = jnp.ones((n_tok, 1), F32)
    zeros = jnp.zeros((n_tok, 1), F32)

    def head_q(t32, fill):
        blk = jnp.concatenate([jnp.tile(fill, (1, MLA_NOPE)), t32, jnp.tile(fill, (1, 32))], axis=1)
        return jnp.tile(blk, (1, MLA_HEADS))

    def aux_k(t32, fill):
        return jnp.concatenate([t32, jnp.tile(fill, (1, LANES - MLA_ROPE))], axis=1)

    tq = (head_q(cos, ones), head_q(sin_up, zeros), head_q(sin_dn, zeros))
    td = tuple(jnp.tile(a, (1, 256 // DF_DIM)) for a in (cos, sin_up, sin_dn))
    tk = (aux_k(cos, ones), aux_k(sin_up, zeros), aux_k(sin_dn, zeros))
    return tq + td + tk


def kernel(x_prompt, x_sample, cache_mla_ckv, cache_mla_krope, cache_diff_k, cache_diff_v,
           state_mlstm_C, state_mlstm_n, state_mlstm_m, c, c_ctx, w_ada, b_ada, g_norm1, w_in,
           g_cq, w_uq, g_ckv, w_ukv, lam_q1, lam_k1, lam_q2, lam_k2, g_subln, b_gate, g_mnorm,
           w_out, g_norm2, w_up, conv_w, conv_b, w_down, g_final):
    bp, sp, _ = x_prompt.shape
    bl, sl, _ = x_sample.shape
    t_len = cache_mla_ckv.shape[2]

    wts = _pack_weights(w_in, g_cq, w_uq, g_ckv, w_ukv, b_gate, g_subln, g_mnorm, g_norm1, g_norm2,
                        w_out, w_up, conv_w, conv_b, w_down, g_final, lam_q1, lam_k1, lam_q2, lam_k2)
    tables = _rope_tables(sl)

    cond = jnp.concatenate([c_ctx[None, :], c, jnp.zeros((8 - 1 - bl, D_MODEL), F32)], axis=0)
    ada = (cond, w_ada, b_ada[:, None, :])
    mods = _ada_call(*ada).reshape(8 * N_MOD, 1, D_MODEL)

    def feat_major(a):
        return jnp.transpose(a, (0, 1, 3, 4, 2)).reshape(bl, DEPTH, 256, t_len)

    kctx, vctx, cdk, cdv = _ctxkv_call(
        cache_mla_ckv, jnp.swapaxes(cache_mla_krope, 2, 3), feat_major(cache_diff_k),
        feat_major(cache_diff_v), wts["w_kk"], wts["w_v"])
    c0_t = jnp.swapaxes(state_mlstm_C.reshape(bl, DEPTH, 2, 2, LANES, ML_DV), -1, -2)
    n0_r = jnp.broadcast_to(state_mlstm_n.reshape(bl, DEPTH, 2, 2, 1, LANES),
                            (bl, DEPTH, 2, 2, ML_NROWS, LANES))
    s0 = jnp.concatenate([c0_t, n0_r], axis=-2)
    m0 = jnp.broadcast_to(jnp.pad(state_mlstm_m, [(0, 0)] * 3 + [(0, 8 - ML_HEADS)])[..., None],
                          (bl, DEPTH, 2, 8, LANES))

    n_ctx, n_lat = bp * sp, bl * sl
    xs = (x_prompt.reshape(n_ctx, D_MODEL), x_sample.reshape(n_lat, D_MODEL))
    state_bufs, c_buf = None, None
    n_col, m_col = [], []
    for l in range(DEPTH):
        lam_init = 0.8 - 0.6 * math.exp(-0.3 * l)
        final = l == DEPTH - 1
        x_ctx, x_lat, lat_off = (xs[0], xs[1], 0) if len(xs) == 2 else (xs[0], xs[0], n_ctx)
        (q, k, v, dq, dk, dv, mq, mkt, mv, mo, gt, *state_bufs) = _pre_call(
            False, l, x_ctx, 0, bp, sp, mods, wts, None, state_bufs)
        oab_c = _attn_call(False, l, lam_init, (q, k, v), (dq, dk, dv), wts)
        oc_c, c_buf, n_fin, m_fin = _mlstm_call(False, l, mq, mkt, mv, mo, gt, wts, c_buf=c_buf)
        n_col.append(n_fin[..., 0, :].reshape(bp, 2, ML_HEADS, ML_DK))
        m_col.append(m_fin[:, :, :ML_HEADS, 0])
        (q, k, v, dq, dk, dv, mq, mkt, mv, mo, gt) = _pre_call(
            True, l, x_lat, lat_off, bl, sl, mods, wts, tables)
        oab_l = _attn_call(True, l, lam_init, (q, k, v), (dq, dk, dv), wts, (kctx, vctx), (cdk, cdv))
        oc_l = _mlstm_call(True, l, mq, mkt, mv, mo, gt, wts, s0, m0)
        out = _post_call(l, final, xs, n_ctx, sp, sl,
                         (oab_c.reshape(n_ctx, 512), oab_l.reshape(n_lat, 512)), (oc_c, oc_l),
                         mods, wts, ada)
        if final:
            xs = out
        else:
            xs, mods = (out[0],), out[1].reshape(8 * N_MOD, 1, D_MODEL)

    xp = xs[0].reshape(bp, sp, D_MODEL)
    xs = xs[1].reshape(bl, sl, D_MODEL)
    ckv_all, kr_all, dk_all, dv_all = state_bufs

    def token_major(a):
        return jnp.transpose(a.reshape(bp, DEPTH, DF_HEADS, 2 * DF_DIM, sp), (0, 1, 4, 2, 3))

    return (xp, xs, ckv_all, jnp.swapaxes(kr_all, 2, 3), token_major(dk_all), token_major(dv_all),
            c_buf.reshape(bp, DEPTH, 2, ML_HEADS, ML_DK, ML_DV),
            jnp.stack(n_col, axis=1), jnp.stack(m_col, axis=1))
```

```python
import functools
import math

import jax
import jax.numpy as jnp
import numpy as np
from jax import lax
from jax.experimental import pallas as pl
from jax.experimental.pallas import tpu as pltpu

F32 = jnp.float32
BF16 = jnp.bfloat16

D_MODEL = 1024
DEPTH = 4
GRID_W = 64
N_MOD = 6
EPS = 1e-6
ROPE_BASE = 10000.0
MLA_HEADS = 4
MLA_Q_RANK = 192
MLA_KV_RANK = 128
MLA_NOPE = 64
MLA_ROPE = 32
MLA_V = 64
DF_HEADS = 4
DF_DIM = 32
ML_HEADS = 4
ML_DK = 64
ML_DV = 128
D_FF = 2816
CONV_W = 3

LANES = 128
VMEM_LIMIT = 48 * 1024 * 1024
VMEM_LIMIT_FFN = 56 * 1024 * 1024

C_CQ, C_CKV, C_AUX, C_DQ, C_DK, C_DV, C_MQ, C_MK, C_MV, C_MO = (
    0, 256, 384, 512, 768, 1024, 1280, 1536, 1792, 2304)
NP_IN = 2816
AUX_GATE = 32

TM_PRE = 256
PRE_SUB = 2
TQ = 512
NB_CTX = 4
ML_CHUNK = 128
ML_NROWS = 128
NB_ML_CTX = 4
NB_ML_LAT = 2
TM_FFN = 512
POST_SUB = 1
FC = 256
N_FC = D_FF // FC
N_CAST = 22
CAST_UP = 2 * D_FF // N_CAST
CAST_DOWN = D_FF // N_CAST
CAST_OUT = 128
N_CAST_OUT = D_MODEL // CAST_OUT

NT = (((1,), (1,)), ((), ()))
LOG2E = 1.4426950408889634


def _cparams(sem):
    return pltpu.CompilerParams(dimension_semantics=sem, vmem_limit_bytes=VMEM_LIMIT)


def _dot(a, b):
    return jnp.dot(a, b, preferred_element_type=F32)


def _dot_nt(a, b):
    return lax.dot_general(a, b, NT, preferred_element_type=F32)


def _rms_rows(x, g, n):
    ms = jnp.sum(x * x, axis=-1, keepdims=True) * (1.0 / n)
    return x * lax.rsqrt(ms + EPS) * g


def _rope(x, cos, sin_up, sin_dn):
    w = x.shape[-1]
    return x * cos + pltpu.roll(x, w - 8, 1) * sin_up + pltpu.roll(x, 8, 1) * sin_dn


def _ada_block(c_ref, w_ref, b_ref):
    c = c_ref[...]
    s = (c * jax.nn.sigmoid(c)).astype(BF16)
    return _dot(s, w_ref[...].astype(BF16)) + b_ref[...]


def _ada_kernel(c_ref, w_ref, b_ref, o_ref):
    o_ref[...] = _ada_block(c_ref, w_ref, b_ref)


def _ada_call(cond, w_ada, b_ada):
    nt = 1024
    return pl.pallas_call(
        _ada_kernel,
        out_shape=jax.ShapeDtypeStruct((8, N_MOD * D_MODEL), F32),
        grid=(N_MOD * D_MODEL // nt,),
        in_specs=[pl.BlockSpec((8, D_MODEL), lambda j: (0, 0)),
                  pl.BlockSpec((None, D_MODEL, nt), lambda j: (0, 0, j)),
                  pl.BlockSpec((None, 1, nt), lambda j: (0, 0, j))],
        out_specs=pl.BlockSpec((8, nt), lambda j: (0, j)),
        compiler_params=_cparams(("arbitrary",)),
        name="ada_mod",
    )(cond, w_ada, b_ada)


def _ctxkv_kernel(ckv_ref, krt_ref, dkt_ref, dvt_ref, wkk_ref, wv_ref, k_ref, v_ref, dkb_ref, dvb_ref):
    t = ckv_ref.shape[0]
    ckv = ckv_ref[...].astype(BF16)
    kr = jnp.concatenate([krt_ref[...], jnp.zeros((LANES - MLA_ROPE, t), F32)], axis=0).T
    kin = jnp.concatenate([ckv, kr.astype(BF16)], axis=1)
    k_ref[...] = _dot(kin, wkk_ref[...]).T.astype(BF16)
    v_ref[...] = _dot(ckv, wv_ref[...]).astype(BF16)
    dkb_ref[...] = dkt_ref[...].astype(BF16)
    dvb_ref[...] = dvt_ref[...].T.astype(BF16)


def _ctxkv_call(cache_ckv, cache_kr_t, cache_dk_t, cache_dv_t, wkk, wv):
    b, _, t, _ = cache_ckv.shape

    def cache_t(w):
        return pl.BlockSpec((None, None, w, t), lambda l, i: (i, l, 0, 0))

    def out(w):
        return pl.BlockSpec((None, None, t, w), lambda l, i: (l, i, 0, 0))

    def out_t(w):
        return pl.BlockSpec((None, None, w, t), lambda l, i: (l, i, 0, 0))

    return pl.pallas_call(
        _ctxkv_kernel,
        out_shape=(jax.ShapeDtypeStruct((DEPTH, b, 512, t), BF16),
                   jax.ShapeDtypeStruct((DEPTH, b, t, 256), BF16),
                   jax.ShapeDtypeStruct((DEPTH, b, 256, t), BF16),
                   jax.ShapeDtypeStruct((DEPTH, b, t, 256), BF16)),
        grid=(DEPTH, b),
        in_specs=[pl.BlockSpec((None, None, t, MLA_KV_RANK), lambda l, i: (i, l, 0, 0)),
                  cache_t(MLA_ROPE), cache_t(256), cache_t(256),
                  pl.BlockSpec((None, 256, 512), lambda l, i: (l, 0, 0)),
                  pl.BlockSpec((None, MLA_KV_RANK, 256), lambda l, i: (l, 0, 0))],
        out_specs=(out_t(512), out(256), out_t(256), out(256)),
        compiler_params=_cparams(("arbitrary", "arbitrary")),
        name="ctx_kv",
    )(cache_ckv, cache_kr_t, cache_dk_t, cache_dv_t, wkk, wv)


def _pre_kernel(latent, kinds, *refs):
    for n in range(PRE_SUB):
        views = []
        for kind, r in zip(kinds, refs):
            if kind == "alias":
                continue
            if kind == "rows":
                r = r.at[n * TM_PRE:(n + 1) * TM_PRE]
            elif kind == "lanes":
                r = r.at[:, n * TM_PRE:(n + 1) * TM_PRE]
            elif kind == "batch":
                r = r.at[n]
            views.append(r)
        _pre_tile(latent, *views)


def _pre_tile(latent, *refs):
    (x_ref, sh_ref, sc_ref, g1_ref, win_ref, gcq_ref, wuq_ref, gckv_ref, wkk_ref, wv_ref,
     bg_ref) = refs[:11]
    refs = refs[11:]
    if latent:
        (cq_t, sqa_t, sqb_t, cd_t, sda_t, sdb_t, ck_t, ska_t, skb_t) = refs[:9]
        refs = refs[9:]
    (q_ref, k_ref, v_ref, dq_ref, dk_ref, dv_ref, mq_ref, mk_ref, mv_ref, mo_ref,
     gt_ref) = refs[:11]
    refs = refs[11:]
    if not latent:
        ckv_out, kr_out, dk_out, dv_out = refs

    x = x_ref[...]
    h = _rms_rows(x, g1_ref[...], D_MODEL) * (1.0 + sc_ref[...]) + sh_ref[...]
    proj = _dot(h.astype(BF16), win_ref[...])

    cq = _rms_rows(proj[:, C_CQ:C_CQ + 256], gcq_ref[...], MLA_Q_RANK)
    q = _dot(cq.astype(BF16), wuq_ref[...])
    if latent:
        q = _rope(q, cq_t[...], sqa_t[...], sqb_t[...])
    q_ref[...] = (q * ((MLA_NOPE + MLA_ROPE) ** -0.5 * LOG2E)).astype(BF16)

    c_kv = _rms_rows(proj[:, C_CKV:C_CKV + MLA_KV_RANK], gckv_ref[...], MLA_KV_RANK)
    aux = proj[:, C_AUX:C_AUX + LANES] + bg_ref[...]
    if latent:
        aux = _rope(aux, ck_t[...], ska_t[...], skb_t[...])
    aux_t = aux.T
    gt_ref[...] = aux_t[AUX_GATE:AUX_GATE + 32, :]
    if not latent:
        ckv_out[...] = c_kv
        kr_out[...] = aux_t[:MLA_ROPE, :]
    ckv_b = c_kv.astype(BF16)
    kin = jnp.concatenate([ckv_b, aux.astype(BF16)], axis=1)
    k_ref[...] = _dot(kin, wkk_ref[...]).T.astype(BF16)
    v_ref[...] = _dot(ckv_b, wv_ref[...]).astype(BF16)

    dq = proj[:, C_DQ:C_DQ + 256]
    dk = proj[:, C_DK:C_DK + 256]
    dv = proj[:, C_DV:C_DV + 256]
    if not latent:
        dk_t = dk.T
        dk_out[...] = dk_t
        dv_out[...] = dv.T
    else:
        dq = _rope(dq, cd_t[...], sda_t[...], sdb_t[...])
        dk_t = _rope(dk, cd_t[...], sda_t[...], sdb_t[...]).T
    dq_ref[...] = (dq * (DF_DIM ** -0.5 * LOG2E)).astype(BF16)
    dk_ref[...] = dk_t.astype(BF16)
    dv_ref[...] = dv.astype(BF16)

    lane = lax.broadcasted_iota(jnp.int32, (x.shape[0], LANES), 1)
    for h in range(ML_HEADS):
        blk = proj[:, C_MQ + (h // 2) * LANES:C_MQ + (h // 2 + 1) * LANES]
        mq_ref[h * LANES:(h + 1) * LANES, :] = (
            jnp.where((lane >> 6) == h % 2, blk, 0.0).T.astype(BF16))
    mk_ref[...] = (proj[:, C_MK:C_MK + 256] * (ML_DK ** -0.5)).astype(BF16)
    mv_ref[...] = proj[:, C_MV:C_MV + 512].T.astype(BF16)
    mo_ref[...] = proj[:, C_MO:C_MO + 512].T


def _pre_call(latent, l, x2d, tok_off, b, s, mods, wts, tables, state_bufs=None):
    ns = s // TM_PRE
    tile_off = tok_off // TM_PRE
    tm2 = PRE_SUB * TM_PRE
    seq_split = ns > 1
    assert (ns % PRE_SUB == 0) if seq_split else (b % PRE_SUB == 0 and ns == 1)
    assert tile_off % PRE_SUB == 0
    grid = (ns // PRE_SUB, b) if seq_split else (1, b // PRE_SUB)

    def tok(width):
        if seq_split:
            return pl.BlockSpec((None, tm2, width), lambda j, i: (i, j, 0)), "rows"
        return pl.BlockSpec((PRE_SUB, TM_PRE, width), lambda j, i: (i, j, 0)), "batch"

    def feat(width):
        if seq_split:
            return pl.BlockSpec((None, width, tm2), lambda j, i: (i, 0, j)), "lanes"
        return pl.BlockSpec((PRE_SUB, width, TM_PRE), lambda j, i: (i, 0, j)), "batch"

    def mod(chunk):
        if latent:
            return pl.BlockSpec((None, 1, D_MODEL), lambda j, i: ((1 + i) * 6 + chunk, 0, 0))
        return pl.BlockSpec((None, 1, D_MODEL), lambda j, i: (chunk, 0, 0))

    def lw(*shape):
        nd = len(shape)
        return pl.BlockSpec((None,) + shape, lambda j, i: (l,) + (0,) * nd)

    if seq_split:
        x_spec = pl.BlockSpec((tm2, D_MODEL), lambda j, i: ((tile_off + i * ns) // PRE_SUB + j, 0))
    else:
        x_spec = pl.BlockSpec((tm2, D_MODEL), lambda j, i: (tile_off // PRE_SUB + i, 0))
    in_specs = [x_spec, mod(0), mod(1), lw(1, D_MODEL), lw(D_MODEL, NP_IN), lw(1, 256),
                lw(256, 512), lw(1, MLA_KV_RANK), lw(256, 512), lw(MLA_KV_RANK, 256), lw(1, LANES)]
    kinds = ["rows"] + [None] * 10
    args = [x2d, mods, mods, wts["g_norm1"], wts["w_in"], wts["g_cq"], wts["w_uq"], wts["g_ckv"],
            wts["w_kk"], wts["w_v"], wts["b_gate"]]
    if latent:
        for t in tables:
            in_specs.append(pl.BlockSpec((tm2, t.shape[1]), lambda j, i: (j, 0)))
            kinds.append("rows")
            args.append(t)

    widths = [(512, BF16, False), (512, BF16, True), (256, BF16, False), (256, BF16, False),
              (256, BF16, True), (256, BF16, False), (512, BF16, True), (256, BF16, False),
              (512, BF16, True), (512, F32, True), (32, F32, True)]
    out_shape = [jax.ShapeDtypeStruct((b, w, s) if tr else (b, s, w), dt) for w, dt, tr in widths]
    out_pairs = [feat(w) if tr else tok(w) for w, _, tr in widths]
    aliases = {}
    if not latent:
        state = [(MLA_KV_RANK, False), (MLA_ROPE, True), (256, True), (256, True)]
        for k, (w, tr) in enumerate(state):
            if tr:
                spec = pl.BlockSpec((PRE_SUB, None, w, TM_PRE), lambda j, i: (i, l, 0, j))
                shape = (b, DEPTH, w, s)
            else:
                spec = pl.BlockSpec((PRE_SUB, None, TM_PRE, w), lambda j, i: (i, l, j, 0))
                shape = (b, DEPTH, s, w)
            out_shape.append(jax.ShapeDtypeStruct(shape, F32))
            out_pairs.append((spec, "batch"))
            if state_bufs is not None:
                aliases[len(args)] = len(widths) + k
                in_specs.append(pl.BlockSpec(memory_space=pl.ANY))
                kinds.append("alias")
                args.append(state_bufs[k])
    out_specs = tuple(p[0] for p in out_pairs)
    kinds += [p[1] for p in out_pairs]
    return pl.pallas_call(
        functools.partial(_pre_kernel, latent, tuple(kinds)),
        out_shape=tuple(out_shape), grid=grid, in_specs=in_specs, out_specs=out_specs,
        input_output_aliases=aliases,
        compiler_params=_cparams(("arbitrary", "arbitrary")),
        name="pre_lat" if latent else "pre_ctx",
    )(*args)


def _softmax_parts(s_list):
    m = functools.reduce(jnp.maximum, [jnp.max(s, axis=1, keepdims=True) for s in s_list])
    p_list = [jnp.exp2(s - m) for s in s_list]
    l = functools.reduce(jnp.add, [jnp.sum(p, axis=1, keepdims=True) for p in p_list])
    return p_list, l


def _attn_kernel(latent, lam_init, nb, *refs):
    n_seg = 5 if latent else 3
    o_ref = refs[-1]
    mla_in, diff_in, shared = refs[:n_seg], refs[n_seg:2 * n_seg], refs[2 * n_seg:-1]
    for n in range(nb):
        _mla_body(latent, *[r.at[n] for r in mla_in], o_ref.at[n])
        _diff_body(latent, lam_init, *[r.at[n] for r in diff_in], *shared, o_ref.at[n])


def _mla_body(latent, *refs):
    if latent:
        q_ref, k_ref, v_ref, kc_ref, vc_ref, o_ref = refs
        segs = [(kc_ref, vc_ref), (k_ref, v_ref)]
    else:
        q_ref, k_ref, v_ref, o_ref = refs
        segs = [(k_ref, v_ref)]
    lane = lax.broadcasted_iota(jnp.int32, (q_ref.shape[0], LANES), 1)

    def scores(h):
        hs = slice(h * LANES, (h + 1) * LANES)
        return [_dot(q_ref[:, hs], kr[hs, :]) for kr, _ in segs]

    outs = []
    s_next = scores(0)
    for h in range(MLA_HEADS):
        ps = slice((h // 2) * LANES, (h // 2 + 1) * LANES)
        s_list = s_next
        if h + 1 < MLA_HEADS:
            s_next = scores(h + 1)
        p_list, l = _softmax_parts(s_list)
        pv = functools.reduce(jnp.add, [_dot(p.astype(BF16), vr[:, ps])
                                        for p, (_, vr) in zip(p_list, segs)])
        outs.append(pv / l)
    o_ref[:, 0:LANES] = jnp.where(lane < MLA_V, outs[0], outs[1]).astype(BF16)
    o_ref[:, LANES:2 * LANES] = jnp.where(lane < MLA_V, outs[2], outs[3]).astype(BF16)


def _diff_body(latent, lam_init, *refs):
    if latent:
        (q_ref, k_ref, v_ref, kc_ref, vc_ref, lq1, lk1, lq2, lk2, g_ref, o_ref) = refs
        segs = [(kc_ref, vc_ref), (k_ref, v_ref)]
    else:
        (q_ref, k_ref, v_ref, lq1, lk1, lq2, lk2, g_ref, o_ref) = refs
        segs = [(k_ref, v_ref)]
    lam = (jnp.exp(jnp.sum(lq1[...] * lk1[...], axis=1, keepdims=True))
           - jnp.exp(jnp.sum(lq2[...] * lk2[...], axis=1, keepdims=True)) + lam_init)
    lane = lax.broadcasted_iota(jnp.int32, (q_ref.shape[0], LANES), 1)
    grp = lane >> 5
    qf = q_ref[...].astype(F32)

    def scores(u):
        h, c = u // 2, u % 2
        ps = slice((h // 2) * LANES, (h // 2 + 1) * LANES)
        qm = jnp.where(grp == 2 * (h % 2) + c, qf[:, ps], 0.0).astype(BF16)
        return [_dot(qm, kr[ps, :]) for kr, _ in segs]

    outs = []
    s_next = scores(0)
    for h in range(DF_HEADS):
        ps = slice((h // 2) * LANES, (h // 2 + 1) * LANES)
        hh = h % 2
        parts = []
        for c in range(2):
            s_list = s_next
            if 2 * h + c + 1 < 2 * DF_HEADS:
                s_next = scores(2 * h + c + 1)
            parts.append(_softmax_parts(s_list))
        (p0, l0), (p1, l1) = parts
        ratio = lam * l0 / l1
        pv = functools.reduce(jnp.add, [
            _dot((a0 - a1 * ratio).astype(BF16), vr[:, ps])
            for a0, a1, (_, vr) in zip(p0, p1, segs)]) * (1.0 / l0)
        valid = (lane >> 6) == hh
        ms = jnp.sum(jnp.where(valid, pv * pv, 0.0), axis=1, keepdims=True) * (1.0 / (2 * DF_DIM))
        outs.append(pv * lax.rsqrt(ms + EPS) * g_ref[:, ps] * (1.0 - lam_init))
    o_ref[:, 2 * LANES:3 * LANES] = jnp.where(lane < 2 * DF_DIM, outs[0], outs[1]).astype(BF16)
    o_ref[:, 3 * LANES:4 * LANES] = jnp.where(lane < 2 * DF_DIM, outs[2], outs[3]).astype(BF16)


def _attn_call(latent, l, lam_init, qkv, dqkv, wts, ctx=None, dctx=None):
    b, s, _ = qkv[0].shape
    nb = 1 if latent else NB_CTX
    tq = min(TQ, s)
    grid = (b // nb, s // tq)

    def group(wq, wk, wv, ctx_pair):
        specs = [pl.BlockSpec((nb, tq, wq), lambda i, j: (i, j, 0)),
                 pl.BlockSpec((nb, wk, s), lambda i, j: (i, 0, 0)),
                 pl.BlockSpec((nb, s, wv), lambda i, j: (i, 0, 0))]
        if latent:
            t = ctx_pair[1].shape[2]
            specs += [pl.BlockSpec((None, nb, wk, t), lambda i, j: (l, i, 0, 0)),
                      pl.BlockSpec((None, nb, t, wv), lambda i, j: (l, i, 0, 0))]
        return specs

    in_specs = group(512, 512, 256, ctx) + group(256, 256, 256, dctx)
    args = list(qkv) + (list(ctx) if latent else []) + list(dqkv) + (list(dctx) if latent else [])
    for name in ("lam_q1", "lam_k1", "lam_q2", "lam_k2"):
        in_specs.append(pl.BlockSpec((None, 1, DF_DIM), lambda i, j: (l, 0, 0)))
        args.append(wts[name])
    in_specs.append(pl.BlockSpec((None, 1, 256), lambda i, j: (l, 0, 0)))
    args.append(wts["g_subln"])
    return pl.pallas_call(
        functools.partial(_attn_kernel, latent, lam_init, nb),
        out_shape=jax.ShapeDtypeStruct((b, s, 512), BF16),
        grid=grid, in_specs=in_specs,
        out_specs=pl.BlockSpec((nb, tq, 512), lambda i, j: (i, j, 0)),
        compiler_params=_cparams(("arbitrary", "arbitrary")),
        name="attn_lat" if latent else "attn_ctx",
    )(*args)


def _log_sigmoid(x):
    return jnp.minimum(x, 0.0) - jnp.log1p(jnp.exp(-jnp.abs(x)))


def _mlstm_chunk(d, c, mq_ref, mk_ref, mv_ref, gt_ref, s_ref, m_ref, h_ref):
    L = ML_CHUNK
    rows = pl.ds(pl.multiple_of(c * L, L), L)
    s_i = lax.broadcasted_iota(jnp.int32, (L, L), 0)
    t_i = lax.broadcasted_iota(jnp.int32, (L, L), 1)
    mask = (s_i <= t_i) if d == 0 else (s_i >= t_i)
    tri = jnp.where(mask, 1.0, 0.0).astype(BF16)

    ig = gt_ref[16 * d:16 * d + 8, rows]
    lf = _log_sigmoid(gt_ref[16 * d + 8:16 * d + 16, rows])
    hi = lf.astype(BF16).astype(F32)
    r1 = lf - hi
    mid = r1.astype(BF16).astype(F32)
    parts = _dot(jnp.concatenate([hi, mid, r1 - mid], axis=0).astype(BF16), tri)
    bc = parts[0:8] + parts[8:16] + parts[16:24]
    rvec = ig - bc
    total = jnp.sum(lf, axis=1, keepdims=True)
    mm = m_ref[d]
    gvec = total + rvec
    m_new = jnp.maximum(total + mm, jnp.max(gvec, axis=1, keepdims=True))
    ws = jnp.exp(gvec - m_new).astype(BF16)
    cdec = jnp.exp(total + mm - m_new)
    m_ref[d] = m_new

    rv_t = jnp.concatenate([rvec, jnp.zeros((LANES - 8, L), F32)], axis=0).T

    s_old = [s_ref[d, pair].astype(BF16) for pair in range(2)]
    upd = []
    for h in range(ML_HEADS):
        pair = h // 2
        hs = slice(h * ML_DV, (h + 1) * ML_DV)
        qt = mq_ref[hs, rows]
        kp = mk_ref[rows, pair * LANES:(pair + 1) * LANES]
        vt = mv_ref[hs, rows]
        mmh = mm[h:h + 1, 0:1]
        rm = jnp.where(mask, rv_t[:, h:h + 1], -jnp.inf)
        a = jnp.maximum(jnp.max(rm, axis=0, keepdims=True), mmh)
        wqk = jnp.exp(rm - a) * _dot(kp, qt)
        dec = jnp.exp(mmh - a)
        qc = _dot(s_old[pair], qt)
        num = _dot(vt, wqk.astype(BF16)) + dec * qc[:ML_DV]
        den = jnp.sum(wqk, axis=0, keepdims=True) + dec * qc[ML_DV:ML_DV + 1]
        inv = 1.0 / jnp.maximum(jnp.abs(den), jnp.exp(-(a + bc[h:h + 1, :])))
        h_ref[d, hs, rows] = num * inv
        wsr = ws[h:h + 1, :]
        vaug = jnp.concatenate([vt * wsr, jnp.broadcast_to(wsr, (ML_NROWS, L))], axis=0)
        upd.append(_dot(vaug, kp))

    low = lax.broadcasted_iota(jnp.int32, (ML_DV + ML_NROWS, LANES), 1) < ML_DK
    for pair in range(2):
        h0, h1 = 2 * pair, 2 * pair + 1
        cd = jnp.where(low, cdec[h0:h0 + 1, 0:1], cdec[h1:h1 + 1, 0:1])
        s_ref[d, pair] = cd * s_ref[d, pair] + jnp.where(low, upd[h0], upd[h1])


def _mlstm_kernel(latent, seq, nb, has_alias, *refs):
    if latent:
        (mq_ref, mk_ref, mv_ref, mo_ref, gt_ref, g_ref, s0_ref, m0_ref,
         o_ref, s_ref, m_ref, h_ref) = refs
        s_ref[...] = s0_ref[...]
        m_ref[...] = m0_ref[...]
    else:
        if has_alias:
            refs = refs[:6] + refs[7:]
        (mq_ref, mk_ref, mv_ref, mo_ref, gt_ref, g_ref,
         o_ref, cf_ref, nf_ref, mf_ref, s_ref, m_ref, h_ref) = refs
        s_ref[...] = jnp.zeros(s_ref.shape, F32)
        m_ref[...] = jnp.zeros(m_ref.shape, F32)
    nc = seq // ML_CHUNK

    def body(j, carry):
        for n in range(nb):
            views = (mq_ref.at[n], mk_ref.at[n], mv_ref.at[n], gt_ref.at[n], s_ref.at[n],
                     m_ref.at[n], h_ref.at[n])
            _mlstm_chunk(0, j, *views)
            _mlstm_chunk(1, nc - 1 - j, *views)
        return carry

    lax.fori_loop(0, nc, body, 0, unroll=min(nc, 4))

    for n in range(nb):
        for h in range(ML_HEADS):
            hs = slice(h * ML_DV, (h + 1) * ML_DV)
            for j in range(nc):
                ts = slice(j * LANES, (j + 1) * LANES)
                hsum = h_ref[n, 0, hs, ts] + h_ref[n, 1, hs, ts]
                ms = jnp.sum(hsum * hsum, axis=0, keepdims=True) * (1.0 / ML_DV)
                y = hsum * lax.rsqrt(ms + EPS) * g_ref[hs, :]
                o_ref[hs, n * seq + j * LANES:n * seq + (j + 1) * LANES] = (
                    jax.nn.sigmoid(mo_ref[n, hs, ts]) * y).astype(BF16)
    if not latent:
        for n in range(nb):
            for d in range(2):
                for pair in range(2):
                    cf_ref[n, d, pair] = s_ref[n, d, pair, :ML_DV, :].T
        nf_ref[...] = s_ref[:, :, :, ML_DV:ML_DV + 8, :]
        mf_ref[...] = m_ref[...]


def _mlstm_call(latent, l, mqt, mk, mvt, mot, gt, wts, s0=None, m0=None, c_buf=None):
    assert ML_CHUNK == LANES
    b, _, s = mqt.shape
    nb = NB_ML_LAT if latent else NB_ML_CTX

    def feat(w):
        return pl.BlockSpec((nb, w, s), lambda i: (i, 0, 0))

    in_specs = [feat(512), pl.BlockSpec((nb, s, 256), lambda i: (i, 0, 0)), feat(512), feat(512),
                feat(32), pl.BlockSpec((None, 512, LANES), lambda i: (l, 0, 0))]
    args = [mqt, mk, mvt, mot, gt, wts["g_mnorm"]]
    s_spec_shape = (2, 2, ML_DV + ML_NROWS, LANES)
    m_spec_shape = (2, 8, LANES)
    scratch = [pltpu.VMEM((nb,) + s_spec_shape, F32), pltpu.VMEM((nb,) + m_spec_shape, F32),
               pltpu.VMEM((nb, 2, ML_HEADS * ML_DV, s), F32)]
    o_shape = jax.ShapeDtypeStruct((ML_HEADS * ML_DV, b * s), BF16)
    o_spec = pl.BlockSpec((ML_HEADS * ML_DV, nb * s), lambda i: (0, i))
    if latent:
        in_specs += [pl.BlockSpec((nb, None) + s_spec_shape, lambda i: (i, l, 0, 0, 0, 0)),
                     pl.BlockSpec((nb, None) + m_spec_shape, lambda i: (i, l, 0, 0, 0))]
        args += [s0, m0]
        out_shape = o_shape
        out_specs = o_spec
    aliases = {}
    if not latent:
        half = (2, 2, LANES, LANES)
        out_shape = (o_shape,
                     jax.ShapeDtypeStruct((b, DEPTH) + half, F32),
                     jax.ShapeDtypeStruct((b, 2, 2, 8, LANES), F32),
                     jax.ShapeDtypeStruct((b,) + m_spec_shape, F32))
        out_specs = (o_spec,
                     pl.BlockSpec((nb, None) + half, lambda i: (i, l, 0, 0, 0, 0)),
                     pl.BlockSpec((nb, 2, 2, 8, LANES), lambda i: (i, 0, 0, 0, 0)),
                     pl.BlockSpec((nb,) + m_spec_shape, lambda i: (i, 0, 0, 0)))
        if c_buf is not None:
            aliases[len(args)] = 1
            in_specs.append(pl.BlockSpec(memory_space=pl.ANY))
            args.append(c_buf)
    return pl.pallas_call(
        functools.partial(_mlstm_kernel, latent, s, nb, c_buf is not None),
        out_shape=out_shape, grid=(b // nb,), in_specs=in_specs, out_specs=out_specs,
        input_output_aliases=aliases,
        scratch_shapes=scratch,
        compiler_params=_cparams(("arbitrary",)),
        name="mlstm_lat" if latent else "mlstm_ctx",
    )(*args)


def _post_kernel(nt_ctx, seg_ctx, split_x, final, *refs):
    n_x = 2 if split_x else 1
    x_refs, refs = refs[:n_x], refs[n_x:]
    (oab_c, oab_l, oc_c, oc_l, wo_ref, wu_ref, wd_ref, gt1_ref, sh2_ref, sc2_ref, gt2_ref, g2_ref,
     cw_ref, cb_ref, gf_ref) = refs[:15]
    refs = refs[15:]
    if final:
        o_refs, refs = refs[:2], refs[2:]
    else:
        (cond_ref, wada_ref, bada_ref), o_refs, mods_next_ref, refs = (
            refs[:3], refs[3:4], refs[4], refs[5:])
    wo_s, wu_s, wd_s, x1_ref, h2_ref, act_ref = refs
    i = pl.program_id(0)

    @pl.when(i < N_CAST)
    def _():
        wu_s[:, pl.ds(pl.multiple_of(i * CAST_UP, CAST_UP), CAST_UP)] = wu_ref[...].astype(BF16)
        wd_s[pl.ds(pl.multiple_of(i * CAST_DOWN, CAST_DOWN), CAST_DOWN), :] = wd_ref[...].astype(BF16)

    @pl.when(i < N_CAST_OUT)
    def _():
        wo_s[pl.ds(pl.multiple_of(i * CAST_OUT, CAST_OUT), CAST_OUT), :] = wo_ref[...].astype(BF16)

    @pl.when(i >= N_CAST)
    def _():
        t = i - N_CAST
        is_ctx = t < nt_ctx
        if not final:
            mods_next_ref[...] = _ada_block(cond_ref, wada_ref, bada_ref)

        th = TM_FFN // POST_SUB
        seg = jnp.where(is_ctx, seg_ctx, GRID_W)
        row = lax.broadcasted_iota(jnp.int32, (th, FC), 0)
        first_w = (row & (GRID_W - 1)) == 0
        last_w = (row & (GRID_W - 1)) == GRID_W - 1
        seg_first = (row & (seg - 1)) == 0
        seg_last = (row & (seg - 1)) == seg - 1

        def conv(u, cs):
            prev = pltpu.roll(u, 1, 0)
            prev = jnp.where(first_w, jnp.where(seg_first, 0.0, prev), prev)
            nxt = pltpu.roll(u, th - 1, 0)
            nxt = jnp.where(last_w, jnp.where(seg_last, 0.0, nxt), nxt)
            return (cb_ref[:, cs] + prev * cw_ref[0:1, cs] + u * cw_ref[1:2, cs]
                    + nxt * cw_ref[2:3, cs])

        outs = []
        for hf in range(POST_SUB):
            rs = slice(hf * th, (hf + 1) * th)

            def pick(a, b):
                return jnp.where(is_ctx, a, b)

            x = pick(x_refs[0][rs, :], x_refs[1][rs, :]) if split_x else x_refs[0][rs, :]
            mix = (_dot(pick(oab_c[rs, :], oab_l[rs, :]), wo_s[:2 * 256, :])
                   + lax.dot_general(pick(oc_c[:, rs], oc_l[:, rs]), wo_s[2 * 256:, :],
                                     (((0,), (0,)), ((), ())), preferred_element_type=F32))
            x1 = x + gt1_ref[...] * mix
            x1_ref[rs, :] = x1
            h2 = _rms_rows(x1, g2_ref[...], D_MODEL) * (1.0 + sc2_ref[...]) + sh2_ref[...]
            h2_ref[rs, :] = h2.astype(BF16)

            for j in range(N_FC):
                vs = slice(j * FC, (j + 1) * FC)
                gs = slice(D_FF + j * FC, D_FF + (j + 1) * FC)
                val = conv(_dot(h2_ref[rs, :], wu_s[:, vs]), vs)
                gate = conv(_dot(h2_ref[rs, :], wu_s[:, gs]), gs)
                act_ref[rs, vs] = (gate * jax.nn.sigmoid(gate) * val).astype(BF16)

            x2 = x1_ref[rs, :] + gt2_ref[...] * _dot(act_ref[rs, :], wd_s[...])
            if not final:
                o_refs[0][rs, :] = x2
            else:
                outs.append(_rms_rows(x2, gf_ref[...], D_MODEL))

        if final:
            y = jnp.concatenate(outs, axis=0)

            @pl.when(is_ctx)
            def _():
                o_refs[0][...] = y

            @pl.when(jnp.logical_not(is_ctx))
            def _():
                o_refs[1][...] = y


def _post_call(l, final, xs, n_ctx, s_ctx, s_lat, oab, oc, mods, wts, ada):
    split_x = len(xs) == 2
    n_lat = oab[1].shape[0]
    nt_ctx, nt_lat = n_ctx // TM_FFN, n_lat // TM_FFN
    nt = nt_ctx + nt_lat
    tiles_per_batch = s_lat // TM_FFN
    assert s_ctx <= TM_FFN and TM_FFN % s_ctx == 0 and s_lat % TM_FFN == 0

    def tile(i):
        return jnp.maximum(i - N_CAST, 0)

    def ctx_t(i):
        return jnp.minimum(tile(i), nt_ctx - 1)

    def lat_t(i):
        return jnp.maximum(tile(i) - nt_ctx, 0)

    def mod(chunk):
        def index(i):
            row = jnp.where(tile(i) < nt_ctx, 0, 1 + lat_t(i) // tiles_per_batch)
            return (row * 6 + chunk, 0, 0)
        return pl.BlockSpec((None, 1, D_MODEL), index)

    def resident(*shape):
        nd = len(shape)
        return pl.BlockSpec((None,) + shape, lambda i: (l,) + (0,) * nd,
                            pipeline_mode=pl.Buffered(1))

    if split_x:
        x_specs = [pl.BlockSpec((TM_FFN, D_MODEL), lambda i: (ctx_t(i), 0)),
                   pl.BlockSpec((TM_FFN, D_MODEL), lambda i: (lat_t(i), 0))]
    else:
        x_specs = [pl.BlockSpec((TM_FFN, D_MODEL), lambda i: (tile(i), 0))]
    in_specs = x_specs + [
        pl.BlockSpec((TM_FFN, 512), lambda i: (ctx_t(i), 0)),
        pl.BlockSpec((TM_FFN, 512), lambda i: (lat_t(i), 0)),
        pl.BlockSpec((ML_HEADS * ML_DV, TM_FFN), lambda i: (0, ctx_t(i))),
        pl.BlockSpec((ML_HEADS * ML_DV, TM_FFN), lambda i: (0, lat_t(i))),
        pl.BlockSpec((None, CAST_OUT, D_MODEL), lambda i: (l, jnp.minimum(i, N_CAST_OUT - 1), 0)),
        pl.BlockSpec((None, D_MODEL, CAST_UP), lambda i: (l, 0, jnp.minimum(i, N_CAST - 1))),
        pl.BlockSpec((None, CAST_DOWN, D_MODEL), lambda i: (l, jnp.minimum(i, N_CAST - 1), 0)),
        mod(2), mod(3), mod(4), mod(5),
        resident(1, D_MODEL), resident(CONV_W, 2 * D_FF), resident(1, 2 * D_FF),
        pl.BlockSpec((1, D_MODEL), lambda i: (0, 0)),
    ]
    args = list(xs) + [oab[0], oab[1], oc[0], oc[1], wts["w_out"], wts["w_up"], wts["w_down"],
                       mods, mods, mods, mods, wts["g_norm2"], wts["conv_w"], wts["conv_b"],
                       wts["g_final"]]
    if final:
        out_shape = (jax.ShapeDtypeStruct((n_ctx, D_MODEL), F32),
                     jax.ShapeDtypeStruct((n_lat, D_MODEL), F32))
        out_specs = (pl.BlockSpec((TM_FFN, D_MODEL), lambda i: (ctx_t(i), 0)),
                     pl.BlockSpec((TM_FFN, D_MODEL), lambda i: (lat_t(i), 0)))
    else:
        ada_w = N_MOD * D_MODEL // nt
        assert ada_w % LANES == 0
        in_specs += [pl.BlockSpec((8, D_MODEL), lambda i: (0, 0)),
                     pl.BlockSpec((None, D_MODEL, ada_w), lambda i: (l + 1, 0, tile(i))),
                     pl.BlockSpec((None, 1, ada_w), lambda i: (l + 1, 0, tile(i)))]
        args += [ada[0], ada[1], ada[2]]
        out_shape = (jax.ShapeDtypeStruct((n_ctx + n_lat, D_MODEL), F32),
                     jax.ShapeDtypeStruct((8, N_MOD * D_MODEL), F32))
        out_specs = (pl.BlockSpec((TM_FFN, D_MODEL), lambda i: (tile(i), 0)),
                     pl.BlockSpec((8, ada_w), lambda i: (0, tile(i))))
    return pl.pallas_call(
        functools.partial(_post_kernel, nt_ctx, s_ctx, split_x, final),
        out_shape=out_shape, grid=(N_CAST + nt,), in_specs=in_specs, out_specs=out_specs,
        scratch_shapes=[pltpu.VMEM((D_MODEL, D_MODEL), BF16), pltpu.VMEM((D_MODEL, 2 * D_FF), BF16),
                        pltpu.VMEM((D_FF, D_MODEL), BF16),
                        pltpu.VMEM((TM_FFN, D_MODEL), F32), pltpu.VMEM((TM_FFN, D_MODEL), BF16),
                        pltpu.VMEM((TM_FFN, D_FF), BF16)],
        compiler_params=pltpu.CompilerParams(dimension_semantics=("arbitrary",),
                                             vmem_limit_bytes=VMEM_LIMIT_FFN),
        name="post",
    )(*args)


W_IN_BODY = (352, 2656)


def _pack_in_kernel(w_ref, o_ref):
    tc = w_ref.shape[1]

    def put(dst, blk):
        o_ref[:, dst:dst + blk.shape[0]] = blk.T.astype(BF16)

    def rows(lo, n):
        return w_ref[lo:lo + n, :]

    def zeros(n):
        return jnp.zeros((n, tc), F32)

    put(C_CQ, jnp.concatenate([rows(0, MLA_Q_RANK), zeros(C_CKV - C_CQ - MLA_Q_RANK)], axis=0))
    put(C_CKV, rows(MLA_Q_RANK, MLA_KV_RANK))
    aux = [rows(MLA_Q_RANK + MLA_KV_RANK, MLA_ROPE)]
    for g in range(4):
        aux += [rows(W_IN_BODY[1] + ML_HEADS * g, ML_HEADS), zeros(8 - ML_HEADS)]
    aux.append(zeros(LANES - AUX_GATE - 32))
    put(C_AUX, jnp.concatenate(aux, axis=0))
    for lo in range(W_IN_BODY[0], W_IN_BODY[1], 256):
        put(C_DQ + lo - W_IN_BODY[0], rows(lo, 256))


def _pack_in_call(w_in):
    w_in_t = jnp.swapaxes(w_in, 1, 2)
    tc = 256
    return pl.pallas_call(
        _pack_in_kernel,
        out_shape=jax.ShapeDtypeStruct((DEPTH, D_MODEL, NP_IN), BF16),
        grid=(DEPTH, D_MODEL // tc),
        in_specs=[pl.BlockSpec((None, w_in_t.shape[1], tc), lambda l, i: (l, 0, i))],
        out_specs=pl.BlockSpec((None, tc, NP_IN), lambda l, i: (l, i, 0)),
        compiler_params=_cparams(("arbitrary", "arbitrary")),
        name="pack_w_in",
    )(w_in_t)


def _pack_weights(w_in, g_cq, w_uq, g_ckv, w_ukv, b_gate, g_subln, g_mnorm, g_norm1, g_norm2,
                  w_out, w_up, conv_w, conv_b, w_down, g_final, lam_q1, lam_k1, lam_q2, lam_k2):
    def cols(a, lo, n, pad=0):
        blk = a[..., lo:lo + n]
        if pad:
            blk = jnp.pad(blk, [(0, 0)] * (a.ndim - 1) + [(0, pad)])
        return blk

    w_in_p = _pack_in_call(w_in)

    hd = MLA_NOPE + MLA_ROPE
    w_uq_p = jnp.pad(w_uq.reshape(DEPTH, MLA_Q_RANK, MLA_HEADS, hd),
                     [(0, 0), (0, 256 - MLA_Q_RANK), (0, 0), (0, LANES - hd)])
    w_uq_p = w_uq_p.reshape(DEPTH, 256, MLA_HEADS * LANES).astype(BF16)

    w_ukv4 = w_ukv.reshape(DEPTH, MLA_KV_RANK, MLA_HEADS, MLA_NOPE + MLA_V)
    w_k = jnp.pad(w_ukv4[..., :MLA_NOPE], [(0, 0), (0, 0), (0, 0), (0, LANES - MLA_NOPE)])
    w_k = w_k.reshape(DEPTH, MLA_KV_RANK, MLA_HEADS * LANES)
    j = jnp.arange(LANES)[:, None]
    cix = jnp.arange(MLA_HEADS * LANES)[None, :]
    place = ((j < MLA_ROPE) & ((cix % LANES) == MLA_NOPE + j)).astype(F32)
    w_kk = jnp.concatenate([w_k, jnp.broadcast_to(place, (DEPTH, LANES, MLA_HEADS * LANES))],
                           axis=1).astype(BF16)
    w_v = w_ukv4[..., MLA_NOPE:].reshape(DEPTH, MLA_KV_RANK, MLA_HEADS * MLA_V).astype(BF16)

    return dict(
        w_in=w_in_p, w_uq=w_uq_p, w_kk=w_kk, w_v=w_v,
        g_norm1=g_norm1[:, None, :], g_norm2=g_norm2[:, None, :],
        g_cq=jnp.pad(g_cq, [(0, 0), (0, 256 - MLA_Q_RANK)])[:, None, :],
        g_ckv=g_ckv[:, None, :],
        b_gate=jnp.pad(jnp.pad(b_gate.reshape(DEPTH, 4, ML_HEADS), [(0, 0), (0, 0), (0, 4)])
                       .reshape(DEPTH, 32), [(0, 0), (AUX_GATE, LANES - AUX_GATE - 32)])[:, None, :],
        g_subln=jnp.tile(g_subln, (1, DF_HEADS))[:, None, :],
        g_mnorm=jnp.broadcast_to(g_mnorm[:, :, None], (DEPTH, ML_HEADS * ML_DV, LANES)),
        w_out=w_out, w_up=w_up, w_down=w_down,
        conv_w=conv_w, conv_b=conv_b[:, None, :], g_final=g_final[None, :],
        lam_q1=lam_q1[:, None, :], lam_k1=lam_k1[:, None, :],
        lam_q2=lam_q2[:, None, :], lam_k2=lam_k2[:, None, :],
    )


def _rope_tables(n_tok):
    t = np.arange(n_tok)
    row = (t // GRID_W).astype(np.float64)
    col = (t % GRID_W).astype(np.float64)
    nf = MLA_ROPE // 4
    inv = ROPE_BASE ** (-np.arange(nf, dtype=np.float64) / nf)
    ar = row[:, None] * inv[None, :]
    ac = col[:, None] * inv[None, :]
    ang = np.concatenate([ar, ar, ac, ac], axis=-1)
    quarter = (np.arange(MLA_ROPE) // nf) % 2
    cos = jnp.asarray(np.cos(ang), F32)
    sin_up = jnp.asarray(np.where(quarter == 0, -np.sin(ang), 0.0), F32)
    sin_dn = jnp.asarray(np.where(quarter == 1, np.sin(ang), 0.0), F32)
    ones = jnp.ones((n_tok, 1), F32)
    zeros = jnp.zeros((n_tok, 1), F32)

    def head_q(t32, fill):
        blk = jnp.concatenate([jnp.tile(fill, (1, MLA_NOPE)), t32, jnp.tile(fill, (1, 32))], axis=1)
        return jnp.tile(blk, (1, MLA_HEADS))

    def aux_k(t32, fill):
        return jnp.concatenate([t32, jnp.tile(fill, (1, LANES - MLA_ROPE))], axis=1)

    tq = (head_q(cos, ones), head_q(sin_up, zeros), head_q(sin_dn, zeros))
    td = tuple(jnp.tile(a, (1, 256 // DF_DIM)) for a in (cos, sin_up, sin_dn))
    tk = (aux_k(cos, ones), aux_k(sin_up, zeros), aux_k(sin_dn, zeros))
    return tq + td + tk


def kernel(x_prompt, x_sample, cache_mla_ckv, cache_mla_krope, cache_diff_k, cache_diff_v,
           state_mlstm_C, state_mlstm_n, state_mlstm_m, c, c_ctx, w_ada, b_ada, g_norm1, w_in,
           g_cq, w_uq, g_ckv, w_ukv, lam_q1, lam_k1, lam_q2, lam_k2, g_subln, b_gate, g_mnorm,
           w_out, g_norm2, w_up, conv_w, conv_b, w_down, g_final):
    bp, sp, _ = x_prompt.shape
    bl, sl, _ = x_sample.shape
    t_len = cache_mla_ckv.shape[2]

    wts = _pack_weights(w_in, g_cq, w_uq, g_ckv, w_ukv, b_gate, g_subln, g_mnorm, g_norm1, g_norm2,
                        w_out, w_up, conv_w, conv_b, w_down, g_final, lam_q1, lam_k1, lam_q2, lam_k2)
    tables = _rope_tables(sl)

    cond = jnp.concatenate([c_ctx[None, :], c, jnp.zeros((8 - 1 - bl, D_MODEL), F32)], axis=0)
    ada = (cond, w_ada, b_ada[:, None, :])
    mods = _ada_call(*ada).reshape(8 * N_MOD, 1, D_MODEL)

    def feat_major(a):
        return jnp.transpose(a, (0, 1, 3, 4, 2)).reshape(bl, DEPTH, 256, t_len)

    kctx, vctx, cdk, cdv = _ctxkv_call(
        cache_mla_ckv, jnp.swapaxes(cache_mla_krope, 2, 3), feat_major(cache_diff_k),
        feat_major(cache_diff_v), wts["w_kk"], wts["w_v"])
    c0_t = jnp.swapaxes(state_mlstm_C.reshape(bl, DEPTH, 2, 2, LANES, ML_DV), -1, -2)
    n0_r = jnp.broadcast_to(state_mlstm_n.reshape(bl, DEPTH, 2, 2, 1, LANES),
                            (bl, DEPTH, 2, 2, ML_NROWS, LANES))
    s0 = jnp.concatenate([c0_t, n0_r], axis=-2)
    m0 = jnp.broadcast_to(jnp.pad(state_mlstm_m, [(0, 0)] * 3 + [(0, 8 - ML_HEADS)])[..., None],
                          (bl, DEPTH, 2, 8, LANES))

    n_ctx, n_lat = bp * sp, bl * sl
    xs = (x_prompt.reshape(n_ctx, D_MODEL), x_sample.reshape(n_lat, D_MODEL))
    state_bufs = [jnp.zeros((bp, DEPTH, sp, MLA_KV_RANK), F32), jnp.zeros((bp, DEPTH, MLA_ROPE, sp), F32),
                  jnp.zeros((bp, DEPTH, 256, sp), F32), jnp.zeros((bp, DEPTH, 256, sp), F32)]
    c_buf = jnp.zeros((bp, DEPTH, 2, 2, LANES, LANES), F32)
    n_col, m_col = [], []
    for l in range(DEPTH):
        lam_init = 0.8 - 0.6 * math.exp(-0.3 * l)
        final = l == DEPTH - 1
        x_ctx, x_lat, lat_off = (xs[0], xs[1], 0) if len(xs) == 2 else (xs[0], xs[0], n_ctx)
        (q, k, v, dq, dk, dv, mq, mkt, mv, mo, gt, *state_bufs) = _pre_call(
            False, l, x_ctx, 0, bp, sp, mods, wts, None, state_bufs)
        oab_c = _attn_call(False, l, lam_init, (q, k, v), (dq, dk, dv), wts)
        oc_c, c_buf, n_fin, m_fin = _mlstm_call(False, l, mq, mkt, mv, mo, gt, wts, c_buf=c_buf)
        n_col.append(n_fin[..., 0, :].reshape(bp, 2, ML_HEADS, ML_DK))
        m_col.append(m_fin[:, :, :ML_HEADS, 0])
        (q, k, v, dq, dk, dv, mq, mkt, mv, mo, gt) = _pre_call(
            True, l, x_lat, lat_off, bl, sl, mods, wts, tables)
        oab_l = _attn_call(True, l, lam_init, (q, k, v), (dq, dk, dv), wts, (kctx, vctx), (cdk, cdv))
        oc_l = _mlstm_call(True, l, mq, mkt, mv, mo, gt, wts, s0, m0)
        out = _post_call(l, final, xs, n_ctx, sp, sl,
                         (oab_c.reshape(n_ctx, 512), oab_l.reshape(n_lat, 512)), (oc_c, oc_l),
                         mods, wts, ada)
        if final:
            xs = out
        else:
            xs, mods = (out[0],), out[1].reshape(8 * N_MOD, 1, D_MODEL)

    xp = xs[0].reshape(bp, sp, D_MODEL)
    xs = xs[1].reshape(bl, sl, D_MODEL)
    ckv_all, kr_all, dk_all, dv_all = state_bufs

    def token_major(a):
        return jnp.transpose(a.reshape(bp, DEPTH, DF_HEADS, 2 * DF_DIM, sp), (0, 1, 4, 2, 3))

    return (xp, xs, ckv_all, jnp.swapaxes(kr_all, 2, 3), token_major(dk_all), token_major(dv_all),
            c_buf.reshape(bp, DEPTH, 2, ML_HEADS, ML_DK, ML_DV),
            jnp.stack(n_col, axis=1), jnp.stack(m_col, axis=1))
```

```python
import functools
import math

import jax
import jax.numpy as jnp
import numpy as np
from jax import lax
from jax.experimental import pallas as pl
from jax.experimental.pallas import tpu as pltpu

F32 = jnp.float32
BF16 = jnp.bfloat16

D_MODEL = 1024
DEPTH = 4
GRID_W = 64
N_MOD = 6
EPS = 1e-6
ROPE_BASE = 10000.0
MLA_HEADS = 4
MLA_Q_RANK = 192
MLA_KV_RANK = 128
MLA_NOPE = 64
MLA_ROPE = 32
MLA_V = 64
DF_HEADS = 4
DF_DIM = 32
ML_HEADS = 4
ML_DK = 64
ML_DV = 128
D_FF = 2816
CONV_W = 3

LANES = 128
VMEM_LIMIT = 48 * 1024 * 1024
VMEM_LIMIT_FFN = 56 * 1024 * 1024

C_CQ, C_CKV, C_AUX, C_DQ, C_DK, C_DV, C_MQ, C_MK, C_MV, C_MO = (
    0, 256, 384, 512, 768, 1024, 1280, 1536, 1792, 2304)
NP_IN = 2816
AUX_GATE = 32

NAT_COLS = ((0, 512), (512, 256), (768, 256), (1024, 256), (1280, 256))
NAT_W = 1536
FEAT_ROWS = ((0, 512), (512, 512), (1024, 512), (1536, 256))
FEAT_W = 1792
F32_ROWS = ((0, 512), (512, 32))
F32_W = 544
TM_PRE = 256
PRE_SUB = 2
TQ = 512
NB_CTX = 4
ML_CHUNK = 128
ML_NROWS = 128
NB_ML_CTX = 4
NB_ML_LAT = 1
TM_FFN = 512
POST_SUB = 1
FC = 256
N_FC = D_FF // FC
N_CAST = 22
CAST_UP = 2 * D_FF // N_CAST
CAST_DOWN = D_FF // N_CAST
CAST_OUT = 128
N_CAST_OUT = D_MODEL // CAST_OUT

NT = (((1,), (1,)), ((), ()))
LOG2E = 1.4426950408889634


def _cparams(sem):
    return pltpu.CompilerParams(dimension_semantics=sem, vmem_limit_bytes=VMEM_LIMIT)


def _dot(a, b):
    return jnp.dot(a, b, preferred_element_type=F32)


def _dot_nt(a, b):
    return lax.dot_general(a, b, NT, preferred_element_type=F32)


def _rms_rows(x, g, n):
    ms = jnp.sum(x * x, axis=-1, keepdims=True) * (1.0 / n)
    return x * lax.rsqrt(ms + EPS) * g


def _rope(x, cos, sin_up, sin_dn):
    w = x.shape[-1]
    return x * cos + pltpu.roll(x, w - 8, 1) * sin_up + pltpu.roll(x, 8, 1) * sin_dn


def _ada_block(c_ref, w_ref, b_ref):
    c = c_ref[...]
    s = (c * jax.nn.sigmoid(c)).astype(BF16)
    return _dot(s, w_ref[...].astype(BF16)) + b_ref[...]


def _ada_kernel(c_ref, w_ref, b_ref, o_ref):
    o_ref[...] = _ada_block(c_ref, w_ref, b_ref)


def _ada_call(cond, w_ada, b_ada):
    nt = 1024
    return pl.pallas_call(
        _ada_kernel,
        out_shape=jax.ShapeDtypeStruct((8, N_MOD * D_MODEL), F32),
        grid=(N_MOD * D_MODEL // nt,),
        in_specs=[pl.BlockSpec((8, D_MODEL), lambda j: (0, 0)),
                  pl.BlockSpec((None, D_MODEL, nt), lambda j: (0, 0, j)),
                  pl.BlockSpec((None, 1, nt), lambda j: (0, 0, j))],
        out_specs=pl.BlockSpec((8, nt), lambda j: (0, j)),
        compiler_params=_cparams(("arbitrary",)),
        name="ada_mod",
    )(cond, w_ada, b_ada)


def _ctxkv_kernel(ckv_ref, krt_ref, dkt_ref, dvt_ref, wkk_ref, wv_ref, k_ref, v_ref, dkb_ref, dvb_ref):
    t = ckv_ref.shape[0]
    ckv = ckv_ref[...].astype(BF16)
    kr = jnp.concatenate([krt_ref[...], jnp.zeros((LANES - MLA_ROPE, t), F32)], axis=0).T
    kin = jnp.concatenate([ckv, kr.astype(BF16)], axis=1)
    k_ref[...] = _dot(kin, wkk_ref[...]).T.astype(BF16)
    v_ref[...] = _dot(ckv, wv_ref[...]).astype(BF16)
    dkb_ref[...] = dkt_ref[...].astype(BF16)
    dvb_ref[...] = dvt_ref[...].T.astype(BF16)


def _ctxkv_call(cache_ckv, cache_kr_t, cache_dk_t, cache_dv_t, wkk, wv):
    b, _, t, _ = cache_ckv.shape

    def cache_t(w):
        return pl.BlockSpec((None, None, w, t), lambda l, i: (i, l, 0, 0))

    def out(w):
        return pl.BlockSpec((None, None, t, w), lambda l, i: (l, i, 0, 0))

    def out_t(w):
        return pl.BlockSpec((None, None, w, t), lambda l, i: (l, i, 0, 0))

    return pl.pallas_call(
        _ctxkv_kernel,
        out_shape=(jax.ShapeDtypeStruct((DEPTH, b, 512, t), BF16),
                   jax.ShapeDtypeStruct((DEPTH, b, t, 256), BF16),
                   jax.ShapeDtypeStruct((DEPTH, b, 256, t), BF16),
                   jax.ShapeDtypeStruct((DEPTH, b, t, 256), BF16)),
        grid=(DEPTH, b),
        in_specs=[pl.BlockSpec((None, None, t, MLA_KV_RANK), lambda l, i: (i, l, 0, 0)),
                  cache_t(MLA_ROPE), cache_t(256), cache_t(256),
                  pl.BlockSpec((None, 256, 512), lambda l, i: (l, 0, 0)),
                  pl.BlockSpec((None, MLA_KV_RANK, 256), lambda l, i: (l, 0, 0))],
        out_specs=(out_t(512), out(256), out_t(256), out(256)),
        compiler_params=_cparams(("arbitrary", "arbitrary")),
        name="ctx_kv",
    )(cache_ckv, cache_kr_t, cache_dk_t, cache_dv_t, wkk, wv)


def _pre_kernel(latent, kinds, *refs):
    for n in range(PRE_SUB):
        views = []
        for kind, r in zip(kinds, refs):
            if kind == "alias":
                continue
            if kind == "rows":
                r = r.at[n * TM_PRE:(n + 1) * TM_PRE]
            elif kind == "lanes":
                r = r.at[:, n * TM_PRE:(n + 1) * TM_PRE]
            elif kind == "batch":
                r = r.at[n]
            views.append(r)
        _pre_tile(latent, *views)


def _pre_tile(latent, *refs):
    (x_ref, sh_ref, sc_ref, g1_ref, win_ref, gcq_ref, wuq_ref, gckv_ref, wkk_ref, wv_ref,
     bg_ref) = refs[:11]
    refs = refs[11:]
    if latent:
        (cq_t, sqa_t, sqb_t, cd_t, sda_t, sdb_t, ck_t, ska_t, skb_t) = refs[:9]
        refs = refs[9:]
    nat_ref, feat_ref, f32_ref = refs[:3]
    refs = refs[3:]
    q_ref, v_ref, dq_ref, dv_ref, mk_ref = (nat_ref.at[:, lo:lo + w] for lo, w in NAT_COLS)
    k_ref, mq_ref, mv_ref, dk_ref = (feat_ref.at[lo:lo + w] for lo, w in FEAT_ROWS)
    mo_ref, gt_ref = (f32_ref.at[lo:lo + w] for lo, w in F32_ROWS)
    if not latent:
        ckv_out, kr_out, dk_out, dv_out = refs

    x = x_ref[...]
    h = _rms_rows(x, g1_ref[...], D_MODEL) * (1.0 + sc_ref[...]) + sh_ref[...]
    proj = _dot(h.astype(BF16), win_ref[...])

    cq = _rms_rows(proj[:, C_CQ:C_CQ + 256], gcq_ref[...], MLA_Q_RANK)
    q = _dot(cq.astype(BF16), wuq_ref[...])
    if latent:
        q = _rope(q, cq_t[...], sqa_t[...], sqb_t[...])
    q_ref[...] = (q * ((MLA_NOPE + MLA_ROPE) ** -0.5 * LOG2E)).astype(BF16)

    c_kv = _rms_rows(proj[:, C_CKV:C_CKV + MLA_KV_RANK], gckv_ref[...], MLA_KV_RANK)
    aux = proj[:, C_AUX:C_AUX + LANES] + bg_ref[...]
    if latent:
        aux = _rope(aux, ck_t[...], ska_t[...], skb_t[...])
    aux_t = aux.T
    gt_ref[...] = aux_t[AUX_GATE:AUX_GATE + 32, :]
    if not latent:
        ckv_out[...] = c_kv
        kr_out[...] = aux_t[:MLA_ROPE, :]
    ckv_b = c_kv.astype(BF16)
    kin = jnp.concatenate([ckv_b, aux.astype(BF16)], axis=1)
    k_ref[...] = _dot(kin, wkk_ref[...]).T.astype(BF16)
    v_ref[...] = _dot(ckv_b, wv_ref[...]).astype(BF16)

    dq = proj[:, C_DQ:C_DQ + 256]
    dk = proj[:, C_DK:C_DK + 256]
    dv = proj[:, C_DV:C_DV + 256]
    if not latent:
        dk_t = dk.T
        dk_out[...] = dk_t
        dv_out[...] = dv.T
    else:
        dq = _rope(dq, cd_t[...], sda_t[...], sdb_t[...])
        dk_t = _rope(dk, cd_t[...], sda_t[...], sdb_t[...]).T
    dq_ref[...] = (dq * (DF_DIM ** -0.5 * LOG2E)).astype(BF16)
    dk_ref[...] = dk_t.astype(BF16)
    dv_ref[...] = dv.astype(BF16)

    lane = lax.broadcasted_iota(jnp.int32, (x.shape[0], LANES), 1)
    for h in range(ML_HEADS):
        blk = proj[:, C_MQ + (h // 2) * LANES:C_MQ + (h // 2 + 1) * LANES]
        mq_ref[h * LANES:(h + 1) * LANES, :] = (
            jnp.where((lane >> 6) == h % 2, blk, 0.0).T.astype(BF16))
    mk_ref[...] = (proj[:, C_MK:C_MK + 256] * (ML_DK ** -0.5)).astype(BF16)
    mv_ref[...] = proj[:, C_MV:C_MV + 512].T.astype(BF16)
    mo_ref[...] = proj[:, C_MO:C_MO + 512].T


def _pre_call(latent, l, x2d, tok_off, b, s, mods, wts, tables, state_bufs=None):
    ns = s // TM_PRE
    tile_off = tok_off // TM_PRE
    tm2 = PRE_SUB * TM_PRE
    seq_split = ns > 1
    assert (ns % PRE_SUB == 0) if seq_split else (b % PRE_SUB == 0 and ns == 1)
    assert tile_off % PRE_SUB == 0
    grid = (ns // PRE_SUB, b) if seq_split else (1, b // PRE_SUB)

    def tok(width):
        if seq_split:
            return pl.BlockSpec((None, tm2, width), lambda j, i: (i, j, 0)), "rows"
        return pl.BlockSpec((PRE_SUB, TM_PRE, width), lambda j, i: (i, j, 0)), "batch"

    def feat(width):
        if seq_split:
            return pl.BlockSpec((None, width, tm2), lambda j, i: (i, 0, j)), "lanes"
        return pl.BlockSpec((PRE_SUB, width, TM_PRE), lambda j, i: (i, 0, j)), "batch"

    def mod(chunk):
        if latent:
            return pl.BlockSpec((None, 1, D_MODEL), lambda j, i: ((1 + i) * 6 + chunk, 0, 0))
        return pl.BlockSpec((None, 1, D_MODEL), lambda j, i: (chunk, 0, 0))

    def lw(*shape):
        nd = len(shape)
        return pl.BlockSpec((None,) + shape, lambda j, i: (l,) + (0,) * nd)

    if seq_split:
        x_spec = pl.BlockSpec((tm2, D_MODEL), lambda j, i: ((tile_off + i * ns) // PRE_SUB + j, 0))
    else:
        x_spec = pl.BlockSpec((tm2, D_MODEL), lambda j, i: (tile_off // PRE_SUB + i, 0))
    in_specs = [x_spec, mod(0), mod(1), lw(1, D_MODEL), lw(D_MODEL, NP_IN), lw(1, 256),
                lw(256, 512), lw(1, MLA_KV_RANK), lw(256, 512), lw(MLA_KV_RANK, 256), lw(1, LANES)]
    kinds = ["rows"] + [None] * 10
    args = [x2d, mods, mods, wts["g_norm1"], wts["w_in"], wts["g_cq"], wts["w_uq"], wts["g_ckv"],
            wts["w_kk"], wts["w_v"], wts["b_gate"]]
    if latent:
        for t in tables:
            in_specs.append(pl.BlockSpec((tm2, t.shape[1]), lambda j, i: (j, 0)))
            kinds.append("rows")
            args.append(t)

    widths = [(NAT_W, BF16, False), (FEAT_W, BF16, True), (F32_W, F32, True)]
    out_shape = [jax.ShapeDtypeStruct((b, w, s) if tr else (b, s, w), dt) for w, dt, tr in widths]
    out_pairs = [feat(w) if tr else tok(w) for w, _, tr in widths]
    aliases = {}
    if not latent:
        state = [(MLA_KV_RANK, False), (MLA_ROPE, True), (256, True), (256, True)]
        for k, (w, tr) in enumerate(state):
            if tr:
                spec = pl.BlockSpec((PRE_SUB, None, w, TM_PRE), lambda j, i: (i, l, 0, j))
                shape = (b, DEPTH, w, s)
            else:
                spec = pl.BlockSpec((PRE_SUB, None, TM_PRE, w), lambda j, i: (i, l, j, 0))
                shape = (b, DEPTH, s, w)
            out_shape.append(jax.ShapeDtypeStruct(shape, F32))
            out_pairs.append((spec, "batch"))
            if state_bufs is not None:
                aliases[len(args)] = len(widths) + k
                in_specs.append(pl.BlockSpec(memory_space=pl.ANY))
                kinds.append("alias")
                args.append(state_bufs[k])
    out_specs = tuple(p[0] for p in out_pairs)
    kinds += [p[1] for p in out_pairs]
    return pl.pallas_call(
        functools.partial(_pre_kernel, latent, tuple(kinds)),
        out_shape=tuple(out_shape), grid=grid, in_specs=in_specs, out_specs=out_specs,
        input_output_aliases=aliases,
        compiler_params=_cparams(("arbitrary", "arbitrary")),
        name="pre_lat" if latent else "pre_ctx",
    )(*args)


def _softmax_parts(s_list):
    m = functools.reduce(jnp.maximum, [jnp.max(s, axis=1, keepdims=True) for s in s_list])
    p_list = [jnp.exp2(s - m) for s in s_list]
    l = functools.reduce(jnp.add, [jnp.sum(p, axis=1, keepdims=True) for p in p_list])
    return p_list, l


def _attn_kernel(latent, lam_init, nb, *refs):
    n_seg = 5 if latent else 3
    o_ref = refs[-1]
    mla_in, diff_in, shared = refs[:n_seg], refs[n_seg:2 * n_seg], refs[2 * n_seg:-1]
    for n in range(nb):
        _mla_body(latent, *[r.at[n] for r in mla_in], o_ref.at[n])
        _diff_body(latent, lam_init, *[r.at[n] for r in diff_in], *shared, o_ref.at[n])


def _mla_body(latent, *refs):
    if latent:
        q_ref, k_ref, v_ref, kc_ref, vc_ref, o_ref = refs
        segs = [(kc_ref, vc_ref), (k_ref, v_ref)]
    else:
        q_ref, k_ref, v_ref, o_ref = refs
        segs = [(k_ref, v_ref)]
    lane = lax.broadcasted_iota(jnp.int32, (q_ref.shape[0], LANES), 1)

    def scores(h):
        hs = slice(h * LANES, (h + 1) * LANES)
        return [_dot(q_ref[:, hs], kr[hs, :]) for kr, _ in segs]

    outs = []
    s_next = scores(0)
    for h in range(MLA_HEADS):
        ps = slice((h // 2) * LANES, (h // 2 + 1) * LANES)
        s_list = s_next
        if h + 1 < MLA_HEADS:
            s_next = scores(h + 1)
        p_list, l = _softmax_parts(s_list)
        pv = functools.reduce(jnp.add, [_dot(p.astype(BF16), vr[:, ps])
                                        for p, (_, vr) in zip(p_list, segs)])
        outs.append(pv / l)
    o_ref[:, 0:LANES] = jnp.where(lane < MLA_V, outs[0], outs[1]).astype(BF16)
    o_ref[:, LANES:2 * LANES] = jnp.where(lane < MLA_V, outs[2], outs[3]).astype(BF16)


def _diff_body(latent, lam_init, *refs):
    if latent:
        (q_ref, k_ref, v_ref, kc_ref, vc_ref, lq1, lk1, lq2, lk2, g_ref, o_ref) = refs
        segs = [(kc_ref, vc_ref), (k_ref, v_ref)]
    else:
        (q_ref, k_ref, v_ref, lq1, lk1, lq2, lk2, g_ref, o_ref) = refs
        segs = [(k_ref, v_ref)]
    lam = (jnp.exp(jnp.sum(lq1[...] * lk1[...], axis=1, keepdims=True))
           - jnp.exp(jnp.sum(lq2[...] * lk2[...], axis=1, keepdims=True)) + lam_init)
    lane = lax.broadcasted_iota(jnp.int32, (q_ref.shape[0], LANES), 1)
    grp = lane >> 5
    qf = q_ref[...].astype(F32)

    def scores(u):
        h, c = u // 2, u % 2
        ps = slice((h // 2) * LANES, (h // 2 + 1) * LANES)
        qm = jnp.where(grp == 2 * (h % 2) + c, qf[:, ps], 0.0).astype(BF16)
        return [_dot(qm, kr[ps, :]) for kr, _ in segs]

    outs = []
    s_next = scores(0)
    for h in range(DF_HEADS):
        ps = slice((h // 2) * LANES, (h // 2 + 1) * LANES)
        hh = h % 2
        parts = []
        for c in range(2):
            s_list = s_next
            if 2 * h + c + 1 < 2 * DF_HEADS:
                s_next = scores(2 * h + c + 1)
            parts.append(_softmax_parts(s_list))
        (p0, l0), (p1, l1) = parts
        ratio = lam * l0 / l1
        pv = functools.reduce(jnp.add, [
            _dot((a0 - a1 * ratio).astype(BF16), vr[:, ps])
            for a0, a1, (_, vr) in zip(p0, p1, segs)]) * (1.0 / l0)
        valid = (lane >> 6) == hh
        ms = jnp.sum(jnp.where(valid, pv * pv, 0.0), axis=1, keepdims=True) * (1.0 / (2 * DF_DIM))
        outs.append(pv * lax.rsqrt(ms + EPS) * g_ref[:, ps] * (1.0 - lam_init))
    o_ref[:, 2 * LANES:3 * LANES] = jnp.where(lane < 2 * DF_DIM, outs[0], outs[1]).astype(BF16)
    o_ref[:, 3 * LANES:4 * LANES] = jnp.where(lane < 2 * DF_DIM, outs[2], outs[3]).astype(BF16)


def _attn_call(latent, l, lam_init, nat, feat, wts, ctx=None, dctx=None):
    b, s, _ = nat.shape
    nb = 1 if latent else NB_CTX
    tq = min(TQ, s)
    grid = (b // nb, s // tq)

    def group(q_col, k_row, v_col, ctx_pair):
        (q_lo, wq), (k_lo, wk), (v_lo, wv) = q_col, k_row, v_col
        specs = [pl.BlockSpec((nb, tq, wq), lambda i, j: (i, j, q_lo // wq)),
                 pl.BlockSpec((nb, wk, s), lambda i, j: (i, k_lo // wk, 0)),
                 pl.BlockSpec((nb, s, wv), lambda i, j: (i, 0, v_lo // wv))]
        if latent:
            t = ctx_pair[1].shape[2]
            specs += [pl.BlockSpec((None, nb, wk, t), lambda i, j: (l, i, 0, 0)),
                      pl.BlockSpec((None, nb, t, wv), lambda i, j: (l, i, 0, 0))]
        return specs

    in_specs = (group(NAT_COLS[0], FEAT_ROWS[0], NAT_COLS[1], ctx)
                + group(NAT_COLS[2], FEAT_ROWS[3], NAT_COLS[3], dctx))
    qkv = [nat, feat, nat]
    args = qkv + (list(ctx) if latent else []) + qkv + (list(dctx) if latent else [])
    for name in ("lam_q1", "lam_k1", "lam_q2", "lam_k2"):
        in_specs.append(pl.BlockSpec((None, 1, DF_DIM), lambda i, j: (l, 0, 0)))
        args.append(wts[name])
    in_specs.append(pl.BlockSpec((None, 1, 256), lambda i, j: (l, 0, 0)))
    args.append(wts["g_subln"])
    return pl.pallas_call(
        functools.partial(_attn_kernel, latent, lam_init, nb),
        out_shape=jax.ShapeDtypeStruct((b, s, 512), BF16),
        grid=grid, in_specs=in_specs,
        out_specs=pl.BlockSpec((nb, tq, 512), lambda i, j: (i, j, 0)),
        compiler_params=_cparams(("arbitrary", "arbitrary")),
        name="attn_lat" if latent else "attn_ctx",
    )(*args)


def _log_sigmoid(x):
    return jnp.minimum(x, 0.0) - jnp.log1p(jnp.exp(-jnp.abs(x)))


def _mlstm_chunk(d, c, mq_ref, mk_ref, mv_ref, gt_ref, s_ref, m_ref, h_ref):
    L = ML_CHUNK
    rows = pl.ds(pl.multiple_of(c * L, L), L)
    s_i = lax.broadcasted_iota(jnp.int32, (L, L), 0)
    t_i = lax.broadcasted_iota(jnp.int32, (L, L), 1)
    mask = (s_i <= t_i) if d == 0 else (s_i >= t_i)
    tri = jnp.where(mask, 1.0, 0.0).astype(BF16)

    ig = gt_ref[16 * d:16 * d + 8, rows]
    lf = _log_sigmoid(gt_ref[16 * d + 8:16 * d + 16, rows])
    hi = lf.astype(BF16).astype(F32)
    r1 = lf - hi
    mid = r1.astype(BF16).astype(F32)
    parts = _dot(jnp.concatenate([hi, mid, r1 - mid], axis=0).astype(BF16), tri)
    bc = parts[0:8] + parts[8:16] + parts[16:24]
    rvec = ig - bc
    total = jnp.sum(lf, axis=1, keepdims=True)
    mm = m_ref[d]
    gvec = total + rvec
    m_new = jnp.maximum(total + mm, jnp.max(gvec, axis=1, keepdims=True))
    ws = jnp.exp(gvec - m_new).astype(BF16)
    cdec = jnp.exp(total + mm - m_new)
    m_ref[d] = m_new

    rv_t = jnp.concatenate([rvec, jnp.zeros((LANES - 8, L), F32)], axis=0).T

    s_old = [s_ref[d, pair].astype(BF16) for pair in range(2)]
    upd = []
    for h in range(ML_HEADS):
        pair = h // 2
        hs = slice(h * ML_DV, (h + 1) * ML_DV)
        qt = mq_ref[hs, rows]
        kp = mk_ref[rows, pair * LANES:(pair + 1) * LANES]
        vt = mv_ref[hs, rows]
        mmh = mm[h:h + 1, 0:1]
        rm = jnp.where(mask, rv_t[:, h:h + 1], -jnp.inf)
        a = jnp.maximum(jnp.max(rm, axis=0, keepdims=True), mmh)
        wqk = jnp.exp(rm - a) * _dot(kp, qt)
        dec = jnp.exp(mmh - a)
        qc = _dot(s_old[pair], qt)
        num = _dot(vt, wqk.astype(BF16)) + dec * qc[:ML_DV]
        den = jnp.sum(wqk, axis=0, keepdims=True) + dec * qc[ML_DV:ML_DV + 1]
        inv = 1.0 / jnp.maximum(jnp.abs(den), jnp.exp(-(a + bc[h:h + 1, :])))
        h_ref[d, hs, rows] = num * inv
        wsr = ws[h:h + 1, :]
        vaug = jnp.concatenate([vt * wsr, jnp.broadcast_to(wsr, (ML_NROWS, L))], axis=0)
        upd.append(_dot(vaug, kp))

    low = lax.broadcasted_iota(jnp.int32, (ML_DV + ML_NROWS, LANES), 1) < ML_DK
    for pair in range(2):
        h0, h1 = 2 * pair, 2 * pair + 1
        cd = jnp.where(low, cdec[h0:h0 + 1, 0:1], cdec[h1:h1 + 1, 0:1])
        s_ref[d, pair] = cd * s_ref[d, pair] + jnp.where(low, upd[h0], upd[h1])


def _mlstm_kernel(latent, seq, nb, has_alias, *refs):
    if latent:
        (mq_ref, mk_ref, mv_ref, mo_ref, gt_ref, g_ref, s0_ref, m0_ref,
         o_ref, s_ref, m_ref, h_ref) = refs
        s_ref[...] = s0_ref[...]
        m_ref[...] = m0_ref[...]
    else:
        if has_alias:
            refs = refs[:6] + refs[7:]
        (mq_ref, mk_ref, mv_ref, mo_ref, gt_ref, g_ref,
         o_ref, cf_ref, nf_ref, mf_ref, s_ref, m_ref, h_ref) = refs
        s_ref[...] = jnp.zeros(s_ref.shape, F32)
        m_ref[...] = jnp.zeros(m_ref.shape, F32)
    nc = seq // ML_CHUNK

    def body(j, carry):
        for n in range(nb):
            views = (mq_ref.at[n], mk_ref.at[n], mv_ref.at[n], gt_ref.at[n], s_ref.at[n],
                     m_ref.at[n], h_ref.at[n])
            _mlstm_chunk(0, j, *views)
            _mlstm_chunk(1, nc - 1 - j, *views)
        return carry

    lax.fori_loop(0, nc, body, 0, unroll=min(nc, 4))

    for n in range(nb):
        for h in range(ML_HEADS):
            hs = slice(h * ML_DV, (h + 1) * ML_DV)
            for j in range(nc):
                ts = slice(j * LANES, (j + 1) * LANES)
                hsum = h_ref[n, 0, hs, ts] + h_ref[n, 1, hs, ts]
                ms = jnp.sum(hsum * hsum, axis=0, keepdims=True) * (1.0 / ML_DV)
                y = hsum * lax.rsqrt(ms + EPS) * g_ref[hs, :]
                o_ref[hs, n * seq + j * LANES:n * seq + (j + 1) * LANES] = (
                    jax.nn.sigmoid(mo_ref[n, hs, ts]) * y).astype(BF16)
    if not latent:
        for n in range(nb):
            for d in range(2):
                for pair in range(2):
                    cf_ref[n, d, pair] = s_ref[n, d, pair, :ML_DV, :].T
        nf_ref[...] = s_ref[:, :, :, ML_DV:ML_DV + 8, :]
        mf_ref[...] = m_ref[...]


def _mlstm_call(latent, l, nat, feat, f32a, wts, s0=None, m0=None, c_buf=None):
    assert ML_CHUNK == LANES
    b, s, _ = nat.shape
    nb = NB_ML_LAT if latent else NB_ML_CTX

    def rows(part):
        lo, w = part
        return pl.BlockSpec((nb, w, s), lambda i: (i, lo // w, 0))

    mk_lo, mk_w = NAT_COLS[4]
    in_specs = [rows(FEAT_ROWS[1]), pl.BlockSpec((nb, s, mk_w), lambda i: (i, 0, mk_lo // mk_w)),
                rows(FEAT_ROWS[2]), rows(F32_ROWS[0]), rows(F32_ROWS[1]),
                pl.BlockSpec((None, 512, LANES), lambda i: (l, 0, 0))]
    args = [feat, nat, feat, f32a, f32a, wts["g_mnorm"]]
    s_spec_shape = (2, 2, ML_DV + ML_NROWS, LANES)
    m_spec_shape = (2, 8, LANES)
    scratch = [pltpu.VMEM((nb,) + s_spec_shape, F32), pltpu.VMEM((nb,) + m_spec_shape, F32),
               pltpu.VMEM((nb, 2, ML_HEADS * ML_DV, s), F32)]
    o_shape = jax.ShapeDtypeStruct((ML_HEADS * ML_DV, b * s), BF16)
    o_spec = pl.BlockSpec((ML_HEADS * ML_DV, nb * s), lambda i: (0, i))
    if latent:
        in_specs += [pl.BlockSpec((nb, None) + s_spec_shape, lambda i: (i, l, 0, 0, 0, 0)),
                     pl.BlockSpec((nb, None) + m_spec_shape, lambda i: (i, l, 0, 0, 0))]
        args += [s0, m0]
        out_shape = o_shape
        out_specs = o_spec
    aliases = {}
    if not latent:
        half = (2, 2, LANES, LANES)
        out_shape = (o_shape,
                     jax.ShapeDtypeStruct((b, DEPTH) + half, F32),
                     jax.ShapeDtypeStruct((b, 2, 2, 8, LANES), F32),
                     jax.ShapeDtypeStruct((b,) + m_spec_shape, F32))
        out_specs = (o_spec,
                     pl.BlockSpec((nb, None) + half, lambda i: (i, l, 0, 0, 0, 0)),
                     pl.BlockSpec((nb, 2, 2, 8, LANES), lambda i: (i, 0, 0, 0, 0)),
                     pl.BlockSpec((nb,) + m_spec_shape, lambda i: (i, 0, 0, 0)))
        if c_buf is not None:
            aliases[len(args)] = 1
            in_specs.append(pl.BlockSpec(memory_space=pl.ANY))
            args.append(c_buf)
    return pl.pallas_call(
        functools.partial(_mlstm_kernel, latent, s, nb, c_buf is not None),
        out_shape=out_shape, grid=(b // nb,), in_specs=in_specs, out_specs=out_specs,
        input_output_aliases=aliases,
        scratch_shapes=scratch,
        compiler_params=_cparams(("arbitrary",)),
        name="mlstm_lat" if latent else "mlstm_ctx",
    )(*args)


def _post_kernel(nt_ctx, seg_ctx, split_x, final, *refs):
    n_x = 2 if split_x else 1
    x_refs, refs = refs[:n_x], refs[n_x:]
    (oab_c, oab_l, oc_c, oc_l, wo_ref, wu_ref, wd_ref, gt1_ref, sh2_ref, sc2_ref, gt2_ref, g2_ref,
     cw_ref, cb_ref, gf_ref) = refs[:15]
    refs = refs[15:]
    if final:
        o_refs, refs = refs[:2], refs[2:]
    else:
        (cond_ref, wada_ref, bada_ref), o_refs, mods_next_ref, refs = (
            refs[:3], refs[3:4], refs[4], refs[5:])
    wo_s, wu_s, wd_s, x1_ref, h2_ref, act_ref = refs
    i = pl.program_id(0)

    @pl.when(i < N_CAST)
    def _():
        wu_s[:, pl.ds(pl.multiple_of(i * CAST_UP, CAST_UP), CAST_UP)] = wu_ref[...].astype(BF16)
        wd_s[pl.ds(pl.multiple_of(i * CAST_DOWN, CAST_DOWN), CAST_DOWN), :] = wd_ref[...].astype(BF16)

    @pl.when(i < N_CAST_OUT)
    def _():
        wo_s[pl.ds(pl.multiple_of(i * CAST_OUT, CAST_OUT), CAST_OUT), :] = wo_ref[...].astype(BF16)

    @pl.when(i >= N_CAST)
    def _():
        t = i - N_CAST
        is_ctx = t < nt_ctx
        if not final:
            mods_next_ref[...] = _ada_block(cond_ref, wada_ref, bada_ref)

        th = TM_FFN // POST_SUB
        seg = jnp.where(is_ctx, seg_ctx, GRID_W)
        row = lax.broadcasted_iota(jnp.int32, (th, FC), 0)
        first_w = (row & (GRID_W - 1)) == 0
        last_w = (row & (GRID_W - 1)) == GRID_W - 1
        seg_first = (row & (seg - 1)) == 0
        seg_last = (row & (seg - 1)) == seg - 1

        def conv(u, cs):
            prev = pltpu.roll(u, 1, 0)
            prev = jnp.where(first_w, jnp.where(seg_first, 0.0, prev), prev)
            nxt = pltpu.roll(u, th - 1, 0)
            nxt = jnp.where(last_w, jnp.where(seg_last, 0.0, nxt), nxt)
            return (cb_ref[:, cs] + prev * cw_ref[0:1, cs] + u * cw_ref[1:2, cs]
                    + nxt * cw_ref[2:3, cs])

        outs = []
        for hf in range(POST_SUB):
            rs = slice(hf * th, (hf + 1) * th)

            def pick(a, b):
                return jnp.where(is_ctx, a, b)

            x = pick(x_refs[0][rs, :], x_refs[1][rs, :]) if split_x else x_refs[0][rs, :]
            mix = (_dot(pick(oab_c[rs, :], oab_l[rs, :]), wo_s[:2 * 256, :])
                   + lax.dot_general(pick(oc_c[:, rs], oc_l[:, rs]), wo_s[2 * 256:, :],
                                     (((0,), (0,)), ((), ())), preferred_element_type=F32))
            x1 = x + gt1_ref[...] * mix
            x1_ref[rs, :] = x1
            h2 = _rms_rows(x1, g2_ref[...], D_MODEL) * (1.0 + sc2_ref[...]) + sh2_ref[...]
            h2_ref[rs, :] = h2.astype(BF16)

            for j in range(N_FC):
                vs = slice(j * FC, (j + 1) * FC)
                gs = slice(D_FF + j * FC, D_FF + (j + 1) * FC)
                val = conv(_dot(h2_ref[rs, :], wu_s[:, vs]), vs)
                gate = conv(_dot(h2_ref[rs, :], wu_s[:, gs]), gs)
                act_ref[rs, vs] = (gate * jax.nn.sigmoid(gate) * val).astype(BF16)

            x2 = x1_ref[rs, :] + gt2_ref[...] * _dot(act_ref[rs, :], wd_s[...])
            if not final:
                o_refs[0][rs, :] = x2
            else:
                outs.append(_rms_rows(x2, gf_ref[...], D_MODEL))

        if final:
            y = jnp.concatenate(outs, axis=0)

            @pl.when(is_ctx)
            def _():
                o_refs[0][...] = y

            @pl.when(jnp.logical_not(is_ctx))
            def _():
                o_refs[1][...] = y


def _post_call(l, final, xs, n_ctx, s_ctx, s_lat, oab, oc, mods, wts, ada):
    split_x = len(xs) == 2
    n_lat = oab[1].shape[0]
    nt_ctx, nt_lat = n_ctx // TM_FFN, n_lat // TM_FFN
    nt = nt_ctx + nt_lat
    tiles_per_batch = s_lat // TM_FFN
    assert s_ctx <= TM_FFN and TM_FFN % s_ctx == 0 and s_lat % TM_FFN == 0

    def tile(i):
        return jnp.maximum(i - N_CAST, 0)

    def ctx_t(i):
        return jnp.minimum(tile(i), nt_ctx - 1)

    def lat_t(i):
        return jnp.maximum(tile(i) - nt_ctx, 0)

    def mod(chunk):
        def index(i):
            row = jnp.where(tile(i) < nt_ctx, 0, 1 + lat_t(i) // tiles_per_batch)
            return (row * 6 + chunk, 0, 0)
        return pl.BlockSpec((None, 1, D_MODEL), index)

    def resident(*shape):
        nd = len(shape)
        return pl.BlockSpec((None,) + shape, lambda i: (l,) + (0,) * nd,
                            pipeline_mode=pl.Buffered(1))

    if split_x:
        x_specs = [pl.BlockSpec((TM_FFN, D_MODEL), lambda i: (ctx_t(i), 0)),
                   pl.BlockSpec((TM_FFN, D_MODEL), lambda i: (lat_t(i), 0))]
    else:
        x_specs = [pl.BlockSpec((TM_FFN, D_MODEL), lambda i: (tile(i), 0))]
    in_specs = x_specs + [
        pl.BlockSpec((TM_FFN, 512), lambda i: (ctx_t(i), 0)),
        pl.BlockSpec((TM_FFN, 512), lambda i: (lat_t(i), 0)),
        pl.BlockSpec((ML_HEADS * ML_DV, TM_FFN), lambda i: (0, ctx_t(i))),
        pl.BlockSpec((ML_HEADS * ML_DV, TM_FFN), lambda i: (0, lat_t(i))),
        pl.BlockSpec((None, CAST_OUT, D_MODEL), lambda i: (l, jnp.minimum(i, N_CAST_OUT - 1), 0)),
        pl.BlockSpec((None, D_MODEL, CAST_UP), lambda i: (l, 0, jnp.minimum(i, N_CAST - 1))),
        pl.BlockSpec((None, CAST_DOWN, D_MODEL), lambda i: (l, jnp.minimum(i, N_CAST - 1), 0)),
        mod(2), mod(3), mod(4), mod(5),
        resident(1, D_MODEL), resident(CONV_W, 2 * D_FF), resident(1, 2 * D_FF),
        pl.BlockSpec((1, D_MODEL), lambda i: (0, 0)),
    ]
    args = list(xs) + [oab[0], oab[1], oc[0], oc[1], wts["w_out"], wts["w_up"], wts["w_down"],
                       mods, mods, mods, mods, wts["g_norm2"], wts["conv_w"], wts["conv_b"],
                       wts["g_final"]]
    if final:
        out_shape = (jax.ShapeDtypeStruct((n_ctx, D_MODEL), F32),
                     jax.ShapeDtypeStruct((n_lat, D_MODEL), F32))
        out_specs = (pl.BlockSpec((TM_FFN, D_MODEL), lambda i: (ctx_t(i), 0)),
                     pl.BlockSpec((TM_FFN, D_MODEL), lambda i: (lat_t(i), 0)))
    else:
        ada_w = N_MOD * D_MODEL // nt
        assert ada_w % LANES == 0
        in_specs += [pl.BlockSpec((8, D_MODEL), lambda i: (0, 0)),
                     pl.BlockSpec((None, D_MODEL, ada_w), lambda i: (l + 1, 0, tile(i))),
                     pl.BlockSpec((None, 1, ada_w), lambda i: (l + 1, 0, tile(i)))]
        args += [ada[0], ada[1], ada[2]]
        out_shape = (jax.ShapeDtypeStruct((n_ctx + n_lat, D_MODEL), F32),
                     jax.ShapeDtypeStruct((8, N_MOD * D_MODEL), F32))
        out_specs = (pl.BlockSpec((TM_FFN, D_MODEL), lambda i: (tile(i), 0)),
                     pl.BlockSpec((8, ada_w), lambda i: (0, tile(i))))
    return pl.pallas_call(
        functools.partial(_post_kernel, nt_ctx, s_ctx, split_x, final),
        out_shape=out_shape, grid=(N_CAST + nt,), in_specs=in_specs, out_specs=out_specs,
        scratch_shapes=[pltpu.VMEM((D_MODEL, D_MODEL), BF16), pltpu.VMEM((D_MODEL, 2 * D_FF), BF16),
                        pltpu.VMEM((D_FF, D_MODEL), BF16),
                        pltpu.VMEM((TM_FFN, D_MODEL), F32), pltpu.VMEM((TM_FFN, D_MODEL), BF16),
                        pltpu.VMEM((TM_FFN, D_FF), BF16)],
        compiler_params=pltpu.CompilerParams(dimension_semantics=("arbitrary",),
                                             vmem_limit_bytes=VMEM_LIMIT_FFN),
        name="post",
    )(*args)


W_IN_BODY = (352, 2656)


def _pack_in_kernel(w_ref, o_ref):
    tc = w_ref.shape[1]

    def put(dst, blk):
        o_ref[:, dst:dst + blk.shape[0]] = blk.T.astype(BF16)

    def rows(lo, n):
        return w_ref[lo:lo + n, :]

    def zeros(n):
        return jnp.zeros((n, tc), F32)

    put(C_CQ, jnp.concatenate([rows(0, MLA_Q_RANK), zeros(C_CKV - C_CQ - MLA_Q_RANK)], axis=0))
    put(C_CKV, rows(MLA_Q_RANK, MLA_KV_RANK))
    aux = [rows(MLA_Q_RANK + MLA_KV_RANK, MLA_ROPE)]
    for g in range(4):
        aux += [rows(W_IN_BODY[1] + ML_HEADS * g, ML_HEADS), zeros(8 - ML_HEADS)]
    aux.append(zeros(LANES - AUX_GATE - 32))
    put(C_AUX, jnp.concatenate(aux, axis=0))
    for lo in range(W_IN_BODY[0], W_IN_BODY[1], 256):
        put(C_DQ + lo - W_IN_BODY[0], rows(lo, 256))


def _pack_in_call(w_in):
    w_in_t = jnp.swapaxes(w_in, 1, 2)
    tc = 256
    return pl.pallas_call(
        _pack_in_kernel,
        out_shape=jax.ShapeDtypeStruct((DEPTH, D_MODEL, NP_IN), BF16),
        grid=(DEPTH, D_MODEL // tc),
        in_specs=[pl.BlockSpec((None, w_in_t.shape[1], tc), lambda l, i: (l, 0, i))],
        out_specs=pl.BlockSpec((None, tc, NP_IN), lambda l, i: (l, i, 0)),
        compiler_params=_cparams(("arbitrary", "arbitrary")),
        name="pack_w_in",
    )(w_in_t)


def _pack_weights(w_in, g_cq, w_uq, g_ckv, w_ukv, b_gate, g_subln, g_mnorm, g_norm1, g_norm2,
                  w_out, w_up, conv_w, conv_b, w_down, g_final, lam_q1, lam_k1, lam_q2, lam_k2):
    def cols(a, lo, n, pad=0):
        blk = a[..., lo:lo + n]
        if pad:
            blk = jnp.pad(blk, [(0, 0)] * (a.ndim - 1) + [(0, pad)])
        return blk

    w_in_p = _pack_in_call(w_in)

    hd = MLA_NOPE + MLA_ROPE
    w_uq_p = jnp.pad(w_uq.reshape(DEPTH, MLA_Q_RANK, MLA_HEADS, hd),
                     [(0, 0), (0, 256 - MLA_Q_RANK), (0, 0), (0, LANES - hd)])
    w_uq_p = w_uq_p.reshape(DEPTH, 256, MLA_HEADS * LANES).astype(BF16)

    w_ukv4 = w_ukv.reshape(DEPTH, MLA_KV_RANK, MLA_HEADS, MLA_NOPE + MLA_V)
    w_k = jnp.pad(w_ukv4[..., :MLA_NOPE], [(0, 0), (0, 0), (0, 0), (0, LANES - MLA_NOPE)])
    w_k = w_k.reshape(DEPTH, MLA_KV_RANK, MLA_HEADS * LANES)
    j = jnp.arange(LANES)[:, None]
    cix = jnp.arange(MLA_HEADS * LANES)[None, :]
    place = ((j < MLA_ROPE) & ((cix % LANES) == MLA_NOPE + j)).astype(F32)
    w_kk = jnp.concatenate([w_k, jnp.broadcast_to(place, (DEPTH, LANES, MLA_HEADS * LANES))],
                           axis=1).astype(BF16)
    w_v = w_ukv4[..., MLA_NOPE:].reshape(DEPTH, MLA_KV_RANK, MLA_HEADS * MLA_V).astype(BF16)

    return dict(
        w_in=w_in_p, w_uq=w_uq_p, w_kk=w_kk, w_v=w_v,
        g_norm1=g_norm1[:, None, :], g_norm2=g_norm2[:, None, :],
        g_cq=jnp.pad(g_cq, [(0, 0), (0, 256 - MLA_Q_RANK)])[:, None, :],
        g_ckv=g_ckv[:, None, :],
        b_gate=jnp.pad(jnp.pad(b_gate.reshape(DEPTH, 4, ML_HEADS), [(0, 0), (0, 0), (0, 4)])
                       .reshape(DEPTH, 32), [(0, 0), (AUX_GATE, LANES - AUX_GATE - 32)])[:, None, :],
        g_subln=jnp.tile(g_subln, (1, DF_HEADS))[:, None, :],
        g_mnorm=jnp.broadcast_to(g_mnorm[:, :, None], (DEPTH, ML_HEADS * ML_DV, LANES)),
        w_out=w_out, w_up=w_up, w_down=w_down,
        conv_w=conv_w, conv_b=conv_b[:, None, :], g_final=g_final[None, :],
        lam_q1=lam_q1[:, None, :], lam_k1=lam_k1[:, None, :],
        lam_q2=lam_q2[:, None, :], lam_k2=lam_k2[:, None, :],
    )


def _rope_tables(n_tok):
    t = np.arange(n_tok)
    row = (t // GRID_W).astype(np.float64)
    col = (t % GRID_W).astype(np.float64)
    nf = MLA_ROPE // 4
    inv = ROPE_BASE ** (-np.arange(nf, dtype=np.float64) / nf)
    ar = row[:, None] * inv[None, :]
    ac = col[:, None] * inv[None, :]
    ang = np.concatenate([ar, ar, ac, ac], axis=-1)
    quarter = (np.arange(MLA_ROPE) // nf) % 2
    cos = jnp.asarray(np.cos(ang), F32)
    sin_up = jnp.asarray(np.where(quarter == 0, -np.sin(ang), 0.0), F32)
    sin_dn = jnp.asarray(np.where(quarter == 1, np.sin(ang), 0.0), F32)
    ones = jnp.ones((n_tok, 1), F32)
    zeros = jnp.zeros((n_tok, 1), F32)

    def head_q(t32, fill):
        blk = jnp.concatenate([jnp.tile(fill, (1, MLA_NOPE)), t32, jnp.tile(fill, (1, 32))], axis=1)
        return jnp.tile(blk, (1, MLA_HEADS))

    def aux_k(t32, fill):
        return jnp.concatenate([t32, jnp.tile(fill, (1, LANES - MLA_ROPE))], axis=1)

    tq = (head_q(cos, ones), head_q(sin_up, zeros), head_q(sin_dn, zeros))
    td = tuple(jnp.tile(a, (1, 256 // DF_DIM)) for a in (cos, sin_up, sin_dn))
    tk = (aux_k(cos, ones), aux_k(sin_up, zeros), aux_k(sin_dn, zeros))
    return tq + td + tk


def kernel(x_prompt, x_sample, cache_mla_ckv, cache_mla_krope, cache_diff_k, cache_diff_v,
           state_mlstm_C, state_mlstm_n, state_mlstm_m, c, c_ctx, w_ada, b_ada, g_norm1, w_in,
           g_cq, w_uq, g_ckv, w_ukv, lam_q1, lam_k1, lam_q2, lam_k2, g_subln, b_gate, g_mnorm,
           w_out, g_norm2, w_up, conv_w, conv_b, w_down, g_final):
    bp, sp, _ = x_prompt.shape
    bl, sl, _ = x_sample.shape
    t_len = cache_mla_ckv.shape[2]

    wts = _pack_weights(w_in, g_cq, w_uq, g_ckv, w_ukv, b_gate, g_subln, g_mnorm, g_norm1, g_norm2,
                        w_out, w_up, conv_w, conv_b, w_down, g_final, lam_q1, lam_k1, lam_q2, lam_k2)
    tables = _rope_tables(sl)

    cond = jnp.concatenate([c_ctx[None, :], c, jnp.zeros((8 - 1 - bl, D_MODEL), F32)], axis=0)
    ada = (cond, w_ada, b_ada[:, None, :])
    mods = _ada_call(*ada).reshape(8 * N_MOD, 1, D_MODEL)

    def feat_major(a):
        return jnp.transpose(a, (0, 1, 3, 4, 2)).reshape(bl, DEPTH, 256, t_len)

    kctx, vctx, cdk, cdv = _ctxkv_call(
        cache_mla_ckv, jnp.swapaxes(cache_mla_krope, 2, 3), feat_major(cache_diff_k),
        feat_major(cache_diff_v), wts["w_kk"], wts["w_v"])
    c0_t = jnp.swapaxes(state_mlstm_C.reshape(bl, DEPTH, 2, 2, LANES, ML_DV), -1, -2)
    n0_r = jnp.broadcast_to(state_mlstm_n.reshape(bl, DEPTH, 2, 2, 1, LANES),
                            (bl, DEPTH, 2, 2, ML_NROWS, LANES))
    s0 = jnp.concatenate([c0_t, n0_r], axis=-2)
    m0 = jnp.broadcast_to(jnp.pad(state_mlstm_m, [(0, 0)] * 3 + [(0, 8 - ML_HEADS)])[..., None],
                          (bl, DEPTH, 2, 8, LANES))

    n_ctx, n_lat = bp * sp, bl * sl
    xs = (x_prompt.reshape(n_ctx, D_MODEL), x_sample.reshape(n_lat, D_MODEL))
    state_bufs = [jnp.zeros((bp, DEPTH, sp, MLA_KV_RANK), F32), jnp.zeros((bp, DEPTH, MLA_ROPE, sp), F32),
                  jnp.zeros((bp, DEPTH, 256, sp), F32), jnp.zeros((bp, DEPTH, 256, sp), F32)]
    c_buf = jnp.zeros((bp, DEPTH, 2, 2, LANES, LANES), F32)
    n_col, m_col = [], []
    for l in range(DEPTH):
        lam_init = 0.8 - 0.6 * math.exp(-0.3 * l)
        final = l == DEPTH - 1
        x_ctx, x_lat, lat_off = (xs[0], xs[1], 0) if len(xs) == 2 else (xs[0], xs[0], n_ctx)
        nat, feat, f32a, *state_bufs = _pre_call(
            False, l, x_ctx, 0, bp, sp, mods, wts, None, state_bufs)
        oab_c = _attn_call(False, l, lam_init, nat, feat, wts)
        oc_c, c_buf, n_fin, m_fin = _mlstm_call(False, l, nat, feat, f32a, wts, c_buf=c_buf)
        n_col.append(n_fin[..., 0, :].reshape(bp, 2, ML_HEADS, ML_DK))
        m_col.append(m_fin[:, :, :ML_HEADS, 0])
        nat, feat, f32a = _pre_call(True, l, x_lat, lat_off, bl, sl, mods, wts, tables)
        oab_l = _attn_call(True, l, lam_init, nat, feat, wts, (kctx, vctx), (cdk, cdv))
        oc_l = _mlstm_call(True, l, nat, feat, f32a, wts, s0, m0)
        out = _post_call(l, final, xs, n_ctx, sp, sl,
                         (oab_c.reshape(n_ctx, 512), oab_l.reshape(n_lat, 512)), (oc_c, oc_l),
                         mods, wts, ada)
        if final:
            xs = out
        else:
            xs, mods = (out[0],), out[1].reshape(8 * N_MOD, 1, D_MODEL)

    xp = xs[0].reshape(bp, sp, D_MODEL)
    xs = xs[1].reshape(bl, sl, D_MODEL)
    ckv_all, kr_all, dk_all, dv_all = state_bufs

    def token_major(a):
        return jnp.transpose(a.reshape(bp, DEPTH, DF_HEADS, 2 * DF_DIM, sp), (0, 1, 4, 2, 3))

    return (xp, xs, ckv_all, jnp.swapaxes(kr_all, 2, 3), token_major(dk_all), token_major(dv_all),
            c_buf.reshape(bp, DEPTH, 2, ML_HEADS, ML_DK, ML_DV),
            jnp.stack(n_col, axis=1), jnp.stack(m_col, axis=1))
```

```python
import functools
import math

import jax
import jax.numpy as jnp
import numpy as np
from jax import lax
from jax.experimental import pallas as pl
from jax.experimental.pallas import tpu as pltpu

F32 = jnp.float32
BF16 = jnp.bfloat16

D_MODEL = 1024
DEPTH = 4
GRID_W = 64
N_MOD = 6
EPS = 1e-6
ROPE_BASE = 10000.0
MLA_HEADS = 4
MLA_Q_RANK = 192
MLA_KV_RANK = 128
MLA_NOPE = 64
MLA_ROPE = 32
MLA_V = 64
DF_HEADS = 4
DF_DIM = 32
ML_HEADS = 4
ML_DK = 64
ML_DV = 128
D_FF = 2816
CONV_W = 3

LANES = 128
VMEM_LIMIT = 48 * 1024 * 1024
VMEM_LIMIT_FFN = 56 * 1024 * 1024

C_CQ, C_CKV, C_AUX, C_DQ, C_DK, C_DV, C_MQ, C_MK, C_MV, C_MO = (
    0, 256, 384, 512, 768, 1024, 1280, 1536, 1792, 2304)
NP_IN = 2816
AUX_GATE = 32

TM_PRE = 256
PRE_SUB_CTX = 4
PRE_SUB_LAT = 2
TQ = 512
NB_CTX = 4
ML_CHUNK = 128
ML_NROWS = 128
NB_ML_CTX = 4
NB_ML_LAT = 1
TM_FFN = 512
POST_SUB = 1
FC = 256
N_FC = D_FF // FC
N_CAST = 22
CAST_UP = 2 * D_FF // N_CAST
CAST_DOWN = D_FF // N_CAST
CAST_OUT = 128
N_CAST_OUT = D_MODEL // CAST_OUT

NT = (((1,), (1,)), ((), ()))
LOG2E = 1.4426950408889634


def _cparams(sem):
    return pltpu.CompilerParams(dimension_semantics=sem, vmem_limit_bytes=VMEM_LIMIT)


def _dot(a, b):
    return jnp.dot(a, b, preferred_element_type=F32)


def _dot_nt(a, b):
    return lax.dot_general(a, b, NT, preferred_element_type=F32)


def _rms_rows(x, g, n):
    ms = jnp.sum(x * x, axis=-1, keepdims=True) * (1.0 / n)
    return x * lax.rsqrt(ms + EPS) * g


def _rope(x, cos, sin_up, sin_dn):
    w = x.shape[-1]
    return x * cos + pltpu.roll(x, w - 8, 1) * sin_up + pltpu.roll(x, 8, 1) * sin_dn


def _ada_block(c_ref, w_ref, b_ref):
    c = c_ref[...]
    s = (c * jax.nn.sigmoid(c)).astype(BF16)
    return _dot(s, w_ref[...].astype(BF16)) + b_ref[...]


def _ada_kernel(c_ref, w_ref, b_ref, o_ref):
    o_ref[...] = _ada_block(c_ref, w_ref, b_ref)


def _ada_call(cond, w_ada, b_ada):
    nt = 1024
    return pl.pallas_call(
        _ada_kernel,
        out_shape=jax.ShapeDtypeStruct((8, N_MOD * D_MODEL), F32),
        grid=(N_MOD * D_MODEL // nt,),
        in_specs=[pl.BlockSpec((8, D_MODEL), lambda j: (0, 0)),
                  pl.BlockSpec((None, D_MODEL, nt), lambda j: (0, 0, j)),
                  pl.BlockSpec((None, 1, nt), lambda j: (0, 0, j))],
        out_specs=pl.BlockSpec((8, nt), lambda j: (0, j)),
        compiler_params=_cparams(("arbitrary",)),
        name="ada_mod",
    )(cond, w_ada, b_ada)


def _ctxkv_kernel(ckv_ref, krt_ref, dkt_ref, dvt_ref, wkk_ref, wv_ref, k_ref, v_ref, dkb_ref, dvb_ref):
    t = ckv_ref.shape[0]
    ckv = ckv_ref[...].astype(BF16)
    kr = jnp.concatenate([krt_ref[...], jnp.zeros((LANES - MLA_ROPE, t), F32)], axis=0).T
    kin = jnp.concatenate([ckv, kr.astype(BF16)], axis=1)
    k_ref[...] = _dot(kin, wkk_ref[...]).T.astype(BF16)
    v_ref[...] = _dot(ckv, wv_ref[...]).astype(BF16)
    dkb_ref[...] = dkt_ref[...].astype(BF16)
    dvb_ref[...] = dvt_ref[...].T.astype(BF16)


def _ctxkv_call(cache_ckv, cache_kr_t, cache_dk_t, cache_dv_t, wkk, wv):
    b, _, t, _ = cache_ckv.shape

    def cache_t(w):
        return pl.BlockSpec((None, None, w, t), lambda l, i: (i, l, 0, 0))

    def out(w):
        return pl.BlockSpec((None, None, t, w), lambda l, i: (l, i, 0, 0))

    def out_t(w):
        return pl.BlockSpec((None, None, w, t), lambda l, i: (l, i, 0, 0))

    return pl.pallas_call(
        _ctxkv_kernel,
        out_shape=(jax.ShapeDtypeStruct((DEPTH, b, 512, t), BF16),
                   jax.ShapeDtypeStruct((DEPTH, b, t, 256), BF16),
                   jax.ShapeDtypeStruct((DEPTH, b, 256, t), BF16),
                   jax.ShapeDtypeStruct((DEPTH, b, t, 256), BF16)),
        grid=(DEPTH, b),
        in_specs=[pl.BlockSpec((None, None, t, MLA_KV_RANK), lambda l, i: (i, l, 0, 0)),
                  cache_t(MLA_ROPE), cache_t(256), cache_t(256),
                  pl.BlockSpec((None, 256, 512), lambda l, i: (l, 0, 0)),
                  pl.BlockSpec((None, MLA_KV_RANK, 256), lambda l, i: (l, 0, 0))],
        out_specs=(out_t(512), out(256), out_t(256), out(256)),
        compiler_params=_cparams(("arbitrary", "arbitrary")),
        name="ctx_kv",
    )(cache_ckv, cache_kr_t, cache_dk_t, cache_dv_t, wkk, wv)


def _pre_kernel(latent, nsub, kinds, *refs):
    for n in range(nsub):
        views = []
        for kind, r in zip(kinds, refs):
            if kind == "alias":
                continue
            if kind == "rows":
                r = r.at[n * TM_PRE:(n + 1) * TM_PRE]
            elif kind == "lanes":
                r = r.at[:, n * TM_PRE:(n + 1) * TM_PRE]
            elif kind == "batch":
                r = r.at[n]
            views.append(r)
        _pre_tile(latent, *views)


def _pre_tile(latent, *refs):
    (x_ref, sh_ref, sc_ref, g1_ref, win_ref, gcq_ref, wuq_ref, gckv_ref, wkk_ref, wv_ref,
     bg_ref) = refs[:11]
    refs = refs[11:]
    if latent:
        (cq_t, sqa_t, sqb_t, cd_t, sda_t, sdb_t, ck_t, ska_t, skb_t) = refs[:9]
        refs = refs[9:]
    (q_ref, k_ref, v_ref, dq_ref, dk_ref, dv_ref, mq_ref, mk_ref, mv_ref, mo_ref,
     gt_ref) = refs[:11]
    refs = refs[11:]
    if not latent:
        ckv_out, kr_out, dk_out, dv_out = refs

    x = x_ref[...]
    h = _rms_rows(x, g1_ref[...], D_MODEL) * (1.0 + sc_ref[...]) + sh_ref[...]
    proj = _dot(h.astype(BF16), win_ref[...])

    cq = _rms_rows(proj[:, C_CQ:C_CQ + 256], gcq_ref[...], MLA_Q_RANK)
    q = _dot(cq.astype(BF16), wuq_ref[...])
    if latent:
        q = _rope(q, cq_t[...], sqa_t[...], sqb_t[...])
    q_ref[...] = (q * ((MLA_NOPE + MLA_ROPE) ** -0.5 * LOG2E)).astype(BF16)

    c_kv = _rms_rows(proj[:, C_CKV:C_CKV + MLA_KV_RANK], gckv_ref[...], MLA_KV_RANK)
    aux = proj[:, C_AUX:C_AUX + LANES] + bg_ref[...]
    if latent:
        aux = _rope(aux, ck_t[...], ska_t[...], skb_t[...])
    aux_t = aux.T
    gt_ref[...] = aux_t[AUX_GATE:AUX_GATE + 32, :]
    if not latent:
        ckv_out[...] = c_kv
        kr_out[...] = aux_t[:MLA_ROPE, :]
    ckv_b = c_kv.astype(BF16)
    kin = jnp.concatenate([ckv_b, aux.astype(BF16)], axis=1)
    k_ref[...] = _dot(kin, wkk_ref[...]).T.astype(BF16)
    v_ref[...] = _dot(ckv_b, wv_ref[...]).astype(BF16)

    dq = proj[:, C_DQ:C_DQ + 256]
    dk = proj[:, C_DK:C_DK + 256]
    dv = proj[:, C_DV:C_DV + 256]
    if not latent:
        dk_t = dk.T
        dk_out[...] = dk_t
        dv_out[...] = dv.T
    else:
        dq = _rope(dq, cd_t[...], sda_t[...], sdb_t[...])
        dk_t = _rope(dk, cd_t[...], sda_t[...], sdb_t[...]).T
    dq_ref[...] = (dq * (DF_DIM ** -0.5 * LOG2E)).astype(BF16)
    dk_ref[...] = dk_t.astype(BF16)
    dv_ref[...] = dv.astype(BF16)

    lane = lax.broadcasted_iota(jnp.int32, (x.shape[0], LANES), 1)
    for h in range(ML_HEADS):
        blk = proj[:, C_MQ + (h // 2) * LANES:C_MQ + (h // 2 + 1) * LANES]
        mq_ref[h * LANES:(h + 1) * LANES, :] = (
            jnp.where((lane >> 6) == h % 2, blk, 0.0).T.astype(BF16))
    mk_ref[...] = (proj[:, C_MK:C_MK + 256] * (ML_DK ** -0.5)).astype(BF16)
    mv_ref[...] = proj[:, C_MV:C_MV + 512].T.astype(BF16)
    mo_ref[...] = proj[:, C_MO:C_MO + 512].T


def _pre_call(latent, l, x2d, tok_off, b, s, mods, wts, tables, state_bufs=None):
    ns = s // TM_PRE
    tile_off = tok_off // TM_PRE
    nsub = PRE_SUB_LAT if latent else PRE_SUB_CTX
    tm2 = nsub * TM_PRE
    seq_split = ns > 1
    assert (ns % nsub == 0) if seq_split else (b % nsub == 0 and ns == 1)
    assert tile_off % nsub == 0
    grid = (ns // nsub, b) if seq_split else (1, b // nsub)

    def tok(width):
        if seq_split:
            return pl.BlockSpec((None, tm2, width), lambda j, i: (i, j, 0)), "rows"
        return pl.BlockSpec((nsub, TM_PRE, width), lambda j, i: (i, j, 0)), "batch"

    def feat(width):
        if seq_split:
            return pl.BlockSpec((None, width, tm2), lambda j, i: (i, 0, j)), "lanes"
        return pl.BlockSpec((nsub, width, TM_PRE), lambda j, i: (i, 0, j)), "batch"

    def mod(chunk):
        if latent:
            return pl.BlockSpec((None, 1, D_MODEL), lambda j, i: ((1 + i) * 6 + chunk, 0, 0))
        return pl.BlockSpec((None, 1, D_MODEL), lambda j, i: (chunk, 0, 0))

    def lw(*shape):
        nd = len(shape)
        return pl.BlockSpec((None,) + shape, lambda j, i: (l,) + (0,) * nd)

    if seq_split:
        x_spec = pl.BlockSpec((tm2, D_MODEL), lambda j, i: ((tile_off + i * ns) // nsub + j, 0))
    else:
        x_spec = pl.BlockSpec((tm2, D_MODEL), lambda j, i: (tile_off // nsub + i, 0))
    in_specs = [x_spec, mod(0), mod(1), lw(1, D_MODEL), lw(D_MODEL, NP_IN), lw(1, 256),
                lw(256, 512), lw(1, MLA_KV_RANK), lw(256, 512), lw(MLA_KV_RANK, 256), lw(1, LANES)]
    kinds = ["rows"] + [None] * 10
    args = [x2d, mods, mods, wts["g_norm1"], wts["w_in"], wts["g_cq"], wts["w_uq"], wts["g_ckv"],
            wts["w_kk"], wts["w_v"], wts["b_gate"]]
    if latent:
        for t in tables:
            in_specs.append(pl.BlockSpec((tm2, t.shape[1]), lambda j, i: (j, 0)))
            kinds.append("rows")
            args.append(t)

    widths = [(512, BF16, False), (512, BF16, True), (256, BF16, False), (256, BF16, False),
              (256, BF16, True), (256, BF16, False), (512, BF16, True), (256, BF16, False),
              (512, BF16, True), (512, F32, True), (32, F32, True)]
    out_shape = [jax.ShapeDtypeStruct((b, w, s) if tr else (b, s, w), dt) for w, dt, tr in widths]
    out_pairs = [feat(w) if tr else tok(w) for w, _, tr in widths]
    aliases = {}
    if not latent:
        state = [(MLA_KV_RANK, False), (MLA_ROPE, True), (256, True), (256, True)]
        for k, (w, tr) in enumerate(state):
            if tr:
                spec = pl.BlockSpec((nsub, None, w, TM_PRE), lambda j, i: (i, l, 0, j))
                shape = (b, DEPTH, w, s)
            else:
                spec = pl.BlockSpec((nsub, None, TM_PRE, w), lambda j, i: (i, l, j, 0))
                shape = (b, DEPTH, s, w)
            out_shape.append(jax.ShapeDtypeStruct(shape, F32))
            out_pairs.append((spec, "batch"))
            if state_bufs is not None:
                aliases[len(args)] = len(widths) + k
                in_specs.append(pl.BlockSpec(memory_space=pl.ANY))
                kinds.append("alias")
                args.append(state_bufs[k])
    out_specs = tuple(p[0] for p in out_pairs)
    kinds += [p[1] for p in out_pairs]
    return pl.pallas_call(
        functools.partial(_pre_kernel, latent, nsub, tuple(kinds)),
        out_shape=tuple(out_shape), grid=grid, in_specs=in_specs, out_specs=out_specs,
        input_output_aliases=aliases,
        compiler_params=_cparams(("arbitrary", "arbitrary")),
        name="pre_lat" if latent else "pre_ctx",
    )(*args)


def _softmax_parts(s_list):
    m = functools.reduce(jnp.maximum, [jnp.max(s, axis=1, keepdims=True) for s in s_list])
    p_list = [jnp.exp2(s - m) for s in s_list]
    l = functools.reduce(jnp.add, [jnp.sum(p, axis=1, keepdims=True) for p in p_list])
    return p_list, l


def _attn_kernel(latent, lam_init, nb, *refs):
    n_seg = 5 if latent else 3
    o_ref = refs[-1]
    mla_in, diff_in, shared = refs[:n_seg], refs[n_seg:2 * n_seg], refs[2 * n_seg:-1]
    for n in range(nb):
        _mla_body(latent, *[r.at[n] for r in mla_in], o_ref.at[n])
        _diff_body(latent, lam_init, *[r.at[n] for r in diff_in], *shared, o_ref.at[n])


def _mla_body(latent, *refs):
    if latent:
        q_ref, k_ref, v_ref, kc_ref, vc_ref, o_ref = refs
        segs = [(kc_ref, vc_ref), (k_ref, v_ref)]
    else:
        q_ref, k_ref, v_ref, o_ref = refs
        segs = [(k_ref, v_ref)]
    lane = lax.broadcasted_iota(jnp.int32, (q_ref.shape[0], LANES), 1)

    def scores(h):
        hs = slice(h * LANES, (h + 1) * LANES)
        return [_dot(q_ref[:, hs], kr[hs, :]) for kr, _ in segs]

    outs = []
    s_next = scores(0)
    for h in range(MLA_HEADS):
        ps = slice((h // 2) * LANES, (h // 2 + 1) * LANES)
        s_list = s_next
        if h + 1 < MLA_HEADS:
            s_next = scores(h + 1)
        p_list, l = _softmax_parts(s_list)
        pv = functools.reduce(jnp.add, [_dot(p.astype(BF16), vr[:, ps])
                                        for p, (_, vr) in zip(p_list, segs)])
        outs.append(pv / l)
    o_ref[:, 0:LANES] = jnp.where(lane < MLA_V, outs[0], outs[1]).astype(BF16)
    o_ref[:, LANES:2 * LANES] = jnp.where(lane < MLA_V, outs[2], outs[3]).astype(BF16)


def _diff_body(latent, lam_init, *refs):
    if latent:
        (q_ref, k_ref, v_ref, kc_ref, vc_ref, lq1, lk1, lq2, lk2, g_ref, o_ref) = refs
        segs = [(kc_ref, vc_ref), (k_ref, v_ref)]
    else:
        (q_ref, k_ref, v_ref, lq1, lk1, lq2, lk2, g_ref, o_ref) = refs
        segs = [(k_ref, v_ref)]
    lam = (jnp.exp(jnp.sum(lq1[...] * lk1[...], axis=1, keepdims=True))
           - jnp.exp(jnp.sum(lq2[...] * lk2[...], axis=1, keepdims=True)) + lam_init)
    lane = lax.broadcasted_iota(jnp.int32, (q_ref.shape[0], LANES), 1)
    grp = lane >> 5
    qf = q_ref[...].astype(F32)

    def scores(u):
        h, c = u // 2, u % 2
        ps = slice((h // 2) * LANES, (h // 2 + 1) * LANES)
        qm = jnp.where(grp == 2 * (h % 2) + c, qf[:, ps], 0.0).astype(BF16)
        return [_dot(qm, kr[ps, :]) for kr, _ in segs]

    outs = []
    s_next = scores(0)
    for h in range(DF_HEADS):
        ps = slice((h // 2) * LANES, (h // 2 + 1) * LANES)
        hh = h % 2
        parts = []
        for c in range(2):
            s_list = s_next
            if 2 * h + c + 1 < 2 * DF_HEADS:
                s_next = scores(2 * h + c + 1)
            parts.append(_softmax_parts(s_list))
        (p0, l0), (p1, l1) = parts
        ratio = lam * l0 / l1
        pv = functools.reduce(jnp.add, [
            _dot((a0 - a1 * ratio).astype(BF16), vr[:, ps])
            for a0, a1, (_, vr) in zip(p0, p1, segs)]) * (1.0 / l0)
        valid = (lane >> 6) == hh
        ms = jnp.sum(jnp.where(valid, pv * pv, 0.0), axis=1, keepdims=True) * (1.0 / (2 * DF_DIM))
        outs.append(pv * lax.rsqrt(ms + EPS) * g_ref[:, ps] * (1.0 - lam_init))
    o_ref[:, 2 * LANES:3 * LANES] = jnp.where(lane < 2 * DF_DIM, outs[0], outs[1]).astype(BF16)
    o_ref[:, 3 * LANES:4 * LANES] = jnp.where(lane < 2 * DF_DIM, outs[2], outs[3]).astype(BF16)


def _attn_call(latent, l, lam_init, qkv, dqkv, wts, ctx=None, dctx=None):
    b, s, _ = qkv[0].shape
    nb = 1 if latent else NB_CTX
    tq = min(TQ, s)
    grid = (b // nb, s // tq)

    def group(wq, wk, wv, ctx_pair):
        specs = [pl.BlockSpec((nb, tq, wq), lambda i, j: (i, j, 0)),
                 pl.BlockSpec((nb, wk, s), lambda i, j: (i, 0, 0)),
                 pl.BlockSpec((nb, s, wv), lambda i, j: (i, 0, 0))]
        if latent:
            t = ctx_pair[1].shape[2]
            specs += [pl.BlockSpec((None, nb, wk, t), lambda i, j: (l, i, 0, 0)),
                      pl.BlockSpec((None, nb, t, wv), lambda i, j: (l, i, 0, 0))]
        return specs

    in_specs = group(512, 512, 256, ctx) + group(256, 256, 256, dctx)
    args = list(qkv) + (list(ctx) if latent else []) + list(dqkv) + (list(dctx) if latent else [])
    for name in ("lam_q1", "lam_k1", "lam_q2", "lam_k2"):
        in_specs.append(pl.BlockSpec((None, 1, DF_DIM), lambda i, j: (l, 0, 0)))
        args.append(wts[name])
    in_specs.append(pl.BlockSpec((None, 1, 256), lambda i, j: (l, 0, 0)))
    args.append(wts["g_subln"])
    return pl.pallas_call(
        functools.partial(_attn_kernel, latent, lam_init, nb),
        out_shape=jax.ShapeDtypeStruct((b, s, 512), BF16),
        grid=grid, in_specs=in_specs,
        out_specs=pl.BlockSpec((nb, tq, 512), lambda i, j: (i, j, 0)),
        compiler_params=_cparams(("arbitrary", "arbitrary")),
        name="attn_lat" if latent else "attn_ctx",
    )(*args)


def _log_sigmoid(x):
    return jnp.minimum(x, 0.0) - jnp.log1p(jnp.exp(-jnp.abs(x)))


def _mlstm_chunk(d, c, mq_ref, mk_ref, mv_ref, gt_ref, s_ref, m_ref, h_ref):
    L = ML_CHUNK
    rows = pl.ds(pl.multiple_of(c * L, L), L)
    s_i = lax.broadcasted_iota(jnp.int32, (L, L), 0)
    t_i = lax.broadcasted_iota(jnp.int32, (L, L), 1)
    mask = (s_i <= t_i) if d == 0 else (s_i >= t_i)
    tri = jnp.where(mask, 1.0, 0.0).astype(BF16)

    ig = gt_ref[16 * d:16 * d + 8, rows]
    lf = _log_sigmoid(gt_ref[16 * d + 8:16 * d + 16, rows])
    hi = lf.astype(BF16).astype(F32)
    r1 = lf - hi
    mid = r1.astype(BF16).astype(F32)
    parts = _dot(jnp.concatenate([hi, mid, r1 - mid], axis=0).astype(BF16), tri)
    bc = parts[0:8] + parts[8:16] + parts[16:24]
    rvec = ig - bc
    total = jnp.sum(lf, axis=1, keepdims=True)
    mm = m_ref[d]
    gvec = total + rvec
    m_new = jnp.maximum(total + mm, jnp.max(gvec, axis=1, keepdims=True))
    ws = jnp.exp(gvec - m_new).astype(BF16)
    cdec = jnp.exp(total + mm - m_new)
    m_ref[d] = m_new

    rv_t = jnp.concatenate([rvec, jnp.zeros((LANES - 8, L), F32)], axis=0).T

    s_old = [s_ref[d, pair].astype(BF16) for pair in range(2)]
    upd = []
    for h in range(ML_HEADS):
        pair = h // 2
        hs = slice(h * ML_DV, (h + 1) * ML_DV)
        qt = mq_ref[hs, rows]
        kp = mk_ref[rows, pair * LANES:(pair + 1) * LANES]
        vt = mv_ref[hs, rows]
        mmh = mm[h:h + 1, 0:1]
        rm = jnp.where(mask, rv_t[:, h:h + 1], -jnp.inf)
        a = jnp.maximum(jnp.max(rm, axis=0, keepdims=True), mmh)
        wqk = jnp.exp(rm - a) * _dot(kp, qt)
        dec = jnp.exp(mmh - a)
        qc = _dot(s_old[pair], qt)
        num = _dot(vt, wqk.astype(BF16)) + dec * qc[:ML_DV]
        den = jnp.sum(wqk, axis=0, keepdims=True) + dec * qc[ML_DV:ML_DV + 1]
        inv = 1.0 / jnp.maximum(jnp.abs(den), jnp.exp(-(a + bc[h:h + 1, :])))
        h_ref[d, hs, rows] = num * inv
        wsr = ws[h:h + 1, :]
        vaug = jnp.concatenate([vt * wsr, jnp.broadcast_to(wsr, (ML_NROWS, L))], axis=0)
        upd.append(_dot(vaug, kp))

    low = lax.broadcasted_iota(jnp.int32, (ML_DV + ML_NROWS, LANES), 1) < ML_DK
    for pair in range(2):
        h0, h1 = 2 * pair, 2 * pair + 1
        cd = jnp.where(low, cdec[h0:h0 + 1, 0:1], cdec[h1:h1 + 1, 0:1])
        s_ref[d, pair] = cd * s_ref[d, pair] + jnp.where(low, upd[h0], upd[h1])


def _mlstm_kernel(latent, seq, nb, has_alias, *refs):
    if latent:
        (mq_ref, mk_ref, mv_ref, mo_ref, gt_ref, g_ref, s0_ref, m0_ref,
         o_ref, s_ref, m_ref, h_ref) = refs
        s_ref[...] = s0_ref[...]
        m_ref[...] = m0_ref[...]
    else:
        if has_alias:
            refs = refs[:6] + refs[7:]
        (mq_ref, mk_ref, mv_ref, mo_ref, gt_ref, g_ref,
         o_ref, cf_ref, nf_ref, mf_ref, s_ref, m_ref, h_ref) = refs
        s_ref[...] = jnp.zeros(s_ref.shape, F32)
        m_ref[...] = jnp.zeros(m_ref.shape, F32)
    nc = seq // ML_CHUNK

    def body(j, carry):
        for n in range(nb):
            views = (mq_ref.at[n], mk_ref.at[n], mv_ref.at[n], gt_ref.at[n], s_ref.at[n],
                     m_ref.at[n], h_ref.at[n])
            _mlstm_chunk(0, j, *views)
            _mlstm_chunk(1, nc - 1 - j, *views)
        return carry

    lax.fori_loop(0, nc, body, 0, unroll=min(nc, 4))

    for n in range(nb):
        for h in range(ML_HEADS):
            hs = slice(h * ML_DV, (h + 1) * ML_DV)
            for j in range(nc):
                ts = slice(j * LANES, (j + 1) * LANES)
                hsum = h_ref[n, 0, hs, ts] + h_ref[n, 1, hs, ts]
                ms = jnp.sum(hsum * hsum, axis=0, keepdims=True) * (1.0 / ML_DV)
                y = hsum * lax.rsqrt(ms + EPS) * g_ref[hs, :]
                o_ref[hs, n * seq + j * LANES:n * seq + (j + 1) * LANES] = (
                    jax.nn.sigmoid(mo_ref[n, hs, ts]) * y).astype(BF16)
    if not latent:
        for n in range(nb):
            for d in range(2):
                for pair in range(2):
                    cf_ref[n, d, pair] = s_ref[n, d, pair, :ML_DV, :].T
        nf_ref[...] = s_ref[:, :, :, ML_DV:ML_DV + 8, :]
        mf_ref[...] = m_ref[...]


def _mlstm_call(latent, l, mqt, mk, mvt, mot, gt, wts, s0=None, m0=None, c_buf=None):
    assert ML_CHUNK == LANES
    b, _, s = mqt.shape
    nb = NB_ML_LAT if latent else NB_ML_CTX

    def feat(w):
        return pl.BlockSpec((nb, w, s), lambda i: (i, 0, 0))

    in_specs = [feat(512), pl.BlockSpec((nb, s, 256), lambda i: (i, 0, 0)), feat(512), feat(512),
                feat(32), pl.BlockSpec((None, 512, LANES), lambda i: (l, 0, 0))]
    args = [mqt, mk, mvt, mot, gt, wts["g_mnorm"]]
    s_spec_shape = (2, 2, ML_DV + ML_NROWS, LANES)
    m_spec_shape = (2, 8, LANES)
    scratch = [pltpu.VMEM((nb,) + s_spec_shape, F32), pltpu.VMEM((nb,) + m_spec_shape, F32),
               pltpu.VMEM((nb, 2, ML_HEADS * ML_DV, s), F32)]
    o_shape = jax.ShapeDtypeStruct((ML_HEADS * ML_DV, b * s), BF16)
    o_spec = pl.BlockSpec((ML_HEADS * ML_DV, nb * s), lambda i: (0, i))
    if latent:
        in_specs += [pl.BlockSpec((nb, None) + s_spec_shape, lambda i: (i, l, 0, 0, 0, 0)),
                     pl.BlockSpec((nb, None) + m_spec_shape, lambda i: (i, l, 0, 0, 0))]
        args += [s0, m0]
        out_shape = o_shape
        out_specs = o_spec
    aliases = {}
    if not latent:
        half = (2, 2, LANES, LANES)
        out_shape = (o_shape,
                     jax.ShapeDtypeStruct((b, DEPTH) + half, F32),
                     jax.ShapeDtypeStruct((b, 2, 2, 8, LANES), F32),
                     jax.ShapeDtypeStruct((b,) + m_spec_shape, F32))
        out_specs = (o_spec,
                     pl.BlockSpec((nb, None) + half, lambda i: (i, l, 0, 0, 0, 0)),
                     pl.BlockSpec((nb, 2, 2, 8, LANES), lambda i: (i, 0, 0, 0, 0)),
                     pl.BlockSpec((nb,) + m_spec_shape, lambda i: (i, 0, 0, 0)))
        if c_buf is not None:
            aliases[len(args)] = 1
            in_specs.append(pl.BlockSpec(memory_space=pl.ANY))
            args.append(c_buf)
    return pl.pallas_call(
        functools.partial(_mlstm_kernel, latent, s, nb, c_buf is not None),
        out_shape=out_shape, grid=(b // nb,), in_specs=in_specs, out_specs=out_specs,
        input_output_aliases=aliases,
        scratch_shapes=scratch,
        compiler_params=_cparams(("arbitrary",)),
        name="mlstm_lat" if latent else "mlstm_ctx",
    )(*args)


def _post_kernel(nt_ctx, seg_ctx, split_x, final, *refs):
    n_x = 2 if split_x else 1
    x_refs, refs = refs[:n_x], refs[n_x:]
    (oab_c, oab_l, oc_c, oc_l, wo_ref, wu_ref, wd_ref, gt1_ref, sh2_ref, sc2_ref, gt2_ref, g2_ref,
     cw_ref, cb_ref, gf_ref) = refs[:15]
    refs = refs[15:]
    if final:
        o_refs, refs = refs[:2], refs[2:]
    else:
        (cond_ref, wada_ref, bada_ref), o_refs, mods_next_ref, refs = (
            refs[:3], refs[3:4], refs[4], refs[5:])
    wo_s, wu_s, wd_s, x1_ref, h2_ref, act_ref = refs
    i = pl.program_id(0)

    @pl.when(i < N_CAST)
    def _():
        wu_s[:, pl.ds(pl.multiple_of(i * CAST_UP, CAST_UP), CAST_UP)] = wu_ref[...].astype(BF16)
        wd_s[pl.ds(pl.multiple_of(i * CAST_DOWN, CAST_DOWN), CAST_DOWN), :] = wd_ref[...].astype(BF16)

    @pl.when(i < N_CAST_OUT)
    def _():
        wo_s[pl.ds(pl.multiple_of(i * CAST_OUT, CAST_OUT), CAST_OUT), :] = wo_ref[...].astype(BF16)

    @pl.when(i >= N_CAST)
    def _():
        t = i - N_CAST
        is_ctx = t < nt_ctx
        if not final:
            mods_next_ref[...] = _ada_block(cond_ref, wada_ref, bada_ref)

        th = TM_FFN // POST_SUB
        seg = jnp.where(is_ctx, seg_ctx, GRID_W)
        row = lax.broadcasted_iota(jnp.int32, (th, FC), 0)
        first_w = (row & (GRID_W - 1)) == 0
        last_w = (row & (GRID_W - 1)) == GRID_W - 1
        seg_first = (row & (seg - 1)) == 0
        seg_last = (row & (seg - 1)) == seg - 1

        def conv(u, cs):
            prev = pltpu.roll(u, 1, 0)
            prev = jnp.where(first_w, jnp.where(seg_first, 0.0, prev), prev)
            nxt = pltpu.roll(u, th - 1, 0)
            nxt = jnp.where(last_w, jnp.where(seg_last, 0.0, nxt), nxt)
            return (cb_ref[:, cs] + prev * cw_ref[0:1, cs] + u * cw_ref[1:2, cs]
                    + nxt * cw_ref[2:3, cs])

        outs = []
        for hf in range(POST_SUB):
            rs = slice(hf * th, (hf + 1) * th)

            def pick(a, b):
                return jnp.where(is_ctx, a, b)

            x = pick(x_refs[0][rs, :], x_refs[1][rs, :]) if split_x else x_refs[0][rs, :]
            mix = (_dot(pick(oab_c[rs, :], oab_l[rs, :]), wo_s[:2 * 256, :])
                   + lax.dot_general(pick(oc_c[:, rs], oc_l[:, rs]), wo_s[2 * 256:, :],
                                     (((0,), (0,)), ((), ())), preferred_element_type=F32))
            x1 = x + gt1_ref[...] * mix
            x1_ref[rs, :] = x1
            h2 = _rms_rows(x1, g2_ref[...], D_MODEL) * (1.0 + sc2_ref[...]) + sh2_ref[...]
            h2_ref[rs, :] = h2.astype(BF16)

            for j in range(N_FC):
                vs = slice(j * FC, (j + 1) * FC)
                gs = slice(D_FF + j * FC, D_FF + (j + 1) * FC)
                val = conv(_dot(h2_ref[rs, :], wu_s[:, vs]), vs)
                gate = conv(_dot(h2_ref[rs, :], wu_s[:, gs]), gs)
                act_ref[rs, vs] = (gate * jax.nn.sigmoid(gate) * val).astype(BF16)

            x2 = x1_ref[rs, :] + gt2_ref[...] * _dot(act_ref[rs, :], wd_s[...])
            if not final:
                o_refs[0][rs, :] = x2
            else:
                outs.append(_rms_rows(x2, gf_ref[...], D_MODEL))

        if final:
            y = jnp.concatenate(outs, axis=0)

            @pl.when(is_ctx)
            def _():
                o_refs[0][...] = y

            @pl.when(jnp.logical_not(is_ctx))
            def _():
                o_refs[1][...] = y


def _post_call(l, final, xs, n_ctx, s_ctx, s_lat, oab, oc, mods, wts, ada):
    split_x = len(xs) == 2
    n_lat = oab[1].shape[0]
    nt_ctx, nt_lat = n_ctx // TM_FFN, n_lat // TM_FFN
    nt = nt_ctx + nt_lat
    tiles_per_batch = s_lat // TM_FFN
    assert s_ctx <= TM_FFN and TM_FFN % s_ctx == 0 and s_lat % TM_FFN == 0

    def tile(i):
        return jnp.maximum(i - N_CAST, 0)

    def ctx_t(i):
        return jnp.minimum(tile(i), nt_ctx - 1)

    def lat_t(i):
        return jnp.maximum(tile(i) - nt_ctx, 0)

    def mod(chunk):
        def index(i):
            row = jnp.where(tile(i) < nt_ctx, 0, 1 + lat_t(i) // tiles_per_batch)
            return (row * 6 + chunk, 0, 0)
        return pl.BlockSpec((None, 1, D_MODEL), index)

    def resident(*shape):
        nd = len(shape)
        return pl.BlockSpec((None,) + shape, lambda i: (l,) + (0,) * nd,
                            pipeline_mode=pl.Buffered(1))

    if split_x:
        x_specs = [pl.BlockSpec((TM_FFN, D_MODEL), lambda i: (ctx_t(i), 0)),
                   pl.BlockSpec((TM_FFN, D_MODEL), lambda i: (lat_t(i), 0))]
    else:
        x_specs = [pl.BlockSpec((TM_FFN, D_MODEL), lambda i: (tile(i), 0))]
    in_specs = x_specs + [
        pl.BlockSpec((TM_FFN, 512), lambda i: (ctx_t(i), 0)),
        pl.BlockSpec((TM_FFN, 512), lambda i: (lat_t(i), 0)),
        pl.BlockSpec((ML_HEADS * ML_DV, TM_FFN), lambda i: (0, ctx_t(i))),
        pl.BlockSpec((ML_HEADS * ML_DV, TM_FFN), lambda i: (0, lat_t(i))),
        pl.BlockSpec((None, CAST_OUT, D_MODEL), lambda i: (l, jnp.minimum(i, N_CAST_OUT - 1), 0)),
        pl.BlockSpec((None, D_MODEL, CAST_UP), lambda i: (l, 0, jnp.minimum(i, N_CAST - 1))),
        pl.BlockSpec((None, CAST_DOWN, D_MODEL), lambda i: (l, jnp.minimum(i, N_CAST - 1), 0)),
        mod(2), mod(3), mod(4), mod(5),
        resident(1, D_MODEL), resident(CONV_W, 2 * D_FF), resident(1, 2 * D_FF),
        pl.BlockSpec((1, D_MODEL), lambda i: (0, 0)),
    ]
    args = list(xs) + [oab[0], oab[1], oc[0], oc[1], wts["w_out"], wts["w_up"], wts["w_down"],
                       mods, mods, mods, mods, wts["g_norm2"], wts["conv_w"], wts["conv_b"],
                       wts["g_final"]]
    if final:
        out_shape = (jax.ShapeDtypeStruct((n_ctx, D_MODEL), F32),
                     jax.ShapeDtypeStruct((n_lat, D_MODEL), F32))
        out_specs = (pl.BlockSpec((TM_FFN, D_MODEL), lambda i: (ctx_t(i), 0)),
                     pl.BlockSpec((TM_FFN, D_MODEL), lambda i: (lat_t(i), 0)))
    else:
        ada_w = N_MOD * D_MODEL // nt
        assert ada_w % LANES == 0
        in_specs += [pl.BlockSpec((8, D_MODEL), lambda i: (0, 0)),
                     pl.BlockSpec((None, D_MODEL, ada_w), lambda i: (l + 1, 0, tile(i))),
                     pl.BlockSpec((None, 1, ada_w), lambda i: (l + 1, 0, tile(i)))]
        args += [ada[0], ada[1], ada[2]]
        out_shape = (jax.ShapeDtypeStruct((n_ctx + n_lat, D_MODEL), F32),
                     jax.ShapeDtypeStruct((8, N_MOD * D_MODEL), F32))
        out_specs = (pl.BlockSpec((TM_FFN, D_MODEL), lambda i: (tile(i), 0)),
                     pl.BlockSpec((8, ada_w), lambda i: (0, tile(i))))
    return pl.pallas_call(
        functools.partial(_post_kernel, nt_ctx, s_ctx, split_x, final),
        out_shape=out_shape, grid=(N_CAST + nt,), in_specs=in_specs, out_specs=out_specs,
        scratch_shapes=[pltpu.VMEM((D_MODEL, D_MODEL), BF16), pltpu.VMEM((D_MODEL, 2 * D_FF), BF16),
                        pltpu.VMEM((D_FF, D_MODEL), BF16),
                        pltpu.VMEM((TM_FFN, D_MODEL), F32), pltpu.VMEM((TM_FFN, D_MODEL), BF16),
                        pltpu.VMEM((TM_FFN, D_FF), BF16)],
        compiler_params=pltpu.CompilerParams(dimension_semantics=("arbitrary",),
                                             vmem_limit_bytes=VMEM_LIMIT_FFN),
        name="post",
    )(*args)


W_IN_BODY = (352, 2656)


def _pack_in_kernel(w_ref, o_ref):
    tc = w_ref.shape[1]

    def put(dst, blk):
        o_ref[:, dst:dst + blk.shape[0]] = blk.T.astype(BF16)

    def rows(lo, n):
        return w_ref[lo:lo + n, :]

    def zeros(n):
        return jnp.zeros((n, tc), F32)

    put(C_CQ, jnp.concatenate([rows(0, MLA_Q_RANK), zeros(C_CKV - C_CQ - MLA_Q_RANK)], axis=0))
    put(C_CKV, rows(MLA_Q_RANK, MLA_KV_RANK))
    aux = [rows(MLA_Q_RANK + MLA_KV_RANK, MLA_ROPE)]
    for g in range(4):
        aux += [rows(W_IN_BODY[1] + ML_HEADS * g, ML_HEADS), zeros(8 - ML_HEADS)]
    aux.append(zeros(LANES - AUX_GATE - 32))
    put(C_AUX, jnp.concatenate(aux, axis=0))
    for lo in range(W_IN_BODY[0], W_IN_BODY[1], 256):
        put(C_DQ + lo - W_IN_BODY[0], rows(lo, 256))


def _pack_in_call(w_in):
    w_in_t = jnp.swapaxes(w_in, 1, 2)
    tc = 256
    return pl.pallas_call(
        _pack_in_kernel,
        out_shape=jax.ShapeDtypeStruct((DEPTH, D_MODEL, NP_IN), BF16),
        grid=(DEPTH, D_MODEL // tc),
        in_specs=[pl.BlockSpec((None, w_in_t.shape[1], tc), lambda l, i: (l, 0, i))],
        out_specs=pl.BlockSpec((None, tc, NP_IN), lambda l, i: (l, i, 0)),
        compiler_params=_cparams(("arbitrary", "arbitrary")),
        name="pack_w_in",
    )(w_in_t)


def _pack_weights(w_in, g_cq, w_uq, g_ckv, w_ukv, b_gate, g_subln, g_mnorm, g_norm1, g_norm2,
                  w_out, w_up, conv_w, conv_b, w_down, g_final, lam_q1, lam_k1, lam_q2, lam_k2):
    def cols(a, lo, n, pad=0):
        blk = a[..., lo:lo + n]
        if pad:
            blk = jnp.pad(blk, [(0, 0)] * (a.ndim - 1) + [(0, pad)])
        return blk

    w_in_p = _pack_in_call(w_in)

    hd = MLA_NOPE + MLA_ROPE
    w_uq_p = jnp.pad(w_uq.reshape(DEPTH, MLA_Q_RANK, MLA_HEADS, hd),
                     [(0, 0), (0, 256 - MLA_Q_RANK), (0, 0), (0, LANES - hd)])
    w_uq_p = w_uq_p.reshape(DEPTH, 256, MLA_HEADS * LANES).astype(BF16)

    w_ukv4 = w_ukv.reshape(DEPTH, MLA_KV_RANK, MLA_HEADS, MLA_NOPE + MLA_V)
    w_k = jnp.pad(w_ukv4[..., :MLA_NOPE], [(0, 0), (0, 0), (0, 0), (0, LANES - MLA_NOPE)])
    w_k = w_k.reshape(DEPTH, MLA_KV_RANK, MLA_HEADS * LANES)
    j = jnp.arange(LANES)[:, None]
    cix = jnp.arange(MLA_HEADS * LANES)[None, :]
    place = ((j < MLA_ROPE) & ((cix % LANES) == MLA_NOPE + j)).astype(F32)
    w_kk = jnp.concatenate([w_k, jnp.broadcast_to(place, (DEPTH, LANES, MLA_HEADS * LANES))],
                           axis=1).astype(BF16)
    w_v = w_ukv4[..., MLA_NOPE:].reshape(DEPTH, MLA_KV_RANK, MLA_HEADS * MLA_V).astype(BF16)

    return dict(
        w_in=w_in_p, w_uq=w_uq_p, w_kk=w_kk, w_v=w_v,
        g_norm1=g_norm1[:, None, :], g_norm2=g_norm2[:, None, :],
        g_cq=jnp.pad(g_cq, [(0, 0), (0, 256 - MLA_Q_RANK)])[:, None, :],
        g_ckv=g_ckv[:, None, :],
        b_gate=jnp.pad(jnp.pad(b_gate.reshape(DEPTH, 4, ML_HEADS), [(0, 0), (0, 0), (0, 4)])
                       .reshape(DEPTH, 32), [(0, 0), (AUX_GATE, LANES - AUX_GATE - 32)])[:, None, :],
        g_subln=jnp.tile(g_subln, (1, DF_HEADS))[:, None, :],
        g_mnorm=jnp.broadcast_to(g_mnorm[:, :, None], (DEPTH, ML_HEADS * ML_DV, LANES)),
        w_out=w_out, w_up=w_up, w_down=w_down,
        conv_w=conv_w, conv_b=conv_b[:, None, :], g_final=g_final[None, :],
        lam_q1=lam_q1[:, None, :], lam_k1=lam_k1[:, None, :],
        lam_q2=lam_q2[:, None, :], lam_k2=lam_k2[:, None, :],
    )


def _rope_tables(n_tok):
    t = np.arange(n_tok)
    row = (t // GRID_W).astype(np.float64)
    col = (t % GRID_W).astype(np.float64)
    nf = MLA_ROPE // 4
    inv = ROPE_BASE ** (-np.arange(nf, dtype=np.float64) / nf)
    ar = row[:, None] * inv[None, :]
    ac = col[:, None] * inv[None, :]
    ang = np.concatenate([ar, ar, ac, ac], axis=-1)
    quarter = (np.arange(MLA_ROPE) // nf) % 2
    cos = jnp.asarray(np.cos(ang), F32)
    sin_up = jnp.asarray(np.where(quarter == 0, -np.sin(ang), 0.0), F32)
    sin_dn = jnp.asarray(np.where(quarter == 1, np.sin(ang), 0.0), F32)
    ones = jnp.ones((n_tok, 1), F32)
    zeros = jnp.zeros((n_tok, 1), F32)

    def head_q(t32, fill):
        blk = jnp.concatenate([jnp.tile(fill, (1, MLA_NOPE)), t32, jnp.tile(fill, (1, 32))], axis=1)
        return jnp.tile(blk, (1, MLA_HEADS))

    def aux_k(t32, fill):
        return jnp.concatenate([t32, jnp.tile(fill, (1, LANES - MLA_ROPE))], axis=1)

    tq = (head_q(cos, ones), head_q(sin_up, zeros), head_q(sin_dn, zeros))
    td = tuple(jnp.tile(a, (1, 256 // DF_DIM)) for a in (cos, sin_up, sin_dn))
    tk = (aux_k(cos, ones), aux_k(sin_up, zeros), aux_k(sin_dn, zeros))
    return tq + td + tk


def kernel(x_prompt, x_sample, cache_mla_ckv, cache_mla_krope, cache_diff_k, cache_diff_v,
           state_mlstm_C, state_mlstm_n, state_mlstm_m, c, c_ctx, w_ada, b_ada, g_norm1, w_in,
           g_cq, w_uq, g_ckv, w_ukv, lam_q1, lam_k1, lam_q2, lam_k2, g_subln, b_gate, g_mnorm,
           w_out, g_norm2, w_up, conv_w, conv_b, w_down, g_final):
    bp, sp, _ = x_prompt.shape
    bl, sl, _ = x_sample.shape
    t_len = cache_mla_ckv.shape[2]

    wts = _pack_weights(w_in, g_cq, w_uq, g_ckv, w_ukv, b_gate, g_subln, g_mnorm, g_norm1, g_norm2,
                        w_out, w_up, conv_w, conv_b, w_down, g_final, lam_q1, lam_k1, lam_q2, lam_k2)
    tables = _rope_tables(sl)

    cond = jnp.concatenate([c_ctx[None, :], c, jnp.zeros((8 - 1 - bl, D_MODEL), F32)], axis=0)
    ada = (cond, w_ada, b_ada[:, None, :])
    mods = _ada_call(*ada).reshape(8 * N_MOD, 1, D_MODEL)

    def feat_major(a):
        return jnp.transpose(a, (0, 1, 3, 4, 2)).reshape(bl, DEPTH, 256, t_len)

    kctx, vctx, cdk, cdv = _ctxkv_call(
        cache_mla_ckv, jnp.swapaxes(cache_mla_krope, 2, 3), feat_major(cache_diff_k),
        feat_major(cache_diff_v), wts["w_kk"], wts["w_v"])
    c0_t = jnp.swapaxes(state_mlstm_C.reshape(bl, DEPTH, 2, 2, LANES, ML_DV), -1, -2)
    n0_r = jnp.broadcast_to(state_mlstm_n.reshape(bl, DEPTH, 2, 2, 1, LANES),
                            (bl, DEPTH, 2, 2, ML_NROWS, LANES))
    s0 = jnp.concatenate([c0_t, n0_r], axis=-2)
    m0 = jnp.broadcast_to(jnp.pad(state_mlstm_m, [(0, 0)] * 3 + [(0, 8 - ML_HEADS)])[..., None],
                          (bl, DEPTH, 2, 8, LANES))

    n_ctx, n_lat = bp * sp, bl * sl
    xs = (x_prompt.reshape(n_ctx, D_MODEL), x_sample.reshape(n_lat, D_MODEL))
    state_bufs = [jnp.zeros((bp, DEPTH, sp, MLA_KV_RANK), F32), jnp.zeros((bp, DEPTH, MLA_ROPE, sp), F32),
                  jnp.zeros((bp, DEPTH, 256, sp), F32), jnp.zeros((bp, DEPTH, 256, sp), F32)]
    c_buf = jnp.zeros((bp, DEPTH, 2, 2, LANES, LANES), F32)
    n_col, m_col = [], []
    for l in range(DEPTH):
        lam_init = 0.8 - 0.6 * math.exp(-0.3 * l)
        final = l == DEPTH - 1
        x_ctx, x_lat, lat_off = (xs[0], xs[1], 0) if len(xs) == 2 else (xs[0], xs[0], n_ctx)
        (q, k, v, dq, dk, dv, mq, mkt, mv, mo, gt, *state_bufs) = _pre_call(
            False, l, x_ctx, 0, bp, sp, mods, wts, None, state_bufs)
        oab_c = _attn_call(False, l, lam_init, (q, k, v), (dq, dk, dv), wts)
        oc_c, c_buf, n_fin, m_fin = _mlstm_call(False, l, mq, mkt, mv, mo, gt, wts, c_buf=c_buf)
        n_col.append(n_fin[..., 0, :].reshape(bp, 2, ML_HEADS, ML_DK))
        m_col.append(m_fin[:, :, :ML_HEADS, 0])
        (q, k, v, dq, dk, dv, mq, mkt, mv, mo, gt) = _pre_call(
            True, l, x_lat, lat_off, bl, sl, mods, wts, tables)
        oab_l = _attn_call(True, l, lam_init, (q, k, v), (dq, dk, dv), wts, (kctx, vctx), (cdk, cdv))
        oc_l = _mlstm_call(True, l, mq, mkt, mv, mo, gt, wts, s0, m0)
        out = _post_call(l, final, xs, n_ctx, sp, sl,
                         (oab_c.reshape(n_ctx, 512), oab_l.reshape(n_lat, 512)), (oc_c, oc_l),
                         mods, wts, ada)
        if final:
            xs = out
        else:
            xs, mods = (out[0],), out[1].reshape(8 * N_MOD, 1, D_MODEL)

    xp = xs[0].reshape(bp, sp, D_MODEL)
    xs = xs[1].reshape(bl, sl, D_MODEL)
    ckv_all, kr_all, dk_all, dv_all = state_bufs

    def token_major(a):
        return jnp.transpose(a.reshape(bp, DEPTH, DF_HEADS, 2 * DF_DIM, sp), (0, 1, 4, 2, 3))

    return (xp, xs, ckv_all, jnp.swapaxes(kr_all, 2, 3), token_major(dk_all), token_major(dv_all),
            c_buf.reshape(bp, DEPTH, 2, ML_HEADS, ML_DK, ML_DV),
            jnp.stack(n_col, axis=1), jnp.stack(m_col, axis=1))
```

```python
import functools
import math

import jax
import jax.numpy as jnp
import numpy as np
from jax import lax
from jax.experimental import pallas as pl
from jax.experimental.pallas import tpu as pltpu

F32 = jnp.float32
BF16 = jnp.bfloat16

D_MODEL = 1024
DEPTH = 4
GRID_W = 64
N_MOD = 6
EPS = 1e-6
ROPE_BASE = 10000.0
MLA_HEADS = 4
MLA_Q_RANK = 192
MLA_KV_RANK = 128
MLA_NOPE = 64
MLA_ROPE = 32
MLA_V = 64
DF_HEADS = 4
DF_DIM = 32
ML_HEADS = 4
ML_DK = 64
ML_DV = 128
D_FF = 2816
CONV_W = 3

LANES = 128
VMEM_LIMIT = 48 * 1024 * 1024
VMEM_LIMIT_FFN = 56 * 1024 * 1024

C_CQ, C_CKV, C_AUX, C_DQ, C_DK, C_DV, C_MQ, C_MK, C_MV, C_MO = (
    0, 256, 384, 512, 768, 1024, 1280, 1536, 1792, 2304)
NP_IN = 2816
AUX_GATE = 32

TM_PRE = 256
PRE_SUB = 2
TQ = 512
NB_CTX = 4
ML_CHUNK = 128
ML_NROWS = 128
NB_ML_CTX = 4
NB_ML_LAT = 2
TM_FFN = 512
POST_SUB = 1
FC = 256
N_FC = D_FF // FC
N_CAST = 22
CAST_UP = 2 * D_FF // N_CAST
CAST_DOWN = D_FF // N_CAST
CAST_OUT = 128
N_CAST_OUT = D_MODEL // CAST_OUT

NT = (((1,), (1,)), ((), ()))
LOG2E = 1.4426950408889634


def _cparams(sem):
    return pltpu.CompilerParams(dimension_semantics=sem, vmem_limit_bytes=VMEM_LIMIT)


def _dot(a, b):
    return jnp.dot(a, b, preferred_element_type=F32)


def _dot_nt(a, b):
    return lax.dot_general(a, b, NT, preferred_element_type=F32)


def _rms_rows(x, g, n):
    ms = jnp.sum(x * x, axis=-1, keepdims=True) * (1.0 / n)
    return x * lax.rsqrt(ms + EPS) * g


def _rope(x, cos, sin_up, sin_dn):
    w = x.shape[-1]
    return x * cos + pltpu.roll(x, w - 8, 1) * sin_up + pltpu.roll(x, 8, 1) * sin_dn


def _ada_block(c_ref, w_ref, b_ref):
    c = c_ref[...]
    s = (c * jax.nn.sigmoid(c)).astype(BF16)
    return _dot(s, w_ref[...].astype(BF16)) + b_ref[...]


def _ada_kernel(c_ref, w_ref, b_ref, o_ref):
    o_ref[...] = _ada_block(c_ref, w_ref, b_ref)


def _ada_call(cond, w_ada, b_ada):
    nt = 1024
    return pl.pallas_call(
        _ada_kernel,
        out_shape=jax.ShapeDtypeStruct((8, N_MOD * D_MODEL), F32),
        grid=(N_MOD * D_MODEL // nt,),
        in_specs=[pl.BlockSpec((8, D_MODEL), lambda j: (0, 0)),
                  pl.BlockSpec((None, D_MODEL, nt), lambda j: (0, 0, j)),
                  pl.BlockSpec((None, 1, nt), lambda j: (0, 0, j))],
        out_specs=pl.BlockSpec((8, nt), lambda j: (0, j)),
        compiler_params=_cparams(("arbitrary",)),
        name="ada_mod",
    )(cond, w_ada, b_ada)


def _ctxkv_kernel(ckv_ref, krt_ref, dkt_ref, dvt_ref, wkk_ref, wv_ref, k_ref, v_ref, dkb_ref, dvb_ref):
    t = ckv_ref.shape[0]
    ckv = ckv_ref[...].astype(BF16)
    kr = jnp.concatenate([krt_ref[...], jnp.zeros((LANES - MLA_ROPE, t), F32)], axis=0).T
    kin = jnp.concatenate([ckv, kr.astype(BF16)], axis=1)
    k_ref[...] = _dot(kin, wkk_ref[...]).T.astype(BF16)
    v_ref[...] = _dot(ckv, wv_ref[...]).astype(BF16)
    dkb_ref[...] = dkt_ref[...].astype(BF16)
    dvb_ref[...] = dvt_ref[...].T.astype(BF16)


def _ctxkv_call(cache_ckv, cache_kr_t, cache_dk_t, cache_dv_t, wkk, wv):
    b, _, t, _ = cache_ckv.shape

    def cache_t(w):
        return pl.BlockSpec((None, None, w, t), lambda l, i: (i, l, 0, 0))

    def out(w):
        return pl.BlockSpec((None, None, t, w), lambda l, i: (l, i, 0, 0))

    def out_t(w):
        return pl.BlockSpec((None, None, w, t), lambda l, i: (l, i, 0, 0))

    return pl.pallas_call(
        _ctxkv_kernel,
        out_shape=(jax.ShapeDtypeStruct((DEPTH, b, 512, t), BF16),
                   jax.ShapeDtypeStruct((DEPTH, b, t, 256), BF16),
                   jax.ShapeDtypeStruct((DEPTH, b, 256, t), BF16),
                   jax.ShapeDtypeStruct((DEPTH, b, t, 256), BF16)),
        grid=(DEPTH, b),
        in_specs=[pl.BlockSpec((None, None, t, MLA_KV_RANK), lambda l, i: (i, l, 0, 0)),
                  cache_t(MLA_ROPE), cache_t(256), cache_t(256),
                  pl.BlockSpec((None, 256, 512), lambda l, i: (l, 0, 0)),
                  pl.BlockSpec((None, MLA_KV_RANK, 256), lambda l, i: (l, 0, 0))],
        out_specs=(out_t(512), out(256), out_t(256), out(256)),
        compiler_params=_cparams(("arbitrary", "arbitrary")),
        name="ctx_kv",
    )(cache_ckv, cache_kr_t, cache_dk_t, cache_dv_t, wkk, wv)


def _pre_kernel(latent, kinds, *refs):
    x_ref, sh_ref, sc_ref, g1_ref, win_ref = refs[:5]
    h = _rms_rows(x_ref[...], g1_ref[...], D_MODEL) * (1.0 + sc_ref[...]) + sh_ref[...]
    proj_all = _dot(h.astype(BF16), win_ref[...])
    for n in range(PRE_SUB):
        views = []
        for kind, r in zip(kinds, refs):
            if kind == "alias":
                continue
            if kind == "rows":
                r = r.at[n * TM_PRE:(n + 1) * TM_PRE]
            elif kind == "lanes":
                r = r.at[:, n * TM_PRE:(n + 1) * TM_PRE]
            elif kind == "batch":
                r = r.at[n]
            views.append(r)
        _pre_tile(latent, proj_all[n * TM_PRE:(n + 1) * TM_PRE], *views)


def _pre_tile(latent, proj, *refs):
    (x_ref, sh_ref, sc_ref, g1_ref, win_ref, gcq_ref, wuq_ref, gckv_ref, wkk_ref, wv_ref,
     bg_ref) = refs[:11]
    refs = refs[11:]
    if latent:
        (cq_t, sqa_t, sqb_t, cd_t, sda_t, sdb_t, ck_t, ska_t, skb_t) = refs[:9]
        refs = refs[9:]
    (q_ref, k_ref, v_ref, dq_ref, dk_ref, dv_ref, mq_ref, mk_ref, mv_ref, mo_ref,
     gt_ref) = refs[:11]
    refs = refs[11:]
    if not latent:
        ckv_out, kr_out, dk_out, dv_out = refs

    cq = _rms_rows(proj[:, C_CQ:C_CQ + 256], gcq_ref[...], MLA_Q_RANK)
    q = _dot(cq.astype(BF16), wuq_ref[...])
    if latent:
        q = _rope(q, cq_t[...], sqa_t[...], sqb_t[...])
    q_ref[...] = (q * ((MLA_NOPE + MLA_ROPE) ** -0.5 * LOG2E)).astype(BF16)

    c_kv = _rms_rows(proj[:, C_CKV:C_CKV + MLA_KV_RANK], gckv_ref[...], MLA_KV_RANK)
    aux = proj[:, C_AUX:C_AUX + LANES] + bg_ref[...]
    if latent:
        aux = _rope(aux, ck_t[...], ska_t[...], skb_t[...])
    aux_t = aux.T
    gt_ref[...] = aux_t[AUX_GATE:AUX_GATE + 32, :]
    if not latent:
        ckv_out[...] = c_kv
        kr_out[...] = aux_t[:MLA_ROPE, :]
    ckv_b = c_kv.astype(BF16)
    kin = jnp.concatenate([ckv_b, aux.astype(BF16)], axis=1)
    k_ref[...] = _dot(kin, wkk_ref[...]).T.astype(BF16)
    v_ref[...] = _dot(ckv_b, wv_ref[...]).astype(BF16)

    dq = proj[:, C_DQ:C_DQ + 256]
    dk = proj[:, C_DK:C_DK + 256]
    dv = proj[:, C_DV:C_DV + 256]
    if not latent:
        dk_t = dk.T
        dk_out[...] = dk_t
        dv_out[...] = dv.T
    else:
        dq = _rope(dq, cd_t[...], sda_t[...], sdb_t[...])
        dk_t = _rope(dk, cd_t[...], sda_t[...], sdb_t[...]).T
    dq_ref[...] = (dq * (DF_DIM ** -0.5 * LOG2E)).astype(BF16)
    dk_ref[...] = dk_t.astype(BF16)
    dv_ref[...] = dv.astype(BF16)

    lane = lax.broadcasted_iota(jnp.int32, (proj.shape[0], LANES), 1)
    for h in range(ML_HEADS):
        blk = proj[:, C_MQ + (h // 2) * LANES:C_MQ + (h // 2 + 1) * LANES]
        mq_ref[h * LANES:(h + 1) * LANES, :] = (
            jnp.where((lane >> 6) == h % 2, blk, 0.0).T.astype(BF16))
    mk_ref[...] = (proj[:, C_MK:C_MK + 256] * (ML_DK ** -0.5)).astype(BF16)
    mv_ref[...] = proj[:, C_MV:C_MV + 512].T.astype(BF16)
    mo_ref[...] = proj[:, C_MO:C_MO + 512].T


def _pre_call(latent, l, x2d, tok_off, b, s, mods, wts, tables, state_bufs=None):
    ns = s // TM_PRE
    tile_off = tok_off // TM_PRE
    tm2 = PRE_SUB * TM_PRE
    seq_split = ns > 1
    assert (ns % PRE_SUB == 0) if seq_split else (b % PRE_SUB == 0 and ns == 1)
    assert tile_off % PRE_SUB == 0
    grid = (ns // PRE_SUB, b) if seq_split else (1, b // PRE_SUB)

    def tok(width):
        if seq_split:
            return pl.BlockSpec((None, tm2, width), lambda j, i: (i, j, 0)), "rows"
        return pl.BlockSpec((PRE_SUB, TM_PRE, width), lambda j, i: (i, j, 0)), "batch"

    def feat(width):
        if seq_split:
            return pl.BlockSpec((None, width, tm2), lambda j, i: (i, 0, j)), "lanes"
        return pl.BlockSpec((PRE_SUB, width, TM_PRE), lambda j, i: (i, 0, j)), "batch"

    def mod(chunk):
        if latent:
            return pl.BlockSpec((None, 1, D_MODEL), lambda j, i: ((1 + i) * 6 + chunk, 0, 0))
        return pl.BlockSpec((None, 1, D_MODEL), lambda j, i: (chunk, 0, 0))

    def lw(*shape):
        nd = len(shape)
        return pl.BlockSpec((None,) + shape, lambda j, i: (l,) + (0,) * nd)

    if seq_split:
        x_spec = pl.BlockSpec((tm2, D_MODEL), lambda j, i: ((tile_off + i * ns) // PRE_SUB + j, 0))
    else:
        x_spec = pl.BlockSpec((tm2, D_MODEL), lambda j, i: (tile_off // PRE_SUB + i, 0))
    in_specs = [x_spec, mod(0), mod(1), lw(1, D_MODEL), lw(D_MODEL, NP_IN), lw(1, 256),
                lw(256, 512), lw(1, MLA_KV_RANK), lw(256, 512), lw(MLA_KV_RANK, 256), lw(1, LANES)]
    kinds = ["rows"] + [None] * 10
    args = [x2d, mods, mods, wts["g_norm1"], wts["w_in"], wts["g_cq"], wts["w_uq"], wts["g_ckv"],
            wts["w_kk"], wts["w_v"], wts["b_gate"]]
    if latent:
        for t in tables:
            in_specs.append(pl.BlockSpec((tm2, t.shape[1]), lambda j, i: (j, 0)))
            kinds.append("rows")
            args.append(t)

    widths = [(512, BF16, False), (512, BF16, True), (256, BF16, False), (256, BF16, False),
              (256, BF16, True), (256, BF16, False), (512, BF16, True), (256, BF16, False),
              (512, BF16, True), (512, F32, True), (32, F32, True)]
    out_shape = [jax.ShapeDtypeStruct((b, w, s) if tr else (b, s, w), dt) for w, dt, tr in widths]
    out_pairs = [feat(w) if tr else tok(w) for w, _, tr in widths]
    aliases = {}
    if not latent:
        state = [(MLA_KV_RANK, False), (MLA_ROPE, True), (256, True), (256, True)]
        for k, (w, tr) in enumerate(state):
            if tr:
                spec = pl.BlockSpec((PRE_SUB, None, w, TM_PRE), lambda j, i: (i, l, 0, j))
                shape = (b, DEPTH, w, s)
            else:
                spec = pl.BlockSpec((PRE_SUB, None, TM_PRE, w), lambda j, i: (i, l, j, 0))
                shape = (b, DEPTH, s, w)
            out_shape.append(jax.ShapeDtypeStruct(shape, F32))
            out_pairs.append((spec, "batch"))
            if state_bufs is not None:
                aliases[len(args)] = len(widths) + k
                in_specs.append(pl.BlockSpec(memory_space=pl.ANY))
                kinds.append("alias")
                args.append(state_bufs[k])
    out_specs = tuple(p[0] for p in out_pairs)
    kinds += [p[1] for p in out_pairs]
    return pl.pallas_call(
        functools.partial(_pre_kernel, latent, tuple(kinds)),
        out_shape=tuple(out_shape), grid=grid, in_specs=in_specs, out_specs=out_specs,
        input_output_aliases=aliases,
        compiler_params=_cparams(("arbitrary", "arbitrary")),
        name="pre_lat" if latent else "pre_ctx",
    )(*args)


def _softmax_parts(s_list):
    m = functools.reduce(jnp.maximum, [jnp.max(s, axis=1, keepdims=True) for s in s_list])
    p_list = [jnp.exp2(s - m) for s in s_list]
    l = functools.reduce(jnp.add, [jnp.sum(p, axis=1, keepdims=True) for p in p_list])
    return p_list, l


def _attn_kernel(latent, lam_init, nb, *refs):
    n_seg = 5 if latent else 3
    o_ref = refs[-1]
    mla_in, diff_in, shared = refs[:n_seg], refs[n_seg:2 * n_seg], refs[2 * n_seg:-1]
    for n in range(nb):
        _mla_body(latent, *[r.at[n] for r in mla_in], o_ref.at[n])
        _diff_body(latent, lam_init, *[r.at[n] for r in diff_in], *shared, o_ref.at[n])


def _mla_body(latent, *refs):
    if latent:
        q_ref, k_ref, v_ref, kc_ref, vc_ref, o_ref = refs
        segs = [(kc_ref, vc_ref), (k_ref, v_ref)]
    else:
        q_ref, k_ref, v_ref, o_ref = refs
        segs = [(k_ref, v_ref)]
    lane = lax.broadcasted_iota(jnp.int32, (q_ref.shape[0], LANES), 1)

    def scores(h):
        hs = slice(h * LANES, (h + 1) * LANES)
        return [_dot(q_ref[:, hs], kr[hs, :]) for kr, _ in segs]

    outs = []
    s_next = scores(0)
    for h in range(MLA_HEADS):
        ps = slice((h // 2) * LANES, (h // 2 + 1) * LANES)
        s_list = s_next
        if h + 1 < MLA_HEADS:
            s_next = scores(h + 1)
        p_list, l = _softmax_parts(s_list)
        pv = functools.reduce(jnp.add, [_dot(p.astype(BF16), vr[:, ps])
                                        for p, (_, vr) in zip(p_list, segs)])
        outs.append(pv / l)
    o_ref[:, 0:LANES] = jnp.where(lane < MLA_V, outs[0], outs[1]).astype(BF16)
    o_ref[:, LANES:2 * LANES] = jnp.where(lane < MLA_V, outs[2], outs[3]).astype(BF16)


def _diff_body(latent, lam_init, *refs):
    if latent:
        (q_ref, k_ref, v_ref, kc_ref, vc_ref, lq1, lk1, lq2, lk2, g_ref, o_ref) = refs
        segs = [(kc_ref, vc_ref), (k_ref, v_ref)]
    else:
        (q_ref, k_ref, v_ref, lq1, lk1, lq2, lk2, g_ref, o_ref) = refs
        segs = [(k_ref, v_ref)]
    lam = (jnp.exp(jnp.sum(lq1[...] * lk1[...], axis=1, keepdims=True))
           - jnp.exp(jnp.sum(lq2[...] * lk2[...], axis=1, keepdims=True)) + lam_init)
    lane = lax.broadcasted_iota(jnp.int32, (q_ref.shape[0], LANES), 1)
    grp = lane >> 5
    qf = q_ref[...].astype(F32)

    def scores(u):
        h, c = u // 2, u % 2
        ps = slice((h // 2) * LANES, (h // 2 + 1) * LANES)
        qm = jnp.where(grp == 2 * (h % 2) + c, qf[:, ps], 0.0).astype(BF16)
        return [_dot(qm, kr[ps, :]) for kr, _ in segs]

    outs = []
    s_next = scores(0)
    for h in range(DF_HEADS):
        ps = slice((h // 2) * LANES, (h // 2 + 1) * LANES)
        hh = h % 2
        parts = []
        for c in range(2):
            s_list = s_next
            if 2 * h + c + 1 < 2 * DF_HEADS:
                s_next = scores(2 * h + c + 1)
            parts.append(_softmax_parts(s_list))
        (p0, l0), (p1, l1) = parts
        ratio = lam * l0 / l1
        pv = functools.reduce(jnp.add, [
            _dot((a0 - a1 * ratio).astype(BF16), vr[:, ps])
            for a0, a1, (_, vr) in zip(p0, p1, segs)]) * (1.0 / l0)
        valid = (lane >> 6) == hh
        ms = jnp.sum(jnp.where(valid, pv * pv, 0.0), axis=1, keepdims=True) * (1.0 / (2 * DF_DIM))
        outs.append(pv * lax.rsqrt(ms + EPS) * g_ref[:, ps] * (1.0 - lam_init))
    o_ref[:, 2 * LANES:3 * LANES] = jnp.where(lane < 2 * DF_DIM, outs[0], outs[1]).astype(BF16)
    o_ref[:, 3 * LANES:4 * LANES] = jnp.where(lane < 2 * DF_DIM, outs[2], outs[3]).astype(BF16)


def _attn_call(latent, l, lam_init, qkv, dqkv, wts, ctx=None, dctx=None):
    b, s, _ = qkv[0].shape
    nb = 1 if latent else NB_CTX
    tq = min(TQ, s)
    grid = (b // nb, s // tq)

    def group(wq, wk, wv, ctx_pair):
        specs = [pl.BlockSpec((nb, tq, wq), lambda i, j: (i, j, 0)),
                 pl.BlockSpec((nb, wk, s), lambda i, j: (i, 0, 0)),
                 pl.BlockSpec((nb, s, wv), lambda i, j: (i, 0, 0))]
        if latent:
            t = ctx_pair[1].shape[2]
            specs += [pl.BlockSpec((None, nb, wk, t), lambda i, j: (l, i, 0, 0)),
                      pl.BlockSpec((None, nb, t, wv), lambda i, j: (l, i, 0, 0))]
        return specs

    in_specs = group(512, 512, 256, ctx) + group(256, 256, 256, dctx)
    args = list(qkv) + (list(ctx) if latent else []) + list(dqkv) + (list(dctx) if latent else [])
    for name in ("lam_q1", "lam_k1", "lam_q2", "lam_k2"):
        in_specs.append(pl.BlockSpec((None, 1, DF_DIM), lambda i, j: (l, 0, 0)))
        args.append(wts[name])
    in_specs.append(pl.BlockSpec((None, 1, 256), lambda i, j: (l, 0, 0)))
    args.append(wts["g_subln"])
    return pl.pallas_call(
        functools.partial(_attn_kernel, latent, lam_init, nb),
        out_shape=jax.ShapeDtypeStruct((b, s, 512), BF16),
        grid=grid, in_specs=in_specs,
        out_specs=pl.BlockSpec((nb, tq, 512), lambda i, j: (i, j, 0)),
        compiler_params=_cparams(("arbitrary", "arbitrary")),
        name="attn_lat" if latent else "attn_ctx",
    )(*args)


def _log_sigmoid(x):
    return jnp.minimum(x, 0.0) - jnp.log1p(jnp.exp(-jnp.abs(x)))


def _mlstm_chunk(d, c, mq_ref, mk_ref, mv_ref, gt_ref, s_ref, m_ref, h_ref):
    L = ML_CHUNK
    rows = pl.ds(pl.multiple_of(c * L, L), L)
    s_i = lax.broadcasted_iota(jnp.int32, (L, L), 0)
    t_i = lax.broadcasted_iota(jnp.int32, (L, L), 1)
    mask = (s_i <= t_i) if d == 0 else (s_i >= t_i)
    tri = jnp.where(mask, 1.0, 0.0).astype(BF16)

    ig = gt_ref[16 * d:16 * d + 8, rows]
    lf = _log_sigmoid(gt_ref[16 * d + 8:16 * d + 16, rows])
    hi = lf.astype(BF16).astype(F32)
    r1 = lf - hi
    mid = r1.astype(BF16).astype(F32)
    parts = _dot(jnp.concatenate([hi, mid, r1 - mid], axis=0).astype(BF16), tri)
    bc = parts[0:8] + parts[8:16] + parts[16:24]
    rvec = ig - bc
    total = jnp.sum(lf, axis=1, keepdims=True)
    mm = m_ref[d]
    gvec = total + rvec
    m_new = jnp.maximum(total + mm, jnp.max(gvec, axis=1, keepdims=True))
    ws = jnp.exp(gvec - m_new).astype(BF16)
    cdec = jnp.exp(total + mm - m_new)
    m_ref[d] = m_new

    rv_t = jnp.concatenate([rvec, jnp.zeros((LANES - 8, L), F32)], axis=0).T

    s_old = [s_ref[d, pair].astype(BF16) for pair in range(2)]
    upd = []
    for h in range(ML_HEADS):
        pair = h // 2
        hs = slice(h * ML_DV, (h + 1) * ML_DV)
        qt = mq_ref[hs, rows]
        kp = mk_ref[rows, pair * LANES:(pair + 1) * LANES]
        vt = mv_ref[hs, rows]
        mmh = mm[h:h + 1, 0:1]
        rm = jnp.where(mask, rv_t[:, h:h + 1], -jnp.inf)
        a = jnp.maximum(jnp.max(rm, axis=0, keepdims=True), mmh)
        wqk = jnp.exp(rm - a) * _dot(kp, qt)
        dec = jnp.exp(mmh - a)
        qc = _dot(s_old[pair], qt)
        num = _dot(vt, wqk.astype(BF16)) + dec * qc[:ML_DV]
        den = jnp.sum(wqk, axis=0, keepdims=True) + dec * qc[ML_DV:ML_DV + 1]
        inv = 1.0 / jnp.maximum(jnp.abs(den), jnp.exp(-(a + bc[h:h + 1, :])))
        h_ref[d, hs, rows] = num * inv
        wsr = ws[h:h + 1, :]
        vaug = jnp.concatenate([vt * wsr, jnp.broadcast_to(wsr, (ML_NROWS, L))], axis=0)
        upd.append(_dot(vaug, kp))

    low = lax.broadcasted_iota(jnp.int32, (ML_DV + ML_NROWS, LANES), 1) < ML_DK
    for pair in range(2):
        h0, h1 = 2 * pair, 2 * pair + 1
        cd = jnp.where(low, cdec[h0:h0 + 1, 0:1], cdec[h1:h1 + 1, 0:1])
        s_ref[d, pair] = cd * s_ref[d, pair] + jnp.where(low, upd[h0], upd[h1])


def _mlstm_kernel(latent, seq, nb, has_alias, *refs):
    if latent:
        (mq_ref, mk_ref, mv_ref, mo_ref, gt_ref, g_ref, s0_ref, m0_ref,
         o_ref, s_ref, m_ref, h_ref) = refs
        s_ref[...] = s0_ref[...]
        m_ref[...] = m0_ref[...]
    else:
        if has_alias:
            refs = refs[:6] + refs[7:]
        (mq_ref, mk_ref, mv_ref, mo_ref, gt_ref, g_ref,
         o_ref, cf_ref, nf_ref, mf_ref, s_ref, m_ref, h_ref) = refs
        s_ref[...] = jnp.zeros(s_ref.shape, F32)
        m_ref[...] = jnp.zeros(m_ref.shape, F32)
    nc = seq // ML_CHUNK

    def body(j, carry):
        for n in range(nb):
            views = (mq_ref.at[n], mk_ref.at[n], mv_ref.at[n], gt_ref.at[n], s_ref.at[n],
                     m_ref.at[n], h_ref.at[n])
            _mlstm_chunk(0, j, *views)
            _mlstm_chunk(1, nc - 1 - j, *views)
        return carry

    lax.fori_loop(0, nc, body, 0, unroll=min(nc, 4))

    for n in range(nb):
        for h in range(ML_HEADS):
            hs = slice(h * ML_DV, (h + 1) * ML_DV)
            for j in range(nc):
                ts = slice(j * LANES, (j + 1) * LANES)
                hsum = h_ref[n, 0, hs, ts] + h_ref[n, 1, hs, ts]
                ms = jnp.sum(hsum * hsum, axis=0, keepdims=True) * (1.0 / ML_DV)
                y = hsum * lax.rsqrt(ms + EPS) * g_ref[hs, :]
                o_ref[hs, n * seq + j * LANES:n * seq + (j + 1) * LANES] = (
                    jax.nn.sigmoid(mo_ref[n, hs, ts]) * y).astype(BF16)
    if not latent:
        for n in range(nb):
            for d in range(2):
                for pair in range(2):
                    cf_ref[n, d, pair] = s_ref[n, d, pair, :ML_DV, :].T
        nf_ref[...] = s_ref[:, :, :, ML_DV:ML_DV + 8, :]
        mf_ref[...] = m_ref[...]


def _mlstm_call(latent, l, mqt, mk, mvt, mot, gt, wts, s0=None, m0=None, c_buf=None):
    assert ML_CHUNK == LANES
    b, _, s = mqt.shape
    nb = NB_ML_LAT if latent else NB_ML_CTX

    def feat(w):
        return pl.BlockSpec((nb, w, s), lambda i: (i, 0, 0))

    in_specs = [feat(512), pl.BlockSpec((nb, s, 256), lambda i: (i, 0, 0)), feat(512), feat(512),
                feat(32), pl.BlockSpec((None, 512, LANES), lambda i: (l, 0, 0))]
    args = [mqt, mk, mvt, mot, gt, wts["g_mnorm"]]
    s_spec_shape = (2, 2, ML_DV + ML_NROWS, LANES)
    m_spec_shape = (2, 8, LANES)
    scratch = [pltpu.VMEM((nb,) + s_spec_shape, F32), pltpu.VMEM((nb,) + m_spec_shape, F32),
               pltpu.VMEM((nb, 2, ML_HEADS * ML_DV, s), F32)]
    o_shape = jax.ShapeDtypeStruct((ML_HEADS * ML_DV, b * s), BF16)
    o_spec = pl.BlockSpec((ML_HEADS * ML_DV, nb * s), lambda i: (0, i))
    if latent:
        in_specs += [pl.BlockSpec((nb, None) + s_spec_shape, lambda i: (i, l, 0, 0, 0, 0)),
                     pl.BlockSpec((nb, None) + m_spec_shape, lambda i: (i, l, 0, 0, 0))]
        args += [s0, m0]
        out_shape = o_shape
        out_specs = o_spec
    aliases = {}
    if not latent:
        half = (2, 2, LANES, LANES)
        out_shape = (o_shape,
                     jax.ShapeDtypeStruct((b, DEPTH) + half, F32),
                     jax.ShapeDtypeStruct((b, 2, 2, 8, LANES), F32),
                     jax.ShapeDtypeStruct((b,) + m_spec_shape, F32))
        out_specs = (o_spec,
                     pl.BlockSpec((nb, None) + half, lambda i: (i, l, 0, 0, 0, 0)),
                     pl.BlockSpec((nb, 2, 2, 8, LANES), lambda i: (i, 0, 0, 0, 0)),
                     pl.BlockSpec((nb,) + m_spec_shape, lambda i: (i, 0, 0, 0)))
        if c_buf is not None:
            aliases[len(args)] = 1
            in_specs.append(pl.BlockSpec(memory_space=pl.ANY))
            args.append(c_buf)
    return pl.pallas_call(
        functools.partial(_mlstm_kernel, latent, s, nb, c_buf is not None),
        out_shape=out_shape, grid=(b // nb,), in_specs=in_specs, out_specs=out_specs,
        input_output_aliases=aliases,
        scratch_shapes=scratch,
        compiler_params=_cparams(("arbitrary",)),
        name="mlstm_lat" if latent else "mlstm_ctx",
    )(*args)


def _post_kernel(nt_ctx, seg_ctx, split_x, final, *refs):
    n_x = 2 if split_x else 1
    x_refs, refs = refs[:n_x], refs[n_x:]
    (oab_c, oab_l, oc_c, oc_l, wo_ref, wu_ref, wd_ref, gt1_ref, sh2_ref, sc2_ref, gt2_ref, g2_ref,
     cw_ref, cb_ref, gf_ref) = refs[:15]
    refs = refs[15:]
    if final:
        o_refs, refs = refs[:2], refs[2:]
    else:
        (cond_ref, wada_ref, bada_ref), o_refs, mods_next_ref, refs = (
            refs[:3], refs[3:4], refs[4], refs[5:])
    wo_s, wu_s, wd_s, x1_ref, h2_ref, act_ref = refs
    i = pl.program_id(0)

    @pl.when(i < N_CAST)
    def _():
        wu_s[:, pl.ds(pl.multiple_of(i * CAST_UP, CAST_UP), CAST_UP)] = wu_ref[...].astype(BF16)
        wd_s[pl.ds(pl.multiple_of(i * CAST_DOWN, CAST_DOWN), CAST_DOWN), :] = wd_ref[...].astype(BF16)

    @pl.when(i < N_CAST_OUT)
    def _():
        wo_s[pl.ds(pl.multiple_of(i * CAST_OUT, CAST_OUT), CAST_OUT), :] = wo_ref[...].astype(BF16)

    @pl.when(i >= N_CAST)
    def _():
        t = i - N_CAST
        is_ctx = t < nt_ctx
        if not final:
            mods_next_ref[...] = _ada_block(cond_ref, wada_ref, bada_ref)

        th = TM_FFN // POST_SUB
        seg = jnp.where(is_ctx, seg_ctx, GRID_W)
        row = lax.broadcasted_iota(jnp.int32, (th, FC), 0)
        first_w = (row & (GRID_W - 1)) == 0
        last_w = (row & (GRID_W - 1)) == GRID_W - 1
        seg_first = (row & (seg - 1)) == 0
        seg_last = (row & (seg - 1)) == seg - 1

        def conv(u, cs):
            prev = pltpu.roll(u, 1, 0)
            prev = jnp.where(first_w, jnp.where(seg_first, 0.0, prev), prev)
            nxt = pltpu.roll(u, th - 1, 0)
            nxt = jnp.where(last_w, jnp.where(seg_last, 0.0, nxt), nxt)
            return (cb_ref[:, cs] + prev * cw_ref[0:1, cs] + u * cw_ref[1:2, cs]
                    + nxt * cw_ref[2:3, cs])

        outs = []
        for hf in range(POST_SUB):
            rs = slice(hf * th, (hf + 1) * th)

            def pick(a, b):
                return jnp.where(is_ctx, a, b)

            x = pick(x_refs[0][rs, :], x_refs[1][rs, :]) if split_x else x_refs[0][rs, :]
            mix = (_dot(pick(oab_c[rs, :], oab_l[rs, :]), wo_s[:2 * 256, :])
                   + lax.dot_general(pick(oc_c[:, rs], oc_l[:, rs]), wo_s[2 * 256:, :],
                                     (((0,), (0,)), ((), ())), preferred_element_type=F32))
            x1 = x + gt1_ref[...] * mix
            x1_ref[rs, :] = x1
            h2 = _rms_rows(x1, g2_ref[...], D_MODEL) * (1.0 + sc2_ref[...]) + sh2_ref[...]
            h2_ref[rs, :] = h2.astype(BF16)

            for j in range(N_FC):
                vs = slice(j * FC, (j + 1) * FC)
                gs = slice(D_FF + j * FC, D_FF + (j + 1) * FC)
                val = conv(_dot(h2_ref[rs, :], wu_s[:, vs]), vs)
                gate = conv(_dot(h2_ref[rs, :], wu_s[:, gs]), gs)
                act_ref[rs, vs] = (gate * jax.nn.sigmoid(gate) * val).astype(BF16)

            x2 = x1_ref[rs, :] + gt2_ref[...] * _dot(act_ref[rs, :], wd_s[...])
            if not final:
                o_refs[0][rs, :] = x2
            else:
                outs.append(_rms_rows(x2, gf_ref[...], D_MODEL))

        if final:
            y = jnp.concatenate(outs, axis=0)

            @pl.when(is_ctx)
            def _():
                o_refs[0][...] = y

            @pl.when(jnp.logical_not(is_ctx))
            def _():
                o_refs[1][...] = y


def _post_call(l, final, xs, n_ctx, s_ctx, s_lat, oab, oc, mods, wts, ada):
    split_x = len(xs) == 2
    n_lat = oab[1].shape[0]
    nt_ctx, nt_lat = n_ctx // TM_FFN, n_lat // TM_FFN
    nt = nt_ctx + nt_lat
    tiles_per_batch = s_lat // TM_FFN
    assert s_ctx <= TM_FFN and TM_FFN % s_ctx == 0 and s_lat % TM_FFN == 0

    def tile(i):
        return jnp.maximum(i - N_CAST, 0)

    def ctx_t(i):
        return jnp.minimum(tile(i), nt_ctx - 1)

    def lat_t(i):
        return jnp.maximum(tile(i) - nt_ctx, 0)

    def mod(chunk):
        def index(i):
            row = jnp.where(tile(i) < nt_ctx, 0, 1 + lat_t(i) // tiles_per_batch)
            return (row * 6 + chunk, 0, 0)
        return pl.BlockSpec((None, 1, D_MODEL), index)

    def resident(*shape):
        nd = len(shape)
        return pl.BlockSpec((None,) + shape, lambda i: (l,) + (0,) * nd,
                            pipeline_mode=pl.Buffered(1))

    if split_x:
        x_specs = [pl.BlockSpec((TM_FFN, D_MODEL), lambda i: (ctx_t(i), 0)),
                   pl.BlockSpec((TM_FFN, D_MODEL), lambda i: (lat_t(i), 0))]
    else:
        x_specs = [pl.BlockSpec((TM_FFN, D_MODEL), lambda i: (tile(i), 0))]
    in_specs = x_specs + [
        pl.BlockSpec((TM_FFN, 512), lambda i: (ctx_t(i), 0)),
        pl.BlockSpec((TM_FFN, 512), lambda i: (lat_t(i), 0)),
        pl.BlockSpec((ML_HEADS * ML_DV, TM_FFN), lambda i: (0, ctx_t(i))),
        pl.BlockSpec((ML_HEADS * ML_DV, TM_FFN), lambda i: (0, lat_t(i))),
        pl.BlockSpec((None, CAST_OUT, D_MODEL), lambda i: (l, jnp.minimum(i, N_CAST_OUT - 1), 0)),
        pl.BlockSpec((None, D_MODEL, CAST_UP), lambda i: (l, 0, jnp.minimum(i, N_CAST - 1))),
        pl.BlockSpec((None, CAST_DOWN, D_MODEL), lambda i: (l, jnp.minimum(i, N_CAST - 1), 0)),
        mod(2), mod(3), mod(4), mod(5),
        resident(1, D_MODEL), resident(CONV_W, 2 * D_FF), resident(1, 2 * D_FF),
        pl.BlockSpec((1, D_MODEL), lambda i: (0, 0)),
    ]
    args = list(xs) + [oab[0], oab[1], oc[0], oc[1], wts["w_out"], wts["w_up"], wts["w_down"],
                       mods, mods, mods, mods, wts["g_norm2"], wts["conv_w"], wts["conv_b"],
                       wts["g_final"]]
    if final:
        out_shape = (jax.ShapeDtypeStruct((n_ctx, D_MODEL), F32),
                     jax.ShapeDtypeStruct((n_lat, D_MODEL), F32))
        out_specs = (pl.BlockSpec((TM_FFN, D_MODEL), lambda i: (ctx_t(i), 0)),
                     pl.BlockSpec((TM_FFN, D_MODEL), lambda i: (lat_t(i), 0)))
    else:
        ada_w = N_MOD * D_MODEL // nt
        assert ada_w % LANES == 0
        in_specs += [pl.BlockSpec((8, D_MODEL), lambda i: (0, 0)),
                     pl.BlockSpec((None, D_MODEL, ada_w), lambda i: (l + 1, 0, tile(i))),
                     pl.BlockSpec((None, 1, ada_w), lambda i: (l + 1, 0, tile(i)))]
        args += [ada[0], ada[1], ada[2]]
        out_shape = (jax.ShapeDtypeStruct((n_ctx + n_lat, D_MODEL), F32),
                     jax.ShapeDtypeStruct((8, N_MOD * D_MODEL), F32))
        out_specs = (pl.BlockSpec((TM_FFN, D_MODEL), lambda i: (tile(i), 0)),
                     pl.BlockSpec((8, ada_w), lambda i: (0, tile(i))))
    return pl.pallas_call(
        functools.partial(_post_kernel, nt_ctx, s_ctx, split_x, final),
        out_shape=out_shape, grid=(N_CAST + nt,), in_specs=in_specs, out_specs=out_specs,
        scratch_shapes=[pltpu.VMEM((D_MODEL, D_MODEL), BF16), pltpu.VMEM((D_MODEL, 2 * D_FF), BF16),
                        pltpu.VMEM((D_FF, D_MODEL), BF16),
                        pltpu.VMEM((TM_FFN, D_MODEL), F32), pltpu.VMEM((TM_FFN, D_MODEL), BF16),
                        pltpu.VMEM((TM_FFN, D_FF), BF16)],
        compiler_params=pltpu.CompilerParams(dimension_semantics=("arbitrary",),
                                             vmem_limit_bytes=VMEM_LIMIT_FFN),
        name="post",
    )(*args)


W_IN_BODY = (352, 2656)


def _pack_in_kernel(w_ref, o_ref):
    tc = w_ref.shape[1]

    def put(dst, blk):
        o_ref[:, dst:dst + blk.shape[0]] = blk.T.astype(BF16)

    def rows(lo, n):
        return w_ref[lo:lo + n, :]

    def zeros(n):
        return jnp.zeros((n, tc), F32)

    put(C_CQ, jnp.concatenate([rows(0, MLA_Q_RANK), zeros(C_CKV - C_CQ - MLA_Q_RANK)], axis=0))
    put(C_CKV, rows(MLA_Q_RANK, MLA_KV_RANK))
    aux = [rows(MLA_Q_RANK + MLA_KV_RANK, MLA_ROPE)]
    for g in range(4):
        aux += [rows(W_IN_BODY[1] + ML_HEADS * g, ML_HEADS), zeros(8 - ML_HEADS)]
    aux.append(zeros(LANES - AUX_GATE - 32))
    put(C_AUX, jnp.concatenate(aux, axis=0))
    for lo in range(W_IN_BODY[0], W_IN_BODY[1], 256):
        put(C_DQ + lo - W_IN_BODY[0], rows(lo, 256))


def _pack_in_call(w_in):
    w_in_t = jnp.swapaxes(w_in, 1, 2)
    tc = 256
    return pl.pallas_call(
        _pack_in_kernel,
        out_shape=jax.ShapeDtypeStruct((DEPTH, D_MODEL, NP_IN), BF16),
        grid=(DEPTH, D_MODEL // tc),
        in_specs=[pl.BlockSpec((None, w_in_t.shape[1], tc), lambda l, i: (l, 0, i))],
        out_specs=pl.BlockSpec((None, tc, NP_IN), lambda l, i: (l, i, 0)),
        compiler_params=_cparams(("arbitrary", "arbitrary")),
        name="pack_w_in",
    )(w_in_t)


def _pack_weights(w_in, g_cq, w_uq, g_ckv, w_ukv, b_gate, g_subln, g_mnorm, g_norm1, g_norm2,
                  w_out, w_up, conv_w, conv_b, w_down, g_final, lam_q1, lam_k1, lam_q2, lam_k2):
    def cols(a, lo, n, pad=0):
        blk = a[..., lo:lo + n]
        if pad:
            blk = jnp.pad(blk, [(0, 0)] * (a.ndim - 1) + [(0, pad)])
        return blk

    w_in_p = _pack_in_call(w_in)

    hd = MLA_NOPE + MLA_ROPE
    w_uq_p = jnp.pad(w_uq.reshape(DEPTH, MLA_Q_RANK, MLA_HEADS, hd),
                     [(0, 0), (0, 256 - MLA_Q_RANK), (0, 0), (0, LANES - hd)])
    w_uq_p = w_uq_p.reshape(DEPTH, 256, MLA_HEADS * LANES).astype(BF16)

    w_ukv4 = w_ukv.reshape(DEPTH, MLA_KV_RANK, MLA_HEADS, MLA_NOPE + MLA_V)
    w_k = jnp.pad(w_ukv4[..., :MLA_NOPE], [(0, 0), (0, 0), (0, 0), (0, LANES - MLA_NOPE)])
    w_k = w_k.reshape(DEPTH, MLA_KV_RANK, MLA_HEADS * LANES)
    j = jnp.arange(LANES)[:, None]
    cix = jnp.arange(MLA_HEADS * LANES)[None, :]
    place = ((j < MLA_ROPE) & ((cix % LANES) == MLA_NOPE + j)).astype(F32)
    w_kk = jnp.concatenate([w_k, jnp.broadcast_to(place, (DEPTH, LANES, MLA_HEADS * LANES))],
                           axis=1).astype(BF16)
    w_v = w_ukv4[..., MLA_NOPE:].reshape(DEPTH, MLA_KV_RANK, MLA_HEADS * MLA_V).astype(BF16)

    return dict(
        w_in=w_in_p, w_uq=w_uq_p, w_kk=w_kk, w_v=w_v,
        g_norm1=g_norm1[:, None, :], g_norm2=g_norm2[:, None, :],
        g_cq=jnp.pad(g_cq, [(0, 0), (0, 256 - MLA_Q_RANK)])[:, None, :],
        g_ckv=g_ckv[:, None, :],
        b_gate=jnp.pad(jnp.pad(b_gate.reshape(DEPTH, 4, ML_HEADS), [(0, 0), (0, 0), (0, 4)])
                       .reshape(DEPTH, 32), [(0, 0), (AUX_GATE, LANES - AUX_GATE - 32)])[:, None, :],
        g_subln=jnp.tile(g_subln, (1, DF_HEADS))[:, None, :],
        g_mnorm=jnp.broadcast_to(g_mnorm[:, :, None], (DEPTH, ML_HEADS * ML_DV, LANES)),
        w_out=w_out, w_up=w_up, w_down=w_down,
        conv_w=conv_w, conv_b=conv_b[:, None, :], g_final=g_final[None, :],
        lam_q1=lam_q1[:, None, :], lam_k1=lam_k1[:, None, :],
        lam_q2=lam_q2[:, None, :], lam_k2=lam_k2[:, None, :],
    )


def _rope_tables(n_tok):
    t = np.arange(n_tok)
    row = (t // GRID_W).astype(np.float64)
    col = (t % GRID_W).astype(np.float64)
    nf = MLA_ROPE // 4
    inv = ROPE_BASE ** (-np.arange(nf, dtype=np.float64) / nf)
    ar = row[:, None] * inv[None, :]
    ac = col[:, None] * inv[None, :]
    ang = np.concatenate([ar, ar, ac, ac], axis=-1)
    quarter = (np.arange(MLA_ROPE) // nf) % 2
    cos = jnp.asarray(np.cos(ang), F32)
    sin_up = jnp.asarray(np.where(quarter == 0, -np.sin(ang), 0.0), F32)
    sin_dn = jnp.asarray(np.where(quarter == 1, np.sin(ang), 0.0), F32)
    ones = jnp.ones((n_tok, 1), F32)
    zeros = jnp.zeros((n_tok, 1), F32)

    def head_q(t32, fill):
        blk = jnp.concatenate([jnp.tile(fill, (1, MLA_NOPE)), t32, jnp.tile(fill, (1, 32))], axis=1)
        return jnp.tile(blk, (1, MLA_HEADS))

    def aux_k(t32, fill):
        return jnp.concatenate([t32, jnp.tile(fill, (1, LANES - MLA_ROPE))], axis=1)

    tq = (head_q(cos, ones), head_q(sin_up, zeros), head_q(sin_dn, zeros))
    td = tuple(jnp.tile(a, (1, 256 // DF_DIM)) for a in (cos, sin_up, sin_dn))
    tk = (aux_k(cos, ones), aux_k(sin_up, zeros), aux_k(sin_dn, zeros))
    return tq + td + tk


def kernel(x_prompt, x_sample, cache_mla_ckv, cache_mla_krope, cache_diff_k, cache_diff_v,
           state_mlstm_C, state_mlstm_n, state_mlstm_m, c, c_ctx, w_ada, b_ada, g_norm1, w_in,
           g_cq, w_uq, g_ckv, w_ukv, lam_q1, lam_k1, lam_q2, lam_k2, g_subln, b_gate, g_mnorm,
           w_out, g_norm2, w_up, conv_w, conv_b, w_down, g_final):
    bp, sp, _ = x_prompt.shape
    bl, sl, _ = x_sample.shape
    t_len = cache_mla_ckv.shape[2]

    wts = _pack_weights(w_in, g_cq, w_uq, g_ckv, w_ukv, b_gate, g_subln, g_mnorm, g_norm1, g_norm2,
                        w_out, w_up, conv_w, conv_b, w_down, g_final, lam_q1, lam_k1, lam_q2, lam_k2)
    tables = _rope_tables(sl)

    cond = jnp.concatenate([c_ctx[None, :], c, jnp.zeros((8 - 1 - bl, D_MODEL), F32)], axis=0)
    ada = (cond, w_ada, b_ada[:, None, :])
    mods = _ada_call(*ada).reshape(8 * N_MOD, 1, D_MODEL)

    def feat_major(a):
        return jnp.transpose(a, (0, 1, 3, 4, 2)).reshape(bl, DEPTH, 256, t_len)

    kctx, vctx, cdk, cdv = _ctxkv_call(
        cache_mla_ckv, jnp.swapaxes(cache_mla_krope, 2, 3), feat_major(cache_diff_k),
        feat_major(cache_diff_v), wts["w_kk"], wts["w_v"])
    c0_t = jnp.swapaxes(state_mlstm_C.reshape(bl, DEPTH, 2, 2, LANES, ML_DV), -1, -2)
    n0_r = jnp.broadcast_to(state_mlstm_n.reshape(bl, DEPTH, 2, 2, 1, LANES),
                            (bl, DEPTH, 2, 2, ML_NROWS, LANES))
    s0 = jnp.concatenate([c0_t, n0_r], axis=-2)
    m0 = jnp.broadcast_to(jnp.pad(state_mlstm_m, [(0, 0)] * 3 + [(0, 8 - ML_HEADS)])[..., None],
                          (bl, DEPTH, 2, 8, LANES))

    n_ctx, n_lat = bp * sp, bl * sl
    xs = (x_prompt.reshape(n_ctx, D_MODEL), x_sample.reshape(n_lat, D_MODEL))
    state_bufs = [jnp.zeros((bp, DEPTH, sp, MLA_KV_RANK), F32), jnp.zeros((bp, DEPTH, MLA_ROPE, sp), F32),
                  jnp.zeros((bp, DEPTH, 256, sp), F32), jnp.zeros((bp, DEPTH, 256, sp), F32)]
    c_buf = jnp.zeros((bp, DEPTH, 2, 2, LANES, LANES), F32)
    n_col, m_col = [], []
    for l in range(DEPTH):
        lam_init = 0.8 - 0.6 * math.exp(-0.3 * l)
        final = l == DEPTH - 1
        x_ctx, x_lat, lat_off = (xs[0], xs[1], 0) if len(xs) == 2 else (xs[0], xs[0], n_ctx)
        (q, k, v, dq, dk, dv, mq, mkt, mv, mo, gt, *state_bufs) = _pre_call(
            False, l, x_ctx, 0, bp, sp, mods, wts, None, state_bufs)
        oab_c = _attn_call(False, l, lam_init, (q, k, v), (dq, dk, dv), wts)
        oc_c, c_buf, n_fin, m_fin = _mlstm_call(False, l, mq, mkt, mv, mo, gt, wts, c_buf=c_buf)
        n_col.append(n_fin[..., 0, :].reshape(bp, 2, ML_HEADS, ML_DK))
        m_col.append(m_fin[:, :, :ML_HEADS, 0])
        (q, k, v, dq, dk, dv, mq, mkt, mv, mo, gt) = _pre_call(
            True, l, x_lat, lat_off, bl, sl, mods, wts, tables)
        oab_l = _attn_call(True, l, lam_init, (q, k, v), (dq, dk, dv), wts, (kctx, vctx), (cdk, cdv))
        oc_l = _mlstm_call(True, l, mq, mkt, mv, mo, gt, wts, s0, m0)
        out = _post_call(l, final, xs, n_ctx, sp, sl,
                         (oab_c.reshape(n_ctx, 512), oab_l.reshape(n_lat, 512)), (oc_c, oc_l),
                         mods, wts, ada)
        if final:
            xs = out
        else:
            xs, mods = (out[0],), out[1].reshape(8 * N_MOD, 1, D_MODEL)

    xp = xs[0].reshape(bp, sp, D_MODEL)
    xs = xs[1].reshape(bl, sl, D_MODEL)
    ckv_all, kr_all, dk_all, dv_all = state_bufs

    def token_major(a):
        return jnp.transpose(a.reshape(bp, DEPTH, DF_HEADS, 2 * DF_DIM, sp), (0, 1, 4, 2, 3))

    return (xp, xs, ckv_all, jnp.swapaxes(kr_all, 2, 3), token_major(dk_all), token_major(dv_all),
            c_buf.reshape(bp, DEPTH, 2, ML_HEADS, ML_DK, ML_DV),
            jnp.stack(n_col, axis=1), jnp.stack(m_col, axis=1))
```

```python
import functools
import math

import jax
import jax.numpy as jnp
import numpy as np
from jax import lax
from jax.experimental import pallas as pl
from jax.experimental.pallas import tpu as pltpu

F32 = jnp.float32
BF16 = jnp.bfloat16

D_MODEL = 1024
DEPTH = 4
GRID_W = 64
N_MOD = 6
EPS = 1e-6
ROPE_BASE = 10000.0
MLA_HEADS = 4
MLA_Q_RANK = 192
MLA_KV_RANK = 128
MLA_NOPE = 64
MLA_ROPE = 32
MLA_V = 64
DF_HEADS = 4
DF_DIM = 32
ML_HEADS = 4
ML_DK = 64
ML_DV = 128
D_FF = 2816
CONV_W = 3

LANES = 128
VMEM_LIMIT = 48 * 1024 * 1024
VMEM_LIMIT_FFN = 56 * 1024 * 1024

C_CQ, C_CKV, C_AUX, C_DQ, C_DK, C_DV, C_MQ, C_MK, C_MV, C_MO = (
    0, 256, 384, 512, 768, 1024, 1280, 1536, 1792, 2304)
NP_IN = 2816
AUX_GATE = 32

TM_PRE = 256
PRE_SUB = 2
TQ = 512
NB_CTX = 4
ML_CHUNK = 128
ML_NROWS = 128
ML_STATIC_CHUNKS = 8
NB_ML_CTX = 4
NB_ML_LAT = 2
TM_FFN = 512
POST_SUB = 1
FC = 256
N_FC = D_FF // FC
N_CAST = 22
CAST_UP = 2 * D_FF // N_CAST
CAST_DOWN = D_FF // N_CAST
CAST_OUT = 128
N_CAST_OUT = D_MODEL // CAST_OUT

NT = (((1,), (1,)), ((), ()))
LOG2E = 1.4426950408889634


def _cparams(sem):
    return pltpu.CompilerParams(dimension_semantics=sem, vmem_limit_bytes=VMEM_LIMIT)


def _dot(a, b):
    return jnp.dot(a, b, preferred_element_type=F32)


def _dot_nt(a, b):
    return lax.dot_general(a, b, NT, preferred_element_type=F32)


def _rms_rows(x, g, n):
    ms = jnp.sum(x * x, axis=-1, keepdims=True) * (1.0 / n)
    return x * lax.rsqrt(ms + EPS) * g


def _rope(x, cos, sin_up, sin_dn):
    w = x.shape[-1]
    return x * cos + pltpu.roll(x, w - 8, 1) * sin_up + pltpu.roll(x, 8, 1) * sin_dn


def _ada_block(c_ref, w_ref, b_ref):
    c = c_ref[...]
    s = (c * jax.nn.sigmoid(c)).astype(BF16)
    return _dot(s, w_ref[...].astype(BF16)) + b_ref[...]


def _ada_kernel(c_ref, w_ref, b_ref, o_ref):
    o_ref[...] = _ada_block(c_ref, w_ref, b_ref)


def _ada_call(cond, w_ada, b_ada):
    nt = 1024
    return pl.pallas_call(
        _ada_kernel,
        out_shape=jax.ShapeDtypeStruct((8, N_MOD * D_MODEL), F32),
        grid=(N_MOD * D_MODEL // nt,),
        in_specs=[pl.BlockSpec((8, D_MODEL), lambda j: (0, 0)),
                  pl.BlockSpec((None, D_MODEL, nt), lambda j: (0, 0, j)),
                  pl.BlockSpec((None, 1, nt), lambda j: (0, 0, j))],
        out_specs=pl.BlockSpec((8, nt), lambda j: (0, j)),
        compiler_params=_cparams(("arbitrary",)),
        name="ada_mod",
    )(cond, w_ada, b_ada)


def _ctxkv_kernel(ckv_ref, krt_ref, dkt_ref, dvt_ref, wkk_ref, wv_ref, k_ref, v_ref, dkb_ref, dvb_ref):
    t = ckv_ref.shape[0]
    ckv = ckv_ref[...].astype(BF16)
    kr = jnp.concatenate([krt_ref[...], jnp.zeros((LANES - MLA_ROPE, t), F32)], axis=0).T
    kin = jnp.concatenate([ckv, kr.astype(BF16)], axis=1)
    k_ref[...] = _dot(kin, wkk_ref[...]).T.astype(BF16)
    v_ref[...] = _dot(ckv, wv_ref[...]).astype(BF16)
    dkb_ref[...] = dkt_ref[...].astype(BF16)
    dvb_ref[...] = dvt_ref[...].T.astype(BF16)


def _ctxkv_call(cache_ckv, cache_kr_t, cache_dk_t, cache_dv_t, wkk, wv):
    b, _, t, _ = cache_ckv.shape

    def cache_t(w):
        return pl.BlockSpec((None, None, w, t), lambda l, i: (i, l, 0, 0))

    def out(w):
        return pl.BlockSpec((None, None, t, w), lambda l, i: (l, i, 0, 0))

    def out_t(w):
        return pl.BlockSpec((None, None, w, t), lambda l, i: (l, i, 0, 0))

    return pl.pallas_call(
        _ctxkv_kernel,
        out_shape=(jax.ShapeDtypeStruct((DEPTH, b, 512, t), BF16),
                   jax.ShapeDtypeStruct((DEPTH, b, t, 256), BF16),
                   jax.ShapeDtypeStruct((DEPTH, b, 256, t), BF16),
                   jax.ShapeDtypeStruct((DEPTH, b, t, 256), BF16)),
        grid=(DEPTH, b),
        in_specs=[pl.BlockSpec((None, None, t, MLA_KV_RANK), lambda l, i: (i, l, 0, 0)),
                  cache_t(MLA_ROPE), cache_t(256), cache_t(256),
                  pl.BlockSpec((None, 256, 512), lambda l, i: (l, 0, 0)),
                  pl.BlockSpec((None, MLA_KV_RANK, 256), lambda l, i: (l, 0, 0))],
        out_specs=(out_t(512), out(256), out_t(256), out(256)),
        compiler_params=_cparams(("arbitrary", "arbitrary")),
        name="ctx_kv",
    )(cache_ckv, cache_kr_t, cache_dk_t, cache_dv_t, wkk, wv)


def _pre_kernel(latent, kinds, *refs):
    for n in range(PRE_SUB):
        views = []
        for kind, r in zip(kinds, refs):
            if kind == "alias":
                continue
            if kind == "rows":
                r = r.at[n * TM_PRE:(n + 1) * TM_PRE]
            elif kind == "lanes":
                r = r.at[:, n * TM_PRE:(n + 1) * TM_PRE]
            elif kind == "batch":
                r = r.at[n]
            views.append(r)
        _pre_tile(latent, *views)


def _pre_tile(latent, *refs):
    (x_ref, sh_ref, sc_ref, g1_ref, win_ref, gcq_ref, wuq_ref, gckv_ref, wkk_ref, wv_ref,
     bg_ref) = refs[:11]
    refs = refs[11:]
    if latent:
        (cq_t, sqa_t, sqb_t, cd_t, sda_t, sdb_t, ck_t, ska_t, skb_t) = refs[:9]
        refs = refs[9:]
    (q_ref, k_ref, v_ref, dq_ref, dk_ref, dv_ref, mq_ref, mk_ref, mv_ref, mo_ref,
     gt_ref) = refs[:11]
    refs = refs[11:]
    if not latent:
        ckv_out, kr_out, dk_out, dv_out = refs

    x = x_ref[...]
    h = _rms_rows(x, g1_ref[...], D_MODEL) * (1.0 + sc_ref[...]) + sh_ref[...]
    proj = _dot(h.astype(BF16), win_ref[...])

    cq = _rms_rows(proj[:, C_CQ:C_CQ + 256], gcq_ref[...], MLA_Q_RANK)
    q = _dot(cq.astype(BF16), wuq_ref[...])
    if latent:
        q = _rope(q, cq_t[...], sqa_t[...], sqb_t[...])
    q_ref[...] = (q * ((MLA_NOPE + MLA_ROPE) ** -0.5 * LOG2E)).astype(BF16)

    c_kv = _rms_rows(proj[:, C_CKV:C_CKV + MLA_KV_RANK], gckv_ref[...], MLA_KV_RANK)
    aux = proj[:, C_AUX:C_AUX + LANES] + bg_ref[...]
    if latent:
        aux = _rope(aux, ck_t[...], ska_t[...], skb_t[...])
    aux_t = aux.T
    gt_ref[...] = aux_t[AUX_GATE:AUX_GATE + 32, :]
    if not latent:
        ckv_out[...] = c_kv
        kr_out[...] = aux_t[:MLA_ROPE, :]
    ckv_b = c_kv.astype(BF16)
    kin = jnp.concatenate([ckv_b, aux.astype(BF16)], axis=1)
    k_ref[...] = _dot(kin, wkk_ref[...]).T.astype(BF16)
    v_ref[...] = _dot(ckv_b, wv_ref[...]).astype(BF16)

    dq = proj[:, C_DQ:C_DQ + 256]
    dk = proj[:, C_DK:C_DK + 256]
    dv = proj[:, C_DV:C_DV + 256]
    if not latent:
        dk_t = dk.T
        dk_out[...] = dk_t
        dv_out[...] = dv.T
    else:
        dq = _rope(dq, cd_t[...], sda_t[...], sdb_t[...])
        dk_t = _rope(dk, cd_t[...], sda_t[...], sdb_t[...]).T
    dq_ref[...] = (dq * (DF_DIM ** -0.5 * LOG2E)).astype(BF16)
    dk_ref[...] = dk_t.astype(BF16)
    dv_ref[...] = dv.astype(BF16)

    lane = lax.broadcasted_iota(jnp.int32, (x.shape[0], LANES), 1)
    for h in range(ML_HEADS):
        blk = proj[:, C_MQ + (h // 2) * LANES:C_MQ + (h // 2 + 1) * LANES]
        mq_ref[h * LANES:(h + 1) * LANES, :] = (
            jnp.where((lane >> 6) == h % 2, blk, 0.0).T.astype(BF16))
    mk_ref[...] = (proj[:, C_MK:C_MK + 256] * (ML_DK ** -0.5)).astype(BF16)
    mv_ref[...] = proj[:, C_MV:C_MV + 512].T.astype(BF16)
    mo_ref[...] = proj[:, C_MO:C_MO + 512].T


def _pre_call(latent, l, x2d, tok_off, b, s, mods, wts, tables, state_bufs=None):
    ns = s // TM_PRE
    tile_off = tok_off // TM_PRE
    tm2 = PRE_SUB * TM_PRE
    seq_split = ns > 1
    assert (ns % PRE_SUB == 0) if seq_split else (b % PRE_SUB == 0 and ns == 1)
    assert tile_off % PRE_SUB == 0
    grid = (ns // PRE_SUB, b) if seq_split else (1, b // PRE_SUB)

    def tok(width):
        if seq_split:
            return pl.BlockSpec((None, tm2, width), lambda j, i: (i, j, 0)), "rows"
        return pl.BlockSpec((PRE_SUB, TM_PRE, width), lambda j, i: (i, j, 0)), "batch"

    def feat(width):
        if seq_split:
            return pl.BlockSpec((None, width, tm2), lambda j, i: (i, 0, j)), "lanes"
        return pl.BlockSpec((PRE_SUB, width, TM_PRE), lambda j, i: (i, 0, j)), "batch"

    def mod(chunk):
        if latent:
            return pl.BlockSpec((None, 1, D_MODEL), lambda j, i: ((1 + i) * 6 + chunk, 0, 0))
        return pl.BlockSpec((None, 1, D_MODEL), lambda j, i: (chunk, 0, 0))

    def lw(*shape):
        nd = len(shape)
        return pl.BlockSpec((None,) + shape, lambda j, i: (l,) + (0,) * nd)

    if seq_split:
        x_spec = pl.BlockSpec((tm2, D_MODEL), lambda j, i: ((tile_off + i * ns) // PRE_SUB + j, 0))
    else:
        x_spec = pl.BlockSpec((tm2, D_MODEL), lambda j, i: (tile_off // PRE_SUB + i, 0))
    in_specs = [x_spec, mod(0), mod(1), lw(1, D_MODEL), lw(D_MODEL, NP_IN), lw(1, 256),
                lw(256, 512), lw(1, MLA_KV_RANK), lw(256, 512), lw(MLA_KV_RANK, 256), lw(1, LANES)]
    kinds = ["rows"] + [None] * 10
    args = [x2d, mods, mods, wts["g_norm1"], wts["w_in"], wts["g_cq"], wts["w_uq"], wts["g_ckv"],
            wts["w_kk"], wts["w_v"], wts["b_gate"]]
    if latent:
        for t in tables:
            in_specs.append(pl.BlockSpec((tm2, t.shape[1]), lambda j, i: (j, 0)))
            kinds.append("rows")
            args.append(t)

    widths = [(512, BF16, False), (512, BF16, True), (256, BF16, False), (256, BF16, False),
              (256, BF16, True), (256, BF16, False), (512, BF16, True), (256, BF16, False),
              (512, BF16, True), (512, F32, True), (32, F32, True)]
    out_shape = [jax.ShapeDtypeStruct((b, w, s) if tr else (b, s, w), dt) for w, dt, tr in widths]
    out_pairs = [feat(w) if tr else tok(w) for w, _, tr in widths]
    aliases = {}
    if not latent:
        state = [(MLA_KV_RANK, False), (MLA_ROPE, True), (256, True), (256, True)]
        for k, (w, tr) in enumerate(state):
            if tr:
                spec = pl.BlockSpec((PRE_SUB, None, w, TM_PRE), lambda j, i: (i, l, 0, j))
                shape = (b, DEPTH, w, s)
            else:
                spec = pl.BlockSpec((PRE_SUB, None, TM_PRE, w), lambda j, i: (i, l, j, 0))
                shape = (b, DEPTH, s, w)
            out_shape.append(jax.ShapeDtypeStruct(shape, F32))
            out_pairs.append((spec, "batch"))
            if state_bufs is not None:
                aliases[len(args)] = len(widths) + k
                in_specs.append(pl.BlockSpec(memory_space=pl.ANY))
                kinds.append("alias")
                args.append(state_bufs[k])
    out_specs = tuple(p[0] for p in out_pairs)
    kinds += [p[1] for p in out_pairs]
    return pl.pallas_call(
        functools.partial(_pre_kernel, latent, tuple(kinds)),
        out_shape=tuple(out_shape), grid=grid, in_specs=in_specs, out_specs=out_specs,
        input_output_aliases=aliases,
        compiler_params=_cparams(("arbitrary", "arbitrary")),
        name="pre_lat" if latent else "pre_ctx",
    )(*args)


def _softmax_parts(s_list):
    m = functools.reduce(jnp.maximum, [jnp.max(s, axis=1, keepdims=True) for s in s_list])
    p_list = [jnp.exp2(s - m) for s in s_list]
    l = functools.reduce(jnp.add, [jnp.sum(p, axis=1, keepdims=True) for p in p_list])
    return p_list, l


def _attn_kernel(latent, lam_init, nb, *refs):
    n_seg = 5 if latent else 3
    o_ref = refs[-1]
    mla_in, diff_in, shared = refs[:n_seg], refs[n_seg:2 * n_seg], refs[2 * n_seg:-1]
    for n in range(nb):
        _mla_body(latent, *[r.at[n] for r in mla_in], o_ref.at[n])
        _diff_body(latent, lam_init, *[r.at[n] for r in diff_in], *shared, o_ref.at[n])


def _mla_body(latent, *refs):
    if latent:
        q_ref, k_ref, v_ref, kc_ref, vc_ref, o_ref = refs
        segs = [(kc_ref, vc_ref), (k_ref, v_ref)]
    else:
        q_ref, k_ref, v_ref, o_ref = refs
        segs = [(k_ref, v_ref)]
    lane = lax.broadcasted_iota(jnp.int32, (q_ref.shape[0], LANES), 1)

    def scores(h):
        hs = slice(h * LANES, (h + 1) * LANES)
        return [_dot(q_ref[:, hs], kr[hs, :]) for kr, _ in segs]

    outs = []
    s_next = scores(0)
    for h in range(MLA_HEADS):
        ps = slice((h // 2) * LANES, (h // 2 + 1) * LANES)
        s_list = s_next
        if h + 1 < MLA_HEADS:
            s_next = scores(h + 1)
        p_list, l = _softmax_parts(s_list)
        pv = functools.reduce(jnp.add, [_dot(p.astype(BF16), vr[:, ps])
                                        for p, (_, vr) in zip(p_list, segs)])
        outs.append(pv / l)
    o_ref[:, 0:LANES] = jnp.where(lane < MLA_V, outs[0], outs[1]).astype(BF16)
    o_ref[:, LANES:2 * LANES] = jnp.where(lane < MLA_V, outs[2], outs[3]).astype(BF16)


def _diff_body(latent, lam_init, *refs):
    if latent:
        (q_ref, k_ref, v_ref, kc_ref, vc_ref, lq1, lk1, lq2, lk2, g_ref, o_ref) = refs
        segs = [(kc_ref, vc_ref), (k_ref, v_ref)]
    else:
        (q_ref, k_ref, v_ref, lq1, lk1, lq2, lk2, g_ref, o_ref) = refs
        segs = [(k_ref, v_ref)]
    lam = (jnp.exp(jnp.sum(lq1[...] * lk1[...], axis=1, keepdims=True))
           - jnp.exp(jnp.sum(lq2[...] * lk2[...], axis=1, keepdims=True)) + lam_init)
    lane = lax.broadcasted_iota(jnp.int32, (q_ref.shape[0], LANES), 1)
    grp = lane >> 5
    qf = q_ref[...].astype(F32)

    def scores(u):
        h, c = u // 2, u % 2
        ps = slice((h // 2) * LANES, (h // 2 + 1) * LANES)
        qm = jnp.where(grp == 2 * (h % 2) + c, qf[:, ps], 0.0).astype(BF16)
        return [_dot(qm, kr[ps, :]) for kr, _ in segs]

    outs = []
    s_next = scores(0)
    for h in range(DF_HEADS):
        ps = slice((h // 2) * LANES, (h // 2 + 1) * LANES)
        hh = h % 2
        parts = []
        for c in range(2):
            s_list = s_next
            if 2 * h + c + 1 < 2 * DF_HEADS:
                s_next = scores(2 * h + c + 1)
            parts.append(_softmax_parts(s_list))
        (p0, l0), (p1, l1) = parts
        ratio = lam * l0 / l1
        pv = functools.reduce(jnp.add, [
            _dot((a0 - a1 * ratio).astype(BF16), vr[:, ps])
            for a0, a1, (_, vr) in zip(p0, p1, segs)]) * (1.0 / l0)
        valid = (lane >> 6) == hh
        ms = jnp.sum(jnp.where(valid, pv * pv, 0.0), axis=1, keepdims=True) * (1.0 / (2 * DF_DIM))
        outs.append(pv * lax.rsqrt(ms + EPS) * g_ref[:, ps] * (1.0 - lam_init))
    o_ref[:, 2 * LANES:3 * LANES] = jnp.where(lane < 2 * DF_DIM, outs[0], outs[1]).astype(BF16)
    o_ref[:, 3 * LANES:4 * LANES] = jnp.where(lane < 2 * DF_DIM, outs[2], outs[3]).astype(BF16)


def _attn_call(latent, l, lam_init, qkv, dqkv, wts, ctx=None, dctx=None):
    b, s, _ = qkv[0].shape
    nb = 1 if latent else NB_CTX
    tq = min(TQ, s)
    grid = (b // nb, s // tq)

    def group(wq, wk, wv, ctx_pair):
        specs = [pl.BlockSpec((nb, tq, wq), lambda i, j: (i, j, 0)),
                 pl.BlockSpec((nb, wk, s), lambda i, j: (i, 0, 0)),
                 pl.BlockSpec((nb, s, wv), lambda i, j: (i, 0, 0))]
        if latent:
            t = ctx_pair[1].shape[2]
            specs += [pl.BlockSpec((None, nb, wk, t), lambda i, j: (l, i, 0, 0)),
                      pl.BlockSpec((None, nb, t, wv), lambda i, j: (l, i, 0, 0))]
        return specs

    in_specs = group(512, 512, 256, ctx) + group(256, 256, 256, dctx)
    args = list(qkv) + (list(ctx) if latent else []) + list(dqkv) + (list(dctx) if latent else [])
    for name in ("lam_q1", "lam_k1", "lam_q2", "lam_k2"):
        in_specs.append(pl.BlockSpec((None, 1, DF_DIM), lambda i, j: (l, 0, 0)))
        args.append(wts[name])
    in_specs.append(pl.BlockSpec((None, 1, 256), lambda i, j: (l, 0, 0)))
    args.append(wts["g_subln"])
    return pl.pallas_call(
        functools.partial(_attn_kernel, latent, lam_init, nb),
        out_shape=jax.ShapeDtypeStruct((b, s, 512), BF16),
        grid=grid, in_specs=in_specs,
        out_specs=pl.BlockSpec((nb, tq, 512), lambda i, j: (i, j, 0)),
        compiler_params=_cparams(("arbitrary", "arbitrary")),
        name="attn_lat" if latent else "attn_ctx",
    )(*args)


def _log_sigmoid(x):
    return jnp.minimum(x, 0.0) - jnp.log1p(jnp.exp(-jnp.abs(x)))


def _mlstm_chunk(d, c, mq_ref, mk_ref, mv_ref, gt_ref, s_ref, m_ref, h_ref):
    L = ML_CHUNK
    rows = slice(c * L, (c + 1) * L) if isinstance(c, int) else pl.ds(pl.multiple_of(c * L, L), L)
    s_i = lax.broadcasted_iota(jnp.int32, (L, L), 0)
    t_i = lax.broadcasted_iota(jnp.int32, (L, L), 1)
    mask = (s_i <= t_i) if d == 0 else (s_i >= t_i)
    tri = jnp.where(mask, 1.0, 0.0).astype(BF16)

    ig = gt_ref[16 * d:16 * d + 8, rows]
    lf = _log_sigmoid(gt_ref[16 * d + 8:16 * d + 16, rows])
    hi = lf.astype(BF16).astype(F32)
    r1 = lf - hi
    mid = r1.astype(BF16).astype(F32)
    parts = _dot(jnp.concatenate([hi, mid, r1 - mid], axis=0).astype(BF16), tri)
    bc = parts[0:8] + parts[8:16] + parts[16:24]
    rvec = ig - bc
    total = jnp.sum(lf, axis=1, keepdims=True)
    mm = m_ref[d]
    gvec = total + rvec
    m_new = jnp.maximum(total + mm, jnp.max(gvec, axis=1, keepdims=True))
    ws = jnp.exp(gvec - m_new).astype(BF16)
    cdec = jnp.exp(total + mm - m_new)
    m_ref[d] = m_new

    rv_t = jnp.concatenate([rvec, jnp.zeros((LANES - 8, L), F32)], axis=0).T

    s_old = [s_ref[d, pair].astype(BF16) for pair in range(2)]
    upd = []
    for h in range(ML_HEADS):
        pair = h // 2
        hs = slice(h * ML_DV, (h + 1) * ML_DV)
        qt = mq_ref[hs, rows]
        kp = mk_ref[rows, pair * LANES:(pair + 1) * LANES]
        vt = mv_ref[hs, rows]
        mmh = mm[h:h + 1, 0:1]
        rm = jnp.where(mask, rv_t[:, h:h + 1], -jnp.inf)
        a = jnp.maximum(jnp.max(rm, axis=0, keepdims=True), mmh)
        wqk = jnp.exp(rm - a) * _dot(kp, qt)
        dec = jnp.exp(mmh - a)
        qc = _dot(s_old[pair], qt)
        num = _dot(vt, wqk.astype(BF16)) + dec * qc[:ML_DV]
        den = jnp.sum(wqk, axis=0, keepdims=True) + dec * qc[ML_DV:ML_DV + 1]
        inv = 1.0 / jnp.maximum(jnp.abs(den), jnp.exp(-(a + bc[h:h + 1, :])))
        h_ref[d, hs, rows] = num * inv
        wsr = ws[h:h + 1, :]
        vaug = jnp.concatenate([vt * wsr, jnp.broadcast_to(wsr, (ML_NROWS, L))], axis=0)
        upd.append(_dot(vaug, kp))

    low = lax.broadcasted_iota(jnp.int32, (ML_DV + ML_NROWS, LANES), 1) < ML_DK
    for pair in range(2):
        h0, h1 = 2 * pair, 2 * pair + 1
        cd = jnp.where(low, cdec[h0:h0 + 1, 0:1], cdec[h1:h1 + 1, 0:1])
        s_ref[d, pair] = cd * s_ref[d, pair] + jnp.where(low, upd[h0], upd[h1])


def _mlstm_kernel(latent, seq, nb, has_alias, *refs):
    if latent:
        (mq_ref, mk_ref, mv_ref, mo_ref, gt_ref, g_ref, s0_ref, m0_ref,
         o_ref, s_ref, m_ref, h_ref) = refs
        s_ref[...] = s0_ref[...]
        m_ref[...] = m0_ref[...]
    else:
        if has_alias:
            refs = refs[:6] + refs[7:]
        (mq_ref, mk_ref, mv_ref, mo_ref, gt_ref, g_ref,
         o_ref, cf_ref, nf_ref, mf_ref, s_ref, m_ref, h_ref) = refs
        s_ref[...] = jnp.zeros(s_ref.shape, F32)
        m_ref[...] = jnp.zeros(m_ref.shape, F32)
    nc = seq // ML_CHUNK

    def body(j, carry):
        for n in range(nb):
            views = (mq_ref.at[n], mk_ref.at[n], mv_ref.at[n], gt_ref.at[n], s_ref.at[n],
                     m_ref.at[n], h_ref.at[n])
            _mlstm_chunk(0, j, *views)
            _mlstm_chunk(1, nc - 1 - j, *views)
        return carry

    if nc <= ML_STATIC_CHUNKS:
        for j in range(nc):
            body(j, 0)
    else:
        lax.fori_loop(0, nc, body, 0, unroll=min(nc, 4))

    for n in range(nb):
        for h in range(ML_HEADS):
            hs = slice(h * ML_DV, (h + 1) * ML_DV)
            for j in range(nc):
                ts = slice(j * LANES, (j + 1) * LANES)
                hsum = h_ref[n, 0, hs, ts] + h_ref[n, 1, hs, ts]
                ms = jnp.sum(hsum * hsum, axis=0, keepdims=True) * (1.0 / ML_DV)
                y = hsum * lax.rsqrt(ms + EPS) * g_ref[hs, :]
                o_ref[hs, n * seq + j * LANES:n * seq + (j + 1) * LANES] = (
                    jax.nn.sigmoid(mo_ref[n, hs, ts]) * y).astype(BF16)
    if not latent:
        for n in range(nb):
            for d in range(2):
                for pair in range(2):
                    cf_ref[n, d, pair] = s_ref[n, d, pair, :ML_DV, :].T
        nf_ref[...] = s_ref[:, :, :, ML_DV:ML_DV + 8, :]
        mf_ref[...] = m_ref[...]


def _mlstm_call(latent, l, mqt, mk, mvt, mot, gt, wts, s0=None, m0=None, c_buf=None):
    assert ML_CHUNK == LANES
    b, _, s = mqt.shape
    nb = NB_ML_LAT if latent else NB_ML_CTX

    def feat(w):
        return pl.BlockSpec((nb, w, s), lambda i: (i, 0, 0))

    in_specs = [feat(512), pl.BlockSpec((nb, s, 256), lambda i: (i, 0, 0)), feat(512), feat(512),
                feat(32), pl.BlockSpec((None, 512, LANES), lambda i: (l, 0, 0))]
    args = [mqt, mk, mvt, mot, gt, wts["g_mnorm"]]
    s_spec_shape = (2, 2, ML_DV + ML_NROWS, LANES)
    m_spec_shape = (2, 8, LANES)
    scratch = [pltpu.VMEM((nb,) + s_spec_shape, F32), pltpu.VMEM((nb,) + m_spec_shape, F32),
               pltpu.VMEM((nb, 2, ML_HEADS * ML_DV, s), F32)]
    o_shape = jax.ShapeDtypeStruct((ML_HEADS * ML_DV, b * s), BF16)
    o_spec = pl.BlockSpec((ML_HEADS * ML_DV, nb * s), lambda i: (0, i))
    if latent:
        in_specs += [pl.BlockSpec((nb, None) + s_spec_shape, lambda i: (i, l, 0, 0, 0, 0)),
                     pl.BlockSpec((nb, None) + m_spec_shape, lambda i: (i, l, 0, 0, 0))]
        args += [s0, m0]
        out_shape = o_shape
        out_specs = o_spec
    aliases = {}
    if not latent:
        half = (2, 2, LANES, LANES)
        out_shape = (o_shape,
                     jax.ShapeDtypeStruct((b, DEPTH) + half, F32),
                     jax.ShapeDtypeStruct((b, 2, 2, 8, LANES), F32),
                     jax.ShapeDtypeStruct((b,) + m_spec_shape, F32))
        out_specs = (o_spec,
                     pl.BlockSpec((nb, None) + half, lambda i: (i, l, 0, 0, 0, 0)),
                     pl.BlockSpec((nb, 2, 2, 8, LANES), lambda i: (i, 0, 0, 0, 0)),
                     pl.BlockSpec((nb,) + m_spec_shape, lambda i: (i, 0, 0, 0)))
        if c_buf is not None:
            aliases[len(args)] = 1
            in_specs.append(pl.BlockSpec(memory_space=pl.ANY))
            args.append(c_buf)
    return pl.pallas_call(
        functools.partial(_mlstm_kernel, latent, s, nb, c_buf is not None),
        out_shape=out_shape, grid=(b // nb,), in_specs=in_specs, out_specs=out_specs,
        input_output_aliases=aliases,
        scratch_shapes=scratch,
        compiler_params=_cparams(("arbitrary",)),
        name="mlstm_lat" if latent else "mlstm_ctx",
    )(*args)


def _post_kernel(nt_ctx, seg_ctx, split_x, final, *refs):
    n_x = 2 if split_x else 1
    x_refs, refs = refs[:n_x], refs[n_x:]
    (oab_c, oab_l, oc_c, oc_l, wo_ref, wu_ref, wd_ref, gt1_ref, sh2_ref, sc2_ref, gt2_ref, g2_ref,
     cw_ref, cb_ref, gf_ref) = refs[:15]
    refs = refs[15:]
    if final:
        o_refs, refs = refs[:2], refs[2:]
    else:
        (cond_ref, wada_ref, bada_ref), o_refs, mods_next_ref, refs = (
            refs[:3], refs[3:4], refs[4], refs[5:])
    wo_s, wu_s, wd_s, x1_ref, h2_ref, act_ref = refs
    i = pl.program_id(0)

    @pl.when(i < N_CAST)
    def _():
        wu_s[:, pl.ds(pl.multiple_of(i * CAST_UP, CAST_UP), CAST_UP)] = wu_ref[...].astype(BF16)
        wd_s[pl.ds(pl.multiple_of(i * CAST_DOWN, CAST_DOWN), CAST_DOWN), :] = wd_ref[...].astype(BF16)

    @pl.when(i < N_CAST_OUT)
    def _():
        wo_s[pl.ds(pl.multiple_of(i * CAST_OUT, CAST_OUT), CAST_OUT), :] = wo_ref[...].astype(BF16)

    @pl.when(i >= N_CAST)
    def _():
        t = i - N_CAST
        is_ctx = t < nt_ctx
        if not final:
            mods_next_ref[...] = _ada_block(cond_ref, wada_ref, bada_ref)

        th = TM_FFN // POST_SUB
        seg = jnp.where(is_ctx, seg_ctx, GRID_W)
        row = lax.broadcasted_iota(jnp.int32, (th, FC), 0)
        first_w = (row & (GRID_W - 1)) == 0
        last_w = (row & (GRID_W - 1)) == GRID_W - 1
        seg_first = (row & (seg - 1)) == 0
        seg_last = (row & (seg - 1)) == seg - 1

        def conv(u, cs):
            prev = pltpu.roll(u, 1, 0)
            prev = jnp.where(first_w, jnp.where(seg_first, 0.0, prev), prev)
            nxt = pltpu.roll(u, th - 1, 0)
            nxt = jnp.where(last_w, jnp.where(seg_last, 0.0, nxt), nxt)
            return (cb_ref[:, cs] + prev * cw_ref[0:1, cs] + u * cw_ref[1:2, cs]
                    + nxt * cw_ref[2:3, cs])

        outs = []
        for hf in range(POST_SUB):
            rs = slice(hf * th, (hf + 1) * th)

            def pick(a, b):
                return jnp.where(is_ctx, a, b)

            x = pick(x_refs[0][rs, :], x_refs[1][rs, :]) if split_x else x_refs[0][rs, :]
            mix = (_dot(pick(oab_c[rs, :], oab_l[rs, :]), wo_s[:2 * 256, :])
                   + lax.dot_general(pick(oc_c[:, rs], oc_l[:, rs]), wo_s[2 * 256:, :],
                                     (((0,), (0,)), ((), ())), preferred_element_type=F32))
            x1 = x + gt1_ref[...] * mix
            x1_ref[rs, :] = x1
            h2 = _rms_rows(x1, g2_ref[...], D_MODEL) * (1.0 + sc2_ref[...]) + sh2_ref[...]
            h2_ref[rs, :] = h2.astype(BF16)

            for j in range(N_FC):
                vs = slice(j * FC, (j + 1) * FC)
                gs = slice(D_FF + j * FC, D_FF + (j + 1) * FC)
                val = conv(_dot(h2_ref[rs, :], wu_s[:, vs]), vs)
                gate = conv(_dot(h2_ref[rs, :], wu_s[:, gs]), gs)
                act_ref[rs, vs] = (gate * jax.nn.sigmoid(gate) * val).astype(BF16)

            x2 = x1_ref[rs, :] + gt2_ref[...] * _dot(act_ref[rs, :], wd_s[...])
            if not final:
                o_refs[0][rs, :] = x2
            else:
                outs.append(_rms_rows(x2, gf_ref[...], D_MODEL))

        if final:
            y = jnp.concatenate(outs, axis=0)

            @pl.when(is_ctx)
            def _():
                o_refs[0][...] = y

            @pl.when(jnp.logical_not(is_ctx))
            def _():
                o_refs[1][...] = y


def _post_call(l, final, xs, n_ctx, s_ctx, s_lat, oab, oc, mods, wts, ada):
    split_x = len(xs) == 2
    n_lat = oab[1].shape[0]
    nt_ctx, nt_lat = n_ctx // TM_FFN, n_lat // TM_FFN
    nt = nt_ctx + nt_lat
    tiles_per_batch = s_lat // TM_FFN
    assert s_ctx <= TM_FFN and TM_FFN % s_ctx == 0 and s_lat % TM_FFN == 0

    def tile(i):
        return jnp.maximum(i - N_CAST, 0)

    def ctx_t(i):
        return jnp.minimum(tile(i), nt_ctx - 1)

    def lat_t(i):
        return jnp.maximum(tile(i) - nt_ctx, 0)

    def mod(chunk):
        def index(i):
            row = jnp.where(tile(i) < nt_ctx, 0, 1 + lat_t(i) // tiles_per_batch)
            return (row * 6 + chunk, 0, 0)
        return pl.BlockSpec((None, 1, D_MODEL), index)

    def resident(*shape):
        nd = len(shape)
        return pl.BlockSpec((None,) + shape, lambda i: (l,) + (0,) * nd,
                            pipeline_mode=pl.Buffered(1))

    if split_x:
        x_specs = [pl.BlockSpec((TM_FFN, D_MODEL), lambda i: (ctx_t(i), 0)),
                   pl.BlockSpec((TM_FFN, D_MODEL), lambda i: (lat_t(i), 0))]
    else:
        x_specs = [pl.BlockSpec((TM_FFN, D_MODEL), lambda i: (tile(i), 0))]
    in_specs = x_specs + [
        pl.BlockSpec((TM_FFN, 512), lambda i: (ctx_t(i), 0)),
        pl.BlockSpec((TM_FFN, 512), lambda i: (lat_t(i), 0)),
        pl.BlockSpec((ML_HEADS * ML_DV, TM_FFN), lambda i: (0, ctx_t(i))),
        pl.BlockSpec((ML_HEADS * ML_DV, TM_FFN), lambda i: (0, lat_t(i))),
        pl.BlockSpec((None, CAST_OUT, D_MODEL), lambda i: (l, jnp.minimum(i, N_CAST_OUT - 1), 0)),
        pl.BlockSpec((None, D_MODEL, CAST_UP), lambda i: (l, 0, jnp.minimum(i, N_CAST - 1))),
        pl.BlockSpec((None, CAST_DOWN, D_MODEL), lambda i: (l, jnp.minimum(i, N_CAST - 1), 0)),
        mod(2), mod(3), mod(4), mod(5),
        resident(1, D_MODEL), resident(CONV_W, 2 * D_FF), resident(1, 2 * D_FF),
        pl.BlockSpec((1, D_MODEL), lambda i: (0, 0)),
    ]
    args = list(xs) + [oab[0], oab[1], oc[0], oc[1], wts["w_out"], wts["w_up"], wts["w_down"],
                       mods, mods, mods, mods, wts["g_norm2"], wts["conv_w"], wts["conv_b"],
                       wts["g_final"]]
    if final:
        out_shape = (jax.ShapeDtypeStruct((n_ctx, D_MODEL), F32),
                     jax.ShapeDtypeStruct((n_lat, D_MODEL), F32))
        out_specs = (pl.BlockSpec((TM_FFN, D_MODEL), lambda i: (ctx_t(i), 0)),
                     pl.BlockSpec((TM_FFN, D_MODEL), lambda i: (lat_t(i), 0)))
    else:
        ada_w = N_MOD * D_MODEL // nt
        assert ada_w % LANES == 0
        in_specs += [pl.BlockSpec((8, D_MODEL), lambda i: (0, 0)),
                     pl.BlockSpec((None, D_MODEL, ada_w), lambda i: (l + 1, 0, tile(i))),
                     pl.BlockSpec((None, 1, ada_w), lambda i: (l + 1, 0, tile(i)))]
        args += [ada[0], ada[1], ada[2]]
        out_shape = (jax.ShapeDtypeStruct((n_ctx + n_lat, D_MODEL), F32),
                     jax.ShapeDtypeStruct((8, N_MOD * D_MODEL), F32))
        out_specs = (pl.BlockSpec((TM_FFN, D_MODEL), lambda i: (tile(i), 0)),
                     pl.BlockSpec((8, ada_w), lambda i: (0, tile(i))))
    return pl.pallas_call(
        functools.partial(_post_kernel, nt_ctx, s_ctx, split_x, final),
        out_shape=out_shape, grid=(N_CAST + nt,), in_specs=in_specs, out_specs=out_specs,
        scratch_shapes=[pltpu.VMEM((D_MODEL, D_MODEL), BF16), pltpu.VMEM((D_MODEL, 2 * D_FF), BF16),
                        pltpu.VMEM((D_FF, D_MODEL), BF16),
                        pltpu.VMEM((TM_FFN, D_MODEL), F32), pltpu.VMEM((TM_FFN, D_MODEL), BF16),
                        pltpu.VMEM((TM_FFN, D_FF), BF16)],
        compiler_params=pltpu.CompilerParams(dimension_semantics=("arbitrary",),
                                             vmem_limit_bytes=VMEM_LIMIT_FFN),
        name="post",
    )(*args)


W_IN_BODY = (352, 2656)


def _pack_in_kernel(w_ref, o_ref):
    tc = w_ref.shape[1]

    def put(dst, blk):
        o_ref[:, dst:dst + blk.shape[0]] = blk.T.astype(BF16)

    def rows(lo, n):
        return w_ref[lo:lo + n, :]

    def zeros(n):
        return jnp.zeros((n, tc), F32)

    put(C_CQ, jnp.concatenate([rows(0, MLA_Q_RANK), zeros(C_CKV - C_CQ - MLA_Q_RANK)], axis=0))
    put(C_CKV, rows(MLA_Q_RANK, MLA_KV_RANK))
    aux = [rows(MLA_Q_RANK + MLA_KV_RANK, MLA_ROPE)]
    for g in range(4):
        aux += [rows(W_IN_BODY[1] + ML_HEADS * g, ML_HEADS), zeros(8 - ML_HEADS)]
    aux.append(zeros(LANES - AUX_GATE - 32))
    put(C_AUX, jnp.concatenate(aux, axis=0))
    for lo in range(W_IN_BODY[0], W_IN_BODY[1], 256):
        put(C_DQ + lo - W_IN_BODY[0], rows(lo, 256))


def _pack_in_call(w_in):
    w_in_t = jnp.swapaxes(w_in, 1, 2)
    tc = 256
    return pl.pallas_call(
        _pack_in_kernel,
        out_shape=jax.ShapeDtypeStruct((DEPTH, D_MODEL, NP_IN), BF16),
        grid=(DEPTH, D_MODEL // tc),
        in_specs=[pl.BlockSpec((None, w_in_t.shape[1], tc), lambda l, i: (l, 0, i))],
        out_specs=pl.BlockSpec((None, tc, NP_IN), lambda l, i: (l, i, 0)),
        compiler_params=_cparams(("arbitrary", "arbitrary")),
        name="pack_w_in",
    )(w_in_t)


def _pack_weights(w_in, g_cq, w_uq, g_ckv, w_ukv, b_gate, g_subln, g_mnorm, g_norm1, g_norm2,
                  w_out, w_up, conv_w, conv_b, w_down, g_final, lam_q1, lam_k1, lam_q2, lam_k2):
    def cols(a, lo, n, pad=0):
        blk = a[..., lo:lo + n]
        if pad:
            blk = jnp.pad(blk, [(0, 0)] * (a.ndim - 1) + [(0, pad)])
        return blk

    w_in_p = _pack_in_call(w_in)

    hd = MLA_NOPE + MLA_ROPE
    w_uq_p = jnp.pad(w_uq.reshape(DEPTH, MLA_Q_RANK, MLA_HEADS, hd),
                     [(0, 0), (0, 256 - MLA_Q_RANK), (0, 0), (0, LANES - hd)])
    w_uq_p = w_uq_p.reshape(DEPTH, 256, MLA_HEADS * LANES).astype(BF16)

    w_ukv4 = w_ukv.reshape(DEPTH, MLA_KV_RANK, MLA_HEADS, MLA_NOPE + MLA_V)
    w_k = jnp.pad(w_ukv4[..., :MLA_NOPE], [(0, 0), (0, 0), (0, 0), (0, LANES - MLA_NOPE)])
    w_k = w_k.reshape(DEPTH, MLA_KV_RANK, MLA_HEADS * LANES)
    j = jnp.arange(LANES)[:, None]
    cix = jnp.arange(MLA_HEADS * LANES)[None, :]
    place = ((j < MLA_ROPE) & ((cix % LANES) == MLA_NOPE + j)).astype(F32)
    w_kk = jnp.concatenate([w_k, jnp.broadcast_to(place, (DEPTH, LANES, MLA_HEADS * LANES))],
                           axis=1).astype(BF16)
    w_v = w_ukv4[..., MLA_NOPE:].reshape(DEPTH, MLA_KV_RANK, MLA_HEADS * MLA_V).astype(BF16)

    return dict(
        w_in=w_in_p, w_uq=w_uq_p, w_kk=w_kk, w_v=w_v,
        g_norm1=g_norm1[:, None, :], g_norm2=g_norm2[:, None, :],
        g_cq=jnp.pad(g_cq, [(0, 0), (0, 256 - MLA_Q_RANK)])[:, None, :],
        g_ckv=g_ckv[:, None, :],
        b_gate=jnp.pad(jnp.pad(b_gate.reshape(DEPTH, 4, ML_HEADS), [(0, 0), (0, 0), (0, 4)])
                       .reshape(DEPTH, 32), [(0, 0), (AUX_GATE, LANES - AUX_GATE - 32)])[:, None, :],
        g_subln=jnp.tile(g_subln, (1, DF_HEADS))[:, None, :],
        g_mnorm=jnp.broadcast_to(g_mnorm[:, :, None], (DEPTH, ML_HEADS * ML_DV, LANES)),
        w_out=w_out, w_up=w_up, w_down=w_down,
        conv_w=conv_w, conv_b=conv_b[:, None, :], g_final=g_final[None, :],
        lam_q1=lam_q1[:, None, :], lam_k1=lam_k1[:, None, :],
        lam_q2=lam_q2[:, None, :], lam_k2=lam_k2[:, None, :],
    )


def _rope_tables(n_tok):
    t = np.arange(n_tok)
    row = (t // GRID_W).astype(np.float64)
    col = (t % GRID_W).astype(np.float64)
    nf = MLA_ROPE // 4
    inv = ROPE_BASE ** (-np.arange(nf, dtype=np.float64) / nf)
    ar = row[:, None] * inv[None, :]
    ac = col[:, None] * inv[None, :]
    ang = np.concatenate([ar, ar, ac, ac], axis=-1)
    quarter = (np.arange(MLA_ROPE) // nf) % 2
    cos = jnp.asarray(np.cos(ang), F32)
    sin_up = jnp.asarray(np.where(quarter == 0, -np.sin(ang), 0.0), F32)
    sin_dn = jnp.asarray(np.where(quarter == 1, np.sin(ang), 0.0), F32)
    ones = jnp.ones((n_tok, 1), F32)
    zeros = jnp.zeros((n_tok, 1), F32)

    def head_q(t32, fill):
        blk = jnp.concatenate([jnp.tile(fill, (1, MLA_NOPE)), t32, jnp.tile(fill, (1, 32))], axis=1)
        return jnp.tile(blk, (1, MLA_HEADS))

    def aux_k(t32, fill):
        return jnp.concatenate([t32, jnp.tile(fill, (1, LANES - MLA_ROPE))], axis=1)

    tq = (head_q(cos, ones), head_q(sin_up, zeros), head_q(sin_dn, zeros))
    td = tuple(jnp.tile(a, (1, 256 // DF_DIM)) for a in (cos, sin_up, sin_dn))
    tk = (aux_k(cos, ones), aux_k(sin_up, zeros), aux_k(sin_dn, zeros))
    return tq + td + tk


def kernel(x_prompt, x_sample, cache_mla_ckv, cache_mla_krope, cache_diff_k, cache_diff_v,
           state_mlstm_C, state_mlstm_n, state_mlstm_m, c, c_ctx, w_ada, b_ada, g_norm1, w_in,
           g_cq, w_uq, g_ckv, w_ukv, lam_q1, lam_k1, lam_q2, lam_k2, g_subln, b_gate, g_mnorm,
           w_out, g_norm2, w_up, conv_w, conv_b, w_down, g_final):
    bp, sp, _ = x_prompt.shape
    bl, sl, _ = x_sample.shape
    t_len = cache_mla_ckv.shape[2]

    wts = _pack_weights(w_in, g_cq, w_uq, g_ckv, w_ukv, b_gate, g_subln, g_mnorm, g_norm1, g_norm2,
                        w_out, w_up, conv_w, conv_b, w_down, g_final, lam_q1, lam_k1, lam_q2, lam_k2)
    tables = _rope_tables(sl)

    cond = jnp.concatenate([c_ctx[None, :], c, jnp.zeros((8 - 1 - bl, D_MODEL), F32)], axis=0)
    ada = (cond, w_ada, b_ada[:, None, :])
    mods = _ada_call(*ada).reshape(8 * N_MOD, 1, D_MODEL)

    def feat_major(a):
        return jnp.transpose(a, (0, 1, 3, 4, 2)).reshape(bl, DEPTH, 256, t_len)

    kctx, vctx, cdk, cdv = _ctxkv_call(
        cache_mla_ckv, jnp.swapaxes(cache_mla_krope, 2, 3), feat_major(cache_diff_k),
        feat_major(cache_diff_v), wts["w_kk"], wts["w_v"])
    c0_t = jnp.swapaxes(state_mlstm_C.reshape(bl, DEPTH, 2, 2, LANES, ML_DV), -1, -2)
    n0_r = jnp.broadcast_to(state_mlstm_n.reshape(bl, DEPTH, 2, 2, 1, LANES),
                            (bl, DEPTH, 2, 2, ML_NROWS, LANES))
    s0 = jnp.concatenate([c0_t, n0_r], axis=-2)
    m0 = jnp.broadcast_to(jnp.pad(state_mlstm_m, [(0, 0)] * 3 + [(0, 8 - ML_HEADS)])[..., None],
                          (bl, DEPTH, 2, 8, LANES))

    n_ctx, n_lat = bp * sp, bl * sl
    xs = (x_prompt.reshape(n_ctx, D_MODEL), x_sample.reshape(n_lat, D_MODEL))
    state_bufs = [jnp.zeros((bp, DEPTH, sp, MLA_KV_RANK), F32), jnp.zeros((bp, DEPTH, MLA_ROPE, sp), F32),
                  jnp.zeros((bp, DEPTH, 256, sp), F32), jnp.zeros((bp, DEPTH, 256, sp), F32)]
    c_buf = jnp.zeros((bp, DEPTH, 2, 2, LANES, LANES), F32)
    n_col, m_col = [], []
    for l in range(DEPTH):
        lam_init = 0.8 - 0.6 * math.exp(-0.3 * l)
        final = l == DEPTH - 1
        x_ctx, x_lat, lat_off = (xs[0], xs[1], 0) if len(xs) == 2 else (xs[0], xs[0], n_ctx)
        (q, k, v, dq, dk, dv, mq, mkt, mv, mo, gt, *state_bufs) = _pre_call(
            False, l, x_ctx, 0, bp, sp, mods, wts, None, state_bufs)
        oab_c = _attn_call(False, l, lam_init, (q, k, v), (dq, dk, dv), wts)
        oc_c, c_buf, n_fin, m_fin = _mlstm_call(False, l, mq, mkt, mv, mo, gt, wts, c_buf=c_buf)
        n_col.append(n_fin[..., 0, :].reshape(bp, 2, ML_HEADS, ML_DK))
        m_col.append(m_fin[:, :, :ML_HEADS, 0])
        (q, k, v, dq, dk, dv, mq, mkt, mv, mo, gt) = _pre_call(
            True, l, x_lat, lat_off, bl, sl, mods, wts, tables)
        oab_l = _attn_call(True, l, lam_init, (q, k, v), (dq, dk, dv), wts, (kctx, vctx), (cdk, cdv))
        oc_l = _mlstm_call(True, l, mq, mkt, mv, mo, gt, wts, s0, m0)
        out = _post_call(l, final, xs, n_ctx, sp, sl,
                         (oab_c.reshape(n_ctx, 512), oab_l.reshape(n_lat, 512)), (oc_c, oc_l),
                         mods, wts, ada)
        if final:
            xs = out
        else:
            xs, mods = (out[0],), out[1].reshape(8 * N_MOD, 1, D_MODEL)

    xp = xs[0].reshape(bp, sp, D_MODEL)
    xs = xs[1].reshape(bl, sl, D_MODEL)
    ckv_all, kr_all, dk_all, dv_all = state_bufs

    def token_major(a):
        return jnp.transpose(a.reshape(bp, DEPTH, DF_HEADS, 2 * DF_DIM, sp), (0, 1, 4, 2, 3))

    return (xp, xs, ckv_all, jnp.swapaxes(kr_all, 2, 3), token_major(dk_all), token_major(dv_all),
            c_buf.reshape(bp, DEPTH, 2, ML_HEADS, ML_DK, ML_DV),
            jnp.stack(n_col, axis=1), jnp.stack(m_col, axis=1))
```
